```python
import math
import jax, jax.numpy as jnp
from jax import lax
import numpy as np

D_MODEL = 2048
BATCH = 8
SEQ = 2048
DEPTH = 4

CHUNK = 64
N_MIXERS = 3
N_POOL_LAYERS = (DEPTH + 2) // 3
N_SB_LAYERS = (DEPTH + 1) // 3
N_SSM_LAYERS = DEPTH // 3

POOL_WINDOWS = (2, 4, 8, 16)
N_POOL_GROUPS = len(POOL_WINDOWS)
POOL_GROUP_DIM = D_MODEL // N_POOL_GROUPS

SB_HEAD_DIM = 128
SB_HEADS = D_MODEL // SB_HEAD_DIM
Q_BLOCK = 128

SSM_GROUP_CH = 16
SSM_GROUPS = D_MODEL // SSM_GROUP_CH
SSM_STATE = 64
SSM_DT_MIN = 1e-3
SSM_DT_MAX = 1e-1

D_FF = 5632
CONV_WIDTH = 3

RMS_EPS = 1e-6

kernel_name = "hybrid_pool_stickbreak_s5_convffn_trunk"


def rms_norm(x, g):
    xf = x.astype(jnp.float32)
    y = xf * lax.rsqrt(jnp.mean(xf * xf, axis=-1, keepdims=True) + RMS_EPS)
    return (y * g.astype(jnp.float32)).astype(x.dtype)


def multiscale_pool_mixer(h, w, b, scale):
    bsz, seq, _ = h.shape
    hf = h.astype(jnp.float32).reshape(bsz, seq, N_POOL_GROUPS, POOL_GROUP_DIM)
    cs = jnp.cumsum(hf, axis=1)
    cs = jnp.concatenate([jnp.zeros_like(cs[:, :1]), cs], axis=1)
    t = jnp.arange(seq)[:, None]
    win = jnp.array(POOL_WINDOWS, dtype=jnp.int32)[None, :]
    lo = jnp.maximum(t + 1 - win, 0)
    cnt = (t + 1 - lo).astype(jnp.float32)
    grp = jnp.arange(N_POOL_GROUPS)[None, :]
    lower = cs[:, lo, grp]
    mean = (cs[:, 1:] - lower) / cnt[None, :, :, None]
    pooled = mean - hf
    y = jnp.einsum('bsgc,gcd->bsgd', pooled, w.astype(jnp.float32))
    y = y.reshape(bsz, seq, D_MODEL) + b.astype(jnp.float32)
    return (y * scale.astype(jnp.float32)).astype(h.dtype)


def stick_breaking_attention(h, w_qkv, q_gain, k_gain, w_o):
    bsz, seq, _ = h.shape
    qkv = (h @ w_qkv).reshape(bsz, seq, 3, SB_HEADS, SB_HEAD_DIM)
    q = rms_norm(qkv[:, :, 0], q_gain).astype(jnp.float32).transpose(0, 2, 1, 3)
    k = rms_norm(qkv[:, :, 1], k_gain).astype(jnp.float32).transpose(0, 2, 1, 3)
    v = qkv[:, :, 2].transpose(0, 2, 1, 3)
    inv_sqrt_d = 1.0 / math.sqrt(SB_HEAD_DIM)
    outs = []
    for blk in range(seq // Q_BLOCK):
        q0 = blk * Q_BLOCK
        kv_len = q0 + Q_BLOCK
        qb = q[:, :, q0:kv_len]
        kb = k[:, :, :kv_len]
        vb = v[:, :, :kv_len]
        z = jnp.einsum('bhqd,bhkd->bhqk', qb, kb) * inv_sqrt_d
        t_idx = q0 + jnp.arange(Q_BLOCK)[:, None]
        s_idx = jnp.arange(kv_len)[None, :]
        mask = s_idx < t_idx
        log_beta = jax.nn.log_sigmoid(z)
        log_1m_beta = jnp.where(mask, jax.nn.log_sigmoid(-z), 0.0)
        log_remain = lax.cumsum(log_1m_beta, axis=3, reverse=True) - log_1m_beta
        attn = jnp.where(mask, jnp.exp(log_beta + log_remain), 0.0)
        outs.append(jnp.einsum('bhqk,bhkd->bhqd', attn.astype(vb.dtype), vb))
    o = jnp.concatenate(outs, axis=2)
    o = o.transpose(0, 2, 1, 3).reshape(bsz, seq, D_MODEL)
    return o @ w_o


def _ssm_combine(e1, e2):
    a1r, a1i, b1r, b1i = e1
    a2r, a2i, b2r, b2i = e2
    return (a2r * a1r - a2i * a1i,
            a2r * a1i + a2i * a1r,
            a2r * b1r - a2i * b1i + b2r,
            a2r * b1i + a2i * b1r + b2i)


def s5_mixer(h, lam_re, lam_im, log_step, b_re, b_im, c_re, c_im, d_skip, w_glu, b_glu):
    bsz, seq, _ = h.shape
    u = h.astype(jnp.float32).reshape(bsz, seq, SSM_GROUPS, SSM_GROUP_CH)
    lr = lam_re.astype(jnp.float32)
    li = lam_im.astype(jnp.float32)
    step = jnp.exp(log_step.astype(jnp.float32))[:, None]
    mag = jnp.exp(lr * step)
    lb_re = mag * jnp.cos(li * step)
    lb_im = mag * jnp.sin(li * step)
    den = lr * lr + li * li
    f_re = ((lb_re - 1.0) * lr + lb_im * li) / den
    f_im = (lb_im * lr - (lb_re - 1.0) * li) / den
    br = b_re.astype(jnp.float32)
    bi = b_im.astype(jnp.float32)
    bb_re = f_re[..., None] * br - f_im[..., None] * bi
    bb_im = f_re[..., None] * bi + f_im[..., None] * br
    bu_re = jnp.einsum('bsgh,gph->bsgp', u, bb_re)
    bu_im = jnp.einsum('bsgh,gph->bsgp', u, bb_im)
    a_re = jnp.broadcast_to(lb_re, bu_re.shape)
    a_im = jnp.broadcast_to(lb_im, bu_im.shape)
    _, _, xs_re, xs_im = lax.associative_scan(_ssm_combine, (a_re, a_im, bu_re, bu_im), axis=1)
    y = (jnp.einsum('bsgp,ghp->bsgh', xs_re, c_re.astype(jnp.float32))
         - jnp.einsum('bsgp,ghp->bsgh', xs_im, c_im.astype(jnp.float32))
         + d_skip.astype(jnp.float32).reshape(SSM_GROUPS, SSM_GROUP_CH) * u)
    y = jax.nn.gelu(y.reshape(bsz, seq, D_MODEL)).astype(h.dtype)
    gv = y @ w_glu + b_glu
    val, gate = jnp.split(gv, 2, axis=-1)
    return val * jax.nn.sigmoid(gate)


def conv_ffn(h, w_up, conv_w, conv_b, w_down):
    seq = h.shape[1]
    up = h @ w_up
    padded = jnp.pad(up, ((0, 0), (CONV_WIDTH - 1, 0), (0, 0)))
    c = conv_b + sum(conv_w[j] * padded[:, j:j + seq] for j in range(CONV_WIDTH))
    val, gate = jnp.split(c, 2, axis=-1)
    return (jax.nn.silu(gate) * val) @ w_down


def _fwd_setup_inputs(seed: int = 0) -> dict:
    key = jax.random.key(seed)
    ks = jax.random.split(key, 26)
    f32 = jnp.float32
    nrm = lambda k, shape, s: jax.random.normal(k, shape, f32) * s
    lam_im_base = jnp.pi * jnp.arange(SSM_STATE, dtype=f32)
    return {
        "x": jax.random.normal(ks[0], (BATCH, SEQ, D_MODEL), f32),
        "norm_mix_g": 1.0 + nrm(ks[1], (DEPTH, D_MODEL), 0.02),
        "norm_ffn_g": 1.0 + nrm(ks[2], (DEPTH, D_MODEL), 0.02),
        "pool_w": nrm(ks[3], (N_POOL_LAYERS, N_POOL_GROUPS, POOL_GROUP_DIM, POOL_GROUP_DIM), POOL_GROUP_DIM ** -0.5),
        "pool_b": nrm(ks[4], (N_POOL_LAYERS, D_MODEL), 0.01),
        "pool_scale": 1.0 + nrm(ks[5], (N_POOL_LAYERS, D_MODEL), 0.02),
        "sb_w_qkv": nrm(ks[6], (N_SB_LAYERS, D_MODEL, 3 * D_MODEL), D_MODEL ** -0.5),
        "sb_q_gain": 1.0 + nrm(ks[7], (N_SB_LAYERS, SB_HEAD_DIM), 0.02),
        "sb_k_gain": 1.0 + nrm(ks[8], (N_SB_LAYERS, SB_HEAD_DIM), 0.02),
        "sb_w_o": nrm(ks[9], (N_SB_LAYERS, D_MODEL, D_MODEL), D_MODEL ** -0.5),
        "ssm_lam_re": -0.5 + nrm(ks[10], (N_SSM_LAYERS, SSM_GROUPS, SSM_STATE), 0.01),
        "ssm_lam_im": lam_im_base + nrm(ks[11], (N_SSM_LAYERS, SSM_GROUPS, SSM_STATE), 0.01),
        "ssm_log_step": jax.random.uniform(ks[12], (N_SSM_LAYERS, SSM_GROUPS), f32,
                                           math.log(SSM_DT_MIN), math.log(SSM_DT_MAX)),
        "ssm_b_re": nrm(ks[13], (N_SSM_LAYERS, SSM_GROUPS, SSM_STATE, SSM_GROUP_CH), (2 * SSM_GROUP_CH) ** -0.5),
        "ssm_b_im": nrm(ks[14], (N_SSM_LAYERS, SSM_GROUPS, SSM_STATE, SSM_GROUP_CH), (2 * SSM_GROUP_CH) ** -0.5),
        "ssm_c_re": nrm(ks[15], (N_SSM_LAYERS, SSM_GROUPS, SSM_GROUP_CH, SSM_STATE), (2 * SSM_STATE) ** -0.5),
        "ssm_c_im": nrm(ks[16], (N_SSM_LAYERS, SSM_GROUPS, SSM_GROUP_CH, SSM_STATE), (2 * SSM_STATE) ** -0.5),
        "ssm_d": nrm(ks[17], (N_SSM_LAYERS, D_MODEL), 1.0),
        "ssm_w_glu": nrm(ks[18], (N_SSM_LAYERS, D_MODEL, 2 * D_MODEL), D_MODEL ** -0.5),
        "ssm_b_glu": nrm(ks[19], (N_SSM_LAYERS, 2 * D_MODEL), 0.01),
        "ffn_w_up": nrm(ks[20], (DEPTH, D_MODEL, 2 * D_FF), D_MODEL ** -0.5),
        "ffn_conv_w": nrm(ks[21], (DEPTH, CONV_WIDTH, 2 * D_FF), CONV_WIDTH ** -0.5),
        "ffn_conv_b": nrm(ks[22], (DEPTH, 2 * D_FF), 0.01),
        "ffn_w_down": nrm(ks[23], (DEPTH, D_FF, D_MODEL), D_FF ** -0.5),
    }


def _fwd_reference(x, norm_mix_g, norm_ffn_g, pool_w, pool_b, pool_scale,
              sb_w_qkv, sb_q_gain, sb_k_gain, sb_w_o,
              ssm_lam_re, ssm_lam_im, ssm_log_step, ssm_b_re, ssm_b_im,
              ssm_c_re, ssm_c_im, ssm_d, ssm_w_glu, ssm_b_glu,
              ffn_w_up, ffn_conv_w, ffn_conv_b, ffn_w_down):
    for i in range(DEPTH):
        kind = i % N_MIXERS
        j = i // N_MIXERS
        h = rms_norm(x, norm_mix_g[i])
        if kind == 0:
            m = multiscale_pool_mixer(h, pool_w[j], pool_b[j], pool_scale[j])
        elif kind == 1:
            m = stick_breaking_attention(h, sb_w_qkv[j], sb_q_gain[j], sb_k_gain[j], sb_w_o[j])
        else:
            m = s5_mixer(h, ssm_lam_re[j], ssm_lam_im[j], ssm_log_step[j], ssm_b_re[j], ssm_b_im[j],
                         ssm_c_re[j], ssm_c_im[j], ssm_d[j], ssm_w_glu[j], ssm_b_glu[j])
        x = x + m
        x = x + conv_ffn(rms_norm(x, norm_ffn_g[i]), ffn_w_up[i], ffn_conv_w[i], ffn_conv_b[i], ffn_w_down[i])
    return x


import jax as _jax
import jax.numpy as _jnp

TWIN_FORMAT = 'train_step'
FWD_PARAMS = ['x', 'norm_mix_g', 'norm_ffn_g', 'pool_w', 'pool_b', 'pool_scale', 'sb_w_qkv', 'sb_q_gain', 'sb_k_gain', 'sb_w_o', 'ssm_lam_re', 'ssm_lam_im', 'ssm_log_step', 'ssm_b_re', 'ssm_b_im', 'ssm_c_re', 'ssm_c_im', 'ssm_d', 'ssm_w_glu', 'ssm_b_glu', 'ffn_w_up', 'ffn_conv_w', 'ffn_conv_b', 'ffn_w_down']
TWIN_WEIGHTS = ['norm_mix_g', 'norm_ffn_g', 'pool_w', 'pool_b', 'pool_scale', 'sb_w_qkv', 'sb_q_gain', 'sb_k_gain', 'sb_w_o', 'ssm_lam_re', 'ssm_lam_im', 'ssm_log_step', 'ssm_b_re', 'ssm_b_im', 'ssm_c_re', 'ssm_c_im', 'ssm_d', 'ssm_w_glu', 'ssm_b_glu', 'ffn_w_up', 'ffn_conv_w', 'ffn_conv_b', 'ffn_w_down']
TWIN_DIFF_INPUT = 'x'
TWIN_INPUTS = ['x', 'norm_mix_g', 'norm_ffn_g', 'pool_w', 'pool_b', 'pool_scale', 'sb_w_qkv', 'sb_q_gain', 'sb_k_gain', 'sb_w_o', 'ssm_lam_re', 'ssm_lam_im', 'ssm_log_step', 'ssm_b_re', 'ssm_b_im', 'ssm_c_re', 'ssm_c_im', 'ssm_d', 'ssm_w_glu', 'ssm_b_glu', 'ffn_w_up', 'ffn_conv_w', 'ffn_conv_b', 'ffn_w_down', 'loss_target', 'm_norm_mix_g', 'm_norm_ffn_g', 'm_pool_w', 'm_pool_b', 'm_pool_scale', 'm_sb_w_qkv', 'm_sb_q_gain', 'm_sb_k_gain', 'm_sb_w_o', 'm_ssm_lam_re', 'm_ssm_lam_im', 'm_ssm_log_step', 'm_ssm_b_re', 'm_ssm_b_im', 'm_ssm_c_re', 'm_ssm_c_im', 'm_ssm_d', 'm_ssm_w_glu', 'm_ssm_b_glu', 'm_ffn_w_up', 'm_ffn_conv_w', 'm_ffn_conv_b', 'm_ffn_w_down', 'v_norm_mix_g', 'v_norm_ffn_g', 'v_pool_w', 'v_pool_b', 'v_pool_scale', 'v_sb_w_qkv', 'v_sb_q_gain', 'v_sb_k_gain', 'v_sb_w_o', 'v_ssm_lam_re', 'v_ssm_lam_im', 'v_ssm_log_step', 'v_ssm_b_re', 'v_ssm_b_im', 'v_ssm_c_re', 'v_ssm_c_im', 'v_ssm_d', 'v_ssm_w_glu', 'v_ssm_b_glu', 'v_ffn_w_up', 'v_ffn_conv_w', 'v_ffn_conv_b', 'v_ffn_w_down']
TWIN_OUTPUTS = ['loss', 'grad_x', 'grad_norm_mix_g', 'grad_norm_ffn_g', 'grad_pool_w', 'grad_pool_b', 'grad_pool_scale', 'grad_sb_w_qkv', 'grad_sb_q_gain', 'grad_sb_k_gain', 'grad_sb_w_o', 'grad_ssm_lam_re', 'grad_ssm_lam_im', 'grad_ssm_log_step', 'grad_ssm_b_re', 'grad_ssm_b_im', 'grad_ssm_c_re', 'grad_ssm_c_im', 'grad_ssm_d', 'grad_ssm_w_glu', 'grad_ssm_b_glu', 'grad_ffn_w_up', 'grad_ffn_conv_w', 'grad_ffn_conv_b', 'grad_ffn_w_down', 'delta_norm_mix_g', 'delta_norm_ffn_g', 'delta_pool_w', 'delta_pool_b', 'delta_pool_scale', 'delta_sb_w_qkv', 'delta_sb_q_gain', 'delta_sb_k_gain', 'delta_sb_w_o', 'delta_ssm_lam_re', 'delta_ssm_lam_im', 'delta_ssm_log_step', 'delta_ssm_b_re', 'delta_ssm_b_im', 'delta_ssm_c_re', 'delta_ssm_c_im', 'delta_ssm_d', 'delta_ssm_w_glu', 'delta_ssm_b_glu', 'delta_ffn_w_up', 'delta_ffn_conv_w', 'delta_ffn_conv_b', 'delta_ffn_w_down', 'new_m_norm_mix_g', 'new_m_norm_ffn_g', 'new_m_pool_w', 'new_m_pool_b', 'new_m_pool_scale', 'new_m_sb_w_qkv', 'new_m_sb_q_gain', 'new_m_sb_k_gain', 'new_m_sb_w_o', 'new_m_ssm_lam_re', 'new_m_ssm_lam_im', 'new_m_ssm_log_step', 'new_m_ssm_b_re', 'new_m_ssm_b_im', 'new_m_ssm_c_re', 'new_m_ssm_c_im', 'new_m_ssm_d', 'new_m_ssm_w_glu', 'new_m_ssm_b_glu', 'new_m_ffn_w_up', 'new_m_ffn_conv_w', 'new_m_ffn_conv_b', 'new_m_ffn_w_down', 'new_v_norm_mix_g', 'new_v_norm_ffn_g', 'new_v_pool_w', 'new_v_pool_b', 'new_v_pool_scale', 'new_v_sb_w_qkv', 'new_v_sb_q_gain', 'new_v_sb_k_gain', 'new_v_sb_w_o', 'new_v_ssm_lam_re', 'new_v_ssm_lam_im', 'new_v_ssm_log_step', 'new_v_ssm_b_re', 'new_v_ssm_b_im', 'new_v_ssm_c_re', 'new_v_ssm_c_im', 'new_v_ssm_d', 'new_v_ssm_w_glu', 'new_v_ssm_b_glu', 'new_v_ffn_w_up', 'new_v_ffn_conv_w', 'new_v_ffn_conv_b', 'new_v_ffn_w_down']
TWIN_LEAF_KINDS = {'loss': 'loss', 'grad_x': 'grad_x', 'grad_norm_mix_g': 'grad_w', 'grad_norm_ffn_g': 'grad_w', 'grad_pool_w': 'grad_w', 'grad_pool_b': 'grad_w', 'grad_pool_scale': 'grad_w', 'grad_sb_w_qkv': 'grad_w', 'grad_sb_q_gain': 'grad_w', 'grad_sb_k_gain': 'grad_w', 'grad_sb_w_o': 'grad_w', 'grad_ssm_lam_re': 'grad_w', 'grad_ssm_lam_im': 'grad_w', 'grad_ssm_log_step': 'grad_w', 'grad_ssm_b_re': 'grad_w', 'grad_ssm_b_im': 'grad_w', 'grad_ssm_c_re': 'grad_w', 'grad_ssm_c_im': 'grad_w', 'grad_ssm_d': 'grad_w', 'grad_ssm_w_glu': 'grad_w', 'grad_ssm_b_glu': 'grad_w', 'grad_ffn_w_up': 'grad_w', 'grad_ffn_conv_w': 'grad_w', 'grad_ffn_conv_b': 'grad_w', 'grad_ffn_w_down': 'grad_w', 'delta_norm_mix_g': 'delta_w', 'delta_norm_ffn_g': 'delta_w', 'delta_pool_w': 'delta_w', 'delta_pool_b': 'delta_w', 'delta_pool_scale': 'delta_w', 'delta_sb_w_qkv': 'delta_w', 'delta_sb_q_gain': 'delta_w', 'delta_sb_k_gain': 'delta_w', 'delta_sb_w_o': 'delta_w', 'delta_ssm_lam_re': 'delta_w', 'delta_ssm_lam_im': 'delta_w', 'delta_ssm_log_step': 'delta_w', 'delta_ssm_b_re': 'delta_w', 'delta_ssm_b_im': 'delta_w', 'delta_ssm_c_re': 'delta_w', 'delta_ssm_c_im': 'delta_w', 'delta_ssm_d': 'delta_w', 'delta_ssm_w_glu': 'delta_w', 'delta_ssm_b_glu': 'delta_w', 'delta_ffn_w_up': 'delta_w', 'delta_ffn_conv_w': 'delta_w', 'delta_ffn_conv_b': 'delta_w', 'delta_ffn_w_down': 'delta_w', 'new_m_norm_mix_g': 'new_m', 'new_m_norm_ffn_g': 'new_m', 'new_m_pool_w': 'new_m', 'new_m_pool_b': 'new_m', 'new_m_pool_scale': 'new_m', 'new_m_sb_w_qkv': 'new_m', 'new_m_sb_q_gain': 'new_m', 'new_m_sb_k_gain': 'new_m', 'new_m_sb_w_o': 'new_m', 'new_m_ssm_lam_re': 'new_m', 'new_m_ssm_lam_im': 'new_m', 'new_m_ssm_log_step': 'new_m', 'new_m_ssm_b_re': 'new_m', 'new_m_ssm_b_im': 'new_m', 'new_m_ssm_c_re': 'new_m', 'new_m_ssm_c_im': 'new_m', 'new_m_ssm_d': 'new_m', 'new_m_ssm_w_glu': 'new_m', 'new_m_ssm_b_glu': 'new_m', 'new_m_ffn_w_up': 'new_m', 'new_m_ffn_conv_w': 'new_m', 'new_m_ffn_conv_b': 'new_m', 'new_m_ffn_w_down': 'new_m', 'new_v_norm_mix_g': 'new_v', 'new_v_norm_ffn_g': 'new_v', 'new_v_pool_w': 'new_v', 'new_v_pool_b': 'new_v', 'new_v_pool_scale': 'new_v', 'new_v_sb_w_qkv': 'new_v', 'new_v_sb_q_gain': 'new_v', 'new_v_sb_k_gain': 'new_v', 'new_v_sb_w_o': 'new_v', 'new_v_ssm_lam_re': 'new_v', 'new_v_ssm_lam_im': 'new_v', 'new_v_ssm_log_step': 'new_v', 'new_v_ssm_b_re': 'new_v', 'new_v_ssm_b_im': 'new_v', 'new_v_ssm_c_re': 'new_v', 'new_v_ssm_c_im': 'new_v', 'new_v_ssm_d': 'new_v', 'new_v_ssm_w_glu': 'new_v', 'new_v_ssm_b_glu': 'new_v', 'new_v_ffn_w_up': 'new_v', 'new_v_ffn_conv_w': 'new_v', 'new_v_ffn_conv_b': 'new_v', 'new_v_ffn_w_down': 'new_v'}


def _forward(args):
    return _fwd_reference(*[args[k] for k in FWD_PARAMS])


def _output_shape():
    out = _jax.eval_shape(lambda: _forward(_fwd_setup_inputs(0)))
    return out.shape, out.dtype

N_MICROBATCH = 1
ADAM_LR = 0.001
ADAM_B1 = 0.9
ADAM_B2 = 0.999
ADAM_EPS = 1e-08
ADAM_WD = 0.01
ADAM_STEP = 10
PER_EXAMPLE_BATCH_AXIS = {'x': 0, 'loss_target': 0}
SHARED_INPUTS = []
_WEIGHT_DTYPES = {'norm_mix_g': _jnp.float32, 'norm_ffn_g': _jnp.float32, 'pool_w': _jnp.float32, 'pool_b': _jnp.float32, 'pool_scale': _jnp.float32, 'sb_w_qkv': _jnp.float32, 'sb_q_gain': _jnp.float32, 'sb_k_gain': _jnp.float32, 'sb_w_o': _jnp.float32, 'ssm_lam_re': _jnp.float32, 'ssm_lam_im': _jnp.float32, 'ssm_log_step': _jnp.float32, 'ssm_b_re': _jnp.float32, 'ssm_b_im': _jnp.float32, 'ssm_c_re': _jnp.float32, 'ssm_c_im': _jnp.float32, 'ssm_d': _jnp.float32, 'ssm_w_glu': _jnp.float32, 'ssm_b_glu': _jnp.float32, 'ffn_w_up': _jnp.float32, 'ffn_conv_w': _jnp.float32, 'ffn_conv_b': _jnp.float32, 'ffn_w_down': _jnp.float32}
MOMENT_SCALE = {'norm_mix_g': 4.977611e+00, 'norm_ffn_g': 6.457460e+00, 'pool_w': 5.272219e-01, 'pool_b': 3.063322e+00, 'pool_scale': 6.556340e+00, 'sb_w_qkv': 1.396002e-01, 'sb_q_gain': 8.266110e+00, 'sb_k_gain': 8.237625e+00, 'sb_w_o': 1.906385e-01, 'ssm_lam_re': 4.646528e-03, 'ssm_lam_im': 4.447722e-03, 'ssm_log_step': 2.431547e+00, 'ssm_b_re': 3.291429e-03, 'ssm_b_im': 3.264445e-03, 'ssm_c_re': 6.578172e-03, 'ssm_c_im': 6.521743e-03, 'ssm_d': 1.178971e+00, 'ssm_w_glu': 2.873437e-01, 'ssm_b_glu': 9.006102e-01, 'ffn_w_up': 1.157538e-01, 'ffn_conv_w': 8.804415e-01, 'ffn_conv_b': 8.109240e-01, 'ffn_w_down': 1.569388e-01}


def _to_microbatches(a, axis):
    t = _jnp.moveaxis(a, axis, 0)
    t = t.reshape((N_MICROBATCH, t.shape[0] // N_MICROBATCH) + t.shape[1:])
    return _jnp.moveaxis(t, 1, axis + 1)


def setup_inputs(seed: int = 0) -> dict:
    inp = _fwd_setup_inputs(seed)
    key = _jax.random.fold_in(_jax.random.key(seed), 7919)
    shape, _ = _output_shape()
    out = dict(inp)
    out["loss_target"] = _jax.random.normal(_jax.random.fold_in(key, 0), shape, _jnp.float32)
    for i, name in enumerate(TWIN_WEIGHTS):
        w = inp[name].astype(_jnp.float32)
        if MOMENT_SCALE is None:
            s = _jnp.sqrt(_jnp.mean(_jnp.square(w)) + 1e-30)
        else:
            s = MOMENT_SCALE[name]
        km, kv = _jax.random.split(_jax.random.fold_in(key, i + 1))
        out[name] = w
        out["m_" + name] = s * _jax.random.normal(km, w.shape, _jnp.float32)
        out["v_" + name] = (s * s) * _jax.random.uniform(kv, w.shape, _jnp.float32, 0.5, 1.5)
    if N_MICROBATCH > 1:
        for name, axis in PER_EXAMPLE_BATCH_AXIS.items():
            out[name] = _to_microbatches(out[name], axis)
    return {'x': out['x'], 'norm_mix_g': out['norm_mix_g'], 'norm_ffn_g': out['norm_ffn_g'], 'pool_w': out['pool_w'], 'pool_b': out['pool_b'], 'pool_scale': out['pool_scale'], 'sb_w_qkv': out['sb_w_qkv'], 'sb_q_gain': out['sb_q_gain'], 'sb_k_gain': out['sb_k_gain'], 'sb_w_o': out['sb_w_o'], 'ssm_lam_re': out['ssm_lam_re'], 'ssm_lam_im': out['ssm_lam_im'], 'ssm_log_step': out['ssm_log_step'], 'ssm_b_re': out['ssm_b_re'], 'ssm_b_im': out['ssm_b_im'], 'ssm_c_re': out['ssm_c_re'], 'ssm_c_im': out['ssm_c_im'], 'ssm_d': out['ssm_d'], 'ssm_w_glu': out['ssm_w_glu'], 'ssm_b_glu': out['ssm_b_glu'], 'ffn_w_up': out['ffn_w_up'], 'ffn_conv_w': out['ffn_conv_w'], 'ffn_conv_b': out['ffn_conv_b'], 'ffn_w_down': out['ffn_w_down'], 'loss_target': out['loss_target'], 'm_norm_mix_g': out['m_norm_mix_g'], 'm_norm_ffn_g': out['m_norm_ffn_g'], 'm_pool_w': out['m_pool_w'], 'm_pool_b': out['m_pool_b'], 'm_pool_scale': out['m_pool_scale'], 'm_sb_w_qkv': out['m_sb_w_qkv'], 'm_sb_q_gain': out['m_sb_q_gain'], 'm_sb_k_gain': out['m_sb_k_gain'], 'm_sb_w_o': out['m_sb_w_o'], 'm_ssm_lam_re': out['m_ssm_lam_re'], 'm_ssm_lam_im': out['m_ssm_lam_im'], 'm_ssm_log_step': out['m_ssm_log_step'], 'm_ssm_b_re': out['m_ssm_b_re'], 'm_ssm_b_im': out['m_ssm_b_im'], 'm_ssm_c_re': out['m_ssm_c_re'], 'm_ssm_c_im': out['m_ssm_c_im'], 'm_ssm_d': out['m_ssm_d'], 'm_ssm_w_glu': out['m_ssm_w_glu'], 'm_ssm_b_glu': out['m_ssm_b_glu'], 'm_ffn_w_up': out['m_ffn_w_up'], 'm_ffn_conv_w': out['m_ffn_conv_w'], 'm_ffn_conv_b': out['m_ffn_conv_b'], 'm_ffn_w_down': out['m_ffn_w_down'], 'v_norm_mix_g': out['v_norm_mix_g'], 'v_norm_ffn_g': out['v_norm_ffn_g'], 'v_pool_w': out['v_pool_w'], 'v_pool_b': out['v_pool_b'], 'v_pool_scale': out['v_pool_scale'], 'v_sb_w_qkv': out['v_sb_w_qkv'], 'v_sb_q_gain': out['v_sb_q_gain'], 'v_sb_k_gain': out['v_sb_k_gain'], 'v_sb_w_o': out['v_sb_w_o'], 'v_ssm_lam_re': out['v_ssm_lam_re'], 'v_ssm_lam_im': out['v_ssm_lam_im'], 'v_ssm_log_step': out['v_ssm_log_step'], 'v_ssm_b_re': out['v_ssm_b_re'], 'v_ssm_b_im': out['v_ssm_b_im'], 'v_ssm_c_re': out['v_ssm_c_re'], 'v_ssm_c_im': out['v_ssm_c_im'], 'v_ssm_d': out['v_ssm_d'], 'v_ssm_w_glu': out['v_ssm_w_glu'], 'v_ssm_b_glu': out['v_ssm_b_glu'], 'v_ffn_w_up': out['v_ffn_w_up'], 'v_ffn_conv_w': out['v_ffn_conv_w'], 'v_ffn_conv_b': out['v_ffn_conv_b'], 'v_ffn_w_down': out['v_ffn_w_down']}


def _loss(weights, diff, rest, loss_target):
    with _jax.named_scope("forward"):
        args = {**rest, TWIN_DIFF_INPUT: diff, **{k: w.astype(_WEIGHT_DTYPES[k]) for k, w in weights.items()}}
        y = _forward(args)
    with _jax.named_scope("loss_head"):
        err = _jnp.square(y.astype(_jnp.float32) - loss_target)
        return 0.5 * _jnp.sum(_jnp.mean(err, axis=-1)) if err.ndim else 0.5 * err


def _adamw(w, g, m, v):
    m = ADAM_B1 * m + (1.0 - ADAM_B1) * g
    v = ADAM_B2 * v + (1.0 - ADAM_B2) * _jnp.square(g)
    m_hat = m / (1.0 - ADAM_B1 ** ADAM_STEP)
    v_hat = v / (1.0 - ADAM_B2 ** ADAM_STEP)
    delta = -ADAM_LR * (m_hat / (_jnp.sqrt(v_hat) + ADAM_EPS) + ADAM_WD * w)
    return delta, m, v


def reference(x, norm_mix_g, norm_ffn_g, pool_w, pool_b, pool_scale, sb_w_qkv, sb_q_gain, sb_k_gain, sb_w_o, ssm_lam_re, ssm_lam_im, ssm_log_step, ssm_b_re, ssm_b_im, ssm_c_re, ssm_c_im, ssm_d, ssm_w_glu, ssm_b_glu, ffn_w_up, ffn_conv_w, ffn_conv_b, ffn_w_down, loss_target, m_norm_mix_g, m_norm_ffn_g, m_pool_w, m_pool_b, m_pool_scale, m_sb_w_qkv, m_sb_q_gain, m_sb_k_gain, m_sb_w_o, m_ssm_lam_re, m_ssm_lam_im, m_ssm_log_step, m_ssm_b_re, m_ssm_b_im, m_ssm_c_re, m_ssm_c_im, m_ssm_d, m_ssm_w_glu, m_ssm_b_glu, m_ffn_w_up, m_ffn_conv_w, m_ffn_conv_b, m_ffn_w_down, v_norm_mix_g, v_norm_ffn_g, v_pool_w, v_pool_b, v_pool_scale, v_sb_w_qkv, v_sb_q_gain, v_sb_k_gain, v_sb_w_o, v_ssm_lam_re, v_ssm_lam_im, v_ssm_log_step, v_ssm_b_re, v_ssm_b_im, v_ssm_c_re, v_ssm_c_im, v_ssm_d, v_ssm_w_glu, v_ssm_b_glu, v_ffn_w_up, v_ffn_conv_w, v_ffn_conv_b, v_ffn_w_down):
    given = dict(x=x, norm_mix_g=norm_mix_g, norm_ffn_g=norm_ffn_g, pool_w=pool_w, pool_b=pool_b, pool_scale=pool_scale, sb_w_qkv=sb_w_qkv, sb_q_gain=sb_q_gain, sb_k_gain=sb_k_gain, sb_w_o=sb_w_o, ssm_lam_re=ssm_lam_re, ssm_lam_im=ssm_lam_im, ssm_log_step=ssm_log_step, ssm_b_re=ssm_b_re, ssm_b_im=ssm_b_im, ssm_c_re=ssm_c_re, ssm_c_im=ssm_c_im, ssm_d=ssm_d, ssm_w_glu=ssm_w_glu, ssm_b_glu=ssm_b_glu, ffn_w_up=ffn_w_up, ffn_conv_w=ffn_conv_w, ffn_conv_b=ffn_conv_b, ffn_w_down=ffn_w_down, loss_target=loss_target, m_norm_mix_g=m_norm_mix_g, m_norm_ffn_g=m_norm_ffn_g, m_pool_w=m_pool_w, m_pool_b=m_pool_b, m_pool_scale=m_pool_scale, m_sb_w_qkv=m_sb_w_qkv, m_sb_q_gain=m_sb_q_gain, m_sb_k_gain=m_sb_k_gain, m_sb_w_o=m_sb_w_o, m_ssm_lam_re=m_ssm_lam_re, m_ssm_lam_im=m_ssm_lam_im, m_ssm_log_step=m_ssm_log_step, m_ssm_b_re=m_ssm_b_re, m_ssm_b_im=m_ssm_b_im, m_ssm_c_re=m_ssm_c_re, m_ssm_c_im=m_ssm_c_im, m_ssm_d=m_ssm_d, m_ssm_w_glu=m_ssm_w_glu, m_ssm_b_glu=m_ssm_b_glu, m_ffn_w_up=m_ffn_w_up, m_ffn_conv_w=m_ffn_conv_w, m_ffn_conv_b=m_ffn_conv_b, m_ffn_w_down=m_ffn_w_down, v_norm_mix_g=v_norm_mix_g, v_norm_ffn_g=v_norm_ffn_g, v_pool_w=v_pool_w, v_pool_b=v_pool_b, v_pool_scale=v_pool_scale, v_sb_w_qkv=v_sb_w_qkv, v_sb_q_gain=v_sb_q_gain, v_sb_k_gain=v_sb_k_gain, v_sb_w_o=v_sb_w_o, v_ssm_lam_re=v_ssm_lam_re, v_ssm_lam_im=v_ssm_lam_im, v_ssm_log_step=v_ssm_log_step, v_ssm_b_re=v_ssm_b_re, v_ssm_b_im=v_ssm_b_im, v_ssm_c_re=v_ssm_c_re, v_ssm_c_im=v_ssm_c_im, v_ssm_d=v_ssm_d, v_ssm_w_glu=v_ssm_w_glu, v_ssm_b_glu=v_ssm_b_glu, v_ffn_w_up=v_ffn_w_up, v_ffn_conv_w=v_ffn_conv_w, v_ffn_conv_b=v_ffn_conv_b, v_ffn_w_down=v_ffn_w_down)
    weights = {n: given[n] for n in TWIN_WEIGHTS}
    shared = {n: given[n] for n in SHARED_INPUTS}
    per_example = {n: given[n] for n in ['x']}
    grad_fn = _jax.value_and_grad(_loss, argnums=(0, 1))

    def one_microbatch(ex, loss_target):
        ex = dict(ex)
        diff = ex.pop(TWIN_DIFF_INPUT)
        return grad_fn(weights, diff, {**shared, **ex}, loss_target)

    if N_MICROBATCH == 1:
        loss, (grad_w, grad_x) = one_microbatch(per_example, given["loss_target"])
    else:
        def body(carry, xs):
            loss_sum, grad_sum = carry
            l_k, (gw_k, gx_k) = one_microbatch(xs[0], xs[1])
            with _jax.named_scope("update"):
                return (loss_sum + l_k, _jax.tree.map(_jnp.add, grad_sum, gw_k)), gx_k

        init = (_jnp.zeros((), _jnp.float32), _jax.tree.map(_jnp.zeros_like, weights))
        (loss, grad_w), grad_x = _jax.lax.scan(body, init, (per_example, given["loss_target"]))
    with _jax.named_scope("update"):
        delta_w, new_m, new_v = {}, {}, {}
        for n in TWIN_WEIGHTS:
            delta_w[n], new_m[n], new_v[n] = _adamw(weights[n], grad_w[n], given["m_" + n], given["v_" + n])
    return (loss, grad_x, *[grad_w[n] for n in TWIN_WEIGHTS], *[delta_w[n] for n in TWIN_WEIGHTS],
            *[new_m[n] for n in TWIN_WEIGHTS], *[new_v[n] for n in TWIN_WEIGHTS])
```

```python
import functools
import math

import jax
import jax.numpy as jnp
from jax import lax
from jax.experimental import pallas as pl
from jax.experimental.pallas import tpu as pltpu

F32 = jnp.float32
BF16 = jnp.bfloat16

N_DEV = 8
D_MODEL = 2048
D_FF = 5632
DEPTH = 4
POOL_GROUPS = 4
POOL_DIM = 512
HEADS = 16
HEAD_DIM = 128
SSM_GROUPS = 128
SSM_CH = 16
SSM_STATE = 64
SSM_BLOCK_GROUPS = 8
SSM_BLOCK_LANES = SSM_BLOCK_GROUPS * SSM_STATE
RMS_EPS = 1e-6
ADAM_LR = 0.001
ADAM_B1 = 0.9
ADAM_B2 = 0.999
ADAM_EPS = 1e-08
ADAM_WD = 0.01
ADAM_STEP = 10

VMEM_LIMIT_BYTES = 56 * 1024 * 1024
LANE = 128
SUBLANE = 8
MESH = pl.DeviceIdType.MESH


def _cparams(*sem):
    return pltpu.CompilerParams(dimension_semantics=tuple(sem), vmem_limit_bytes=VMEM_LIMIT_BYTES)


def _sds(shape, dtype):
    return jax.ShapeDtypeStruct(tuple(shape), dtype)


def _mm(a, b, *, dims, grid, a_spec, b_spec, o_spec, out_shape, out_dtype, name, k_axis=None, acc_shape=None,
        res=None, res_spec=None):
    nk = grid[k_axis] if k_axis is not None else 1
    has_res = res is not None

    def body(*refs):
        if has_res:
            a_ref, b_ref, r_ref, o_ref = refs[:4]
            scr = refs[4:]
        else:
            a_ref, b_ref, o_ref = refs[:3]
            r_ref = None
            scr = refs[3:]
        p = lax.dot_general(a_ref[...], b_ref[...], (dims, ((), ())), preferred_element_type=F32)
        if k_axis is None:
            if has_res:
                p = p + r_ref[...]
            o_ref[...] = p.astype(o_ref.dtype)
        else:
            acc = scr[0]
            k = pl.program_id(k_axis)

            @pl.when(k == 0)
            def _():
                acc[...] = p

            @pl.when(k > 0)
            def _():
                acc[...] += p

            @pl.when(k == nk - 1)
            def _():
                r = acc[...]
                if has_res:
                    r = r + r_ref[...]
                o_ref[...] = r.astype(o_ref.dtype)

    sem = ["parallel"] * len(grid)
    if k_axis is not None:
        sem[k_axis] = "arbitrary"
    in_specs = [a_spec, b_spec] + ([res_spec] if has_res else [])
    args = (a, b) + ((res,) if has_res else ())
    scratch = [pltpu.VMEM(acc_shape, F32)] if k_axis is not None else []
    return pl.pallas_call(
        body, name=name, grid=grid, in_specs=in_specs, out_specs=o_spec, out_shape=_sds(out_shape, out_dtype),
        scratch_shapes=scratch, compiler_params=_cparams(*sem),
    )(*args)


NN = ((1,), (0,))
NT = ((1,), (1,))
TN = ((0,), (0,))


def _mm_nn_pieces(a, wp, *, tm, name, out_dtype=F32):
    s, k = a.shape
    tm = min(tm, s)
    p, _, c = wp.shape
    return _mm(a, wp, dims=NN, grid=(s // tm, p),
               a_spec=pl.BlockSpec((tm, k), lambda m, n: (m, 0)),
               b_spec=pl.BlockSpec((None, k, c), lambda m, n: (n, 0, 0)),
               o_spec=pl.BlockSpec((tm, c), lambda m, n: (m, n)),
               out_shape=(s, p * c), out_dtype=out_dtype, name=name)


def _mm_nt_pieces(a, wp, *, tm, tn, name, out_dtype=F32):
    s = a.shape[0]
    tm = min(tm, s)
    p, n, c = wp.shape
    return _mm(a, wp, dims=NT, grid=(s // tm, n // tn, p), k_axis=2, acc_shape=(tm, tn),
               a_spec=pl.BlockSpec((tm, c), lambda m, j, k: (m, k)),
               b_spec=pl.BlockSpec((None, tn, c), lambda m, j, k: (k, j, 0)),
               o_spec=pl.BlockSpec((tm, tn), lambda m, j, k: (m, j)),
               out_shape=(s, n), out_dtype=out_dtype, name=name)


def _mm_tn_pieces(a, g, *, pieces, tm, name, out_dtype=BF16):
    s, m = a.shape
    c = g.shape[1] // pieces
    return _mm(a, g, dims=TN, grid=(pieces, m // tm),
               a_spec=pl.BlockSpec((s, tm), lambda n, i: (0, i)),
               b_spec=pl.BlockSpec((s, c), lambda n, i: (0, n)),
               o_spec=pl.BlockSpec((None, tm, c), lambda n, i: (n, i, 0)),
               out_shape=(pieces, m, c), out_dtype=out_dtype, name=name)


def _mm_nn(a, w, *, tm, tn, name, out_dtype=F32, res=None):
    s, k = a.shape
    tm = min(tm, s)
    n = w.shape[1]
    return _mm(a, w, dims=NN, grid=(s // tm, n // tn),
               a_spec=pl.BlockSpec((tm, k), lambda m, j: (m, 0)),
               b_spec=pl.BlockSpec((k, tn), lambda m, j: (0, j)),
               o_spec=pl.BlockSpec((tm, tn), lambda m, j: (m, j)),
               res=res, res_spec=pl.BlockSpec((tm, tn), lambda m, j: (m, j)),
               out_shape=(s, n), out_dtype=out_dtype, name=name)


def _mm_nt(a, w, *, tm, tn, name, out_dtype=F32):
    s, k = a.shape
    tm = min(tm, s)
    n = w.shape[0]
    return _mm(a, w, dims=NT, grid=(s // tm, n // tn),
               a_spec=pl.BlockSpec((tm, k), lambda m, j: (m, 0)),
               b_spec=pl.BlockSpec((tn, k), lambda m, j: (j, 0)),
               o_spec=pl.BlockSpec((tm, tn), lambda m, j: (m, j)),
               out_shape=(s, n), out_dtype=out_dtype, name=name)


def _mm_tn(a, g, *, tm, tn, name, out_dtype=BF16):
    s, m = a.shape
    n = g.shape[1]
    return _mm(a, g, dims=TN, grid=(m // tm, n // tn),
               a_spec=pl.BlockSpec((s, tm), lambda i, j: (0, i)),
               b_spec=pl.BlockSpec((s, tn), lambda i, j: (0, j)),
               o_spec=pl.BlockSpec((tm, tn), lambda i, j: (i, j)),
               out_shape=(m, n), out_dtype=out_dtype, name=name)


ROW_TILE = 256


def _rms_fwd(x, g_row, *, want_f32, name):
    s, d = x.shape

    def body(x_ref, g_ref, hb_ref, *rest):
        xv = x_ref[...]
        r = lax.rsqrt(jnp.mean(xv * xv, axis=-1, keepdims=True) + RMS_EPS)
        h = (xv * r) * g_ref[...]
        hb_ref[...] = h.astype(BF16)
        if want_f32:
            rest[0][...] = h

    row = pl.BlockSpec((ROW_TILE, d), lambda i: (i, 0))
    out_shape = [_sds((s, d), BF16)] + ([_sds((s, d), F32)] if want_f32 else [])
    out = pl.pallas_call(
        body, name=name, grid=(s // ROW_TILE,),
        in_specs=[row, pl.BlockSpec((1, d), lambda i: (0, 0))],
        out_specs=[row] * len(out_shape), out_shape=out_shape, compiler_params=_cparams("parallel"),
    )(x, g_row)
    return out if want_f32 else (out[0], None)


def _rms_bwd(x, g_row, dh, dres, *, name):
    s, d = x.shape

    def body(x_ref, g_ref, dh_ref, dres_ref, dx_ref, dg_ref):
        xv = x_ref[...]
        r = lax.rsqrt(jnp.mean(xv * xv, axis=-1, keepdims=True) + RMS_EPS)
        xn = xv * r
        dhv = dh_ref[...]
        dxn = dhv * g_ref[...]
        dx_ref[...] = dres_ref[...] + r * (dxn - xn * jnp.mean(dxn * xn, axis=-1, keepdims=True))
        part = jnp.sum(dhv * xn, axis=0, keepdims=True)

        @pl.when(pl.program_id(0) == 0)
        def _():
            dg_ref[...] = part

        @pl.when(pl.program_id(0) > 0)
        def _():
            dg_ref[...] += part

    row = pl.BlockSpec((ROW_TILE, d), lambda i: (i, 0))
    vec = pl.BlockSpec((1, d), lambda i: (0, 0))
    return pl.pallas_call(
        body, name=name, grid=(s // ROW_TILE,), in_specs=[row, vec, row, row], out_specs=[row, vec],
        out_shape=[_sds((s, d), F32), _sds((1, d), F32)], compiler_params=_cparams("arbitrary"),
    )(x, g_row, dh, dres)


def _shift_down(v, k):
    row = lax.broadcasted_iota(jnp.int32, v.shape, 0)
    return jnp.where(row >= k, pltpu.roll(v, k, 0), 0.0)


def _shift_up(v, k):
    n = v.shape[0]
    row = lax.broadcasted_iota(jnp.int32, v.shape, 0)
    return jnp.where(row < n - k, pltpu.roll(v, n - k, 0), 0.0)


def _sigmoid(z):
    return 1.0 / (1.0 + jnp.exp(-z))


FF_COL_TILE = 256


def _conv3(u, w, b):
    return b + w[0:1, :] * _shift_down(u, 2) + w[1:2, :] * _shift_down(u, 1) + w[2:3, :] * u


def _conv_gate_fwd(up, conv_w, conv_b, *, name):
    s = up.shape[0]
    f = up.shape[1] // 2
    nt = f // FF_COL_TILE

    def body(uv_ref, ug_ref, wv_ref, wg_ref, bv_ref, bg_ref, a_ref):
        vc = _conv3(uv_ref[...], wv_ref[...], bv_ref[...])
        gc = _conv3(ug_ref[...], wg_ref[...], bg_ref[...])
        a_ref[...] = ((gc * _sigmoid(gc)) * vc).astype(BF16)

    def col(rows, off):
        return pl.BlockSpec((rows, FF_COL_TILE), lambda n: (0, n + off))

    return pl.pallas_call(
        body, name=name, grid=(nt,),
        in_specs=[col(s, 0), col(s, nt), col(3, 0), col(3, nt), col(1, 0), col(1, nt)],
        out_specs=col(s, 0), out_shape=_sds((s, f), BF16), compiler_params=_cparams("parallel"),
    )(up, up, conv_w, conv_w, conv_b, conv_b)


def _conv_gate_bwd(up, da, conv_w, conv_b, *, name):
    s = up.shape[0]
    f = up.shape[1] // 2
    nt = f // FF_COL_TILE

    def conv_bwd(u, w, dc):
        d0 = _shift_up(dc, 2)
        d1 = _shift_up(dc, 1)
        dup = w[0:1, :] * d0 + w[1:2, :] * d1 + w[2:3, :] * dc
        dw = jnp.concatenate([jnp.sum(u * d0, axis=0, keepdims=True), jnp.sum(u * d1, axis=0, keepdims=True),
                              jnp.sum(u * dc, axis=0, keepdims=True)], axis=0)
        return dup, dw, jnp.sum(dc, axis=0, keepdims=True)

    def body(uv_ref, ug_ref, da_ref, wv_ref, wg_ref, bv_ref, bg_ref,
             duv_ref, dug_ref, dwv_ref, dwg_ref, dbv_ref, dbg_ref):
        uv = uv_ref[...]
        ug = ug_ref[...]
        vc = _conv3(uv, wv_ref[...], bv_ref[...])
        gc = _conv3(ug, wg_ref[...], bg_ref[...])
        sg = _sigmoid(gc)
        dav = da_ref[...]
        dvc = dav * (gc * sg)
        dgc = dav * vc * (sg * (1.0 + gc * (1.0 - sg)))
        dup, dw, db = conv_bwd(uv, wv_ref[...], dvc)
        duv_ref[...] = dup.astype(BF16)
        dwv_ref[...] = dw
        dbv_ref[...] = db
        dup, dw, db = conv_bwd(ug, wg_ref[...], dgc)
        dug_ref[...] = dup.astype(BF16)
        dwg_ref[...] = dw
        dbg_ref[...] = db

    def col(rows, off):
        return pl.BlockSpec((rows, FF_COL_TILE), lambda n: (0, n + off))

    dup_v, dup_g, dw_v, dw_g, db_v, db_g = pl.pallas_call(
        body, name=name, grid=(nt,),
        in_specs=[col(s, 0), col(s, nt), col(s, 0), col(3, 0), col(3, nt), col(1, 0), col(1, nt)],
        out_specs=[col(s, 0), col(s, 0), col(3, 0), col(3, 0), col(1, 0), col(1, 0)],
        out_shape=[_sds((s, f), BF16), _sds((s, f), BF16), _sds((3, f), F32), _sds((3, f), F32),
                   _sds((1, f), F32), _sds((1, f), F32)],
        compiler_params=_cparams("parallel"),
    )(up, up, da, conv_w, conv_w, conv_b, conv_b)
    return dup_v, dup_g, jnp.concatenate([dw_v, dw_g], axis=1), jnp.concatenate([db_v, db_g], axis=1)


def _pool_counts(shape, g):
    win = jnp.left_shift(jnp.int32(2), g)
    t = lax.broadcasted_iota(jnp.int32, shape, 0)
    return win, jnp.minimum(t + 1, win).astype(F32)


def _window_sum(v, g, shift):
    for k in range(POOL_GROUPS):
        v = jnp.where(g >= k, v + shift(v, 1 << k), v)
    return v


def _pool_fwd(hf, x, w, b, scale, *, name):
    s, d = hf.shape

    def body(h_ref, x_ref, w_ref, b_ref, sc_ref, o_ref):
        g = pl.program_id(0)
        h = h_ref[...]
        _, cnt = _pool_counts(h.shape, g)
        pooled = _window_sum(h, g, _shift_down) / cnt - h
        y = jnp.dot(pooled.astype(BF16), w_ref[...], preferred_element_type=F32) + b_ref[...]
        o_ref[...] = x_ref[...] + y * sc_ref[...]

    col = pl.BlockSpec((s, POOL_DIM), lambda g: (0, g))
    vec = pl.BlockSpec((1, POOL_DIM), lambda g: (0, g))
    return pl.pallas_call(
        body, name=name, grid=(POOL_GROUPS,),
        in_specs=[col, col, pl.BlockSpec((None, POOL_DIM, POOL_DIM), lambda g: (g, 0, 0)), vec, vec],
        out_specs=col, out_shape=_sds((s, d), F32), compiler_params=_cparams("parallel"),
    )(hf, x, w, b, scale)


def _pool_bwd(hf, dm, w, b, scale, *, name):
    s, d = hf.shape

    def body(h_ref, dm_ref, w_ref, b_ref, sc_ref, dh_ref, dw_ref, db_ref, dsc_ref):
        g = pl.program_id(0)
        h = h_ref[...]
        _, cnt = _pool_counts(h.shape, g)
        pooled = (_window_sum(h, g, _shift_down) / cnt - h).astype(BF16)
        wv = w_ref[...]
        y = jnp.dot(pooled, wv, preferred_element_type=F32) + b_ref[...]
        dmv = dm_ref[...]
        dsc_ref[...] = jnp.sum(dmv * y, axis=0, keepdims=True)
        dy = dmv * sc_ref[...]
        db_ref[...] = jnp.sum(dy, axis=0, keepdims=True)
        dyb = dy.astype(BF16)
        dw_ref[...] = lax.dot_general(pooled, dyb, (TN, ((), ())), preferred_element_type=F32).astype(BF16)
        dp = lax.dot_general(dyb, wv, (NT, ((), ())), preferred_element_type=F32)
        dh_ref[...] = _window_sum(dp / cnt, g, _shift_up) - dp

    col = pl.BlockSpec((s, POOL_DIM), lambda g: (0, g))
    vec = pl.BlockSpec((1, POOL_DIM), lambda g: (0, g))
    mat = pl.BlockSpec((None, POOL_DIM, POOL_DIM), lambda g: (g, 0, 0))
    return pl.pallas_call(
        body, name=name, grid=(POOL_GROUPS,), in_specs=[col, col, mat, vec, vec], out_specs=[col, mat, vec, vec],
        out_shape=[_sds((s, d), F32), _sds((POOL_GROUPS, POOL_DIM, POOL_DIM), BF16), _sds((1, d), F32),
                   _sds((1, d), F32)],
        compiler_params=_cparams("parallel"),
    )(hf, dm, w, b, scale)


ATT_TQ = 256
ATT_TK = 256


def _qk_norm_fwd(qkv, gains, *, name):
    s = qkv.shape[0]

    def body(x_ref, g_ref, o_ref):
        xv = x_ref[...]
        r = lax.rsqrt(jnp.mean(xv * xv, axis=-1, keepdims=True) + RMS_EPS)
        o_ref[...] = ((xv * r) * g_ref[...]).astype(BF16)

    blk = pl.BlockSpec((s, HEAD_DIM), lambda hd: (0, hd))
    return pl.pallas_call(
        body, name=name, grid=(2 * HEADS,),
        in_specs=[blk, pl.BlockSpec((None, 1, HEAD_DIM), lambda hd: (hd // HEADS, 0, 0))],
        out_specs=blk, out_shape=_sds((s, 2 * HEADS * HEAD_DIM), BF16), compiler_params=_cparams("parallel"),
    )(qkv, gains)


def _qk_norm_bwd(qkv, gains, dn, *, which, name):
    s = qkv.shape[0]

    def body(x_ref, g_ref, dn_ref, dx_ref, dg_ref):
        xv = x_ref[...]
        r = lax.rsqrt(jnp.mean(xv * xv, axis=-1, keepdims=True) + RMS_EPS)
        xn = xv * r
        dnv = dn_ref[...]
        dxn = dnv * g_ref[...]
        dx_ref[...] = (r * (dxn - xn * jnp.mean(dxn * xn, axis=-1, keepdims=True))).astype(BF16)
        part = jnp.sum(dnv * xn, axis=0, keepdims=True)

        @pl.when(pl.program_id(0) == 0)
        def _():
            dg_ref[...] = part

        @pl.when(pl.program_id(0) > 0)
        def _():
            dg_ref[...] += part

    blk = pl.BlockSpec((s, HEAD_DIM), lambda hd: (0, hd))
    return pl.pallas_call(
        body, name=name, grid=(HEADS,),
        in_specs=[pl.BlockSpec((s, HEAD_DIM), lambda hd: (0, hd + which * HEADS)),
                  pl.BlockSpec((None, 1, HEAD_DIM), lambda hd: (which, 0, 0)), blk],
        out_specs=[blk, pl.BlockSpec((1, HEAD_DIM), lambda hd: (0, 0))],
        out_shape=[_sds((s, HEADS * HEAD_DIM), BF16), _sds((1, HEAD_DIM), F32)],
        compiler_params=_cparams("arbitrary"),
    )(qkv, gains, dn)


def _split_dot(v, tri):
    hi = v.astype(BF16)
    lo = (v - hi.astype(F32)).astype(BF16)
    return (jnp.dot(hi, tri, preferred_element_type=F32) + jnp.dot(lo, tri, preferred_element_type=F32))


def _causal_mask(qi, j):
    tpos = qi * ATT_TQ + lax.broadcasted_iota(jnp.int32, (ATT_TQ, ATT_TK), 0)
    spos = j * ATT_TK + lax.broadcasted_iota(jnp.int32, (ATT_TQ, ATT_TK), 1)
    return spos < tpos


def _att_tile(q, kj, qi, j):
    z = lax.dot_general(q, kj, (NT, ((), ())), preferred_element_type=F32) * (1.0 / math.sqrt(HEAD_DIM))
    mask = _causal_mask(qi, j)
    lb = jnp.minimum(z, 0.0) - jnp.log1p(jnp.exp(-jnp.abs(z)))
    l1m = jnp.where(mask, lb - z, 0.0)
    return lb, l1m, mask


def _tri(rel):
    r = lax.broadcasted_iota(jnp.int32, (ATT_TK, ATT_TK), 0)
    c = lax.broadcasted_iota(jnp.int32, (ATT_TK, ATT_TK), 1)
    return jnp.where(rel(r, c), 1.0, 0.0).astype(BF16)


def _sb_fwd(qkn, vb, *, name):
    s = vb.shape[0]

    def body(q_ref, k_ref, v_ref, o_ref):
        qi = pl.program_id(1)
        q = q_ref[...]
        after = _tri(lambda r, c: r > c)

        def step(t, carry):
            acc, run = carry
            j = qi - t
            rows = pl.ds(pl.multiple_of(j * ATT_TK, ATT_TK), ATT_TK)
            lb, l1m, mask = _att_tile(q, k_ref[rows, :], qi, j)
            remain = _split_dot(l1m, after) + run
            attn = jnp.where(mask, jnp.exp(lb + remain), 0.0)
            acc = acc + jnp.dot(attn.astype(BF16), v_ref[rows, :], preferred_element_type=F32)
            return acc, run + jnp.sum(l1m, axis=1, keepdims=True)

        acc, _ = lax.fori_loop(0, qi + 1, step, (jnp.zeros((ATT_TQ, HEAD_DIM), F32), jnp.zeros((ATT_TQ, 1), F32)))
        o_ref[...] = acc.astype(BF16)

    return pl.pallas_call(
        body, name=name, grid=(HEADS, s // ATT_TQ),
        in_specs=[pl.BlockSpec((ATT_TQ, HEAD_DIM), lambda hd, i: (i, hd)),
                  pl.BlockSpec((s, HEAD_DIM), lambda hd, i: (0, hd + HEADS)),
                  pl.BlockSpec((s, HEAD_DIM), lambda hd, i: (0, hd))],
        out_specs=pl.BlockSpec((ATT_TQ, HEAD_DIM), lambda hd, i: (i, hd)),
        out_shape=_sds((s, HEADS * HEAD_DIM), BF16), compiler_params=_cparams("parallel", "parallel"),
    )(qkn, qkn, vb)


def _sb_bwd(qkn, vb, dob, *, name):
    s = vb.shape[0]
    nkb = s // ATT_TK

    def body(q_ref, k_ref, v_ref, do_ref, dq_ref, dk_ref, dv_ref, a_buf, sig_buf):
        qi = pl.program_id(1)
        q = q_ref[...]
        do = do_ref[...]
        after = _tri(lambda r, c: r > c)
        before = _tri(lambda r, c: r < c)

        @pl.when(qi == 0)
        def _():
            dk_ref[...] = jnp.zeros_like(dk_ref)
            dv_ref[...] = jnp.zeros_like(dv_ref)

        def down(t, run):
            j = qi - t
            rows = pl.ds(pl.multiple_of(j * ATT_TK, ATT_TK), ATT_TK)
            lb, l1m, mask = _att_tile(q, k_ref[rows, :], qi, j)
            remain = _split_dot(l1m, after) + run
            a_buf[j] = jnp.where(mask, jnp.exp(lb + remain), 0.0)
            sig_buf[j] = jnp.exp(lb)
            return run + jnp.sum(l1m, axis=1, keepdims=True)

        lax.fori_loop(0, qi + 1, down, jnp.zeros((ATT_TQ, 1), F32))

        def up(j, carry):
            dq, run = carry
            rows = pl.ds(pl.multiple_of(j * ATT_TK, ATT_TK), ATT_TK)
            a = a_buf[j]
            sig = sig_buf[j]
            mask = _causal_mask(qi, j)
            da = lax.dot_general(do, v_ref[rows, :], (NT, ((), ())), preferred_element_type=F32)
            p = a * da
            c = _split_dot(p, before) + run
            dz = jnp.where(mask, p * (1.0 - sig) - c * sig, 0.0) * (1.0 / math.sqrt(HEAD_DIM))
            dzb = dz.astype(BF16)
            dq = dq + jnp.dot(dzb, k_ref[rows, :], preferred_element_type=F32)
            dk_ref[rows, :] += lax.dot_general(dzb, q, (TN, ((), ())), preferred_element_type=F32)
            dv_ref[rows, :] += lax.dot_general(a.astype(BF16), do, (TN, ((), ())), preferred_element_type=F32)
            return dq, run + jnp.sum(p, axis=1, keepdims=True)

        dq, _ = lax.fori_loop(0, qi + 1, up, (jnp.zeros((ATT_TQ, HEAD_DIM), F32), jnp.zeros((ATT_TQ, 1), F32)))
        dq_ref[...] = dq

    qblk = pl.BlockSpec((ATT_TQ, HEAD_DIM), lambda hd, i: (i, hd))
    full = pl.BlockSpec((s, HEAD_DIM), lambda hd, i: (0, hd))
    return pl.pallas_call(
        body, name=name, grid=(HEADS, s // ATT_TQ),
        in_specs=[qblk, pl.BlockSpec((s, HEAD_DIM), lambda hd, i: (0, hd + HEADS)), full, qblk],
        out_specs=[qblk, full, full],
        out_shape=[_sds((s, HEADS * HEAD_DIM), F32)] * 3,
        scratch_shapes=[pltpu.VMEM((nkb, ATT_TQ, ATT_TK), F32), pltpu.VMEM((nkb, ATT_TQ, ATT_TK), F32)],
        compiler_params=_cparams("parallel", "arbitrary"),
    )(qkn, qkn, vb, dob)


def _ssm_discretize(lam_re, lam_im, log_step, bt_re, bt_im):
    step = jnp.exp(log_step)
    mag = jnp.exp(lam_re * step)
    lb_re = mag * jnp.cos(lam_im * step)
    lb_im = mag * jnp.sin(lam_im * step)
    den = lam_re * lam_re + lam_im * lam_im
    f_re = ((lb_re - 1.0) * lam_re + lb_im * lam_im) / den
    f_im = (lb_im * lam_re - (lb_re - 1.0) * lam_im) / den
    return lb_re, lb_im, f_re * bt_re - f_im * bt_im, f_re * bt_im + f_im * bt_re


_SSM_LAM = (SSM_GROUPS, 1, SSM_STATE)
_SSM_STEP = (SSM_GROUPS, 1, 1)
_SSM_BT = (SSM_GROUPS, SSM_CH, SSM_STATE)


def _ssm_prep_fwd(lam_re, lam_im, log_step, bt_re, bt_im, *, name):
    def body(lr, li, ls, br, bi, o_ar, o_ai, o_br, o_bi):
        o_ar[...], o_ai[...], o_br[...], o_bi[...] = _ssm_discretize(lr[...], li[...], ls[...], br[...], bi[...])

    return pl.pallas_call(
        body, name=name, out_shape=[_sds(_SSM_LAM, F32), _sds(_SSM_LAM, F32), _sds(_SSM_BT, F32), _sds(_SSM_BT, F32)],
    )(lam_re, lam_im, log_step, bt_re, bt_im)


def _ssm_prep_bwd(lam_re, lam_im, log_step, bt_re, bt_im, d_ar, d_ai, d_br, d_bi, *, name):
    def body(lr, li, ls, br, bi, g_ar, g_ai, g_br, g_bi, o_lr, o_li, o_ls, o_br, o_bi):
        _, vjp = jax.vjp(_ssm_discretize, lr[...], li[...], ls[...], br[...], bi[...])
        o_lr[...], o_li[...], o_ls[...], o_br[...], o_bi[...] = vjp((g_ar[...], g_ai[...], g_br[...], g_bi[...]))

    return pl.pallas_call(
        body, name=name,
        out_shape=[_sds(_SSM_LAM, F32), _sds(_SSM_LAM, F32), _sds(_SSM_STEP, F32), _sds(_SSM_BT, F32), _sds(_SSM_BT, F32)],
    )(lam_re, lam_im, log_step, bt_re, bt_im, d_ar, d_ai, d_br, d_bi)


def _bd_masks():
    rowg = lax.broadcasted_iota(jnp.int32, (LANE, LANE), 0) // SSM_CH
    low = lax.broadcasted_iota(jnp.int32, (LANE, LANE), 1) < SSM_STATE
    return rowg, low


def _bd_expand(w):
    rowg, low = _bd_masks()
    high = jnp.logical_not(low)
    wr = pltpu.roll(w, SSM_STATE, 1)
    re = [jnp.where((rowg == 2 * k) & low, w, 0.0) + jnp.where((rowg == 2 * k + 1) & high, wr, 0.0) for k in range(4)]
    im = [jnp.where((rowg == 2 * k) & low, wr, 0.0) + jnp.where((rowg == 2 * k + 1) & high, w, 0.0) for k in range(4)]
    return jnp.concatenate(re + im, axis=1)


def _bd_extract(dbd):
    rowg, low = _bd_masks()
    high = jnp.logical_not(low)
    acc = jnp.zeros((LANE, LANE), F32)
    for k in range(4):
        c = dbd[:, LANE * k:LANE * (k + 1)]
        acc = acc + jnp.where((rowg == 2 * k) & low, c, 0.0) + jnp.where((rowg == 2 * k + 1) & low, pltpu.roll(c, SSM_STATE, 1), 0.0)
        c = dbd[:, LANE * (4 + k):LANE * (5 + k)]
        acc = acc + jnp.where((rowg == 2 * k) & high, pltpu.roll(c, SSM_STATE, 1), 0.0) + jnp.where((rowg == 2 * k + 1) & high, c, 0.0)
    return acc


def _cmul(ar, ai, br, bi):
    return ar * br - ai * bi, ar * bi + ai * br


def _scan_rows(xr, xi, ar, ai, *, reverse):
    n = xr.shape[0] // SUBLANE
    lanes = xr.shape[1]
    row = lax.broadcasted_iota(jnp.int32, (SUBLANE, lanes), 0)
    powers = [(ar, ai)]
    for _ in range(SUBLANE - 1):
        powers.append(_cmul(*powers[-1], ar, ai))
    pr = jnp.zeros((SUBLANE, lanes), F32)
    pi = jnp.zeros((SUBLANE, lanes), F32)
    for r in range(SUBLANE):
        e = (SUBLANE - 1 - r) if reverse else r
        pr = jnp.where(row == r, powers[e][0], pr)
        pi = jnp.where(row == r, powers[e][1], pi)

    def shift(v, d):
        if reverse:
            return jnp.where(row < SUBLANE - d, pltpu.roll(v, SUBLANE - d, 0), 0.0)
        return jnp.where(row >= d, pltpu.roll(v, d, 0), 0.0)

    def body(i, carry):
        cr, ci = carry
        g = (n - 1 - i) if reverse else i
        rows = pl.ds(pl.multiple_of(g * SUBLANE, SUBLANE), SUBLANE)
        br = xr[rows, :]
        bi = xi[rows, :]
        for d in (1, 2, 4):
            qr, qi = powers[d - 1]
            sr = shift(br, d)
            si = shift(bi, d)
            br, bi = br + qr * sr - qi * si, bi + qr * si + qi * sr
        br, bi = br + pr * cr - pi * ci, bi + pr * ci + pi * cr
        xr[rows, :] = br
        xi[rows, :] = bi
        edge = 0 if reverse else SUBLANE - 1
        return br[edge:edge + 1, :], bi[edge:edge + 1, :]

    zero = jnp.zeros((1, lanes), F32)
    lax.fori_loop(0, n, body, (zero, zero), unroll=2)


_GELU_C = math.sqrt(2.0 / math.pi)
_GELU_A = 0.044715


def _gelu(v):
    return 0.5 * v * (1.0 + jnp.tanh(_GELU_C * (v + _GELU_A * v * v * v)))


def _gelu_grad(v):
    t = jnp.tanh(_GELU_C * (v + _GELU_A * v * v * v))
    return 0.5 * (1.0 + t) + 0.5 * v * (1.0 - t * t) * (_GELU_C * (1.0 + 3.0 * _GELU_A * v * v))


def _ssm_states(u_b16, eb, ar, ai, xr, xi):
    nl = SSM_BLOCK_LANES
    xr[...] = jnp.dot(u_b16, eb[:, :nl], preferred_element_type=F32)
    xi[...] = jnp.dot(u_b16, eb[:, nl:], preferred_element_type=F32)
    _scan_rows(xr, xi, ar, ai, reverse=False)


def _ssm_specs(s):
    col = pl.BlockSpec((s, LANE), lambda b: (0, b))
    wsm = pl.BlockSpec((LANE, LANE), lambda b: (b, 0))
    lam = pl.BlockSpec((1, SSM_BLOCK_LANES), lambda b: (0, b))
    vec = pl.BlockSpec((1, LANE), lambda b: (0, b))
    return col, wsm, lam, vec


def _ssm_core_fwd(u, wb, wc, a_re, a_im, d_row, *, name):
    s, d = u.shape
    nl = SSM_BLOCK_LANES

    def body(u_ref, wb_ref, wc_ref, ar_ref, ai_ref, d_ref, y_ref, yg_ref, xr, xi):
        uv = u_ref[...]
        eb = _bd_expand(wb_ref[...]).astype(BF16)
        ec = _bd_expand(wc_ref[...]).astype(BF16)
        _ssm_states(uv.astype(BF16), eb, ar_ref[...], ai_ref[...], xr, xi)
        y = (lax.dot_general(xr[...].astype(BF16), ec[:, :nl], (NT, ((), ())), preferred_element_type=F32)
             + lax.dot_general(xi[...].astype(BF16), ec[:, nl:], (NT, ((), ())), preferred_element_type=F32)
             + d_ref[...] * uv)
        y_ref[...] = y
        yg_ref[...] = _gelu(y).astype(BF16)

    col, wsm, lam, vec = _ssm_specs(s)
    return pl.pallas_call(
        body, name=name, grid=(d // LANE,), in_specs=[col, wsm, wsm, lam, lam, vec], out_specs=[col, col],
        out_shape=[_sds((s, d), F32), _sds((s, d), BF16)],
        scratch_shapes=[pltpu.VMEM((s, nl), F32), pltpu.VMEM((s, nl), F32)],
        compiler_params=_cparams("parallel"),
    )(u, wb, wc, a_re, a_im, d_row)


def _ssm_core_bwd(u, ylin, dyg, wb, wc, a_re, a_im, d_row, *, name):
    s, d = u.shape
    nl = SSM_BLOCK_LANES
    n8 = s // SUBLANE

    def body(u_ref, y_ref, dyg_ref, wb_ref, wc_ref, ar_ref, ai_ref, d_ref,
             du_ref, dwb_ref, dwc_ref, dar_ref, dai_ref, dd_ref, xr, xi, gr, gi):
        uv = u_ref[...]
        ub = uv.astype(BF16)
        ar = ar_ref[...]
        ai = ai_ref[...]
        dy = dyg_ref[...] * _gelu_grad(y_ref[...])
        dd_ref[...] = jnp.sum(dy * uv, axis=0, keepdims=True)
        dyb = dy.astype(BF16)
        eb = _bd_expand(wb_ref[...]).astype(BF16)
        ec = _bd_expand(wc_ref[...]).astype(BF16)
        _ssm_states(ub, eb, ar, ai, xr, xi)
        dec = jnp.concatenate(
            [lax.dot_general(dyb, xr[...].astype(BF16), (TN, ((), ())), preferred_element_type=F32),
             lax.dot_general(dyb, xi[...].astype(BF16), (TN, ((), ())), preferred_element_type=F32)], axis=1)
        dwc_ref[...] = _bd_extract(dec)
        gr[...] = jnp.dot(dyb, ec[:, :nl], preferred_element_type=F32)
        gi[...] = jnp.dot(dyb, ec[:, nl:], preferred_element_type=F32)
        _scan_rows(gr, gi, ar, -ai, reverse=True)

        row = lax.broadcasted_iota(jnp.int32, (SUBLANE, nl), 0)

        def lam_grad(i, acc):
            acc_r, acc_i = acc
            rows = pl.ds(pl.multiple_of(i * SUBLANE, SUBLANE), SUBLANE)
            prev = pl.ds(pl.multiple_of(jnp.maximum(i - 1, 0) * SUBLANE, SUBLANE), SUBLANE)
            keep = jnp.where(i > 0, 1.0, 0.0)
            xpr = jnp.where(row == 0, pltpu.roll(xr[prev, :], 1, 0) * keep, pltpu.roll(xr[rows, :], 1, 0))
            xpi = jnp.where(row == 0, pltpu.roll(xi[prev, :], 1, 0) * keep, pltpu.roll(xi[rows, :], 1, 0))
            g_r = gr[rows, :]
            g_i = gi[rows, :]
            return acc_r + g_r * xpr + g_i * xpi, acc_i + g_i * xpr - g_r * xpi

        zero = jnp.zeros((SUBLANE, nl), F32)
        acc_r, acc_i = lax.fori_loop(0, n8, lam_grad, (zero, zero), unroll=2)
        dar_ref[...] = jnp.sum(acc_r, axis=0, keepdims=True)
        dai_ref[...] = jnp.sum(acc_i, axis=0, keepdims=True)

        grb = gr[...].astype(BF16)
        gib = gi[...].astype(BF16)
        deb = jnp.concatenate([lax.dot_general(ub, grb, (TN, ((), ())), preferred_element_type=F32),
                               lax.dot_general(ub, gib, (TN, ((), ())), preferred_element_type=F32)], axis=1)
        dwb_ref[...] = _bd_extract(deb)
        du_ref[...] = (lax.dot_general(grb, eb[:, :nl], (NT, ((), ())), preferred_element_type=F32)
                       + lax.dot_general(gib, eb[:, nl:], (NT, ((), ())), preferred_element_type=F32)
                       + d_ref[...] * dy)

    col, wsm, lam, vec = _ssm_specs(s)
    return pl.pallas_call(
        body, name=name, grid=(d // LANE,), in_specs=[col, col, col, wsm, wsm, lam, lam, vec],
        out_specs=[col, wsm, wsm, lam, lam, vec],
        out_shape=[_sds((s, d), F32), _sds((d, LANE), F32), _sds((d, LANE), F32),
                   _sds((1, SSM_GROUPS * SSM_STATE), F32), _sds((1, SSM_GROUPS * SSM_STATE), F32), _sds((1, d), F32)],
        scratch_shapes=[pltpu.VMEM((s, nl), F32)] * 4,
        compiler_params=_cparams("parallel"),
    )(u, ylin, dyg, wb, wc, a_re, a_im, d_row)


GLU_PIECE = 512


def _glu_fwd(yg, wp, b_row, x, *, tm, name):
    s, d = yg.shape
    tm = min(tm, s)
    half = N_DEV // 2

    def body(y_ref, wv_ref, wg_ref, bv_ref, bg_ref, x_ref, o_ref, val_ref, gate_ref):
        yv = y_ref[...]
        val = jnp.dot(yv, wv_ref[...], preferred_element_type=F32) + bv_ref[...]
        gate = jnp.dot(yv, wg_ref[...], preferred_element_type=F32) + bg_ref[...]
        val_ref[...] = val
        gate_ref[...] = gate
        o_ref[...] = x_ref[...] + val * _sigmoid(gate)

    blk = pl.BlockSpec((tm, GLU_PIECE), lambda m, n: (m, n))
    return pl.pallas_call(
        body, name=name, grid=(s // tm, half),
        in_specs=[pl.BlockSpec((tm, d), lambda m, n: (m, 0)),
                  pl.BlockSpec((None, d, GLU_PIECE), lambda m, n: (n, 0, 0)),
                  pl.BlockSpec((None, d, GLU_PIECE), lambda m, n: (n + half, 0, 0)),
                  pl.BlockSpec((1, GLU_PIECE), lambda m, n: (0, n)),
                  pl.BlockSpec((1, GLU_PIECE), lambda m, n: (0, n + half)), blk],
        out_specs=[blk, blk, blk], out_shape=[_sds((s, d), F32)] * 3,
        compiler_params=_cparams("parallel", "parallel"),
    )(yg, wp, wp, b_row, b_row, x)


def _glu_bwd(dout, val, gate, *, name):
    s, d = dout.shape

    def body(do_ref, val_ref, gate_ref, dgv_ref, db_ref):
        sg = _sigmoid(gate_ref[...])
        dov = do_ref[...]
        dgv = jnp.concatenate([dov * sg, dov * val_ref[...] * (sg * (1.0 - sg))], axis=1)
        dgv_ref[...] = dgv.astype(BF16)
        part = jnp.sum(dgv, axis=0, keepdims=True)

        @pl.when(pl.program_id(0) == 0)
        def _():
            db_ref[...] = part

        @pl.when(pl.program_id(0) > 0)
        def _():
            db_ref[...] += part

    row = pl.BlockSpec((ROW_TILE, d), lambda i: (i, 0))
    return pl.pallas_call(
        body, name=name, grid=(s // ROW_TILE,), in_specs=[row, row, row],
        out_specs=[pl.BlockSpec((ROW_TILE, 2 * d), lambda i: (i, 0)), pl.BlockSpec((1, 2 * d), lambda i: (0, 0))],
        out_shape=[_sds((s, 2 * d), BF16), _sds((1, 2 * d), F32)], compiler_params=_cparams("arbitrary"),
    )(dout, val, gate)


def _loss_head(y, target, *, name):
    s, d = y.shape

    def body(y_ref, t_ref, dy_ref, l_ref):
        e = y_ref[...] - t_ref[...]
        dy_ref[...] = e * (1.0 / d)
        part = jnp.zeros((SUBLANE, LANE), F32) + jnp.sum(e * e) * (0.5 / d)

        @pl.when(pl.program_id(0) == 0)
        def _():
            l_ref[...] = part

        @pl.when(pl.program_id(0) > 0)
        def _():
            l_ref[...] += part

    row = pl.BlockSpec((ROW_TILE, d), lambda i: (i, 0))
    return pl.pallas_call(
        body, name=name, grid=(s // ROW_TILE,), in_specs=[row, row],
        out_specs=[row, pl.BlockSpec((SUBLANE, LANE), lambda i: (0, 0))],
        out_shape=[_sds((s, d), F32), _sds((SUBLANE, LANE), F32)], compiler_params=_cparams("arbitrary"),
    )(y, target)


def _adamw_math(w, g, m, v):
    m = ADAM_B1 * m + (1.0 - ADAM_B1) * g
    v = ADAM_B2 * v + (1.0 - ADAM_B2) * (g * g)
    m_hat = m / (1.0 - ADAM_B1 ** ADAM_STEP)
    v_hat = v / (1.0 - ADAM_B2 ** ADAM_STEP)
    return -ADAM_LR * (m_hat / (jnp.sqrt(v_hat) + ADAM_EPS) + ADAM_WD * w), m, v


ADAM_ROWS = 64
PACK_ROWS = 64


def _sum_pieces(p_ref):
    g = p_ref[0].astype(F32)
    for k in range(1, N_DEV):
        g = g + p_ref[k].astype(F32)
    return g


def _adamw_pieces(w, pieces, m, v, *, name):
    r, c = w.shape

    def body(w_ref, p_ref, m_ref, v_ref, g_ref, d_ref, nm_ref, nv_ref):
        g = _sum_pieces(p_ref)
        g_ref[...] = g
        d_ref[...], nm_ref[...], nv_ref[...] = _adamw_math(w_ref[...], g, m_ref[...], v_ref[...])

    blk = pl.BlockSpec((ADAM_ROWS, c), lambda i: (i, 0))
    return pl.pallas_call(
        body, name=name, grid=(r // ADAM_ROWS,),
        in_specs=[blk, pl.BlockSpec((N_DEV, ADAM_ROWS, c), lambda i: (0, i, 0)), blk, blk],
        out_specs=[blk] * 4, out_shape=[_sds((r, c), F32)] * 4, compiler_params=_cparams("parallel"),
    )(w, pieces, m, v)


def _sum_parts(parts, *, name):
    _, r, c = parts.shape

    def body(p_ref, o_ref):
        o_ref[...] = _sum_pieces(p_ref)

    return pl.pallas_call(
        body, name=name, grid=(r // PACK_ROWS,),
        in_specs=[pl.BlockSpec((N_DEV, PACK_ROWS, c), lambda i: (0, i, 0))],
        out_specs=pl.BlockSpec((PACK_ROWS, c), lambda i: (i, 0)), out_shape=_sds((r, c), F32),
        compiler_params=_cparams("parallel"),
    )(parts)


def _adamw_flat(w, g, m, v, *, name):
    r, c = w.shape

    def body(w_ref, g_ref, m_ref, v_ref, d_ref, nm_ref, nv_ref):
        d_ref[...], nm_ref[...], nv_ref[...] = _adamw_math(w_ref[...], g_ref[...], m_ref[...], v_ref[...])

    blk = pl.BlockSpec((PACK_ROWS, c), lambda i: (i, 0))
    return pl.pallas_call(
        body, name=name, grid=(r // PACK_ROWS,), in_specs=[blk] * 4, out_specs=[blk] * 3,
        out_shape=[_sds((r, c), F32)] * 3, compiler_params=_cparams("parallel"),
    )(w, g, m, v)


_ANY = pl.BlockSpec(memory_space=pl.ANY)


def _place():
    return lax.axis_index("x"), lax.axis_index("y"), lax.axis_index("c")


def _slot(px, py, pc):
    return 4 * px + 2 * py + pc


def _all_gather(xs, *, name):
    n = len(xs)

    def body(*refs):
        ins, outs = refs[:n], refs[n:2 * n]
        send_sems, recv_sems, local_sems = refs[2 * n:]
        x, y, c = _place()
        me, sibling = (x, y, c), (x, y, 1 - c)
        chips = [(1 - x, y), (x, 1 - y), (1 - x, 1 - y)]

        def copy(i, k, block, to, src=None):
            rows = outs[i].at[_slot(*block)]
            return pltpu.make_async_remote_copy(
                src_ref=rows if src is None else src, dst_ref=rows, send_sem=send_sems.at[i, k],
                recv_sem=recv_sems.at[i, k], device_id=to, device_id_type=MESH)

        mine = [pltpu.make_async_copy(ins[i], outs[i].at[_slot(*me)], local_sems.at[i]) for i in range(n)]
        for cp in mine:
            cp.start()
        first = []
        for i in range(n):
            first.append(copy(i, 0, me, sibling, src=ins[i]))
            first += [copy(i, 1 + j, me, (*chip, c), src=ins[i]) for j, chip in enumerate(chips)]
        for cp in first:
            cp.start()
        passed = []
        for j, chip in enumerate(chips):
            for i in range(n):
                copy(i, 1 + j, (*chip, c), me).wait_recv()
                cp = copy(i, 4 + j, (*chip, c), sibling)
                cp.start()
                passed.append(cp)
        for i in range(n):
            copy(i, 0, sibling, me).wait_recv()
            for j, chip in enumerate(chips):
                copy(i, 4 + j, (*chip, 1 - c), me).wait_recv()
        for cp in first + passed:
            cp.wait_send()
        for cp in mine:
            cp.wait()

    return pl.pallas_call(
        body, name=name, in_specs=[_ANY] * n, out_specs=[_ANY] * n,
        out_shape=[_sds((N_DEV,) + a.shape, a.dtype) for a in xs],
        scratch_shapes=[pltpu.SemaphoreType.DMA((n, 7)), pltpu.SemaphoreType.DMA((n, 7)), pltpu.SemaphoreType.DMA((n,))],
    )(*xs)


def _exchange_pieces(gs, *, name):
    n = len(gs)
    flips = [(dx, dy, dc) for dx in (0, 1) for dy in (0, 1) for dc in (0, 1)][1:]

    def body(*refs):
        ins, outs = refs[:n], refs[n:2 * n]
        send_sems, recv_sems, local_sems = refs[2 * n:]
        x, y, c = _place()
        me = _slot(x, y, c)
        peers = [((1 - x) if dx else x, (1 - y) if dy else y, (1 - c) if dc else c) for dx, dy, dc in flips]

        def copy(i, k):
            return pltpu.make_async_remote_copy(
                src_ref=ins[i].at[_slot(*peers[k])], dst_ref=outs[i].at[me], send_sem=send_sems.at[i, k],
                recv_sem=recv_sems.at[i, k], device_id=peers[k], device_id_type=MESH)

        def landing(i, k):
            rows = outs[i].at[_slot(*peers[k])]
            return pltpu.make_async_remote_copy(
                src_ref=rows, dst_ref=rows, send_sem=send_sems.at[i, k], recv_sem=recv_sems.at[i, k],
                device_id=peers[k], device_id_type=MESH)

        mine = [pltpu.make_async_copy(ins[i].at[me], outs[i].at[me], local_sems.at[i]) for i in range(n)]
        for cp in mine:
            cp.start()
        sends = [copy(i, k) for i in range(n) for k in range(len(flips))]
        for cp in sends:
            cp.start()
        for i in range(n):
            for k in range(len(flips)):
                landing(i, k).wait_recv()
        for cp in sends:
            cp.wait_send()
        for cp in mine:
            cp.wait()

    return pl.pallas_call(
        body, name=name, in_specs=[_ANY] * n, out_specs=[_ANY] * n,
        out_shape=[_sds(a.shape, a.dtype) for a in gs],
        scratch_shapes=[pltpu.SemaphoreType.DMA((n, 7)), pltpu.SemaphoreType.DMA((n, 7)), pltpu.SemaphoreType.DMA((n,))],
    )(*gs)


MM_TM = 512


def _ffn_fwd(x_mid, g_row, wup_p, wdown, conv_w, conv_b, tag):
    hb, _ = _rms_fwd(x_mid, g_row, want_f32=False, name=f"ffn_norm_{tag}")
    up = _mm_nn_pieces(hb, wup_p, tm=MM_TM, name=f"ffn_up_{tag}")
    a = _conv_gate_fwd(up, conv_w, conv_b, name=f"ffn_conv_{tag}")
    x_out = _mm_nn(a, wdown, tm=1024, tn=512, name=f"ffn_down_{tag}", res=x_mid)
    return x_out, (hb, up, a)


def _ffn_bwd(dx, x_mid, g_row, wup_p, wdown, conv_w, conv_b, saved, tag):
    hb, up, a = saved
    dxb = dx.astype(BF16)
    da = _mm_nt(dxb, wdown, tm=MM_TM, tn=1408, name=f"ffn_da_{tag}")
    dwdown = _mm_tn(a, dxb, tm=512, tn=1024, name=f"ffn_dwdown_{tag}")
    dup_v, dup_g, dconv_w, dconv_b = _conv_gate_bwd(up, da, conv_w, conv_b, name=f"ffn_dconv_{tag}")
    dup = jnp.concatenate([dup_v, dup_g], axis=1)
    dh = _mm_nt_pieces(dup, wup_p, tm=MM_TM, tn=1024, name=f"ffn_dh_{tag}")
    dwup = _mm_tn_pieces(hb, dup, pieces=N_DEV, tm=MM_TM, name=f"ffn_dwup_{tag}")
    dx_mid, dg = _rms_bwd(x_mid, g_row, dh, dx, name=f"ffn_dnorm_{tag}")
    return dx_mid, dg, dwup, dwdown, dconv_w, dconv_b


def _pool_layer_fwd(x, g_row, w, b_row, sc_row, tag):
    _, hf = _rms_fwd(x, g_row, want_f32=True, name=f"pool_norm_{tag}")
    return _pool_fwd(hf, x, w, b_row, sc_row, name=f"pool_fwd_{tag}"), (hf,)


def _pool_layer_bwd(dx_mid, x, g_row, w, b_row, sc_row, saved, tag):
    (hf,) = saved
    dh, dw, db, dsc = _pool_bwd(hf, dx_mid, w, b_row, sc_row, name=f"pool_bwd_{tag}")
    dx, dg = _rms_bwd(x, g_row, dh, dx_mid, name=f"pool_dnorm_{tag}")
    return dx, dg, dw, db, dsc


def _sb_layer_fwd(x, g_row, wqkv_p, gains, wo, tag):
    hb, _ = _rms_fwd(x, g_row, want_f32=False, name=f"sb_norm_{tag}")
    qkv = _mm_nn_pieces(hb, wqkv_p, tm=MM_TM, name=f"sb_qkv_{tag}")
    qkn = _qk_norm_fwd(qkv, gains, name=f"sb_qknorm_{tag}")
    vb = qkv[:, 2 * D_MODEL:].astype(BF16)
    o = _sb_fwd(qkn, vb, name=f"sb_att_{tag}")
    x_mid = _mm_nn(o, wo, tm=MM_TM, tn=512, name=f"sb_out_{tag}", res=x)
    return x_mid, (hb, qkv, qkn, vb, o)


def _sb_layer_bwd(dx_mid, x, g_row, wqkv_p, gains, wo, saved, tag):
    hb, qkv, qkn, vb, o = saved
    dmb = dx_mid.astype(BF16)
    do = _mm_nt(dmb, wo, tm=MM_TM, tn=512, name=f"sb_do_{tag}", out_dtype=BF16)
    dwo = _mm_tn(o, dmb, tm=512, tn=1024, name=f"sb_dwo_{tag}")
    dqn, dkn, dv = _sb_bwd(qkn, vb, do, name=f"sb_datt_{tag}")
    dq, dqg = _qk_norm_bwd(qkv, gains, dqn, which=0, name=f"sb_dqnorm_{tag}")
    dk, dkg = _qk_norm_bwd(qkv, gains, dkn, which=1, name=f"sb_dknorm_{tag}")
    dqkv = jnp.concatenate([dq, dk, dv.astype(BF16)], axis=1)
    dh = _mm_nt_pieces(dqkv, wqkv_p, tm=MM_TM, tn=1024, name=f"sb_dh_{tag}")
    dwqkv = _mm_tn_pieces(hb, dqkv, pieces=N_DEV, tm=MM_TM, name=f"sb_dwqkv_{tag}")
    dx, dg = _rms_bwd(x, g_row, dh, dx_mid, name=f"sb_dnorm_{tag}")
    return dx, dg, dwqkv, dwo, dqg, dkg


def _ssm_params(lam_re, lam_im, log_step, b_re, b_im):
    g, p = SSM_GROUPS, SSM_STATE
    return (lam_re.reshape(g, 1, p), lam_im.reshape(g, 1, p), log_step.reshape(g, 1, 1),
            jnp.transpose(b_re, (0, 2, 1)), jnp.transpose(b_im, (0, 2, 1)))


def _ssm_layer_fwd(x, g_row, raw, c_re, c_im, d_row, wglu_p, bglu_row, tag):
    g, p, ch = SSM_GROUPS, SSM_STATE, SSM_CH
    _, hf = _rms_fwd(x, g_row, want_f32=True, name=f"ssm_norm_{tag}")
    ar, ai, bbr, bbi = _ssm_prep_fwd(*raw, name=f"ssm_prep_{tag}")
    wb = jnp.concatenate([bbr.reshape(g * ch, p), bbi.reshape(g * ch, p)], axis=1)
    wc = jnp.concatenate([c_re.reshape(g * ch, p), -c_im.reshape(g * ch, p)], axis=1)
    a_re, a_im = ar.reshape(1, g * p), ai.reshape(1, g * p)
    ylin, yg = _ssm_core_fwd(hf, wb, wc, a_re, a_im, d_row, name=f"ssm_core_{tag}")
    x_mid, val, gate = _glu_fwd(yg, wglu_p, bglu_row, x, tm=MM_TM, name=f"ssm_glu_{tag}")
    return x_mid, (hf, wb, wc, a_re, a_im, ylin, yg, val, gate)


def _ssm_layer_bwd(dx_mid, x, g_row, raw, d_row, wglu_p, saved, tag):
    g, p, ch = SSM_GROUPS, SSM_STATE, SSM_CH
    hf, wb, wc, a_re, a_im, ylin, yg, val, gate = saved
    dgv, dbglu = _glu_bwd(dx_mid, val, gate, name=f"ssm_dglu_{tag}")
    dyg = _mm_nt_pieces(dgv, wglu_p, tm=MM_TM, tn=1024, name=f"ssm_dyg_{tag}")
    dwglu = _mm_tn_pieces(yg, dgv, pieces=N_DEV, tm=MM_TM, name=f"ssm_dwglu_{tag}")
    du, dwb, dwc, dar, dai, dd = _ssm_core_bwd(hf, ylin, dyg, wb, wc, a_re, a_im, d_row, name=f"ssm_dcore_{tag}")
    dc_re = dwc[:, :p].reshape(g, ch, p)
    dc_im = -dwc[:, p:].reshape(g, ch, p)
    dlr, dli, dls, dbtr, dbti = _ssm_prep_bwd(
        *raw, dar.reshape(g, 1, p), dai.reshape(g, 1, p), dwb[:, :p].reshape(g, ch, p), dwb[:, p:].reshape(g, ch, p),
        name=f"ssm_dprep_{tag}")
    dx, dg = _rms_bwd(x, g_row, du, dx_mid, name=f"ssm_dnorm_{tag}")
    grads = dict(ssm_lam_re=dlr.reshape(1, g, p), ssm_lam_im=dli.reshape(1, g, p), ssm_log_step=dls.reshape(1, g),
                 ssm_b_re=jnp.transpose(dbtr, (0, 2, 1))[None], ssm_b_im=jnp.transpose(dbti, (0, 2, 1))[None],
                 ssm_c_re=dc_re[None], ssm_c_im=dc_im[None], ssm_d=dd, ssm_b_glu=dbglu)
    return dx, dg, dwglu, grads


_WEIGHTS = ["norm_mix_g", "norm_ffn_g", "pool_w", "pool_b", "pool_scale", "sb_w_qkv", "sb_q_gain", "sb_k_gain", "sb_w_o",
            "ssm_lam_re", "ssm_lam_im", "ssm_log_step", "ssm_b_re", "ssm_b_im", "ssm_c_re", "ssm_c_im", "ssm_d",
            "ssm_w_glu", "ssm_b_glu", "ffn_w_up", "ffn_conv_w", "ffn_conv_b", "ffn_w_down"]
_INPUTS = ["x"] + _WEIGHTS + ["loss_target"] + ["m_" + n for n in _WEIGHTS] + ["v_" + n for n in _WEIGHTS]
_REPLICATED = ["norm_mix_g", "norm_ffn_g", "sb_q_gain", "sb_k_gain", "ssm_lam_re", "ssm_lam_im", "ssm_log_step",
               "ssm_b_re", "ssm_b_im", "ssm_c_re", "ssm_c_im", "ffn_conv_b"]
_SMALL_SHARDED = {"pool_b": 1, "pool_scale": 1, "ssm_d": 1, "ssm_b_glu": 1, "ffn_conv_w": 2}
_BIG = ["pool_w", "sb_w_qkv", "sb_w_o", "ssm_w_glu", "ffn_w_up", "ffn_w_down"]
PACK_COLS = 512


def _pack(arrays):
    flat = jnp.concatenate([a.reshape(-1).astype(F32) for a in arrays])
    rows = -(-flat.shape[0] // (PACK_COLS * PACK_ROWS)) * PACK_ROWS
    return jnp.pad(flat, (0, rows * PACK_COLS - flat.shape[0])).reshape(rows, PACK_COLS)


def _unpack(packed, shapes, lead=()):
    flat = packed.reshape(lead + (-1,))
    out, off = [], 0
    for shp in shapes:
        n = math.prod(shp)
        out.append(flat[..., off:off + n].reshape(lead + tuple(shp)))
        off += n
    return out


def _unshard(gathered, axis):
    g = jnp.moveaxis(gathered, 0, axis)
    shp = g.shape
    return g.reshape(shp[:axis] + (shp[axis] * shp[axis + 1],) + shp[axis + 2:])


def _step(p):
    s = p["x"].shape[1]
    x = p["x"].reshape(s, D_MODEL)
    me = _slot(*_place())

    small_local = [p[n] for n in _SMALL_SHARDED]
    pool_w_l = p["pool_w"].astype(BF16).reshape(-1, POOL_DIM)
    ag = _all_gather([pool_w_l, p["sb_w_qkv"][0].astype(BF16), p["sb_w_o"][0].astype(BF16),
                      p["ssm_w_glu"][0].astype(BF16), _pack(small_local)], name="ag_mixers")
    n_pool = p["pool_w"].shape[0]
    pool_w = jnp.transpose(ag[0].reshape(N_DEV, n_pool, POOL_GROUPS, POOL_DIM // N_DEV, POOL_DIM), (1, 2, 0, 3, 4))
    pool_w = pool_w.reshape(n_pool, POOL_GROUPS, POOL_DIM, POOL_DIM)
    wqkv_p = ag[1]
    wo = ag[2].reshape(D_MODEL, D_MODEL)
    wglu_p = ag[3]
    small_full = {}
    for n, g in zip(_SMALL_SHARDED, _unpack(ag[4], [a.shape for a in small_local], lead=(N_DEV,))):
        small_full[n] = _unshard(g, _SMALL_SHARDED[n])
    wup_p, wdown = [], []
    for i in range(DEPTH):
        a, b = _all_gather([p["ffn_w_up"][i].astype(BF16), p["ffn_w_down"][i].astype(BF16)], name=f"ag_ffn_{i}")
        wup_p.append(a)
        wdown.append(b.reshape(D_FF, D_MODEL))

    gains = jnp.stack([p["sb_q_gain"][0], p["sb_k_gain"][0]])[:, None, :]
    ssm_raw = _ssm_params(p["ssm_lam_re"][0], p["ssm_lam_im"][0], p["ssm_log_step"][0], p["ssm_b_re"][0],
                          p["ssm_b_im"][0])

    def mixer_args(i):
        j = i // 3
        g_row = p["norm_mix_g"][i][None]
        if i % 3 == 0:
            return (g_row, pool_w[j], small_full["pool_b"][j][None], small_full["pool_scale"][j][None])
        if i % 3 == 1:
            return (g_row, wqkv_p, gains, wo)
        return (g_row, ssm_raw, p["ssm_c_re"][0], p["ssm_c_im"][0], small_full["ssm_d"], wglu_p, small_full["ssm_b_glu"])

    def ffn_args(i):
        return (p["norm_ffn_g"][i][None], wup_p[i], wdown[i], small_full["ffn_conv_w"][i], p["ffn_conv_b"][i][None])

    xs_in, xs_mid, saved_mix, saved_ffn = [], [], [], []
    for i in range(DEPTH):
        xs_in.append(x)
        fwd = (_pool_layer_fwd, _sb_layer_fwd, _ssm_layer_fwd)[i % 3]
        x, sv = fwd(x, *mixer_args(i), f"l{i}")
        saved_mix.append(sv)
        xs_mid.append(x)
        x, sv = _ffn_fwd(x, *ffn_args(i), f"l{i}")
        saved_ffn.append(sv)
    dx, loss_part = _loss_head(x, p["loss_target"].reshape(s, D_MODEL), name="loss_head")

    grads = {}
    dg_mix, dg_ffn = [None] * DEPTH, [None] * DEPTH
    dconv_w, dconv_b = [None] * DEPTH, [None] * DEPTH
    dpool = {"w": {}, "b": {}, "scale": {}}
    big = {}
    ffn_recv = [None] * DEPTH
    for i in reversed(range(DEPTH)):
        dx, dg_ffn[i], dwup, dwdown, dconv_w[i], dconv_b[i] = _ffn_bwd(dx, xs_mid[i], *ffn_args(i), saved_ffn[i], f"l{i}")
        ffn_recv[i] = _exchange_pieces([dwup, dwdown.reshape(N_DEV, D_FF // N_DEV, D_MODEL)], name=f"rs_ffn_{i}")
        margs = mixer_args(i)
        if i % 3 == 0:
            j = i // 3
            dx, dg_mix[i], dpool["w"][j], dpool["b"][j], dpool["scale"][j] = _pool_layer_bwd(
                dx, xs_in[i], *margs, saved_mix[i], f"l{i}")
        elif i % 3 == 1:
            dx, dg_mix[i], big["sb_w_qkv"], dwo, dqg, dkg = _sb_layer_bwd(dx, xs_in[i], *margs, saved_mix[i], f"l{i}")
            big["sb_w_o"] = dwo.reshape(N_DEV, D_MODEL // N_DEV, D_MODEL)
            grads["sb_q_gain"], grads["sb_k_gain"] = dqg, dkg
        else:
            g_row, raw, _, _, d_row, wg, _ = margs
            dx, dg_mix[i], big["ssm_w_glu"], sg = _ssm_layer_bwd(dx, xs_in[i], g_row, raw, d_row, wg, saved_mix[i], f"l{i}")
            grads.update(sg)
    grad_x = dx.reshape(1, s, D_MODEL)
    grads["norm_mix_g"] = jnp.concatenate(dg_mix, axis=0)
    grads["norm_ffn_g"] = jnp.concatenate(dg_ffn, axis=0)
    grads["ffn_conv_w"] = jnp.stack(dconv_w)
    grads["ffn_conv_b"] = jnp.concatenate(dconv_b, axis=0)
    grads["pool_b"] = jnp.concatenate([dpool["b"][j] for j in range(n_pool)], axis=0)
    grads["pool_scale"] = jnp.concatenate([dpool["scale"][j] for j in range(n_pool)], axis=0)
    dpw = jnp.stack([dpool["w"][j] for j in range(n_pool)])
    dpw = dpw.reshape(n_pool, POOL_GROUPS, N_DEV, POOL_DIM // N_DEV, POOL_DIM)
    big["pool_w"] = jnp.transpose(dpw, (2, 0, 1, 3, 4)).reshape(N_DEV, -1, POOL_DIM)

    mix_names = ["pool_w", "sb_w_qkv", "sb_w_o", "ssm_w_glu"]
    mix_recv = dict(zip(mix_names, _exchange_pieces([big[n] for n in mix_names], name="rs_mixers")))
    small_names = _REPLICATED + list(_SMALL_SHARDED)
    full_shapes = [p[n].shape for n in _REPLICATED] + [small_full[n].shape for n in _SMALL_SHARDED]
    part = _pack([grads[n].reshape(shp) for n, shp in zip(small_names, full_shapes)] + [loss_part[0]])
    (parts,) = _all_gather([part], name="ag_small_grads")
    summed = _unpack(_sum_parts(parts, name="sum_small_grads"), full_shapes + [(LANE,)])
    loss = summed[-1][0]
    small_g = {}
    for n, g in zip(small_names, summed[:-1]):
        if n in _SMALL_SHARDED:
            ax = _SMALL_SHARDED[n]
            g = lax.dynamic_slice_in_dim(g, me * p[n].shape[ax], p[n].shape[ax], axis=ax)
        small_g[n] = g

    out = {}
    local_shapes = [p[n].shape for n in small_names]
    packs = [_pack([p[pre + n] for n in small_names]) for pre in ("", "m_", "v_")]
    res = _adamw_flat(packs[0], _pack([small_g[n] for n in small_names]), packs[1], packs[2], name="adamw_small")
    for kind, packed in zip(("delta", "new_m", "new_v"), res):
        for n, a in zip(small_names, _unpack(packed, local_shapes)):
            out[kind + "_" + n] = a
    for n in small_names:
        out["grad_" + n] = small_g[n]

    def big_update(n, recv, idx=None):
        w, m, v = (p[pre + n] if idx is None else p[pre + n][idx] for pre in ("", "m_", "v_"))
        cols = recv.shape[-1]
        r = _adamw_pieces(w.reshape(-1, cols), recv, m.reshape(-1, cols), v.reshape(-1, cols),
                          name=f"adamw_{n}" + ("" if idx is None else f"_{idx}"))
        return [a.reshape(w.shape) for a in r]

    for n in mix_names:
        out["grad_" + n], out["delta_" + n], out["new_m_" + n], out["new_v_" + n] = big_update(n, mix_recv[n])
    for n, k in (("ffn_w_up", 0), ("ffn_w_down", 1)):
        per_layer = [big_update(n, ffn_recv[i][k], i) for i in range(DEPTH)]
        for q, kind in enumerate(("grad", "delta", "new_m", "new_v")):
            out[kind + "_" + n] = jnp.stack([per_layer[i][q] for i in range(DEPTH)])

    return (loss, grad_x, *[out["grad_" + n] for n in _WEIGHTS], *[out["delta_" + n] for n in _WEIGHTS],
            *[out["new_m_" + n] for n in _WEIGHTS], *[out["new_v_" + n] for n in _WEIGHTS])


def kernel(x, norm_mix_g, norm_ffn_g, pool_w, pool_b, pool_scale, sb_w_qkv, sb_q_gain, sb_k_gain, sb_w_o, ssm_lam_re, ssm_lam_im, ssm_log_step, ssm_b_re, ssm_b_im, ssm_c_re, ssm_c_im, ssm_d, ssm_w_glu, ssm_b_glu, ffn_w_up, ffn_conv_w, ffn_conv_b, ffn_w_down, loss_target, m_norm_mix_g, m_norm_ffn_g, m_pool_w, m_pool_b, m_pool_scale, m_sb_w_qkv, m_sb_q_gain, m_sb_k_gain, m_sb_w_o, m_ssm_lam_re, m_ssm_lam_im, m_ssm_log_step, m_ssm_b_re, m_ssm_b_im, m_ssm_c_re, m_ssm_c_im, m_ssm_d, m_ssm_w_glu, m_ssm_b_glu, m_ffn_w_up, m_ffn_conv_w, m_ffn_conv_b, m_ffn_w_down, v_norm_mix_g, v_norm_ffn_g, v_pool_w, v_pool_b, v_pool_scale, v_sb_w_qkv, v_sb_q_gain, v_sb_k_gain, v_sb_w_o, v_ssm_lam_re, v_ssm_lam_im, v_ssm_log_step, v_ssm_b_re, v_ssm_b_im, v_ssm_c_re, v_ssm_c_im, v_ssm_d, v_ssm_w_glu, v_ssm_b_glu, v_ffn_w_up, v_ffn_conv_w, v_ffn_conv_b, v_ffn_w_down):
    args = (x, norm_mix_g, norm_ffn_g, pool_w, pool_b, pool_scale, sb_w_qkv, sb_q_gain, sb_k_gain, sb_w_o, ssm_lam_re, ssm_lam_im, ssm_log_step, ssm_b_re, ssm_b_im, ssm_c_re, ssm_c_im, ssm_d, ssm_w_glu, ssm_b_glu, ffn_w_up, ffn_conv_w, ffn_conv_b, ffn_w_down, loss_target, m_norm_mix_g, m_norm_ffn_g, m_pool_w, m_pool_b, m_pool_scale, m_sb_w_qkv, m_sb_q_gain, m_sb_k_gain, m_sb_w_o, m_ssm_lam_re, m_ssm_lam_im, m_ssm_log_step, m_ssm_b_re, m_ssm_b_im, m_ssm_c_re, m_ssm_c_im, m_ssm_d, m_ssm_w_glu, m_ssm_b_glu, m_ffn_w_up, m_ffn_conv_w, m_ffn_conv_b, m_ffn_w_down, v_norm_mix_g, v_norm_ffn_g, v_pool_w, v_pool_b, v_pool_scale, v_sb_w_qkv, v_sb_q_gain, v_sb_k_gain, v_sb_w_o, v_ssm_lam_re, v_ssm_lam_im, v_ssm_log_step, v_ssm_b_re, v_ssm_b_im, v_ssm_c_re, v_ssm_c_im, v_ssm_d, v_ssm_w_glu, v_ssm_b_glu, v_ffn_w_up, v_ffn_conv_w, v_ffn_conv_b, v_ffn_w_down)
    return _step(dict(zip(_INPUTS, args)))
```

```python
import functools
import math

import jax
import jax.numpy as jnp
from jax import lax
from jax.experimental import pallas as pl
from jax.experimental.pallas import tpu as pltpu

F32 = jnp.float32
BF16 = jnp.bfloat16

N_DEV = 8
D_MODEL = 2048
D_FF = 5632
DEPTH = 4
POOL_GROUPS = 4
POOL_DIM = 512
HEADS = 16
HEAD_DIM = 128
SSM_GROUPS = 128
SSM_CH = 16
SSM_STATE = 64
SSM_BLOCK_GROUPS = 8
SSM_BLOCK_LANES = SSM_BLOCK_GROUPS * SSM_STATE
RMS_EPS = 1e-6
ADAM_LR = 0.001
ADAM_B1 = 0.9
ADAM_B2 = 0.999
ADAM_EPS = 1e-08
ADAM_WD = 0.01
ADAM_STEP = 10

VMEM_LIMIT_BYTES = 56 * 1024 * 1024
LANE = 128
SUBLANE = 8
MESH = pl.DeviceIdType.MESH


def _cparams(*sem):
    return pltpu.CompilerParams(dimension_semantics=tuple(sem), vmem_limit_bytes=VMEM_LIMIT_BYTES)


def _sds(shape, dtype):
    return jax.ShapeDtypeStruct(tuple(shape), dtype)


def _mm(a, b, *, dims, grid, a_spec, b_spec, o_spec, out_shape, out_dtype, name, k_axis=None, acc_shape=None,
        res=None, res_spec=None):
    nk = grid[k_axis] if k_axis is not None else 1
    has_res = res is not None

    def body(*refs):
        if has_res:
            a_ref, b_ref, r_ref, o_ref = refs[:4]
            scr = refs[4:]
        else:
            a_ref, b_ref, o_ref = refs[:3]
            r_ref = None
            scr = refs[3:]
        p = lax.dot_general(a_ref[...], b_ref[...], (dims, ((), ())), preferred_element_type=F32)
        if k_axis is None:
            if has_res:
                p = p + r_ref[...]
            o_ref[...] = p.astype(o_ref.dtype)
        else:
            acc = scr[0]
            k = pl.program_id(k_axis)

            @pl.when(k == 0)
            def _():
                acc[...] = p

            @pl.when(k > 0)
            def _():
                acc[...] += p

            @pl.when(k == nk - 1)
            def _():
                r = acc[...]
                if has_res:
                    r = r + r_ref[...]
                o_ref[...] = r.astype(o_ref.dtype)

    sem = ["parallel"] * len(grid)
    if k_axis is not None:
        sem[k_axis] = "arbitrary"
    in_specs = [a_spec, b_spec] + ([res_spec] if has_res else [])
    args = (a, b) + ((res,) if has_res else ())
    scratch = [pltpu.VMEM(acc_shape, F32)] if k_axis is not None else []
    return pl.pallas_call(
        body, name=name, grid=grid, in_specs=in_specs, out_specs=o_spec, out_shape=_sds(out_shape, out_dtype),
        scratch_shapes=scratch, compiler_params=_cparams(*sem),
    )(*args)


NN = ((1,), (0,))
NT = ((1,), (1,))
TN = ((0,), (0,))


def _mm_nn_pieces(a, wp, *, tm, name, out_dtype=F32):
    s, k = a.shape
    tm = min(tm, s)
    p, _, c = wp.shape
    return _mm(a, wp, dims=NN, grid=(s // tm, p),
               a_spec=pl.BlockSpec((tm, k), lambda m, n: (m, 0)),
               b_spec=pl.BlockSpec((None, k, c), lambda m, n: (n, 0, 0)),
               o_spec=pl.BlockSpec((tm, c), lambda m, n: (m, n)),
               out_shape=(s, p * c), out_dtype=out_dtype, name=name)


def _mm_nt_pieces(a, wp, *, tm, tn, name, out_dtype=F32):
    s = a.shape[0]
    tm = min(tm, s)
    p, n, c = wp.shape
    return _mm(a, wp, dims=NT, grid=(s // tm, n // tn, p), k_axis=2, acc_shape=(tm, tn),
               a_spec=pl.BlockSpec((tm, c), lambda m, j, k: (m, k)),
               b_spec=pl.BlockSpec((None, tn, c), lambda m, j, k: (k, j, 0)),
               o_spec=pl.BlockSpec((tm, tn), lambda m, j, k: (m, j)),
               out_shape=(s, n), out_dtype=out_dtype, name=name)


def _mm_tn_pieces(a, g, *, pieces, tm, name, out_dtype=BF16):
    s, m = a.shape
    c = g.shape[1] // pieces
    return _mm(a, g, dims=TN, grid=(pieces, m // tm),
               a_spec=pl.BlockSpec((s, tm), lambda n, i: (0, i)),
               b_spec=pl.BlockSpec((s, c), lambda n, i: (0, n)),
               o_spec=pl.BlockSpec((None, tm, c), lambda n, i: (n, i, 0)),
               out_shape=(pieces, m, c), out_dtype=out_dtype, name=name)


def _mm_nn(a, w, *, tm, tn, name, out_dtype=F32, res=None):
    s, k = a.shape
    tm = min(tm, s)
    n = w.shape[1]
    return _mm(a, w, dims=NN, grid=(s // tm, n // tn),
               a_spec=pl.BlockSpec((tm, k), lambda m, j: (m, 0)),
               b_spec=pl.BlockSpec((k, tn), lambda m, j: (0, j)),
               o_spec=pl.BlockSpec((tm, tn), lambda m, j: (m, j)),
               res=res, res_spec=pl.BlockSpec((tm, tn), lambda m, j: (m, j)),
               out_shape=(s, n), out_dtype=out_dtype, name=name)


def _mm_nt(a, w, *, tm, tn, name, out_dtype=F32):
    s, k = a.shape
    tm = min(tm, s)
    n = w.shape[0]
    return _mm(a, w, dims=NT, grid=(s // tm, n // tn),
               a_spec=pl.BlockSpec((tm, k), lambda m, j: (m, 0)),
               b_spec=pl.BlockSpec((tn, k), lambda m, j: (j, 0)),
               o_spec=pl.BlockSpec((tm, tn), lambda m, j: (m, j)),
               out_shape=(s, n), out_dtype=out_dtype, name=name)


def _mm_tn(a, g, *, tm, tn, name, out_dtype=BF16):
    s, m = a.shape
    n = g.shape[1]
    return _mm(a, g, dims=TN, grid=(m // tm, n // tn),
               a_spec=pl.BlockSpec((s, tm), lambda i, j: (0, i)),
               b_spec=pl.BlockSpec((s, tn), lambda i, j: (0, j)),
               o_spec=pl.BlockSpec((tm, tn), lambda i, j: (i, j)),
               out_shape=(m, n), out_dtype=out_dtype, name=name)


ROW_TILE = 256


def _dep_spec():
    return pl.BlockSpec((SUBLANE, LANE), lambda i: (0, 0))


def _rms_fwd(x, g_row, *, want_f32, name, dep=None):
    s, d = x.shape
    n_in = 2 if dep is None else 3

    def body(*refs):
        x_ref, g_ref = refs[:2]
        outs = refs[n_in:]
        xv = x_ref[...]
        r = lax.rsqrt(jnp.mean(xv * xv, axis=-1, keepdims=True) + RMS_EPS)
        h = (xv * r) * g_ref[...]
        outs[0][...] = h.astype(BF16)
        if want_f32:
            outs[1][...] = h

    row = pl.BlockSpec((ROW_TILE, d), lambda i: (i, 0))
    out_shape = [_sds((s, d), BF16)] + ([_sds((s, d), F32)] if want_f32 else [])
    out = pl.pallas_call(
        body, name=name, grid=(s // ROW_TILE,),
        in_specs=[row, pl.BlockSpec((1, d), lambda i: (0, 0))] + ([] if dep is None else [_dep_spec()]),
        out_specs=[row] * len(out_shape), out_shape=out_shape, compiler_params=_cparams("parallel"),
    )(x, g_row, *(() if dep is None else (dep,)))
    return out if want_f32 else (out[0], None)


def _rms_bwd(x, g_row, dh, dres, *, name, dep=None):
    s, d = x.shape

    def body(x_ref, g_ref, dh_ref, dres_ref, *rest):
        dx_ref, dg_ref = rest[-2:]
        xv = x_ref[...]
        r = lax.rsqrt(jnp.mean(xv * xv, axis=-1, keepdims=True) + RMS_EPS)
        xn = xv * r
        dhv = dh_ref[...]
        dxn = dhv * g_ref[...]
        dx_ref[...] = dres_ref[...] + r * (dxn - xn * jnp.mean(dxn * xn, axis=-1, keepdims=True))
        part = jnp.sum(dhv * xn, axis=0, keepdims=True)

        @pl.when(pl.program_id(0) == 0)
        def _():
            dg_ref[...] = part

        @pl.when(pl.program_id(0) > 0)
        def _():
            dg_ref[...] += part

    row = pl.BlockSpec((ROW_TILE, d), lambda i: (i, 0))
    vec = pl.BlockSpec((1, d), lambda i: (0, 0))
    return pl.pallas_call(
        body, name=name, grid=(s // ROW_TILE,), in_specs=[row, vec, row, row] + ([] if dep is None else [_dep_spec()]),
        out_specs=[row, vec], out_shape=[_sds((s, d), F32), _sds((1, d), F32)], compiler_params=_cparams("arbitrary"),
    )(x, g_row, dh, dres, *(() if dep is None else (dep,)))


def _shift_down(v, k):
    row = lax.broadcasted_iota(jnp.int32, v.shape, 0)
    return jnp.where(row >= k, pltpu.roll(v, k, 0), 0.0)


def _shift_up(v, k):
    n = v.shape[0]
    row = lax.broadcasted_iota(jnp.int32, v.shape, 0)
    return jnp.where(row < n - k, pltpu.roll(v, n - k, 0), 0.0)


def _sigmoid(z):
    return 1.0 / (1.0 + jnp.exp(-z))


FF_COL_TILE = 256


def _conv3(u, w, b):
    return b + w[0:1, :] * _shift_down(u, 2) + w[1:2, :] * _shift_down(u, 1) + w[2:3, :] * u


def _conv_gate_fwd(up, conv_w, conv_b, *, name):
    s = up.shape[0]
    f = up.shape[1] // 2
    nt = f // FF_COL_TILE

    def body(uv_ref, ug_ref, wv_ref, wg_ref, bv_ref, bg_ref, a_ref):
        vc = _conv3(uv_ref[...], wv_ref[...], bv_ref[...])
        gc = _conv3(ug_ref[...], wg_ref[...], bg_ref[...])
        a_ref[...] = ((gc * _sigmoid(gc)) * vc).astype(BF16)

    def col(rows, off):
        return pl.BlockSpec((rows, FF_COL_TILE), lambda n: (0, n + off))

    return pl.pallas_call(
        body, name=name, grid=(nt,),
        in_specs=[col(s, 0), col(s, nt), col(3, 0), col(3, nt), col(1, 0), col(1, nt)],
        out_specs=col(s, 0), out_shape=_sds((s, f), BF16), compiler_params=_cparams("parallel"),
    )(up, up, conv_w, conv_w, conv_b, conv_b)


def _conv_gate_bwd(up, da, conv_w, conv_b, *, name):
    s = up.shape[0]
    f = up.shape[1] // 2
    nt = f // FF_COL_TILE

    def conv_bwd(u, w, dc):
        d0 = _shift_up(dc, 2)
        d1 = _shift_up(dc, 1)
        dup = w[0:1, :] * d0 + w[1:2, :] * d1 + w[2:3, :] * dc
        dw = jnp.concatenate([jnp.sum(u * d0, axis=0, keepdims=True), jnp.sum(u * d1, axis=0, keepdims=True),
                              jnp.sum(u * dc, axis=0, keepdims=True)], axis=0)
        return dup, dw, jnp.sum(dc, axis=0, keepdims=True)

    def body(uv_ref, ug_ref, da_ref, wv_ref, wg_ref, bv_ref, bg_ref,
             duv_ref, dug_ref, dwv_ref, dwg_ref, dbv_ref, dbg_ref):
        uv = uv_ref[...]
        ug = ug_ref[...]
        vc = _conv3(uv, wv_ref[...], bv_ref[...])
        gc = _conv3(ug, wg_ref[...], bg_ref[...])
        sg = _sigmoid(gc)
        dav = da_ref[...]
        dvc = dav * (gc * sg)
        dgc = dav * vc * (sg * (1.0 + gc * (1.0 - sg)))
        dup, dw, db = conv_bwd(uv, wv_ref[...], dvc)
        duv_ref[...] = dup.astype(BF16)
        dwv_ref[...] = dw
        dbv_ref[...] = db
        dup, dw, db = conv_bwd(ug, wg_ref[...], dgc)
        dug_ref[...] = dup.astype(BF16)
        dwg_ref[...] = dw
        dbg_ref[...] = db

    def col(rows, off):
        return pl.BlockSpec((rows, FF_COL_TILE), lambda n: (0, n + off))

    dup_v, dup_g, dw_v, dw_g, db_v, db_g = pl.pallas_call(
        body, name=name, grid=(nt,),
        in_specs=[col(s, 0), col(s, nt), col(s, 0), col(3, 0), col(3, nt), col(1, 0), col(1, nt)],
        out_specs=[col(s, 0), col(s, 0), col(3, 0), col(3, 0), col(1, 0), col(1, 0)],
        out_shape=[_sds((s, f), BF16), _sds((s, f), BF16), _sds((3, f), F32), _sds((3, f), F32),
                   _sds((1, f), F32), _sds((1, f), F32)],
        compiler_params=_cparams("parallel"),
    )(up, up, da, conv_w, conv_w, conv_b, conv_b)
    return dup_v, dup_g, jnp.concatenate([dw_v, dw_g], axis=1), jnp.concatenate([db_v, db_g], axis=1)


def _pool_counts(shape, g):
    win = jnp.left_shift(jnp.int32(2), g)
    t = lax.broadcasted_iota(jnp.int32, shape, 0)
    return win, jnp.minimum(t + 1, win).astype(F32)


def _window_sum(v, g, shift):
    for k in range(POOL_GROUPS):
        v = jnp.where(g >= k, v + shift(v, 1 << k), v)
    return v


def _pool_fwd(hf, x, w, b, scale, *, name):
    s, d = hf.shape

    def body(h_ref, x_ref, w_ref, b_ref, sc_ref, o_ref):
        g = pl.program_id(0)
        h = h_ref[...]
        _, cnt = _pool_counts(h.shape, g)
        pooled = _window_sum(h, g, _shift_down) / cnt - h
        y = jnp.dot(pooled.astype(BF16), w_ref[...], preferred_element_type=F32) + b_ref[...]
        o_ref[...] = x_ref[...] + y * sc_ref[...]

    col = pl.BlockSpec((s, POOL_DIM), lambda g: (0, g))
    vec = pl.BlockSpec((1, POOL_DIM), lambda g: (0, g))
    return pl.pallas_call(
        body, name=name, grid=(POOL_GROUPS,),
        in_specs=[col, col, pl.BlockSpec((None, POOL_DIM, POOL_DIM), lambda g: (g, 0, 0)), vec, vec],
        out_specs=col, out_shape=_sds((s, d), F32), compiler_params=_cparams("parallel"),
    )(hf, x, w, b, scale)


def _pool_bwd(hf, dm, w, b, scale, *, name):
    s, d = hf.shape

    def body(h_ref, dm_ref, w_ref, b_ref, sc_ref, dh_ref, dw_ref, db_ref, dsc_ref):
        g = pl.program_id(0)
        h = h_ref[...]
        _, cnt = _pool_counts(h.shape, g)
        pooled = (_window_sum(h, g, _shift_down) / cnt - h).astype(BF16)
        wv = w_ref[...]
        y = jnp.dot(pooled, wv, preferred_element_type=F32) + b_ref[...]
        dmv = dm_ref[...]
        dsc_ref[...] = jnp.sum(dmv * y, axis=0, keepdims=True)
        dy = dmv * sc_ref[...]
        db_ref[...] = jnp.sum(dy, axis=0, keepdims=True)
        dyb = dy.astype(BF16)
        dw_ref[...] = lax.dot_general(pooled, dyb, (TN, ((), ())), preferred_element_type=F32).astype(BF16)
        dp = lax.dot_general(dyb, wv, (NT, ((), ())), preferred_element_type=F32)
        dh_ref[...] = _window_sum(dp / cnt, g, _shift_up) - dp

    col = pl.BlockSpec((s, POOL_DIM), lambda g: (0, g))
    vec = pl.BlockSpec((1, POOL_DIM), lambda g: (0, g))
    mat = pl.BlockSpec((None, POOL_DIM, POOL_DIM), lambda g: (g, 0, 0))
    return pl.pallas_call(
        body, name=name, grid=(POOL_GROUPS,), in_specs=[col, col, mat, vec, vec], out_specs=[col, mat, vec, vec],
        out_shape=[_sds((s, d), F32), _sds((POOL_GROUPS, POOL_DIM, POOL_DIM), BF16), _sds((1, d), F32),
                   _sds((1, d), F32)],
        compiler_params=_cparams("parallel"),
    )(hf, dm, w, b, scale)


ATT_TQ = 256
ATT_TK = 256


def _qk_norm_fwd(qkv, gains, *, name):
    s = qkv.shape[0]

    def body(x_ref, g_ref, o_ref):
        xv = x_ref[...]
        r = lax.rsqrt(jnp.mean(xv * xv, axis=-1, keepdims=True) + RMS_EPS)
        o_ref[...] = ((xv * r) * g_ref[...]).astype(BF16)

    blk = pl.BlockSpec((s, HEAD_DIM), lambda hd: (0, hd))
    return pl.pallas_call(
        body, name=name, grid=(2 * HEADS,),
        in_specs=[blk, pl.BlockSpec((None, 1, HEAD_DIM), lambda hd: (hd // HEADS, 0, 0))],
        out_specs=blk, out_shape=_sds((s, 2 * HEADS * HEAD_DIM), BF16), compiler_params=_cparams("parallel"),
    )(qkv, gains)


def _qk_norm_bwd(qkv, gains, dn, *, which, name):
    s = qkv.shape[0]

    def body(x_ref, g_ref, dn_ref, dx_ref, dg_ref):
        xv = x_ref[...]
        r = lax.rsqrt(jnp.mean(xv * xv, axis=-1, keepdims=True) + RMS_EPS)
        xn = xv * r
        dnv = dn_ref[...]
        dxn = dnv * g_ref[...]
        dx_ref[...] = (r * (dxn - xn * jnp.mean(dxn * xn, axis=-1, keepdims=True))).astype(BF16)
        part = jnp.sum(dnv * xn, axis=0, keepdims=True)

        @pl.when(pl.program_id(0) == 0)
        def _():
            dg_ref[...] = part

        @pl.when(pl.program_id(0) > 0)
        def _():
            dg_ref[...] += part

    blk = pl.BlockSpec((s, HEAD_DIM), lambda hd: (0, hd))
    return pl.pallas_call(
        body, name=name, grid=(HEADS,),
        in_specs=[pl.BlockSpec((s, HEAD_DIM), lambda hd: (0, hd + which * HEADS)),
                  pl.BlockSpec((None, 1, HEAD_DIM), lambda hd: (which, 0, 0)), blk],
        out_specs=[blk, pl.BlockSpec((1, HEAD_DIM), lambda hd: (0, 0))],
        out_shape=[_sds((s, HEADS * HEAD_DIM), BF16), _sds((1, HEAD_DIM), F32)],
        compiler_params=_cparams("arbitrary"),
    )(qkv, gains, dn)


def _split_dot(v, tri):
    hi = v.astype(BF16)
    lo = (v - hi.astype(F32)).astype(BF16)
    return (jnp.dot(hi, tri, preferred_element_type=F32) + jnp.dot(lo, tri, preferred_element_type=F32))


def _causal_mask(qi, j):
    tpos = qi * ATT_TQ + lax.broadcasted_iota(jnp.int32, (ATT_TQ, ATT_TK), 0)
    spos = j * ATT_TK + lax.broadcasted_iota(jnp.int32, (ATT_TQ, ATT_TK), 1)
    return spos < tpos


def _att_tile(q, kj, qi, j):
    z = lax.dot_general(q, kj, (NT, ((), ())), preferred_element_type=F32) * (1.0 / math.sqrt(HEAD_DIM))
    mask = _causal_mask(qi, j)
    lb = jnp.minimum(z, 0.0) - jnp.log1p(jnp.exp(-jnp.abs(z)))
    l1m = jnp.where(mask, lb - z, 0.0)
    return lb, l1m, mask


def _tri(rel):
    r = lax.broadcasted_iota(jnp.int32, (ATT_TK, ATT_TK), 0)
    c = lax.broadcasted_iota(jnp.int32, (ATT_TK, ATT_TK), 1)
    return jnp.where(rel(r, c), 1.0, 0.0).astype(BF16)


def _sb_fwd(qkn, vb, *, name):
    s = vb.shape[0]

    def body(q_ref, k_ref, v_ref, o_ref):
        qi = pl.program_id(1)
        q = q_ref[...]
        after = _tri(lambda r, c: r > c)

        def step(t, carry):
            acc, run = carry
            j = qi - t
            rows = pl.ds(pl.multiple_of(j * ATT_TK, ATT_TK), ATT_TK)
            lb, l1m, mask = _att_tile(q, k_ref[rows, :], qi, j)
            remain = _split_dot(l1m, after) + run
            attn = jnp.where(mask, jnp.exp(lb + remain), 0.0)
            acc = acc + jnp.dot(attn.astype(BF16), v_ref[rows, :], preferred_element_type=F32)
            return acc, run + jnp.sum(l1m, axis=1, keepdims=True)

        acc, _ = lax.fori_loop(0, qi + 1, step, (jnp.zeros((ATT_TQ, HEAD_DIM), F32), jnp.zeros((ATT_TQ, 1), F32)))
        o_ref[...] = acc.astype(BF16)

    return pl.pallas_call(
        body, name=name, grid=(HEADS, s // ATT_TQ),
        in_specs=[pl.BlockSpec((ATT_TQ, HEAD_DIM), lambda hd, i: (i, hd)),
                  pl.BlockSpec((s, HEAD_DIM), lambda hd, i: (0, hd + HEADS)),
                  pl.BlockSpec((s, HEAD_DIM), lambda hd, i: (0, hd))],
        out_specs=pl.BlockSpec((ATT_TQ, HEAD_DIM), lambda hd, i: (i, hd)),
        out_shape=_sds((s, HEADS * HEAD_DIM), BF16), compiler_params=_cparams("parallel", "parallel"),
    )(qkn, qkn, vb)


def _sb_bwd(qkn, vb, dob, *, name):
    s = vb.shape[0]
    nkb = s // ATT_TK

    def body(q_ref, k_ref, v_ref, do_ref, dq_ref, dk_ref, dv_ref, a_buf, sig_buf):
        qi = pl.program_id(1)
        q = q_ref[...]
        do = do_ref[...]
        after = _tri(lambda r, c: r > c)
        before = _tri(lambda r, c: r < c)

        @pl.when(qi == 0)
        def _():
            dk_ref[...] = jnp.zeros_like(dk_ref)
            dv_ref[...] = jnp.zeros_like(dv_ref)

        def down(t, run):
            j = qi - t
            rows = pl.ds(pl.multiple_of(j * ATT_TK, ATT_TK), ATT_TK)
            lb, l1m, mask = _att_tile(q, k_ref[rows, :], qi, j)
            remain = _split_dot(l1m, after) + run
            a_buf[j] = jnp.where(mask, jnp.exp(lb + remain), 0.0)
            sig_buf[j] = jnp.exp(lb)
            return run + jnp.sum(l1m, axis=1, keepdims=True)

        lax.fori_loop(0, qi + 1, down, jnp.zeros((ATT_TQ, 1), F32))

        def up(j, carry):
            dq, run = carry
            rows = pl.ds(pl.multiple_of(j * ATT_TK, ATT_TK), ATT_TK)
            a = a_buf[j]
            sig = sig_buf[j]
            mask = _causal_mask(qi, j)
            da = lax.dot_general(do, v_ref[rows, :], (NT, ((), ())), preferred_element_type=F32)
            p = a * da
            c = _split_dot(p, before) + run
            dz = jnp.where(mask, p * (1.0 - sig) - c * sig, 0.0) * (1.0 / math.sqrt(HEAD_DIM))
            dzb = dz.astype(BF16)
            dq = dq + jnp.dot(dzb, k_ref[rows, :], preferred_element_type=F32)
            dk_ref[rows, :] += lax.dot_general(dzb, q, (TN, ((), ())), preferred_element_type=F32)
            dv_ref[rows, :] += lax.dot_general(a.astype(BF16), do, (TN, ((), ())), preferred_element_type=F32)
            return dq, run + jnp.sum(p, axis=1, keepdims=True)

        dq, _ = lax.fori_loop(0, qi + 1, up, (jnp.zeros((ATT_TQ, HEAD_DIM), F32), jnp.zeros((ATT_TQ, 1), F32)))
        dq_ref[...] = dq

    qblk = pl.BlockSpec((ATT_TQ, HEAD_DIM), lambda hd, i: (i, hd))
    full = pl.BlockSpec((s, HEAD_DIM), lambda hd, i: (0, hd))
    return pl.pallas_call(
        body, name=name, grid=(HEADS, s // ATT_TQ),
        in_specs=[qblk, pl.BlockSpec((s, HEAD_DIM), lambda hd, i: (0, hd + HEADS)), full, qblk],
        out_specs=[qblk, full, full],
        out_shape=[_sds((s, HEADS * HEAD_DIM), F32)] * 3,
        scratch_shapes=[pltpu.VMEM((nkb, ATT_TQ, ATT_TK), F32), pltpu.VMEM((nkb, ATT_TQ, ATT_TK), F32)],
        compiler_params=_cparams("parallel", "arbitrary"),
    )(qkn, qkn, vb, dob)


def _ssm_discretize(lam_re, lam_im, log_step, bt_re, bt_im):
    step = jnp.exp(log_step)
    mag = jnp.exp(lam_re * step)
    lb_re = mag * jnp.cos(lam_im * step)
    lb_im = mag * jnp.sin(lam_im * step)
    den = lam_re * lam_re + lam_im * lam_im
    f_re = ((lb_re - 1.0) * lam_re + lb_im * lam_im) / den
    f_im = (lb_im * lam_re - (lb_re - 1.0) * lam_im) / den
    return lb_re, lb_im, f_re * bt_re - f_im * bt_im, f_re * bt_im + f_im * bt_re


_SSM_LAM = (SSM_GROUPS, 1, SSM_STATE)
_SSM_STEP = (SSM_GROUPS, 1, 1)
_SSM_BT = (SSM_GROUPS, SSM_CH, SSM_STATE)


def _ssm_prep_fwd(lam_re, lam_im, log_step, bt_re, bt_im, *, name):
    def body(lr, li, ls, br, bi, o_ar, o_ai, o_br, o_bi):
        o_ar[...], o_ai[...], o_br[...], o_bi[...] = _ssm_discretize(lr[...], li[...], ls[...], br[...], bi[...])

    return pl.pallas_call(
        body, name=name, out_shape=[_sds(_SSM_LAM, F32), _sds(_SSM_LAM, F32), _sds(_SSM_BT, F32), _sds(_SSM_BT, F32)],
    )(lam_re, lam_im, log_step, bt_re, bt_im)


def _ssm_prep_bwd(lam_re, lam_im, log_step, bt_re, bt_im, d_ar, d_ai, d_br, d_bi, *, name):
    def body(lr, li, ls, br, bi, g_ar, g_ai, g_br, g_bi, o_lr, o_li, o_ls, o_br, o_bi):
        _, vjp = jax.vjp(_ssm_discretize, lr[...], li[...], ls[...], br[...], bi[...])
        o_lr[...], o_li[...], o_ls[...], o_br[...], o_bi[...] = vjp((g_ar[...], g_ai[...], g_br[...], g_bi[...]))

    return pl.pallas_call(
        body, name=name,
        out_shape=[_sds(_SSM_LAM, F32), _sds(_SSM_LAM, F32), _sds(_SSM_STEP, F32), _sds(_SSM_BT, F32), _sds(_SSM_BT, F32)],
    )(lam_re, lam_im, log_step, bt_re, bt_im, d_ar, d_ai, d_br, d_bi)


def _bd_masks():
    rowg = lax.broadcasted_iota(jnp.int32, (LANE, LANE), 0) // SSM_CH
    low = lax.broadcasted_iota(jnp.int32, (LANE, LANE), 1) < SSM_STATE
    return rowg, low


def _bd_expand(w):
    rowg, low = _bd_masks()
    high = jnp.logical_not(low)
    wr = pltpu.roll(w, SSM_STATE, 1)
    re = [jnp.where((rowg == 2 * k) & low, w, 0.0) + jnp.where((rowg == 2 * k + 1) & high, wr, 0.0) for k in range(4)]
    im = [jnp.where((rowg == 2 * k) & low, wr, 0.0) + jnp.where((rowg == 2 * k + 1) & high, w, 0.0) for k in range(4)]
    return jnp.concatenate(re + im, axis=1)


def _bd_extract(dbd):
    rowg, low = _bd_masks()
    high = jnp.logical_not(low)
    acc = jnp.zeros((LANE, LANE), F32)
    for k in range(4):
        c = dbd[:, LANE * k:LANE * (k + 1)]
        acc = acc + jnp.where((rowg == 2 * k) & low, c, 0.0) + jnp.where((rowg == 2 * k + 1) & low, pltpu.roll(c, SSM_STATE, 1), 0.0)
        c = dbd[:, LANE * (4 + k):LANE * (5 + k)]
        acc = acc + jnp.where((rowg == 2 * k) & high, pltpu.roll(c, SSM_STATE, 1), 0.0) + jnp.where((rowg == 2 * k + 1) & high, c, 0.0)
    return acc


def _cmul(ar, ai, br, bi):
    return ar * br - ai * bi, ar * bi + ai * br


def _scan_rows(xr, xi, ar, ai, *, reverse):
    n = xr.shape[0] // SUBLANE
    lanes = xr.shape[1]
    row = lax.broadcasted_iota(jnp.int32, (SUBLANE, lanes), 0)
    powers = [(ar, ai)]
    for _ in range(SUBLANE - 1):
        powers.append(_cmul(*powers[-1], ar, ai))
    pr = jnp.zeros((SUBLANE, lanes), F32)
    pi = jnp.zeros((SUBLANE, lanes), F32)
    for r in range(SUBLANE):
        e = (SUBLANE - 1 - r) if reverse else r
        pr = jnp.where(row == r, powers[e][0], pr)
        pi = jnp.where(row == r, powers[e][1], pi)

    def shift(v, d):
        if reverse:
            return jnp.where(row < SUBLANE - d, pltpu.roll(v, SUBLANE - d, 0), 0.0)
        return jnp.where(row >= d, pltpu.roll(v, d, 0), 0.0)

    def body(i, carry):
        cr, ci = carry
        g = (n - 1 - i) if reverse else i
        rows = pl.ds(pl.multiple_of(g * SUBLANE, SUBLANE), SUBLANE)
        br = xr[rows, :]
        bi = xi[rows, :]
        for d in (1, 2, 4):
            qr, qi = powers[d - 1]
            sr = shift(br, d)
            si = shift(bi, d)
            br, bi = br + qr * sr - qi * si, bi + qr * si + qi * sr
        br, bi = br + pr * cr - pi * ci, bi + pr * ci + pi * cr
        xr[rows, :] = br
        xi[rows, :] = bi
        edge = 0 if reverse else SUBLANE - 1
        return br[edge:edge + 1, :], bi[edge:edge + 1, :]

    zero = jnp.zeros((1, lanes), F32)
    lax.fori_loop(0, n, body, (zero, zero), unroll=2)


_GELU_C = math.sqrt(2.0 / math.pi)
_GELU_A = 0.044715


def _gelu(v):
    return 0.5 * v * (1.0 + jnp.tanh(_GELU_C * (v + _GELU_A * v * v * v)))


def _gelu_grad(v):
    t = jnp.tanh(_GELU_C * (v + _GELU_A * v * v * v))
    return 0.5 * (1.0 + t) + 0.5 * v * (1.0 - t * t) * (_GELU_C * (1.0 + 3.0 * _GELU_A * v * v))


def _ssm_states(u_b16, eb, ar, ai, xr, xi):
    nl = SSM_BLOCK_LANES
    xr[...] = jnp.dot(u_b16, eb[:, :nl], preferred_element_type=F32)
    xi[...] = jnp.dot(u_b16, eb[:, nl:], preferred_element_type=F32)
    _scan_rows(xr, xi, ar, ai, reverse=False)


def _ssm_specs(s):
    col = pl.BlockSpec((s, LANE), lambda b: (0, b))
    wsm = pl.BlockSpec((LANE, LANE), lambda b: (b, 0))
    lam = pl.BlockSpec((1, SSM_BLOCK_LANES), lambda b: (0, b))
    vec = pl.BlockSpec((1, LANE), lambda b: (0, b))
    return col, wsm, lam, vec


def _ssm_core_fwd(u, wb, wc, a_re, a_im, d_row, *, name):
    s, d = u.shape
    nl = SSM_BLOCK_LANES

    def body(u_ref, wb_ref, wc_ref, ar_ref, ai_ref, d_ref, y_ref, yg_ref, xr, xi):
        uv = u_ref[...]
        eb = _bd_expand(wb_ref[...]).astype(BF16)
        ec = _bd_expand(wc_ref[...]).astype(BF16)
        _ssm_states(uv.astype(BF16), eb, ar_ref[...], ai_ref[...], xr, xi)
        y = (lax.dot_general(xr[...].astype(BF16), ec[:, :nl], (NT, ((), ())), preferred_element_type=F32)
             + lax.dot_general(xi[...].astype(BF16), ec[:, nl:], (NT, ((), ())), preferred_element_type=F32)
             + d_ref[...] * uv)
        y_ref[...] = y
        yg_ref[...] = _gelu(y).astype(BF16)

    col, wsm, lam, vec = _ssm_specs(s)
    return pl.pallas_call(
        body, name=name, grid=(d // LANE,), in_specs=[col, wsm, wsm, lam, lam, vec], out_specs=[col, col],
        out_shape=[_sds((s, d), F32), _sds((s, d), BF16)],
        scratch_shapes=[pltpu.VMEM((s, nl), F32), pltpu.VMEM((s, nl), F32)],
        compiler_params=_cparams("parallel"),
    )(u, wb, wc, a_re, a_im, d_row)


def _ssm_core_bwd(u, ylin, dyg, wb, wc, a_re, a_im, d_row, *, name):
    s, d = u.shape
    nl = SSM_BLOCK_LANES
    n8 = s // SUBLANE

    def body(u_ref, y_ref, dyg_ref, wb_ref, wc_ref, ar_ref, ai_ref, d_ref,
             du_ref, dwb_ref, dwc_ref, dar_ref, dai_ref, dd_ref, xr, xi, gr, gi):
        uv = u_ref[...]
        ub = uv.astype(BF16)
        ar = ar_ref[...]
        ai = ai_ref[...]
        dy = dyg_ref[...] * _gelu_grad(y_ref[...])
        dd_ref[...] = jnp.sum(dy * uv, axis=0, keepdims=True)
        dyb = dy.astype(BF16)
        eb = _bd_expand(wb_ref[...]).astype(BF16)
        ec = _bd_expand(wc_ref[...]).astype(BF16)
        _ssm_states(ub, eb, ar, ai, xr, xi)
        dec = jnp.concatenate(
            [lax.dot_general(dyb, xr[...].astype(BF16), (TN, ((), ())), preferred_element_type=F32),
             lax.dot_general(dyb, xi[...].astype(BF16), (TN, ((), ())), preferred_element_type=F32)], axis=1)
        dwc_ref[...] = _bd_extract(dec)
        gr[...] = jnp.dot(dyb, ec[:, :nl], preferred_element_type=F32)
        gi[...] = jnp.dot(dyb, ec[:, nl:], preferred_element_type=F32)
        _scan_rows(gr, gi, ar, -ai, reverse=True)

        row = lax.broadcasted_iota(jnp.int32, (SUBLANE, nl), 0)

        def lam_grad(i, acc):
            acc_r, acc_i = acc
            rows = pl.ds(pl.multiple_of(i * SUBLANE, SUBLANE), SUBLANE)
            prev = pl.ds(pl.multiple_of(jnp.maximum(i - 1, 0) * SUBLANE, SUBLANE), SUBLANE)
            keep = jnp.where(i > 0, 1.0, 0.0)
            xpr = jnp.where(row == 0, pltpu.roll(xr[prev, :], 1, 0) * keep, pltpu.roll(xr[rows, :], 1, 0))
            xpi = jnp.where(row == 0, pltpu.roll(xi[prev, :], 1, 0) * keep, pltpu.roll(xi[rows, :], 1, 0))
            g_r = gr[rows, :]
            g_i = gi[rows, :]
            return acc_r + g_r * xpr + g_i * xpi, acc_i + g_i * xpr - g_r * xpi

        zero = jnp.zeros((SUBLANE, nl), F32)
        acc_r, acc_i = lax.fori_loop(0, n8, lam_grad, (zero, zero), unroll=2)
        dar_ref[...] = jnp.sum(acc_r, axis=0, keepdims=True)
        dai_ref[...] = jnp.sum(acc_i, axis=0, keepdims=True)

        grb = gr[...].astype(BF16)
        gib = gi[...].astype(BF16)
        deb = jnp.concatenate([lax.dot_general(ub, grb, (TN, ((), ())), preferred_element_type=F32),
                               lax.dot_general(ub, gib, (TN, ((), ())), preferred_element_type=F32)], axis=1)
        dwb_ref[...] = _bd_extract(deb)
        du_ref[...] = (lax.dot_general(grb, eb[:, :nl], (NT, ((), ())), preferred_element_type=F32)
                       + lax.dot_general(gib, eb[:, nl:], (NT, ((), ())), preferred_element_type=F32)
                       + d_ref[...] * dy)

    col, wsm, lam, vec = _ssm_specs(s)
    return pl.pallas_call(
        body, name=name, grid=(d // LANE,), in_specs=[col, col, col, wsm, wsm, lam, lam, vec],
        out_specs=[col, wsm, wsm, lam, lam, vec],
        out_shape=[_sds((s, d), F32), _sds((d, LANE), F32), _sds((d, LANE), F32),
                   _sds((1, SSM_GROUPS * SSM_STATE), F32), _sds((1, SSM_GROUPS * SSM_STATE), F32), _sds((1, d), F32)],
        scratch_shapes=[pltpu.VMEM((s, nl), F32)] * 4,
        compiler_params=_cparams("parallel"),
    )(u, ylin, dyg, wb, wc, a_re, a_im, d_row)


GLU_PIECE = 512


def _glu_fwd(yg, wp, b_row, x, *, tm, name):
    s, d = yg.shape
    tm = min(tm, s)
    half = N_DEV // 2

    def body(y_ref, wv_ref, wg_ref, bv_ref, bg_ref, x_ref, o_ref, val_ref, gate_ref):
        yv = y_ref[...]
        val = jnp.dot(yv, wv_ref[...], preferred_element_type=F32) + bv_ref[...]
        gate = jnp.dot(yv, wg_ref[...], preferred_element_type=F32) + bg_ref[...]
        val_ref[...] = val
        gate_ref[...] = gate
        o_ref[...] = x_ref[...] + val * _sigmoid(gate)

    blk = pl.BlockSpec((tm, GLU_PIECE), lambda m, n: (m, n))
    return pl.pallas_call(
        body, name=name, grid=(s // tm, half),
        in_specs=[pl.BlockSpec((tm, d), lambda m, n: (m, 0)),
                  pl.BlockSpec((None, d, GLU_PIECE), lambda m, n: (n, 0, 0)),
                  pl.BlockSpec((None, d, GLU_PIECE), lambda m, n: (n + half, 0, 0)),
                  pl.BlockSpec((1, GLU_PIECE), lambda m, n: (0, n)),
                  pl.BlockSpec((1, GLU_PIECE), lambda m, n: (0, n + half)), blk],
        out_specs=[blk, blk, blk], out_shape=[_sds((s, d), F32)] * 3,
        compiler_params=_cparams("parallel", "parallel"),
    )(yg, wp, wp, b_row, b_row, x)


def _glu_bwd(dout, val, gate, *, name):
    s, d = dout.shape

    def body(do_ref, val_ref, gate_ref, dgv_ref, db_ref):
        sg = _sigmoid(gate_ref[...])
        dov = do_ref[...]
        dgv = jnp.concatenate([dov * sg, dov * val_ref[...] * (sg * (1.0 - sg))], axis=1)
        dgv_ref[...] = dgv.astype(BF16)
        part = jnp.sum(dgv, axis=0, keepdims=True)

        @pl.when(pl.program_id(0) == 0)
        def _():
            db_ref[...] = part

        @pl.when(pl.program_id(0) > 0)
        def _():
            db_ref[...] += part

    row = pl.BlockSpec((ROW_TILE, d), lambda i: (i, 0))
    return pl.pallas_call(
        body, name=name, grid=(s // ROW_TILE,), in_specs=[row, row, row],
        out_specs=[pl.BlockSpec((ROW_TILE, 2 * d), lambda i: (i, 0)), pl.BlockSpec((1, 2 * d), lambda i: (0, 0))],
        out_shape=[_sds((s, 2 * d), BF16), _sds((1, 2 * d), F32)], compiler_params=_cparams("arbitrary"),
    )(dout, val, gate)


def _loss_head(y, target, *, name):
    s, d = y.shape

    def body(y_ref, t_ref, dy_ref, l_ref):
        e = y_ref[...] - t_ref[...]
        dy_ref[...] = e * (1.0 / d)
        part = jnp.zeros((SUBLANE, LANE), F32) + jnp.sum(e * e) * (0.5 / d)

        @pl.when(pl.program_id(0) == 0)
        def _():
            l_ref[...] = part

        @pl.when(pl.program_id(0) > 0)
        def _():
            l_ref[...] += part

    row = pl.BlockSpec((ROW_TILE, d), lambda i: (i, 0))
    return pl.pallas_call(
        body, name=name, grid=(s // ROW_TILE,), in_specs=[row, row],
        out_specs=[row, pl.BlockSpec((SUBLANE, LANE), lambda i: (0, 0))],
        out_shape=[_sds((s, d), F32), _sds((SUBLANE, LANE), F32)], compiler_params=_cparams("arbitrary"),
    )(y, target)


def _adamw_math(w, g, m, v):
    m = ADAM_B1 * m + (1.0 - ADAM_B1) * g
    v = ADAM_B2 * v + (1.0 - ADAM_B2) * (g * g)
    m_hat = m / (1.0 - ADAM_B1 ** ADAM_STEP)
    v_hat = v / (1.0 - ADAM_B2 ** ADAM_STEP)
    return -ADAM_LR * (m_hat / (jnp.sqrt(v_hat) + ADAM_EPS) + ADAM_WD * w), m, v


ADAM_ROWS = 64
PACK_ROWS = 64


def _sum_pieces(p_ref):
    g = p_ref[0].astype(F32)
    for k in range(1, N_DEV):
        g = g + p_ref[k].astype(F32)
    return g


def _adamw_pieces(w, pieces, m, v, *, name):
    r, c = w.shape

    def body(w_ref, p_ref, m_ref, v_ref, g_ref, d_ref, nm_ref, nv_ref):
        g = _sum_pieces(p_ref)
        g_ref[...] = g
        d_ref[...], nm_ref[...], nv_ref[...] = _adamw_math(w_ref[...], g, m_ref[...], v_ref[...])

    blk = pl.BlockSpec((ADAM_ROWS, c), lambda i: (i, 0))
    return pl.pallas_call(
        body, name=name, grid=(r // ADAM_ROWS,),
        in_specs=[blk, pl.BlockSpec((N_DEV, ADAM_ROWS, c), lambda i: (0, i, 0)), blk, blk],
        out_specs=[blk] * 4, out_shape=[_sds((r, c), F32)] * 4, compiler_params=_cparams("parallel"),
    )(w, pieces, m, v)


def _sum_parts(parts, *, name):
    _, r, c = parts.shape

    def body(p_ref, o_ref):
        o_ref[...] = _sum_pieces(p_ref)

    return pl.pallas_call(
        body, name=name, grid=(r // PACK_ROWS,),
        in_specs=[pl.BlockSpec((N_DEV, PACK_ROWS, c), lambda i: (0, i, 0))],
        out_specs=pl.BlockSpec((PACK_ROWS, c), lambda i: (i, 0)), out_shape=_sds((r, c), F32),
        compiler_params=_cparams("parallel"),
    )(parts)


def _adamw_flat(w, g, m, v, *, name):
    r, c = w.shape

    def body(w_ref, g_ref, m_ref, v_ref, d_ref, nm_ref, nv_ref):
        d_ref[...], nm_ref[...], nv_ref[...] = _adamw_math(w_ref[...], g_ref[...], m_ref[...], v_ref[...])

    blk = pl.BlockSpec((PACK_ROWS, c), lambda i: (i, 0))
    return pl.pallas_call(
        body, name=name, grid=(r // PACK_ROWS,), in_specs=[blk] * 4, out_specs=[blk] * 3,
        out_shape=[_sds((r, c), F32)] * 3, compiler_params=_cparams("parallel"),
    )(w, g, m, v)


_ANY = pl.BlockSpec(memory_space=pl.ANY)


def _place():
    return lax.axis_index("x"), lax.axis_index("y"), lax.axis_index("c")


def _slot(px, py, pc):
    return 4 * px + 2 * py + pc


def _all_gather(xs, *, name):
    n = len(xs)

    def body(*refs):
        ins, outs = refs[:n], refs[n:2 * n]
        send_sems, recv_sems, local_sems = refs[2 * n:]
        x, y, c = _place()
        me, sibling = (x, y, c), (x, y, 1 - c)
        chips = [(1 - x, y), (x, 1 - y), (1 - x, 1 - y)]

        def copy(i, k, block, to, src=None):
            rows = outs[i].at[_slot(*block)]
            return pltpu.make_async_remote_copy(
                src_ref=rows if src is None else src, dst_ref=rows, send_sem=send_sems.at[i, k],
                recv_sem=recv_sems.at[i, k], device_id=to, device_id_type=MESH)

        mine = [pltpu.make_async_copy(ins[i], outs[i].at[_slot(*me)], local_sems.at[i]) for i in range(n)]
        for cp in mine:
            cp.start()
        first = []
        for i in range(n):
            first.append(copy(i, 0, me, sibling, src=ins[i]))
            first += [copy(i, 1 + j, me, (*chip, c), src=ins[i]) for j, chip in enumerate(chips)]
        for cp in first:
            cp.start()
        passed = []
        for j, chip in enumerate(chips):
            for i in range(n):
                copy(i, 1 + j, (*chip, c), me).wait_recv()
                cp = copy(i, 4 + j, (*chip, c), sibling)
                cp.start()
                passed.append(cp)
        for i in range(n):
            copy(i, 0, sibling, me).wait_recv()
            for j, chip in enumerate(chips):
                copy(i, 4 + j, (*chip, 1 - c), me).wait_recv()
        for cp in first + passed:
            cp.wait_send()
        for cp in mine:
            cp.wait()

    return pl.pallas_call(
        body, name=name, in_specs=[_ANY] * n, out_specs=[_ANY] * n,
        out_shape=[_sds((N_DEV,) + a.shape, a.dtype) for a in xs],
        scratch_shapes=[pltpu.SemaphoreType.DMA((n, 7)), pltpu.SemaphoreType.DMA((n, 7)), pltpu.SemaphoreType.DMA((n,))],
    )(*xs)


def _exchange_pieces(gs, *, name):
    n = len(gs)
    flips = [(dx, dy, dc) for dx in (0, 1) for dy in (0, 1) for dc in (0, 1)][1:]

    def body(*refs):
        ins, outs = refs[:n], refs[n:2 * n]
        send_sems, recv_sems, local_sems = refs[2 * n:]
        x, y, c = _place()
        me = _slot(x, y, c)
        peers = [((1 - x) if dx else x, (1 - y) if dy else y, (1 - c) if dc else c) for dx, dy, dc in flips]

        def copy(i, k):
            return pltpu.make_async_remote_copy(
                src_ref=ins[i].at[_slot(*peers[k])], dst_ref=outs[i].at[me], send_sem=send_sems.at[i, k],
                recv_sem=recv_sems.at[i, k], device_id=peers[k], device_id_type=MESH)

        def landing(i, k):
            rows = outs[i].at[_slot(*peers[k])]
            return pltpu.make_async_remote_copy(
                src_ref=rows, dst_ref=rows, send_sem=send_sems.at[i, k], recv_sem=recv_sems.at[i, k],
                device_id=peers[k], device_id_type=MESH)

        mine = [pltpu.make_async_copy(ins[i].at[me], outs[i].at[me], local_sems.at[i]) for i in range(n)]
        for cp in mine:
            cp.start()
        sends = [copy(i, k) for i in range(n) for k in range(len(flips))]
        for cp in sends:
            cp.start()
        for i in range(n):
            for k in range(len(flips)):
                landing(i, k).wait_recv()
        for cp in sends:
            cp.wait_send()
        for cp in mine:
            cp.wait()

    return pl.pallas_call(
        body, name=name, in_specs=[_ANY] * n, out_specs=[_ANY] * n,
        out_shape=[_sds(a.shape, a.dtype) for a in gs],
        scratch_shapes=[pltpu.SemaphoreType.DMA((n, 7)), pltpu.SemaphoreType.DMA((n, 7)), pltpu.SemaphoreType.DMA((n,))],
    )(*gs)


_HBM = pl.BlockSpec(memory_space=pltpu.HBM)
_SEM = pl.BlockSpec(memory_space=pltpu.SEMAPHORE)
_EFFECT = pltpu.SideEffectType.DATAFLOW_SIDE_EFFECTING
_CHIP_FLIPS = ((1, 0), (0, 1), (1, 1))
_TOKEN = (SUBLANE, LANE)


def _flip(v, f):
    return (1 - v) if f else v


def _in_hbm(a):
    return pltpu.with_memory_space_constraint(a, pltpu.HBM)


def _ag_peers(x, y, c):
    return [(x, y, 1 - c)] + [(_flip(x, dx), _flip(y, dy), c) for dx, dy in _CHIP_FLIPS]


def _ag_copies(x_refs, land_refs, send_sems, recv_sems, *, landing):
    x, y, c = _place()
    peers = _ag_peers(x, y, c)
    cps = []
    for i in range(len(x_refs)):
        for k, peer in enumerate(peers):
            origin = _slot(*peer) if landing else _slot(x, y, c)
            cps.append(pltpu.make_async_remote_copy(
                src_ref=x_refs[i], dst_ref=land_refs[i].at[origin], send_sem=send_sems.at[4 * i + k],
                recv_sem=recv_sems.at[4 * i + k], device_id=peer, device_id_type=MESH))
    return cps


def _ag_start(xs, *, name, dep=None):
    n = len(xs)
    lands = [lax.empty((N_DEV,) + a.shape, a.dtype) for a in xs]
    n_in = 2 * n + (0 if dep is None else 1)

    def body(*refs):
        x_refs, land_refs = refs[:n], refs[n:2 * n]
        send_sems, recv_sems = refs[n_in], refs[n_in + 1]
        token = refs[-1]
        for cp in _ag_copies(x_refs, land_refs, send_sems, recv_sems, landing=False):
            cp.start()
        token[...] = jnp.zeros_like(token)

    out = pl.pallas_call(
        body, name=name,
        out_shape=(pltpu.SemaphoreType.DMA((4 * n,)), pltpu.SemaphoreType.DMA((4 * n,)),
                   *[pltpu.HBM(a.shape, a.dtype) for a in xs], *[pltpu.HBM(a.shape, a.dtype) for a in lands],
                   _sds(_TOKEN, F32)),
        in_specs=[_HBM] * (2 * n) + ([] if dep is None else [_ANY]),
        out_specs=(_SEM, _SEM, *[_HBM] * (2 * n), pl.BlockSpec(memory_space=pltpu.VMEM)),
        input_output_aliases={i: 2 + i for i in range(2 * n)},
        compiler_params=pltpu.CompilerParams(has_side_effects=_EFFECT),
    )(*[_in_hbm(a) for a in xs], *[_in_hbm(a) for a in lands], *(() if dep is None else (dep,)))
    return (out[0], out[1], list(out[2:2 + n]), list(out[2 + n:2 + 2 * n])), out[-1]


def _ag_wait(started, after, *, name):
    send_sems, recv_sems, xs, lands = started
    n = len(xs)

    def body(*refs):
        x_refs, land_refs = refs[:n], refs[n:2 * n]
        for cp in _ag_copies(x_refs, land_refs, refs[2 * n], refs[2 * n + 1], landing=True):
            cp.wait_send()
            cp.wait_recv()

    out = pl.pallas_call(
        body, name=name,
        out_shape=tuple(pltpu.HBM(a.shape, a.dtype) for a in xs + lands),
        in_specs=[_HBM] * (2 * n) + [_SEM, _SEM, _ANY], out_specs=tuple([_HBM] * (2 * n)),
        input_output_aliases={i: i for i in range(2 * n)},
        compiler_params=pltpu.CompilerParams(has_side_effects=_EFFECT),
    )(*xs, *lands, send_sems, recv_sems, after)
    return list(out[:n]), list(out[n:])


def _ag_forward(xs, lands, *, name):
    n = len(xs)

    def body(*refs):
        x_refs, land = refs[:n], refs[2 * n:3 * n]
        send_sems, recv_sems, local_sems = refs[3 * n:]
        x, y, c = _place()
        sibling = (x, y, 1 - c)
        mine = [pltpu.make_async_copy(x_refs[i], land[i].at[_slot(x, y, c)], local_sems.at[i]) for i in range(n)]
        for cp in mine:
            cp.start()

        def copy(i, j, core):
            dx, dy = _CHIP_FLIPS[j]
            rows = land[i].at[_slot(_flip(x, dx), _flip(y, dy), core)]
            return pltpu.make_async_remote_copy(src_ref=rows, dst_ref=rows, send_sem=send_sems.at[i, j],
                                                recv_sem=recv_sems.at[i, j], device_id=sibling, device_id_type=MESH)

        sends = [copy(i, j, c) for i in range(n) for j in range(3)]
        for cp in sends:
            cp.start()
        for i in range(n):
            for j in range(3):
                copy(i, j, 1 - c).wait_recv()
        for cp in sends:
            cp.wait_send()
        for cp in mine:
            cp.wait()

    return pl.pallas_call(
        body, name=name, in_specs=[_ANY] * (2 * n), out_specs=[_ANY] * n,
        out_shape=[_sds(a.shape, a.dtype) for a in lands], input_output_aliases={n + i: i for i in range(n)},
        scratch_shapes=[pltpu.SemaphoreType.DMA((n, 3)), pltpu.SemaphoreType.DMA((n, 3)), pltpu.SemaphoreType.DMA((n,))],
    )(*xs, *lands)


def _rs_sibling(g4s, *, name):
    n = len(g4s)

    def body(*refs):
        ins, outs = refs[:n], refs[n:2 * n]
        send_sems, recv_sems = refs[2 * n:]
        x, y, c = _place()
        cps = [pltpu.make_async_remote_copy(
            src_ref=ins[i].at[:, 1 - c], dst_ref=outs[i], send_sem=send_sems.at[i], recv_sem=recv_sems.at[i],
            device_id=(x, y, 1 - c), device_id_type=MESH) for i in range(n)]
        for cp in cps:
            cp.start()
        for cp in cps:
            cp.wait_recv()
        for cp in cps:
            cp.wait_send()

    return pl.pallas_call(
        body, name=name, in_specs=[_ANY] * n, out_specs=[_ANY] * n,
        out_shape=[_sds((4,) + a.shape[2:], a.dtype) for a in g4s],
        scratch_shapes=[pltpu.SemaphoreType.DMA((n,)), pltpu.SemaphoreType.DMA((n,))],
    )(*g4s)


def _rs_rows(r):
    return 256 if r % 256 == 0 else 64


def _rs_pair_sum(g4, from_sibling, core, *, name):
    _, _, r, c = g4.shape
    tr = _rs_rows(r)

    def body(core_ref, g_ref, a_ref, o_ref):
        o_ref[...] = (g_ref[...].astype(F32) + a_ref[...].astype(F32)).astype(BF16)

    blk = pl.BlockSpec((None, tr, c), lambda k, i, core_ref: (k, i, 0))
    return pl.pallas_call(
        body, name=name,
        grid_spec=pltpu.PrefetchScalarGridSpec(
            num_scalar_prefetch=1, grid=(4, r // tr),
            in_specs=[pl.BlockSpec((None, None, tr, c), lambda k, i, core_ref: (k, core_ref[0], i, 0)), blk],
            out_specs=blk),
        out_shape=_sds((4, r, c), BF16), compiler_params=_cparams("parallel", "parallel"),
    )(core, g4, from_sibling)


def _rs_copies(h_refs, land_refs, send_sems, recv_sems):
    x, y, c = _place()
    cps = []
    for i in range(len(h_refs)):
        for k, (dx, dy) in enumerate(_CHIP_FLIPS):
            px, py = _flip(x, dx), _flip(y, dy)
            cps.append(pltpu.make_async_remote_copy(
                src_ref=h_refs[i].at[2 * px + py], dst_ref=land_refs[i].at[k], send_sem=send_sems.at[3 * i + k],
                recv_sem=recv_sems.at[3 * i + k], device_id=(px, py, c), device_id_type=MESH))
    return cps


def _rs_start(hs, *, name):
    n = len(hs)
    lands = [lax.empty((3,) + a.shape[1:], a.dtype) for a in hs]

    def body(*refs):
        h_refs, land_refs = refs[:n], refs[n:2 * n]
        token = refs[-1]
        for cp in _rs_copies(h_refs, land_refs, refs[2 * n], refs[2 * n + 1]):
            cp.start()
        token[...] = jnp.zeros_like(token)

    out = pl.pallas_call(
        body, name=name,
        out_shape=(pltpu.SemaphoreType.DMA((3 * n,)), pltpu.SemaphoreType.DMA((3 * n,)),
                   *[pltpu.HBM(a.shape, a.dtype) for a in hs], *[pltpu.HBM(a.shape, a.dtype) for a in lands],
                   _sds(_TOKEN, F32)),
        in_specs=[_HBM] * (2 * n),
        out_specs=(_SEM, _SEM, *[_HBM] * (2 * n), pl.BlockSpec(memory_space=pltpu.VMEM)),
        input_output_aliases={i: 2 + i for i in range(2 * n)},
        compiler_params=pltpu.CompilerParams(has_side_effects=_EFFECT),
    )(*[_in_hbm(a) for a in hs], *[_in_hbm(a) for a in lands])
    return (out[0], out[1], list(out[2:2 + n]), list(out[2 + n:2 + 2 * n])), out[-1]


def _rs_wait(started, after, *, name):
    send_sems, recv_sems, hs, lands = started
    n = len(hs)

    def body(*refs):
        for cp in _rs_copies(refs[:n], refs[n:2 * n], refs[2 * n], refs[2 * n + 1]):
            cp.wait_send()
            cp.wait_recv()

    out = pl.pallas_call(
        body, name=name,
        out_shape=tuple(pltpu.HBM(a.shape, a.dtype) for a in hs + lands),
        in_specs=[_HBM] * (2 * n) + [_SEM, _SEM, _ANY], out_specs=tuple([_HBM] * (2 * n)),
        input_output_aliases={i: i for i in range(2 * n)},
        compiler_params=pltpu.CompilerParams(has_side_effects=_EFFECT),
    )(*hs, *lands, send_sems, recv_sems, after)
    return list(out[:n]), list(out[n:])


def _adamw_chips(w, h, others, m, v, chip, *, name):
    r, c = w.shape

    def body(chip_ref, w_ref, h_ref, o_ref, m_ref, v_ref, g_ref, d_ref, nm_ref, nv_ref):
        g = h_ref[...].astype(F32)
        for k in range(3):
            g = g + o_ref[k].astype(F32)
        g_ref[...] = g
        d_ref[...], nm_ref[...], nv_ref[...] = _adamw_math(w_ref[...], g, m_ref[...], v_ref[...])

    blk = pl.BlockSpec((ADAM_ROWS, c), lambda i, chip_ref: (i, 0))
    return pl.pallas_call(
        body, name=name,
        grid_spec=pltpu.PrefetchScalarGridSpec(
            num_scalar_prefetch=1, grid=(r // ADAM_ROWS,),
            in_specs=[blk, pl.BlockSpec((None, ADAM_ROWS, c), lambda i, chip_ref: (chip_ref[0], i, 0)),
                      pl.BlockSpec((3, ADAM_ROWS, c), lambda i, chip_ref: (0, i, 0)), blk, blk],
            out_specs=[blk] * 4),
        out_shape=[_sds((r, c), F32)] * 4, compiler_params=_cparams("parallel"),
    )(chip, w, h, others, m, v)


def _gather_begin(xs, tag, dep=None):
    return _ag_start(xs, name=f"ag_start_{tag}", dep=dep)


def _gather_end(started, after, tag):
    xs, lands = _ag_wait(started, after, name=f"ag_wait_{tag}")
    return _ag_forward(xs, lands, name=f"ag_forward_{tag}")


def _scatter_begin(gs, tag):
    core = lax.axis_index("c").astype(jnp.int32).reshape(1)
    g4s = [g.reshape((4, 2) + g.shape[1:]) for g in gs]
    got = _rs_sibling(g4s, name=f"rs_sibling_{tag}")
    hs = [_rs_pair_sum(g4, a, core, name=f"rs_pair_sum_{tag}_{i}") for i, (g4, a) in enumerate(zip(g4s, got))]
    return _rs_start(hs, name=f"rs_start_{tag}")


def _scatter_end(started, after, tag):
    return _rs_wait(started, after, name=f"rs_wait_{tag}")


MM_TM = 512


def _ffn_fwd(x_mid, g_row, wup_p, wdown, conv_w, conv_b, tag, dep=None):
    hb, _ = _rms_fwd(x_mid, g_row, want_f32=False, name=f"ffn_norm_{tag}", dep=dep)
    up = _mm_nn_pieces(hb, wup_p, tm=MM_TM, name=f"ffn_up_{tag}")
    a = _conv_gate_fwd(up, conv_w, conv_b, name=f"ffn_conv_{tag}")
    x_out = _mm_nn(a, wdown, tm=1024, tn=512, name=f"ffn_down_{tag}", res=x_mid)
    return x_out, (hb, up, a)


def _ffn_bwd(dx, x_mid, g_row, wup_p, wdown, conv_w, conv_b, saved, tag, scatter):
    hb, up, a = saved
    dxb = dx.astype(BF16)
    da = _mm_nt(dxb, wdown, tm=MM_TM, tn=1408, name=f"ffn_da_{tag}")
    dwdown = _mm_tn(a, dxb, tm=512, tn=1024, name=f"ffn_dwdown_{tag}")
    dup_v, dup_g, dconv_w, dconv_b = _conv_gate_bwd(up, da, conv_w, conv_b, name=f"ffn_dconv_{tag}")
    dup = jnp.concatenate([dup_v, dup_g], axis=1)
    dh = _mm_nt_pieces(dup, wup_p, tm=MM_TM, tn=1024, name=f"ffn_dh_{tag}")
    dwup = _mm_tn_pieces(hb, dup, pieces=N_DEV, tm=MM_TM, name=f"ffn_dwup_{tag}")
    started, token = scatter([dwup, dwdown.reshape(N_DEV, D_FF // N_DEV, D_MODEL)], f"ffn_{tag}")
    dx_mid, dg = _rms_bwd(x_mid, g_row, dh, dx, name=f"ffn_dnorm_{tag}", dep=token)
    return dx_mid, dg, dconv_w, dconv_b, started


def _pool_layer_fwd(x, g_row, w, b_row, sc_row, tag, dep=None):
    _, hf = _rms_fwd(x, g_row, want_f32=True, name=f"pool_norm_{tag}", dep=dep)
    return _pool_fwd(hf, x, w, b_row, sc_row, name=f"pool_fwd_{tag}"), (hf,)


def _pool_layer_bwd(dx_mid, x, g_row, w, b_row, sc_row, saved, tag):
    (hf,) = saved
    dh, dw, db, dsc = _pool_bwd(hf, dx_mid, w, b_row, sc_row, name=f"pool_bwd_{tag}")
    dx, dg = _rms_bwd(x, g_row, dh, dx_mid, name=f"pool_dnorm_{tag}")
    return dx, dg, dw, db, dsc


def _sb_layer_fwd(x, g_row, wqkv_p, gains, wo, tag):
    hb, _ = _rms_fwd(x, g_row, want_f32=False, name=f"sb_norm_{tag}")
    qkv = _mm_nn_pieces(hb, wqkv_p, tm=MM_TM, name=f"sb_qkv_{tag}")
    qkn = _qk_norm_fwd(qkv, gains, name=f"sb_qknorm_{tag}")
    vb = qkv[:, 2 * D_MODEL:].astype(BF16)
    o = _sb_fwd(qkn, vb, name=f"sb_att_{tag}")
    x_mid = _mm_nn(o, wo, tm=MM_TM, tn=512, name=f"sb_out_{tag}", res=x)
    return x_mid, (hb, qkv, qkn, vb, o)


def _sb_layer_bwd(dx_mid, x, g_row, wqkv_p, gains, wo, saved, tag, scatter):
    hb, qkv, qkn, vb, o = saved
    dmb = dx_mid.astype(BF16)
    do = _mm_nt(dmb, wo, tm=MM_TM, tn=512, name=f"sb_do_{tag}", out_dtype=BF16)
    dwo = _mm_tn(o, dmb, tm=512, tn=1024, name=f"sb_dwo_{tag}")
    dqn, dkn, dv = _sb_bwd(qkn, vb, do, name=f"sb_datt_{tag}")
    dq, dqg = _qk_norm_bwd(qkv, gains, dqn, which=0, name=f"sb_dqnorm_{tag}")
    dk, dkg = _qk_norm_bwd(qkv, gains, dkn, which=1, name=f"sb_dknorm_{tag}")
    dqkv = jnp.concatenate([dq, dk, dv.astype(BF16)], axis=1)
    dh = _mm_nt_pieces(dqkv, wqkv_p, tm=MM_TM, tn=1024, name=f"sb_dh_{tag}")
    dwqkv = _mm_tn_pieces(hb, dqkv, pieces=N_DEV, tm=MM_TM, name=f"sb_dwqkv_{tag}")
    started, token = scatter([dwqkv, dwo.reshape(N_DEV, D_MODEL // N_DEV, D_MODEL)], f"sb_{tag}")
    dx, dg = _rms_bwd(x, g_row, dh, dx_mid, name=f"sb_dnorm_{tag}", dep=token)
    return dx, dg, dqg, dkg, started


def _ssm_params(lam_re, lam_im, log_step, b_re, b_im):
    g, p = SSM_GROUPS, SSM_STATE
    return (lam_re.reshape(g, 1, p), lam_im.reshape(g, 1, p), log_step.reshape(g, 1, 1),
            jnp.transpose(b_re, (0, 2, 1)), jnp.transpose(b_im, (0, 2, 1)))


def _ssm_layer_fwd(x, g_row, raw, c_re, c_im, d_row, wglu_p, bglu_row, tag):
    g, p, ch = SSM_GROUPS, SSM_STATE, SSM_CH
    _, hf = _rms_fwd(x, g_row, want_f32=True, name=f"ssm_norm_{tag}")
    ar, ai, bbr, bbi = _ssm_prep_fwd(*raw, name=f"ssm_prep_{tag}")
    wb = jnp.concatenate([bbr.reshape(g * ch, p), bbi.reshape(g * ch, p)], axis=1)
    wc = jnp.concatenate([c_re.reshape(g * ch, p), -c_im.reshape(g * ch, p)], axis=1)
    a_re, a_im = ar.reshape(1, g * p), ai.reshape(1, g * p)
    ylin, yg = _ssm_core_fwd(hf, wb, wc, a_re, a_im, d_row, name=f"ssm_core_{tag}")
    x_mid, val, gate = _glu_fwd(yg, wglu_p, bglu_row, x, tm=MM_TM, name=f"ssm_glu_{tag}")
    return x_mid, (hf, wb, wc, a_re, a_im, ylin, yg, val, gate)


def _ssm_layer_bwd(dx_mid, x, g_row, raw, d_row, wglu_p, saved, tag, scatter):
    g, p, ch = SSM_GROUPS, SSM_STATE, SSM_CH
    hf, wb, wc, a_re, a_im, ylin, yg, val, gate = saved
    dgv, dbglu = _glu_bwd(dx_mid, val, gate, name=f"ssm_dglu_{tag}")
    dyg = _mm_nt_pieces(dgv, wglu_p, tm=MM_TM, tn=1024, name=f"ssm_dyg_{tag}")
    dwglu = _mm_tn_pieces(yg, dgv, pieces=N_DEV, tm=MM_TM, name=f"ssm_dwglu_{tag}")
    du, dwb, dwc, dar, dai, dd = _ssm_core_bwd(hf, ylin, dyg, wb, wc, a_re, a_im, d_row, name=f"ssm_dcore_{tag}")
    dc_re = dwc[:, :p].reshape(g, ch, p)
    dc_im = -dwc[:, p:].reshape(g, ch, p)
    dlr, dli, dls, dbtr, dbti = _ssm_prep_bwd(
        *raw, dar.reshape(g, 1, p), dai.reshape(g, 1, p), dwb[:, :p].reshape(g, ch, p), dwb[:, p:].reshape(g, ch, p),
        name=f"ssm_dprep_{tag}")
    started, token = scatter([dwglu], f"ssm_{tag}")
    dx, dg = _rms_bwd(x, g_row, du, dx_mid, name=f"ssm_dnorm_{tag}", dep=token)
    grads = dict(ssm_lam_re=dlr.reshape(1, g, p), ssm_lam_im=dli.reshape(1, g, p), ssm_log_step=dls.reshape(1, g),
                 ssm_b_re=jnp.transpose(dbtr, (0, 2, 1))[None], ssm_b_im=jnp.transpose(dbti, (0, 2, 1))[None],
                 ssm_c_re=dc_re[None], ssm_c_im=dc_im[None], ssm_d=dd, ssm_b_glu=dbglu)
    return dx, dg, grads, started


_WEIGHTS = ["norm_mix_g", "norm_ffn_g", "pool_w", "pool_b", "pool_scale", "sb_w_qkv", "sb_q_gain", "sb_k_gain", "sb_w_o",
            "ssm_lam_re", "ssm_lam_im", "ssm_log_step", "ssm_b_re", "ssm_b_im", "ssm_c_re", "ssm_c_im", "ssm_d",
            "ssm_w_glu", "ssm_b_glu", "ffn_w_up", "ffn_conv_w", "ffn_conv_b", "ffn_w_down"]
_INPUTS = ["x"] + _WEIGHTS + ["loss_target"] + ["m_" + n for n in _WEIGHTS] + ["v_" + n for n in _WEIGHTS]
_REPLICATED = ["norm_mix_g", "norm_ffn_g", "sb_q_gain", "sb_k_gain", "ssm_lam_re", "ssm_lam_im", "ssm_log_step",
               "ssm_b_re", "ssm_b_im", "ssm_c_re", "ssm_c_im", "ffn_conv_b"]
_SMALL_SHARDED = {"pool_b": 1, "pool_scale": 1, "ssm_d": 1, "ssm_b_glu": 1, "ffn_conv_w": 2}
_BIG = ["pool_w", "sb_w_qkv", "sb_w_o", "ssm_w_glu", "ffn_w_up", "ffn_w_down"]
PACK_COLS = 512


def _pack(arrays):
    flat = jnp.concatenate([a.reshape(-1).astype(F32) for a in arrays])
    rows = -(-flat.shape[0] // (PACK_COLS * PACK_ROWS)) * PACK_ROWS
    return jnp.pad(flat, (0, rows * PACK_COLS - flat.shape[0])).reshape(rows, PACK_COLS)


def _unpack(packed, shapes, lead=()):
    flat = packed.reshape(lead + (-1,))
    out, off = [], 0
    for shp in shapes:
        n = math.prod(shp)
        out.append(flat[..., off:off + n].reshape(lead + tuple(shp)))
        off += n
    return out


def _unshard(gathered, axis):
    g = jnp.moveaxis(gathered, 0, axis)
    shp = g.shape
    return g.reshape(shp[:axis] + (shp[axis] * shp[axis + 1],) + shp[axis + 2:])


def _step(p):
    s = p["x"].shape[1]
    x = p["x"].reshape(s, D_MODEL)
    me = _slot(*_place())

    small_local = [p[n] for n in _SMALL_SHARDED]
    pool_w_l = p["pool_w"].astype(BF16).reshape(-1, POOL_DIM)
    chip = (2 * lax.axis_index("x") + lax.axis_index("y")).astype(jnp.int32).reshape(1)

    def ffn_shards(i):
        return [p["ffn_w_up"][i].astype(BF16), p["ffn_w_down"][i].astype(BF16)]

    st_first, tok = _gather_begin([pool_w_l, _pack(small_local)], "first")
    st_ffn, tok_ffn = [None] * DEPTH, [None] * DEPTH
    st_ffn[0], tok = _gather_begin(ffn_shards(0), "ffn_0", dep=tok)
    st_mix, tok_ffn[0] = _gather_begin([p["sb_w_qkv"][0].astype(BF16), p["sb_w_o"][0].astype(BF16),
                                        p["ssm_w_glu"][0].astype(BF16)], "mixers", dep=tok)
    ag = _gather_end(st_first, tok_ffn[0], "first")
    n_pool = p["pool_w"].shape[0]
    pool_w = jnp.transpose(ag[0].reshape(N_DEV, n_pool, POOL_GROUPS, POOL_DIM // N_DEV, POOL_DIM), (1, 2, 0, 3, 4))
    pool_w = pool_w.reshape(n_pool, POOL_GROUPS, POOL_DIM, POOL_DIM)
    small_full = {}
    for n, g in zip(_SMALL_SHARDED, _unpack(ag[1], [a.shape for a in small_local], lead=(N_DEV,))):
        small_full[n] = _unshard(g, _SMALL_SHARDED[n])
    mix_w = {}
    wup_p, wdown = [None] * DEPTH, [None] * DEPTH

    gains = jnp.stack([p["sb_q_gain"][0], p["sb_k_gain"][0]])[:, None, :]
    ssm_raw = _ssm_params(p["ssm_lam_re"][0], p["ssm_lam_im"][0], p["ssm_log_step"][0], p["ssm_b_re"][0],
                          p["ssm_b_im"][0])

    def mixer_args(i):
        j = i // 3
        g_row = p["norm_mix_g"][i][None]
        if i % 3 == 0:
            return (g_row, pool_w[j], small_full["pool_b"][j][None], small_full["pool_scale"][j][None])
        if i % 3 == 1:
            return (g_row, mix_w["qkv"], gains, mix_w["o"])
        return (g_row, ssm_raw, p["ssm_c_re"][0], p["ssm_c_im"][0], small_full["ssm_d"], mix_w["glu"],
                small_full["ssm_b_glu"])

    def ffn_args(i):
        return (p["norm_ffn_g"][i][None], wup_p[i], wdown[i], small_full["ffn_conv_w"][i], p["ffn_conv_b"][i][None])

    xs_in, xs_mid, saved_mix, saved_ffn = [], [], [], []
    for i in range(DEPTH):
        xs_in.append(x)
        if i == 1:
            mix_w["qkv"], wo_g, mix_w["glu"] = _gather_end(st_mix, x, "mixers")
            mix_w["o"] = wo_g.reshape(D_MODEL, D_MODEL)
        fwd = (_pool_layer_fwd, _sb_layer_fwd, _ssm_layer_fwd)[i % 3]
        x, sv = fwd(x, *mixer_args(i), f"l{i}")
        saved_mix.append(sv)
        xs_mid.append(x)
        wup_p[i], wd_g = _gather_end(st_ffn[i], x, f"ffn_{i}")
        wdown[i] = wd_g.reshape(D_FF, D_MODEL)
        if i + 1 < DEPTH:
            st_ffn[i + 1], tok_ffn[i + 1] = _gather_begin(ffn_shards(i + 1), f"ffn_{i + 1}")
        x, sv = _ffn_fwd(x, *ffn_args(i), f"l{i}", dep=tok_ffn[i + 1] if i + 1 < DEPTH else None)
        saved_ffn.append(sv)
    dx, loss_part = _loss_head(x, p["loss_target"].reshape(s, D_MODEL), name="loss_head")

    grads = {}
    dg_mix, dg_ffn = [None] * DEPTH, [None] * DEPTH
    dconv_w, dconv_b = [None] * DEPTH, [None] * DEPTH
    dpool = {"w": {}, "b": {}, "scale": {}}
    out = {}

    def big_update(n, h, others, idx=None):
        w, m, v = (p[pre + n] if idx is None else p[pre + n][idx] for pre in ("", "m_", "v_"))
        cols = h.shape[-1]
        r = _adamw_chips(w.reshape(-1, cols), h, others, m.reshape(-1, cols), v.reshape(-1, cols), chip,
                         name=f"adamw_{n}" + ("" if idx is None else f"_{idx}"))
        return [a.reshape(w.shape) for a in r]

    kinds = ("grad", "delta", "new_m", "new_v")
    ffn_upd = {"ffn_w_up": [None] * DEPTH, "ffn_w_down": [None] * DEPTH}

    def finish(entry, after):
        names, idx, started, tag = entry
        hs, others = _scatter_end(started, after, tag)
        for n, h, o in zip(names, hs, others):
            upd = big_update(n, h, o, idx)
            if idx is None:
                for kind, a in zip(kinds, upd):
                    out[kind + "_" + n] = a
            else:
                ffn_upd[n][idx] = upd

    pending = []
    for i in reversed(range(DEPTH)):
        dx, dg_ffn[i], dconv_w[i], dconv_b[i], started = _ffn_bwd(
            dx, xs_mid[i], *ffn_args(i), saved_ffn[i], f"l{i}", _scatter_begin)
        for entry in pending:
            finish(entry, dx)
        pending = [(("ffn_w_up", "ffn_w_down"), i, started, f"ffn_l{i}")]
        margs = mixer_args(i)
        if i % 3 == 0:
            j = i // 3
            dx, dg_mix[i], dpool["w"][j], dpool["b"][j], dpool["scale"][j] = _pool_layer_bwd(
                dx, xs_in[i], *margs, saved_mix[i], f"l{i}")
        elif i % 3 == 1:
            dx, dg_mix[i], dqg, dkg, started = _sb_layer_bwd(dx, xs_in[i], *margs, saved_mix[i], f"l{i}", _scatter_begin)
            grads["sb_q_gain"], grads["sb_k_gain"] = dqg, dkg
            pending.append((("sb_w_qkv", "sb_w_o"), None, started, f"sb_l{i}"))
        else:
            g_row, raw, _, _, d_row, wg, _ = margs
            dx, dg_mix[i], sg, started = _ssm_layer_bwd(dx, xs_in[i], g_row, raw, d_row, wg, saved_mix[i], f"l{i}",
                                                       _scatter_begin)
            grads.update(sg)
            pending.append((("ssm_w_glu",), None, started, f"ssm_l{i}"))
    grad_x = dx.reshape(1, s, D_MODEL)
    grads["norm_mix_g"] = jnp.concatenate(dg_mix, axis=0)
    grads["norm_ffn_g"] = jnp.concatenate(dg_ffn, axis=0)
    grads["ffn_conv_w"] = jnp.stack(dconv_w)
    grads["ffn_conv_b"] = jnp.concatenate(dconv_b, axis=0)
    grads["pool_b"] = jnp.concatenate([dpool["b"][j] for j in range(n_pool)], axis=0)
    grads["pool_scale"] = jnp.concatenate([dpool["scale"][j] for j in range(n_pool)], axis=0)
    dpw = jnp.stack([dpool["w"][j] for j in range(n_pool)])
    dpw = dpw.reshape(n_pool, POOL_GROUPS, N_DEV, POOL_DIM // N_DEV, POOL_DIM)
    started, _ = _scatter_begin([jnp.transpose(dpw, (2, 0, 1, 3, 4)).reshape(N_DEV, -1, POOL_DIM)], "pool")
    pending.append((("pool_w",), None, started, "pool"))
    for entry in pending:
        finish(entry, dx)

    small_names = _REPLICATED + list(_SMALL_SHARDED)
    full_shapes = [p[n].shape for n in _REPLICATED] + [small_full[n].shape for n in _SMALL_SHARDED]
    part = _pack([grads[n].reshape(shp) for n, shp in zip(small_names, full_shapes)] + [loss_part[0]])
    (parts,) = _all_gather([part], name="ag_small_grads")
    summed = _unpack(_sum_parts(parts, name="sum_small_grads"), full_shapes + [(LANE,)])
    loss = summed[-1][0]
    small_g = {}
    for n, g in zip(small_names, summed[:-1]):
        if n in _SMALL_SHARDED:
            ax = _SMALL_SHARDED[n]
            g = lax.dynamic_slice_in_dim(g, me * p[n].shape[ax], p[n].shape[ax], axis=ax)
        small_g[n] = g

    local_shapes = [p[n].shape for n in small_names]
    packs = [_pack([p[pre + n] for n in small_names]) for pre in ("", "m_", "v_")]
    res = _adamw_flat(packs[0], _pack([small_g[n] for n in small_names]), packs[1], packs[2], name="adamw_small")
    for kind, packed in zip(("delta", "new_m", "new_v"), res):
        for n, a in zip(small_names, _unpack(packed, local_shapes)):
            out[kind + "_" + n] = a
    for n in small_names:
        out["grad_" + n] = small_g[n]

    for n, per_layer in ffn_upd.items():
        for q, kind in enumerate(kinds):
            out[kind + "_" + n] = jnp.stack([per_layer[i][q] for i in range(DEPTH)])

    return (loss, grad_x, *[out["grad_" + n] for n in _WEIGHTS], *[out["delta_" + n] for n in _WEIGHTS],
            *[out["new_m_" + n] for n in _WEIGHTS], *[out["new_v_" + n] for n in _WEIGHTS])


def kernel(x, norm_mix_g, norm_ffn_g, pool_w, pool_b, pool_scale, sb_w_qkv, sb_q_gain, sb_k_gain, sb_w_o, ssm_lam_re, ssm_lam_im, ssm_log_step, ssm_b_re, ssm_b_im, ssm_c_re, ssm_c_im, ssm_d, ssm_w_glu, ssm_b_glu, ffn_w_up, ffn_conv_w, ffn_conv_b, ffn_w_down, loss_target, m_norm_mix_g, m_norm_ffn_g, m_pool_w, m_pool_b, m_pool_scale, m_sb_w_qkv, m_sb_q_gain, m_sb_k_gain, m_sb_w_o, m_ssm_lam_re, m_ssm_lam_im, m_ssm_log_step, m_ssm_b_re, m_ssm_b_im, m_ssm_c_re, m_ssm_c_im, m_ssm_d, m_ssm_w_glu, m_ssm_b_glu, m_ffn_w_up, m_ffn_conv_w, m_ffn_conv_b, m_ffn_w_down, v_norm_mix_g, v_norm_ffn_g, v_pool_w, v_pool_b, v_pool_scale, v_sb_w_qkv, v_sb_q_gain, v_sb_k_gain, v_sb_w_o, v_ssm_lam_re, v_ssm_lam_im, v_ssm_log_step, v_ssm_b_re, v_ssm_b_im, v_ssm_c_re, v_ssm_c_im, v_ssm_d, v_ssm_w_glu, v_ssm_b_glu, v_ffn_w_up, v_ffn_conv_w, v_ffn_conv_b, v_ffn_w_down):
    args = (x, norm_mix_g, norm_ffn_g, pool_w, pool_b, pool_scale, sb_w_qkv, sb_q_gain, sb_k_gain, sb_w_o, ssm_lam_re, ssm_lam_im, ssm_log_step, ssm_b_re, ssm_b_im, ssm_c_re, ssm_c_im, ssm_d, ssm_w_glu, ssm_b_glu, ffn_w_up, ffn_conv_w, ffn_conv_b, ffn_w_down, loss_target, m_norm_mix_g, m_norm_ffn_g, m_pool_w, m_pool_b, m_pool_scale, m_sb_w_qkv, m_sb_q_gain, m_sb_k_gain, m_sb_w_o, m_ssm_lam_re, m_ssm_lam_im, m_ssm_log_step, m_ssm_b_re, m_ssm_b_im, m_ssm_c_re, m_ssm_c_im, m_ssm_d, m_ssm_w_glu, m_ssm_b_glu, m_ffn_w_up, m_ffn_conv_w, m_ffn_conv_b, m_ffn_w_down, v_norm_mix_g, v_norm_ffn_g, v_pool_w, v_pool_b, v_pool_scale, v_sb_w_qkv, v_sb_q_gain, v_sb_k_gain, v_sb_w_o, v_ssm_lam_re, v_ssm_lam_im, v_ssm_log_step, v_ssm_b_re, v_ssm_b_im, v_ssm_c_re, v_ssm_c_im, v_ssm_d, v_ssm_w_glu, v_ssm_b_glu, v_ffn_w_up, v_ffn_conv_w, v_ffn_conv_b, v_ffn_w_down)
    return _step(dict(zip(_INPUTS, args)))
```

```python
import functools
import math

import jax
import jax.numpy as jnp
from jax import lax
from jax.experimental import pallas as pl
from jax.experimental.pallas import tpu as pltpu

F32 = jnp.float32
BF16 = jnp.bfloat16

N_DEV = 8
D_MODEL = 2048
D_FF = 5632
DEPTH = 4
POOL_GROUPS = 4
POOL_DIM = 512
HEADS = 16
HEAD_DIM = 128
SSM_GROUPS = 128
SSM_CH = 16
SSM_STATE = 64
SSM_BLOCK_GROUPS = 8
SSM_BLOCK_LANES = SSM_BLOCK_GROUPS * SSM_STATE
RMS_EPS = 1e-6
ADAM_LR = 0.001
ADAM_B1 = 0.9
ADAM_B2 = 0.999
ADAM_EPS = 1e-08
ADAM_WD = 0.01
ADAM_STEP = 10

VMEM_LIMIT_BYTES = 56 * 1024 * 1024
LANE = 128
SUBLANE = 8
MESH = pl.DeviceIdType.MESH


def _cparams(*sem):
    return pltpu.CompilerParams(dimension_semantics=tuple(sem), vmem_limit_bytes=VMEM_LIMIT_BYTES)


def _sds(shape, dtype):
    return jax.ShapeDtypeStruct(tuple(shape), dtype)


def _mm(a, b, *, dims, grid, a_spec, b_spec, o_spec, out_shape, out_dtype, name, k_axis=None, acc_shape=None,
        res=None, res_spec=None):
    nk = grid[k_axis] if k_axis is not None else 1
    has_res = res is not None

    def body(*refs):
        if has_res:
            a_ref, b_ref, r_ref, o_ref = refs[:4]
            scr = refs[4:]
        else:
            a_ref, b_ref, o_ref = refs[:3]
            r_ref = None
            scr = refs[3:]
        p = lax.dot_general(a_ref[...], b_ref[...], (dims, ((), ())), preferred_element_type=F32)
        if k_axis is None:
            if has_res:
                p = p + r_ref[...]
            o_ref[...] = p.astype(o_ref.dtype)
        else:
            acc = scr[0]
            k = pl.program_id(k_axis)

            @pl.when(k == 0)
            def _():
                acc[...] = p

            @pl.when(k > 0)
            def _():
                acc[...] += p

            @pl.when(k == nk - 1)
            def _():
                r = acc[...]
                if has_res:
                    r = r + r_ref[...]
                o_ref[...] = r.astype(o_ref.dtype)

    sem = ["parallel"] * len(grid)
    if k_axis is not None:
        sem[k_axis] = "arbitrary"
    in_specs = [a_spec, b_spec] + ([res_spec] if has_res else [])
    args = (a, b) + ((res,) if has_res else ())
    scratch = [pltpu.VMEM(acc_shape, F32)] if k_axis is not None else []
    return pl.pallas_call(
        body, name=name, grid=grid, in_specs=in_specs, out_specs=o_spec, out_shape=_sds(out_shape, out_dtype),
        scratch_shapes=scratch, compiler_params=_cparams(*sem),
    )(*args)


NN = ((1,), (0,))
NT = ((1,), (1,))
TN = ((0,), (0,))


def _mm_nn_pieces(a, wp, *, tm, name, out_dtype=F32):
    s, k = a.shape
    tm = min(tm, s)
    p, _, c = wp.shape
    return _mm(a, wp, dims=NN, grid=(s // tm, p),
               a_spec=pl.BlockSpec((tm, k), lambda m, n: (m, 0)),
               b_spec=pl.BlockSpec((None, k, c), lambda m, n: (n, 0, 0)),
               o_spec=pl.BlockSpec((tm, c), lambda m, n: (m, n)),
               out_shape=(s, p * c), out_dtype=out_dtype, name=name)


def _mm_nt_pieces(a, wp, *, tm, tn, name, out_dtype=F32):
    s = a.shape[0]
    tm = min(tm, s)
    p, n, c = wp.shape
    return _mm(a, wp, dims=NT, grid=(s // tm, n // tn, p), k_axis=2, acc_shape=(tm, tn),
               a_spec=pl.BlockSpec((tm, c), lambda m, j, k: (m, k)),
               b_spec=pl.BlockSpec((None, tn, c), lambda m, j, k: (k, j, 0)),
               o_spec=pl.BlockSpec((tm, tn), lambda m, j, k: (m, j)),
               out_shape=(s, n), out_dtype=out_dtype, name=name)


def _mm_tn_pieces(a, g, *, pieces, tm, name, out_dtype=BF16):
    s, m = a.shape
    c = g.shape[1] // pieces
    return _mm(a, g, dims=TN, grid=(pieces, m // tm),
               a_spec=pl.BlockSpec((s, tm), lambda n, i: (0, i)),
               b_spec=pl.BlockSpec((s, c), lambda n, i: (0, n)),
               o_spec=pl.BlockSpec((None, tm, c), lambda n, i: (n, i, 0)),
               out_shape=(pieces, m, c), out_dtype=out_dtype, name=name)


def _mm_nn(a, w, *, tm, tn, name, out_dtype=F32, res=None):
    s, k = a.shape
    tm = min(tm, s)
    n = w.shape[1]
    return _mm(a, w, dims=NN, grid=(s // tm, n // tn),
               a_spec=pl.BlockSpec((tm, k), lambda m, j: (m, 0)),
               b_spec=pl.BlockSpec((k, tn), lambda m, j: (0, j)),
               o_spec=pl.BlockSpec((tm, tn), lambda m, j: (m, j)),
               res=res, res_spec=pl.BlockSpec((tm, tn), lambda m, j: (m, j)),
               out_shape=(s, n), out_dtype=out_dtype, name=name)


def _mm_nt(a, w, *, tm, tn, name, out_dtype=F32):
    s, k = a.shape
    tm = min(tm, s)
    n = w.shape[0]
    return _mm(a, w, dims=NT, grid=(s // tm, n // tn),
               a_spec=pl.BlockSpec((tm, k), lambda m, j: (m, 0)),
               b_spec=pl.BlockSpec((tn, k), lambda m, j: (j, 0)),
               o_spec=pl.BlockSpec((tm, tn), lambda m, j: (m, j)),
               out_shape=(s, n), out_dtype=out_dtype, name=name)


def _mm_tn(a, g, *, tm, tn, name, out_dtype=BF16):
    s, m = a.shape
    n = g.shape[1]
    return _mm(a, g, dims=TN, grid=(m // tm, n // tn),
               a_spec=pl.BlockSpec((s, tm), lambda i, j: (0, i)),
               b_spec=pl.BlockSpec((s, tn), lambda i, j: (0, j)),
               o_spec=pl.BlockSpec((tm, tn), lambda i, j: (i, j)),
               out_shape=(m, n), out_dtype=out_dtype, name=name)


ROW_TILE = 256


def _dep_spec():
    return pl.BlockSpec((SUBLANE, LANE), lambda i: (0, 0))


def _rms_fwd(x, g_row, *, want_f32, name, dep=None):
    s, d = x.shape
    n_in = 2 if dep is None else 3

    def body(*refs):
        x_ref, g_ref = refs[:2]
        outs = refs[n_in:]
        xv = x_ref[...]
        r = lax.rsqrt(jnp.mean(xv * xv, axis=-1, keepdims=True) + RMS_EPS)
        h = (xv * r) * g_ref[...]
        outs[0][...] = h.astype(BF16)
        if want_f32:
            outs[1][...] = h

    row = pl.BlockSpec((ROW_TILE, d), lambda i: (i, 0))
    out_shape = [_sds((s, d), BF16)] + ([_sds((s, d), F32)] if want_f32 else [])
    out = pl.pallas_call(
        body, name=name, grid=(s // ROW_TILE,),
        in_specs=[row, pl.BlockSpec((1, d), lambda i: (0, 0))] + ([] if dep is None else [_dep_spec()]),
        out_specs=[row] * len(out_shape), out_shape=out_shape, compiler_params=_cparams("parallel"),
    )(x, g_row, *(() if dep is None else (dep,)))
    return out if want_f32 else (out[0], None)


def _rms_bwd(x, g_row, dh, dres, *, name, dep=None):
    s, d = x.shape

    def body(x_ref, g_ref, dh_ref, dres_ref, *rest):
        dx_ref, dg_ref = rest[-2:]
        xv = x_ref[...]
        r = lax.rsqrt(jnp.mean(xv * xv, axis=-1, keepdims=True) + RMS_EPS)
        xn = xv * r
        dhv = dh_ref[...]
        dxn = dhv * g_ref[...]
        dx_ref[...] = dres_ref[...] + r * (dxn - xn * jnp.mean(dxn * xn, axis=-1, keepdims=True))
        part = jnp.sum(dhv * xn, axis=0, keepdims=True)

        @pl.when(pl.program_id(0) == 0)
        def _():
            dg_ref[...] = part

        @pl.when(pl.program_id(0) > 0)
        def _():
            dg_ref[...] += part

    row = pl.BlockSpec((ROW_TILE, d), lambda i: (i, 0))
    vec = pl.BlockSpec((1, d), lambda i: (0, 0))
    return pl.pallas_call(
        body, name=name, grid=(s // ROW_TILE,), in_specs=[row, vec, row, row] + ([] if dep is None else [_dep_spec()]),
        out_specs=[row, vec], out_shape=[_sds((s, d), F32), _sds((1, d), F32)], compiler_params=_cparams("arbitrary"),
    )(x, g_row, dh, dres, *(() if dep is None else (dep,)))


def _shift_down(v, k):
    row = lax.broadcasted_iota(jnp.int32, v.shape, 0)
    return jnp.where(row >= k, pltpu.roll(v, k, 0), 0.0)


def _shift_up(v, k):
    n = v.shape[0]
    row = lax.broadcasted_iota(jnp.int32, v.shape, 0)
    return jnp.where(row < n - k, pltpu.roll(v, n - k, 0), 0.0)


def _sigmoid(z):
    return 1.0 / (1.0 + jnp.exp(-z))


FF_COL_TILE = 256


def _conv3(u, w, b):
    return b + w[0:1, :] * _shift_down(u, 2) + w[1:2, :] * _shift_down(u, 1) + w[2:3, :] * u


def _conv_gate_fwd(up, conv_w, conv_b, *, name):
    s = up.shape[0]
    f = up.shape[1] // 2
    nt = f // FF_COL_TILE

    def body(uv_ref, ug_ref, wv_ref, wg_ref, bv_ref, bg_ref, a_ref):
        vc = _conv3(uv_ref[...], wv_ref[...], bv_ref[...])
        gc = _conv3(ug_ref[...], wg_ref[...], bg_ref[...])
        a_ref[...] = ((gc * _sigmoid(gc)) * vc).astype(BF16)

    def col(rows, off):
        return pl.BlockSpec((rows, FF_COL_TILE), lambda n: (0, n + off))

    return pl.pallas_call(
        body, name=name, grid=(nt,),
        in_specs=[col(s, 0), col(s, nt), col(3, 0), col(3, nt), col(1, 0), col(1, nt)],
        out_specs=col(s, 0), out_shape=_sds((s, f), BF16), compiler_params=_cparams("parallel"),
    )(up, up, conv_w, conv_w, conv_b, conv_b)


def _conv_gate_bwd(up, da, conv_w, conv_b, *, name):
    s = up.shape[0]
    f = up.shape[1] // 2
    nt = f // FF_COL_TILE

    def conv_bwd(u, w, dc):
        d0 = _shift_up(dc, 2)
        d1 = _shift_up(dc, 1)
        dup = w[0:1, :] * d0 + w[1:2, :] * d1 + w[2:3, :] * dc
        dw = jnp.concatenate([jnp.sum(u * d0, axis=0, keepdims=True), jnp.sum(u * d1, axis=0, keepdims=True),
                              jnp.sum(u * dc, axis=0, keepdims=True)], axis=0)
        return dup, dw, jnp.sum(dc, axis=0, keepdims=True)

    def body(uv_ref, ug_ref, da_ref, wv_ref, wg_ref, bv_ref, bg_ref,
             duv_ref, dug_ref, dwv_ref, dwg_ref, dbv_ref, dbg_ref):
        uv = uv_ref[...]
        ug = ug_ref[...]
        vc = _conv3(uv, wv_ref[...], bv_ref[...])
        gc = _conv3(ug, wg_ref[...], bg_ref[...])
        sg = _sigmoid(gc)
        dav = da_ref[...]
        dvc = dav * (gc * sg)
        dgc = dav * vc * (sg * (1.0 + gc * (1.0 - sg)))
        dup, dw, db = conv_bwd(uv, wv_ref[...], dvc)
        duv_ref[...] = dup.astype(BF16)
        dwv_ref[...] = dw
        dbv_ref[...] = db
        dup, dw, db = conv_bwd(ug, wg_ref[...], dgc)
        dug_ref[...] = dup.astype(BF16)
        dwg_ref[...] = dw
        dbg_ref[...] = db

    def col(rows, off):
        return pl.BlockSpec((rows, FF_COL_TILE), lambda n: (0, n + off))

    dup_v, dup_g, dw_v, dw_g, db_v, db_g = pl.pallas_call(
        body, name=name, grid=(nt,),
        in_specs=[col(s, 0), col(s, nt), col(s, 0), col(3, 0), col(3, nt), col(1, 0), col(1, nt)],
        out_specs=[col(s, 0), col(s, 0), col(3, 0), col(3, 0), col(1, 0), col(1, 0)],
        out_shape=[_sds((s, f), BF16), _sds((s, f), BF16), _sds((3, f), F32), _sds((3, f), F32),
                   _sds((1, f), F32), _sds((1, f), F32)],
        compiler_params=_cparams("parallel"),
    )(up, up, da, conv_w, conv_w, conv_b, conv_b)
    return dup_v, dup_g, jnp.concatenate([dw_v, dw_g], axis=1), jnp.concatenate([db_v, db_g], axis=1)


def _pool_counts(shape, g):
    win = jnp.left_shift(jnp.int32(2), g)
    t = lax.broadcasted_iota(jnp.int32, shape, 0)
    return win, jnp.minimum(t + 1, win).astype(F32)


def _window_sum(v, g, shift):
    for k in range(POOL_GROUPS):
        v = jnp.where(g >= k, v + shift(v, 1 << k), v)
    return v


def _pool_fwd(hf, x, w, b, scale, *, name):
    s, d = hf.shape

    def body(h_ref, x_ref, w_ref, b_ref, sc_ref, o_ref):
        g = pl.program_id(0)
        h = h_ref[...]
        _, cnt = _pool_counts(h.shape, g)
        pooled = _window_sum(h, g, _shift_down) / cnt - h
        y = jnp.dot(pooled.astype(BF16), w_ref[...], preferred_element_type=F32) + b_ref[...]
        o_ref[...] = x_ref[...] + y * sc_ref[...]

    col = pl.BlockSpec((s, POOL_DIM), lambda g: (0, g))
    vec = pl.BlockSpec((1, POOL_DIM), lambda g: (0, g))
    return pl.pallas_call(
        body, name=name, grid=(POOL_GROUPS,),
        in_specs=[col, col, pl.BlockSpec((None, POOL_DIM, POOL_DIM), lambda g: (g, 0, 0)), vec, vec],
        out_specs=col, out_shape=_sds((s, d), F32), compiler_params=_cparams("parallel"),
    )(hf, x, w, b, scale)


def _pool_bwd(hf, dm, w, b, scale, *, name):
    s, d = hf.shape

    def body(h_ref, dm_ref, w_ref, b_ref, sc_ref, dh_ref, dw_ref, db_ref, dsc_ref):
        g = pl.program_id(0)
        h = h_ref[...]
        _, cnt = _pool_counts(h.shape, g)
        pooled = (_window_sum(h, g, _shift_down) / cnt - h).astype(BF16)
        wv = w_ref[...]
        y = jnp.dot(pooled, wv, preferred_element_type=F32) + b_ref[...]
        dmv = dm_ref[...]
        dsc_ref[...] = jnp.sum(dmv * y, axis=0, keepdims=True)
        dy = dmv * sc_ref[...]
        db_ref[...] = jnp.sum(dy, axis=0, keepdims=True)
        dyb = dy.astype(BF16)
        dw_ref[...] = lax.dot_general(pooled, dyb, (TN, ((), ())), preferred_element_type=F32).astype(BF16)
        dp = lax.dot_general(dyb, wv, (NT, ((), ())), preferred_element_type=F32)
        dh_ref[...] = _window_sum(dp / cnt, g, _shift_up) - dp

    col = pl.BlockSpec((s, POOL_DIM), lambda g: (0, g))
    vec = pl.BlockSpec((1, POOL_DIM), lambda g: (0, g))
    mat = pl.BlockSpec((None, POOL_DIM, POOL_DIM), lambda g: (g, 0, 0))
    return pl.pallas_call(
        body, name=name, grid=(POOL_GROUPS,), in_specs=[col, col, mat, vec, vec], out_specs=[col, mat, vec, vec],
        out_shape=[_sds((s, d), F32), _sds((POOL_GROUPS, POOL_DIM, POOL_DIM), BF16), _sds((1, d), F32),
                   _sds((1, d), F32)],
        compiler_params=_cparams("parallel"),
    )(hf, dm, w, b, scale)


ATT_TQ = 256
ATT_TK = 256


def _qk_norm_fwd(qkv, gains, *, name):
    s = qkv.shape[0]

    def body(x_ref, g_ref, o_ref):
        xv = x_ref[...]
        r = lax.rsqrt(jnp.mean(xv * xv, axis=-1, keepdims=True) + RMS_EPS)
        o_ref[...] = ((xv * r) * g_ref[...]).astype(BF16)

    blk = pl.BlockSpec((s, HEAD_DIM), lambda hd: (0, hd))
    return pl.pallas_call(
        body, name=name, grid=(2 * HEADS,),
        in_specs=[blk, pl.BlockSpec((None, 1, HEAD_DIM), lambda hd: (hd // HEADS, 0, 0))],
        out_specs=blk, out_shape=_sds((s, 2 * HEADS * HEAD_DIM), BF16), compiler_params=_cparams("parallel"),
    )(qkv, gains)


def _qk_norm_bwd(qkv, gains, dn, *, which, name):
    s = qkv.shape[0]

    def body(x_ref, g_ref, dn_ref, dx_ref, dg_ref):
        xv = x_ref[...]
        r = lax.rsqrt(jnp.mean(xv * xv, axis=-1, keepdims=True) + RMS_EPS)
        xn = xv * r
        dnv = dn_ref[...]
        dxn = dnv * g_ref[...]
        dx_ref[...] = (r * (dxn - xn * jnp.mean(dxn * xn, axis=-1, keepdims=True))).astype(BF16)
        part = jnp.sum(dnv * xn, axis=0, keepdims=True)

        @pl.when(pl.program_id(0) == 0)
        def _():
            dg_ref[...] = part

        @pl.when(pl.program_id(0) > 0)
        def _():
            dg_ref[...] += part

    blk = pl.BlockSpec((s, HEAD_DIM), lambda hd: (0, hd))
    return pl.pallas_call(
        body, name=name, grid=(HEADS,),
        in_specs=[pl.BlockSpec((s, HEAD_DIM), lambda hd: (0, hd + which * HEADS)),
                  pl.BlockSpec((None, 1, HEAD_DIM), lambda hd: (which, 0, 0)), blk],
        out_specs=[blk, pl.BlockSpec((1, HEAD_DIM), lambda hd: (0, 0))],
        out_shape=[_sds((s, HEADS * HEAD_DIM), BF16), _sds((1, HEAD_DIM), F32)],
        compiler_params=_cparams("arbitrary"),
    )(qkv, gains, dn)


def _split_dot(v, tri):
    hi = v.astype(BF16)
    lo = (v - hi.astype(F32)).astype(BF16)
    return (jnp.dot(hi, tri, preferred_element_type=F32) + jnp.dot(lo, tri, preferred_element_type=F32))


def _causal_mask(qi, j):
    tpos = qi * ATT_TQ + lax.broadcasted_iota(jnp.int32, (ATT_TQ, ATT_TK), 0)
    spos = j * ATT_TK + lax.broadcasted_iota(jnp.int32, (ATT_TQ, ATT_TK), 1)
    return spos < tpos


def _att_tile(q, kj, qi, j):
    z = lax.dot_general(q, kj, (NT, ((), ())), preferred_element_type=F32) * (1.0 / math.sqrt(HEAD_DIM))
    mask = _causal_mask(qi, j)
    lb = jnp.minimum(z, 0.0) - jnp.log1p(jnp.exp(-jnp.abs(z)))
    l1m = jnp.where(mask, lb - z, 0.0)
    return lb, l1m, mask


def _tri(rel):
    r = lax.broadcasted_iota(jnp.int32, (ATT_TK, ATT_TK), 0)
    c = lax.broadcasted_iota(jnp.int32, (ATT_TK, ATT_TK), 1)
    return jnp.where(rel(r, c), 1.0, 0.0).astype(BF16)


def _sb_fwd(qkn, vb, *, name):
    s = vb.shape[0]

    def body(q_ref, k_ref, v_ref, o_ref):
        qi = pl.program_id(1)
        q = q_ref[...]
        after = _tri(lambda r, c: r > c)

        def step(t, carry):
            acc, run = carry
            j = qi - t
            rows = pl.ds(pl.multiple_of(j * ATT_TK, ATT_TK), ATT_TK)
            lb, l1m, mask = _att_tile(q, k_ref[rows, :], qi, j)
            remain = _split_dot(l1m, after) + run
            attn = jnp.where(mask, jnp.exp(lb + remain), 0.0)
            acc = acc + jnp.dot(attn.astype(BF16), v_ref[rows, :], preferred_element_type=F32)
            return acc, run + jnp.sum(l1m, axis=1, keepdims=True)

        acc, _ = lax.fori_loop(0, qi + 1, step, (jnp.zeros((ATT_TQ, HEAD_DIM), F32), jnp.zeros((ATT_TQ, 1), F32)))
        o_ref[...] = acc.astype(BF16)

    return pl.pallas_call(
        body, name=name, grid=(HEADS, s // ATT_TQ),
        in_specs=[pl.BlockSpec((ATT_TQ, HEAD_DIM), lambda hd, i: (i, hd)),
                  pl.BlockSpec((s, HEAD_DIM), lambda hd, i: (0, hd + HEADS)),
                  pl.BlockSpec((s, HEAD_DIM), lambda hd, i: (0, hd))],
        out_specs=pl.BlockSpec((ATT_TQ, HEAD_DIM), lambda hd, i: (i, hd)),
        out_shape=_sds((s, HEADS * HEAD_DIM), BF16), compiler_params=_cparams("parallel", "parallel"),
    )(qkn, qkn, vb)


def _sb_bwd(qkn, vb, dob, *, name):
    s = vb.shape[0]
    nkb = s // ATT_TK

    def body(q_ref, k_ref, v_ref, do_ref, dq_ref, dk_ref, dv_ref, a_buf, sig_buf):
        qi = pl.program_id(1)
        q = q_ref[...]
        do = do_ref[...]
        after = _tri(lambda r, c: r > c)
        before = _tri(lambda r, c: r < c)

        @pl.when(qi == 0)
        def _():
            dk_ref[...] = jnp.zeros_like(dk_ref)
            dv_ref[...] = jnp.zeros_like(dv_ref)

        def down(t, run):
            j = qi - t
            rows = pl.ds(pl.multiple_of(j * ATT_TK, ATT_TK), ATT_TK)
            lb, l1m, mask = _att_tile(q, k_ref[rows, :], qi, j)
            remain = _split_dot(l1m, after) + run
            a_buf[j] = jnp.where(mask, jnp.exp(lb + remain), 0.0)
            sig_buf[j] = jnp.exp(lb)
            return run + jnp.sum(l1m, axis=1, keepdims=True)

        lax.fori_loop(0, qi + 1, down, jnp.zeros((ATT_TQ, 1), F32))

        def up(j, carry):
            dq, run = carry
            rows = pl.ds(pl.multiple_of(j * ATT_TK, ATT_TK), ATT_TK)
            a = a_buf[j]
            sig = sig_buf[j]
            mask = _causal_mask(qi, j)
            da = lax.dot_general(do, v_ref[rows, :], (NT, ((), ())), preferred_element_type=F32)
            p = a * da
            c = _split_dot(p, before) + run
            dz = jnp.where(mask, p * (1.0 - sig) - c * sig, 0.0) * (1.0 / math.sqrt(HEAD_DIM))
            dzb = dz.astype(BF16)
            dq = dq + jnp.dot(dzb, k_ref[rows, :], preferred_element_type=F32)
            dk_ref[rows, :] += lax.dot_general(dzb, q, (TN, ((), ())), preferred_element_type=F32)
            dv_ref[rows, :] += lax.dot_general(a.astype(BF16), do, (TN, ((), ())), preferred_element_type=F32)
            return dq, run + jnp.sum(p, axis=1, keepdims=True)

        dq, _ = lax.fori_loop(0, qi + 1, up, (jnp.zeros((ATT_TQ, HEAD_DIM), F32), jnp.zeros((ATT_TQ, 1), F32)))
        dq_ref[...] = dq

    qblk = pl.BlockSpec((ATT_TQ, HEAD_DIM), lambda hd, i: (i, hd))
    full = pl.BlockSpec((s, HEAD_DIM), lambda hd, i: (0, hd))
    return pl.pallas_call(
        body, name=name, grid=(HEADS, s // ATT_TQ),
        in_specs=[qblk, pl.BlockSpec((s, HEAD_DIM), lambda hd, i: (0, hd + HEADS)), full, qblk],
        out_specs=[qblk, full, full],
        out_shape=[_sds((s, HEADS * HEAD_DIM), F32)] * 3,
        scratch_shapes=[pltpu.VMEM((nkb, ATT_TQ, ATT_TK), F32), pltpu.VMEM((nkb, ATT_TQ, ATT_TK), F32)],
        compiler_params=_cparams("parallel", "arbitrary"),
    )(qkn, qkn, vb, dob)


def _ssm_discretize(lam_re, lam_im, log_step, bt_re, bt_im):
    step = jnp.exp(log_step)
    mag = jnp.exp(lam_re * step)
    lb_re = mag * jnp.cos(lam_im * step)
    lb_im = mag * jnp.sin(lam_im * step)
    den = lam_re * lam_re + lam_im * lam_im
    f_re = ((lb_re - 1.0) * lam_re + lb_im * lam_im) / den
    f_im = (lb_im * lam_re - (lb_re - 1.0) * lam_im) / den
    return lb_re, lb_im, f_re * bt_re - f_im * bt_im, f_re * bt_im + f_im * bt_re


_SSM_LAM = (SSM_GROUPS, 1, SSM_STATE)
_SSM_STEP = (SSM_GROUPS, 1, 1)
_SSM_BT = (SSM_GROUPS, SSM_CH, SSM_STATE)


def _ssm_prep_fwd(lam_re, lam_im, log_step, bt_re, bt_im, *, name):
    def body(lr, li, ls, br, bi, o_ar, o_ai, o_br, o_bi):
        o_ar[...], o_ai[...], o_br[...], o_bi[...] = _ssm_discretize(lr[...], li[...], ls[...], br[...], bi[...])

    return pl.pallas_call(
        body, name=name, out_shape=[_sds(_SSM_LAM, F32), _sds(_SSM_LAM, F32), _sds(_SSM_BT, F32), _sds(_SSM_BT, F32)],
    )(lam_re, lam_im, log_step, bt_re, bt_im)


def _ssm_prep_bwd(lam_re, lam_im, log_step, bt_re, bt_im, d_ar, d_ai, d_br, d_bi, *, name):
    def body(lr, li, ls, br, bi, g_ar, g_ai, g_br, g_bi, o_lr, o_li, o_ls, o_br, o_bi):
        _, vjp = jax.vjp(_ssm_discretize, lr[...], li[...], ls[...], br[...], bi[...])
        o_lr[...], o_li[...], o_ls[...], o_br[...], o_bi[...] = vjp((g_ar[...], g_ai[...], g_br[...], g_bi[...]))

    return pl.pallas_call(
        body, name=name,
        out_shape=[_sds(_SSM_LAM, F32), _sds(_SSM_LAM, F32), _sds(_SSM_STEP, F32), _sds(_SSM_BT, F32), _sds(_SSM_BT, F32)],
    )(lam_re, lam_im, log_step, bt_re, bt_im, d_ar, d_ai, d_br, d_bi)


def _bd_masks():
    rowg = lax.broadcasted_iota(jnp.int32, (LANE, LANE), 0) // SSM_CH
    low = lax.broadcasted_iota(jnp.int32, (LANE, LANE), 1) < SSM_STATE
    return rowg, low


def _bd_expand(w):
    rowg, low = _bd_masks()
    high = jnp.logical_not(low)
    wr = pltpu.roll(w, SSM_STATE, 1)
    re = [jnp.where((rowg == 2 * k) & low, w, 0.0) + jnp.where((rowg == 2 * k + 1) & high, wr, 0.0) for k in range(4)]
    im = [jnp.where((rowg == 2 * k) & low, wr, 0.0) + jnp.where((rowg == 2 * k + 1) & high, w, 0.0) for k in range(4)]
    return jnp.concatenate(re + im, axis=1)


def _bd_extract(dbd):
    rowg, low = _bd_masks()
    high = jnp.logical_not(low)
    acc = jnp.zeros((LANE, LANE), F32)
    for k in range(4):
        c = dbd[:, LANE * k:LANE * (k + 1)]
        acc = acc + jnp.where((rowg == 2 * k) & low, c, 0.0) + jnp.where((rowg == 2 * k + 1) & low, pltpu.roll(c, SSM_STATE, 1), 0.0)
        c = dbd[:, LANE * (4 + k):LANE * (5 + k)]
        acc = acc + jnp.where((rowg == 2 * k) & high, pltpu.roll(c, SSM_STATE, 1), 0.0) + jnp.where((rowg == 2 * k + 1) & high, c, 0.0)
    return acc


def _cmul(ar, ai, br, bi):
    return ar * br - ai * bi, ar * bi + ai * br


def _scan_rows(xr, xi, ar, ai, *, reverse):
    n = xr.shape[0] // SUBLANE
    lanes = xr.shape[1]
    row = lax.broadcasted_iota(jnp.int32, (SUBLANE, lanes), 0)
    powers = [(ar, ai)]
    for _ in range(SUBLANE - 1):
        powers.append(_cmul(*powers[-1], ar, ai))
    pr = jnp.zeros((SUBLANE, lanes), F32)
    pi = jnp.zeros((SUBLANE, lanes), F32)
    for r in range(SUBLANE):
        e = (SUBLANE - 1 - r) if reverse else r
        pr = jnp.where(row == r, powers[e][0], pr)
        pi = jnp.where(row == r, powers[e][1], pi)

    def shift(v, d):
        if reverse:
            return jnp.where(row < SUBLANE - d, pltpu.roll(v, SUBLANE - d, 0), 0.0)
        return jnp.where(row >= d, pltpu.roll(v, d, 0), 0.0)

    def body(i, carry):
        cr, ci = carry
        g = (n - 1 - i) if reverse else i
        rows = pl.ds(pl.multiple_of(g * SUBLANE, SUBLANE), SUBLANE)
        br = xr[rows, :]
        bi = xi[rows, :]
        for d in (1, 2, 4):
            qr, qi = powers[d - 1]
            sr = shift(br, d)
            si = shift(bi, d)
            br, bi = br + qr * sr - qi * si, bi + qr * si + qi * sr
        br, bi = br + pr * cr - pi * ci, bi + pr * ci + pi * cr
        xr[rows, :] = br
        xi[rows, :] = bi
        edge = 0 if reverse else SUBLANE - 1
        return br[edge:edge + 1, :], bi[edge:edge + 1, :]

    zero = jnp.zeros((1, lanes), F32)
    lax.fori_loop(0, n, body, (zero, zero), unroll=2)


_GELU_C = math.sqrt(2.0 / math.pi)
_GELU_A = 0.044715


def _gelu(v):
    return 0.5 * v * (1.0 + jnp.tanh(_GELU_C * (v + _GELU_A * v * v * v)))


def _gelu_grad(v):
    t = jnp.tanh(_GELU_C * (v + _GELU_A * v * v * v))
    return 0.5 * (1.0 + t) + 0.5 * v * (1.0 - t * t) * (_GELU_C * (1.0 + 3.0 * _GELU_A * v * v))


def _ssm_states(u_b16, eb, ar, ai, xr, xi):
    nl = SSM_BLOCK_LANES
    xr[...] = jnp.dot(u_b16, eb[:, :nl], preferred_element_type=F32)
    xi[...] = jnp.dot(u_b16, eb[:, nl:], preferred_element_type=F32)
    _scan_rows(xr, xi, ar, ai, reverse=False)


def _ssm_specs(s):
    col = pl.BlockSpec((s, LANE), lambda b: (0, b))
    wsm = pl.BlockSpec((LANE, LANE), lambda b: (b, 0))
    lam = pl.BlockSpec((1, SSM_BLOCK_LANES), lambda b: (0, b))
    vec = pl.BlockSpec((1, LANE), lambda b: (0, b))
    return col, wsm, lam, vec


def _ssm_core_fwd(u, wb, wc, a_re, a_im, d_row, *, name):
    s, d = u.shape
    nl = SSM_BLOCK_LANES

    def body(u_ref, wb_ref, wc_ref, ar_ref, ai_ref, d_ref, y_ref, yg_ref, xr, xi):
        uv = u_ref[...]
        eb = _bd_expand(wb_ref[...]).astype(BF16)
        ec = _bd_expand(wc_ref[...]).astype(BF16)
        _ssm_states(uv.astype(BF16), eb, ar_ref[...], ai_ref[...], xr, xi)
        y = (lax.dot_general(xr[...].astype(BF16), ec[:, :nl], (NT, ((), ())), preferred_element_type=F32)
             + lax.dot_general(xi[...].astype(BF16), ec[:, nl:], (NT, ((), ())), preferred_element_type=F32)
             + d_ref[...] * uv)
        y_ref[...] = y
        yg_ref[...] = _gelu(y).astype(BF16)

    col, wsm, lam, vec = _ssm_specs(s)
    return pl.pallas_call(
        body, name=name, grid=(d // LANE,), in_specs=[col, wsm, wsm, lam, lam, vec], out_specs=[col, col],
        out_shape=[_sds((s, d), F32), _sds((s, d), BF16)],
        scratch_shapes=[pltpu.VMEM((s, nl), F32), pltpu.VMEM((s, nl), F32)],
        compiler_params=_cparams("parallel"),
    )(u, wb, wc, a_re, a_im, d_row)


def _ssm_core_bwd(u, ylin, dyg, wb, wc, a_re, a_im, d_row, *, name):
    s, d = u.shape
    nl = SSM_BLOCK_LANES
    n8 = s // SUBLANE

    def body(u_ref, y_ref, dyg_ref, wb_ref, wc_ref, ar_ref, ai_ref, d_ref,
             du_ref, dwb_ref, dwc_ref, dar_ref, dai_ref, dd_ref, xr, xi, gr, gi):
        uv = u_ref[...]
        ub = uv.astype(BF16)
        ar = ar_ref[...]
        ai = ai_ref[...]
        dy = dyg_ref[...] * _gelu_grad(y_ref[...])
        dd_ref[...] = jnp.sum(dy * uv, axis=0, keepdims=True)
        dyb = dy.astype(BF16)
        eb = _bd_expand(wb_ref[...]).astype(BF16)
        ec = _bd_expand(wc_ref[...]).astype(BF16)
        _ssm_states(ub, eb, ar, ai, xr, xi)
        dec = jnp.concatenate(
            [lax.dot_general(dyb, xr[...].astype(BF16), (TN, ((), ())), preferred_element_type=F32),
             lax.dot_general(dyb, xi[...].astype(BF16), (TN, ((), ())), preferred_element_type=F32)], axis=1)
        dwc_ref[...] = _bd_extract(dec)
        gr[...] = jnp.dot(dyb, ec[:, :nl], preferred_element_type=F32)
        gi[...] = jnp.dot(dyb, ec[:, nl:], preferred_element_type=F32)
        _scan_rows(gr, gi, ar, -ai, reverse=True)

        row = lax.broadcasted_iota(jnp.int32, (SUBLANE, nl), 0)

        def lam_grad(i, acc):
            acc_r, acc_i = acc
            rows = pl.ds(pl.multiple_of(i * SUBLANE, SUBLANE), SUBLANE)
            prev = pl.ds(pl.multiple_of(jnp.maximum(i - 1, 0) * SUBLANE, SUBLANE), SUBLANE)
            keep = jnp.where(i > 0, 1.0, 0.0)
            xpr = jnp.where(row == 0, pltpu.roll(xr[prev, :], 1, 0) * keep, pltpu.roll(xr[rows, :], 1, 0))
            xpi = jnp.where(row == 0, pltpu.roll(xi[prev, :], 1, 0) * keep, pltpu.roll(xi[rows, :], 1, 0))
            g_r = gr[rows, :]
            g_i = gi[rows, :]
            return acc_r + g_r * xpr + g_i * xpi, acc_i + g_i * xpr - g_r * xpi

        zero = jnp.zeros((SUBLANE, nl), F32)
        acc_r, acc_i = lax.fori_loop(0, n8, lam_grad, (zero, zero), unroll=2)
        dar_ref[...] = jnp.sum(acc_r, axis=0, keepdims=True)
        dai_ref[...] = jnp.sum(acc_i, axis=0, keepdims=True)

        grb = gr[...].astype(BF16)
        gib = gi[...].astype(BF16)
        deb = jnp.concatenate([lax.dot_general(ub, grb, (TN, ((), ())), preferred_element_type=F32),
                               lax.dot_general(ub, gib, (TN, ((), ())), preferred_element_type=F32)], axis=1)
        dwb_ref[...] = _bd_extract(deb)
        du_ref[...] = (lax.dot_general(grb, eb[:, :nl], (NT, ((), ())), preferred_element_type=F32)
                       + lax.dot_general(gib, eb[:, nl:], (NT, ((), ())), preferred_element_type=F32)
                       + d_ref[...] * dy)

    col, wsm, lam, vec = _ssm_specs(s)
    return pl.pallas_call(
        body, name=name, grid=(d // LANE,), in_specs=[col, col, col, wsm, wsm, lam, lam, vec],
        out_specs=[col, wsm, wsm, lam, lam, vec],
        out_shape=[_sds((s, d), F32), _sds((d, LANE), F32), _sds((d, LANE), F32),
                   _sds((1, SSM_GROUPS * SSM_STATE), F32), _sds((1, SSM_GROUPS * SSM_STATE), F32), _sds((1, d), F32)],
        scratch_shapes=[pltpu.VMEM((s, nl), F32)] * 4,
        compiler_params=_cparams("parallel"),
    )(u, ylin, dyg, wb, wc, a_re, a_im, d_row)


GLU_PIECE = 512


def _glu_fwd(yg, wp, b_row, x, *, tm, name):
    s, d = yg.shape
    tm = min(tm, s)
    half = N_DEV // 2

    def body(y_ref, wv_ref, wg_ref, bv_ref, bg_ref, x_ref, o_ref, val_ref, gate_ref):
        yv = y_ref[...]
        val = jnp.dot(yv, wv_ref[...], preferred_element_type=F32) + bv_ref[...]
        gate = jnp.dot(yv, wg_ref[...], preferred_element_type=F32) + bg_ref[...]
        val_ref[...] = val
        gate_ref[...] = gate
        o_ref[...] = x_ref[...] + val * _sigmoid(gate)

    blk = pl.BlockSpec((tm, GLU_PIECE), lambda m, n: (m, n))
    return pl.pallas_call(
        body, name=name, grid=(s // tm, half),
        in_specs=[pl.BlockSpec((tm, d), lambda m, n: (m, 0)),
                  pl.BlockSpec((None, d, GLU_PIECE), lambda m, n: (n, 0, 0)),
                  pl.BlockSpec((None, d, GLU_PIECE), lambda m, n: (n + half, 0, 0)),
                  pl.BlockSpec((1, GLU_PIECE), lambda m, n: (0, n)),
                  pl.BlockSpec((1, GLU_PIECE), lambda m, n: (0, n + half)), blk],
        out_specs=[blk, blk, blk], out_shape=[_sds((s, d), F32)] * 3,
        compiler_params=_cparams("parallel", "parallel"),
    )(yg, wp, wp, b_row, b_row, x)


def _glu_bwd(dout, val, gate, *, name):
    s, d = dout.shape

    def body(do_ref, val_ref, gate_ref, dgv_ref, db_ref):
        sg = _sigmoid(gate_ref[...])
        dov = do_ref[...]
        dgv = jnp.concatenate([dov * sg, dov * val_ref[...] * (sg * (1.0 - sg))], axis=1)
        dgv_ref[...] = dgv.astype(BF16)
        part = jnp.sum(dgv, axis=0, keepdims=True)

        @pl.when(pl.program_id(0) == 0)
        def _():
            db_ref[...] = part

        @pl.when(pl.program_id(0) > 0)
        def _():
            db_ref[...] += part

    row = pl.BlockSpec((ROW_TILE, d), lambda i: (i, 0))
    return pl.pallas_call(
        body, name=name, grid=(s // ROW_TILE,), in_specs=[row, row, row],
        out_specs=[pl.BlockSpec((ROW_TILE, 2 * d), lambda i: (i, 0)), pl.BlockSpec((1, 2 * d), lambda i: (0, 0))],
        out_shape=[_sds((s, 2 * d), BF16), _sds((1, 2 * d), F32)], compiler_params=_cparams("arbitrary"),
    )(dout, val, gate)


def _loss_head(y, target, *, name):
    s, d = y.shape

    def body(y_ref, t_ref, dy_ref, l_ref):
        e = y_ref[...] - t_ref[...]
        dy_ref[...] = e * (1.0 / d)
        part = jnp.zeros((SUBLANE, LANE), F32) + jnp.sum(e * e) * (0.5 / d)

        @pl.when(pl.program_id(0) == 0)
        def _():
            l_ref[...] = part

        @pl.when(pl.program_id(0) > 0)
        def _():
            l_ref[...] += part

    row = pl.BlockSpec((ROW_TILE, d), lambda i: (i, 0))
    return pl.pallas_call(
        body, name=name, grid=(s // ROW_TILE,), in_specs=[row, row],
        out_specs=[row, pl.BlockSpec((SUBLANE, LANE), lambda i: (0, 0))],
        out_shape=[_sds((s, d), F32), _sds((SUBLANE, LANE), F32)], compiler_params=_cparams("arbitrary"),
    )(y, target)


def _adamw_math(w, g, m, v):
    m = ADAM_B1 * m + (1.0 - ADAM_B1) * g
    v = ADAM_B2 * v + (1.0 - ADAM_B2) * (g * g)
    m_hat = m / (1.0 - ADAM_B1 ** ADAM_STEP)
    v_hat = v / (1.0 - ADAM_B2 ** ADAM_STEP)
    return -ADAM_LR * (m_hat / (jnp.sqrt(v_hat) + ADAM_EPS) + ADAM_WD * w), m, v


ADAM_ROWS = 64
PACK_ROWS = 64


def _sum_pieces(p_ref):
    g = p_ref[0].astype(F32)
    for k in range(1, N_DEV):
        g = g + p_ref[k].astype(F32)
    return g


def _adamw_pieces(w, pieces, m, v, *, name):
    r, c = w.shape

    def body(w_ref, p_ref, m_ref, v_ref, g_ref, d_ref, nm_ref, nv_ref):
        g = _sum_pieces(p_ref)
        g_ref[...] = g
        d_ref[...], nm_ref[...], nv_ref[...] = _adamw_math(w_ref[...], g, m_ref[...], v_ref[...])

    blk = pl.BlockSpec((ADAM_ROWS, c), lambda i: (i, 0))
    return pl.pallas_call(
        body, name=name, grid=(r // ADAM_ROWS,),
        in_specs=[blk, pl.BlockSpec((N_DEV, ADAM_ROWS, c), lambda i: (0, i, 0)), blk, blk],
        out_specs=[blk] * 4, out_shape=[_sds((r, c), F32)] * 4, compiler_params=_cparams("parallel"),
    )(w, pieces, m, v)


def _sum_parts(parts, *, name):
    _, r, c = parts.shape

    def body(p_ref, o_ref):
        o_ref[...] = _sum_pieces(p_ref)

    return pl.pallas_call(
        body, name=name, grid=(r // PACK_ROWS,),
        in_specs=[pl.BlockSpec((N_DEV, PACK_ROWS, c), lambda i: (0, i, 0))],
        out_specs=pl.BlockSpec((PACK_ROWS, c), lambda i: (i, 0)), out_shape=_sds((r, c), F32),
        compiler_params=_cparams("parallel"),
    )(parts)


def _adamw_flat(w, g, m, v, *, name):
    r, c = w.shape

    def body(w_ref, g_ref, m_ref, v_ref, d_ref, nm_ref, nv_ref):
        d_ref[...], nm_ref[...], nv_ref[...] = _adamw_math(w_ref[...], g_ref[...], m_ref[...], v_ref[...])

    blk = pl.BlockSpec((PACK_ROWS, c), lambda i: (i, 0))
    return pl.pallas_call(
        body, name=name, grid=(r // PACK_ROWS,), in_specs=[blk] * 4, out_specs=[blk] * 3,
        out_shape=[_sds((r, c), F32)] * 3, compiler_params=_cparams("parallel"),
    )(w, g, m, v)


_ANY = pl.BlockSpec(memory_space=pl.ANY)


def _place():
    return lax.axis_index("x"), lax.axis_index("y"), lax.axis_index("c")


def _slot(px, py, pc):
    return 4 * px + 2 * py + pc


def _all_gather(xs, *, name):
    n = len(xs)

    def body(*refs):
        ins, outs = refs[:n], refs[n:2 * n]
        send_sems, recv_sems, local_sems = refs[2 * n:]
        x, y, c = _place()
        me, sibling = (x, y, c), (x, y, 1 - c)
        chips = [(1 - x, y), (x, 1 - y), (1 - x, 1 - y)]

        def copy(i, k, block, to, src=None):
            rows = outs[i].at[_slot(*block)]
            return pltpu.make_async_remote_copy(
                src_ref=rows if src is None else src, dst_ref=rows, send_sem=send_sems.at[i, k],
                recv_sem=recv_sems.at[i, k], device_id=to, device_id_type=MESH)

        mine = [pltpu.make_async_copy(ins[i], outs[i].at[_slot(*me)], local_sems.at[i]) for i in range(n)]
        for cp in mine:
            cp.start()
        first = []
        for i in range(n):
            first.append(copy(i, 0, me, sibling, src=ins[i]))
            first += [copy(i, 1 + j, me, (*chip, c), src=ins[i]) for j, chip in enumerate(chips)]
        for cp in first:
            cp.start()
        passed = []
        for j, chip in enumerate(chips):
            for i in range(n):
                copy(i, 1 + j, (*chip, c), me).wait_recv()
                cp = copy(i, 4 + j, (*chip, c), sibling)
                cp.start()
                passed.append(cp)
        for i in range(n):
            copy(i, 0, sibling, me).wait_recv()
            for j, chip in enumerate(chips):
                copy(i, 4 + j, (*chip, 1 - c), me).wait_recv()
        for cp in first + passed:
            cp.wait_send()
        for cp in mine:
            cp.wait()

    return pl.pallas_call(
        body, name=name, in_specs=[_ANY] * n, out_specs=[_ANY] * n,
        out_shape=[_sds((N_DEV,) + a.shape, a.dtype) for a in xs],
        scratch_shapes=[pltpu.SemaphoreType.DMA((n, 7)), pltpu.SemaphoreType.DMA((n, 7)), pltpu.SemaphoreType.DMA((n,))],
    )(*xs)


def _exchange_pieces(gs, *, name):
    n = len(gs)
    flips = [(dx, dy, dc) for dx in (0, 1) for dy in (0, 1) for dc in (0, 1)][1:]

    def body(*refs):
        ins, outs = refs[:n], refs[n:2 * n]
        send_sems, recv_sems, local_sems = refs[2 * n:]
        x, y, c = _place()
        me = _slot(x, y, c)
        peers = [((1 - x) if dx else x, (1 - y) if dy else y, (1 - c) if dc else c) for dx, dy, dc in flips]

        def copy(i, k):
            return pltpu.make_async_remote_copy(
                src_ref=ins[i].at[_slot(*peers[k])], dst_ref=outs[i].at[me], send_sem=send_sems.at[i, k],
                recv_sem=recv_sems.at[i, k], device_id=peers[k], device_id_type=MESH)

        def landing(i, k):
            rows = outs[i].at[_slot(*peers[k])]
            return pltpu.make_async_remote_copy(
                src_ref=rows, dst_ref=rows, send_sem=send_sems.at[i, k], recv_sem=recv_sems.at[i, k],
                device_id=peers[k], device_id_type=MESH)

        mine = [pltpu.make_async_copy(ins[i].at[me], outs[i].at[me], local_sems.at[i]) for i in range(n)]
        for cp in mine:
            cp.start()
        sends = [copy(i, k) for i in range(n) for k in range(len(flips))]
        for cp in sends:
            cp.start()
        for i in range(n):
            for k in range(len(flips)):
                landing(i, k).wait_recv()
        for cp in sends:
            cp.wait_send()
        for cp in mine:
            cp.wait()

    return pl.pallas_call(
        body, name=name, in_specs=[_ANY] * n, out_specs=[_ANY] * n,
        out_shape=[_sds(a.shape, a.dtype) for a in gs],
        scratch_shapes=[pltpu.SemaphoreType.DMA((n, 7)), pltpu.SemaphoreType.DMA((n, 7)), pltpu.SemaphoreType.DMA((n,))],
    )(*gs)


_HBM = pl.BlockSpec(memory_space=pltpu.HBM)
_SEM = pl.BlockSpec(memory_space=pltpu.SEMAPHORE)
_EFFECT = pltpu.SideEffectType.DATAFLOW_SIDE_EFFECTING
_CHIP_FLIPS = ((1, 0), (0, 1), (1, 1))
_TOKEN = (SUBLANE, LANE)


def _flip(v, f):
    return (1 - v) if f else v


def _in_hbm(a):
    return pltpu.with_memory_space_constraint(a, pltpu.HBM)


def _ag_peers(x, y, c):
    return [(x, y, 1 - c)] + [(_flip(x, dx), _flip(y, dy), c) for dx, dy in _CHIP_FLIPS]


def _ag_copies(x_refs, land_refs, send_sems, recv_sems, *, landing):
    x, y, c = _place()
    peers = _ag_peers(x, y, c)
    cps = []
    for i in range(len(x_refs)):
        for k, peer in enumerate(peers):
            origin = _slot(*peer) if landing else _slot(x, y, c)
            cps.append(pltpu.make_async_remote_copy(
                src_ref=x_refs[i], dst_ref=land_refs[i].at[origin], send_sem=send_sems.at[4 * i + k],
                recv_sem=recv_sems.at[4 * i + k], device_id=peer, device_id_type=MESH))
    return cps


OWN_BLOCK_BYTES = 4 * 1024 * 1024


def _place_own(x, me, *, name):
    r, c = x.shape
    tr = r if r * c * x.dtype.itemsize <= OWN_BLOCK_BYTES else r // 4

    def body(me_ref, x_ref, o_ref):
        o_ref[...] = x_ref[...]

    return pl.pallas_call(
        body, name=name,
        grid_spec=pltpu.PrefetchScalarGridSpec(
            num_scalar_prefetch=1, grid=(r // tr,),
            in_specs=[pl.BlockSpec((tr, c), lambda i, me_ref: (i, 0))],
            out_specs=pl.BlockSpec((None, tr, c), lambda i, me_ref: (me_ref[0], i, 0))),
        out_shape=_sds((N_DEV, r, c), x.dtype), compiler_params=_cparams("parallel"),
    )(me, x)


def _ag_start(xs, *, name, dep=None):
    n = len(xs)
    me = _slot(*_place()).astype(jnp.int32).reshape(1)
    lands = [_place_own(a, me, name=f"{name}_own_{i}") for i, a in enumerate(xs)]
    n_in = 2 * n + (0 if dep is None else 1)

    def body(*refs):
        x_refs, land_refs = refs[:n], refs[n:2 * n]
        send_sems, recv_sems = refs[n_in], refs[n_in + 1]
        token = refs[-1]
        for cp in _ag_copies(x_refs, land_refs, send_sems, recv_sems, landing=False):
            cp.start()
        token[...] = jnp.zeros_like(token)

    out = pl.pallas_call(
        body, name=name,
        out_shape=(pltpu.SemaphoreType.DMA((4 * n,)), pltpu.SemaphoreType.DMA((4 * n,)),
                   *[pltpu.HBM(a.shape, a.dtype) for a in xs], *[pltpu.HBM(a.shape, a.dtype) for a in lands],
                   _sds(_TOKEN, F32)),
        in_specs=[_HBM] * (2 * n) + ([] if dep is None else [_ANY]),
        out_specs=(_SEM, _SEM, *[_HBM] * (2 * n), pl.BlockSpec(memory_space=pltpu.VMEM)),
        input_output_aliases={i: 2 + i for i in range(2 * n)},
        compiler_params=pltpu.CompilerParams(has_side_effects=_EFFECT),
    )(*[_in_hbm(a) for a in xs], *[_in_hbm(a) for a in lands], *(() if dep is None else (dep,)))
    return (out[0], out[1], list(out[2:2 + n]), list(out[2 + n:2 + 2 * n])), out[-1]


def _ag_wait(started, after, *, name):
    send_sems, recv_sems, xs, lands = started
    n = len(xs)

    def body(*refs):
        x_refs, land_refs = refs[:n], refs[n:2 * n]
        for cp in _ag_copies(x_refs, land_refs, refs[2 * n], refs[2 * n + 1], landing=True):
            cp.wait_send()
            cp.wait_recv()

    out = pl.pallas_call(
        body, name=name,
        out_shape=tuple(pltpu.HBM(a.shape, a.dtype) for a in xs + lands),
        in_specs=[_HBM] * (2 * n) + [_SEM, _SEM, _ANY], out_specs=tuple([_HBM] * (2 * n)),
        input_output_aliases={i: i for i in range(2 * n)},
        compiler_params=pltpu.CompilerParams(has_side_effects=_EFFECT),
    )(*xs, *lands, send_sems, recv_sems, after)
    return list(out[:n]), list(out[n:])


def _ag_forward(lands, *, name):
    n = len(lands)

    def body(*refs):
        land = refs[n:2 * n]
        send_sems, recv_sems = refs[2 * n:]
        x, y, c = _place()
        sibling = (x, y, 1 - c)

        def copy(i, j, core):
            dx, dy = _CHIP_FLIPS[j]
            rows = land[i].at[_slot(_flip(x, dx), _flip(y, dy), core)]
            return pltpu.make_async_remote_copy(src_ref=rows, dst_ref=rows, send_sem=send_sems.at[i, j],
                                                recv_sem=recv_sems.at[i, j], device_id=sibling, device_id_type=MESH)

        sends = [copy(i, j, c) for i in range(n) for j in range(3)]
        for cp in sends:
            cp.start()
        for i in range(n):
            for j in range(3):
                copy(i, j, 1 - c).wait_recv()
        for cp in sends:
            cp.wait_send()

    return pl.pallas_call(
        body, name=name, in_specs=[_ANY] * n, out_specs=[_ANY] * n,
        out_shape=[_sds(a.shape, a.dtype) for a in lands], input_output_aliases={i: i for i in range(n)},
        scratch_shapes=[pltpu.SemaphoreType.DMA((n, 3)), pltpu.SemaphoreType.DMA((n, 3))],
    )(*lands)


def _rs_sibling(g4s, *, name):
    n = len(g4s)

    def body(*refs):
        ins, outs = refs[:n], refs[n:2 * n]
        send_sems, recv_sems = refs[2 * n:]
        x, y, c = _place()
        cps = [pltpu.make_async_remote_copy(
            src_ref=ins[i].at[:, 1 - c], dst_ref=outs[i], send_sem=send_sems.at[i], recv_sem=recv_sems.at[i],
            device_id=(x, y, 1 - c), device_id_type=MESH) for i in range(n)]
        for cp in cps:
            cp.start()
        for cp in cps:
            cp.wait_recv()
        for cp in cps:
            cp.wait_send()

    return pl.pallas_call(
        body, name=name, in_specs=[_ANY] * n, out_specs=[_ANY] * n,
        out_shape=[_sds((4,) + a.shape[2:], a.dtype) for a in g4s],
        scratch_shapes=[pltpu.SemaphoreType.DMA((n,)), pltpu.SemaphoreType.DMA((n,))],
    )(*g4s)


def _rs_rows(r):
    return 256 if r % 256 == 0 else 64


def _rs_pair_sum(g4, from_sibling, core, *, name):
    _, _, r, c = g4.shape
    tr = _rs_rows(r)

    def body(core_ref, g_ref, a_ref, o_ref):
        o_ref[...] = (g_ref[...].astype(F32) + a_ref[...].astype(F32)).astype(BF16)

    blk = pl.BlockSpec((None, tr, c), lambda k, i, core_ref: (k, i, 0))
    return pl.pallas_call(
        body, name=name,
        grid_spec=pltpu.PrefetchScalarGridSpec(
            num_scalar_prefetch=1, grid=(4, r // tr),
            in_specs=[pl.BlockSpec((None, None, tr, c), lambda k, i, core_ref: (k, core_ref[0], i, 0)), blk],
            out_specs=blk),
        out_shape=_sds((4, r, c), BF16), compiler_params=_cparams("parallel", "parallel"),
    )(core, g4, from_sibling)


def _rs_copies(h_refs, land_refs, send_sems, recv_sems):
    x, y, c = _place()
    cps = []
    for i in range(len(h_refs)):
        for k, (dx, dy) in enumerate(_CHIP_FLIPS):
            px, py = _flip(x, dx), _flip(y, dy)
            cps.append(pltpu.make_async_remote_copy(
                src_ref=h_refs[i].at[2 * px + py], dst_ref=land_refs[i].at[k], send_sem=send_sems.at[3 * i + k],
                recv_sem=recv_sems.at[3 * i + k], device_id=(px, py, c), device_id_type=MESH))
    return cps


def _rs_start(hs, *, name):
    n = len(hs)
    lands = [lax.empty((3,) + a.shape[1:], a.dtype) for a in hs]

    def body(*refs):
        h_refs, land_refs = refs[:n], refs[n:2 * n]
        token = refs[-1]
        for cp in _rs_copies(h_refs, land_refs, refs[2 * n], refs[2 * n + 1]):
            cp.start()
        token[...] = jnp.zeros_like(token)

    out = pl.pallas_call(
        body, name=name,
        out_shape=(pltpu.SemaphoreType.DMA((3 * n,)), pltpu.SemaphoreType.DMA((3 * n,)),
                   *[pltpu.HBM(a.shape, a.dtype) for a in hs], *[pltpu.HBM(a.shape, a.dtype) for a in lands],
                   _sds(_TOKEN, F32)),
        in_specs=[_HBM] * (2 * n),
        out_specs=(_SEM, _SEM, *[_HBM] * (2 * n), pl.BlockSpec(memory_space=pltpu.VMEM)),
        input_output_aliases={i: 2 + i for i in range(2 * n)},
        compiler_params=pltpu.CompilerParams(has_side_effects=_EFFECT),
    )(*[_in_hbm(a) for a in hs], *[_in_hbm(a) for a in lands])
    return (out[0], out[1], list(out[2:2 + n]), list(out[2 + n:2 + 2 * n])), out[-1]


def _rs_wait(started, after, *, name):
    send_sems, recv_sems, hs, lands = started
    n = len(hs)

    def body(*refs):
        for cp in _rs_copies(refs[:n], refs[n:2 * n], refs[2 * n], refs[2 * n + 1]):
            cp.wait_send()
            cp.wait_recv()

    out = pl.pallas_call(
        body, name=name,
        out_shape=tuple(pltpu.HBM(a.shape, a.dtype) for a in hs + lands),
        in_specs=[_HBM] * (2 * n) + [_SEM, _SEM, _ANY], out_specs=tuple([_HBM] * (2 * n)),
        input_output_aliases={i: i for i in range(2 * n)},
        compiler_params=pltpu.CompilerParams(has_side_effects=_EFFECT),
    )(*hs, *lands, send_sems, recv_sems, after)
    return list(out[:n]), list(out[n:])


def _adamw_chips(w, h, others, m, v, chip, *, name):
    r, c = w.shape

    def body(chip_ref, w_ref, h_ref, o_ref, m_ref, v_ref, g_ref, d_ref, nm_ref, nv_ref):
        g = h_ref[...].astype(F32)
        for k in range(3):
            g = g + o_ref[k].astype(F32)
        g_ref[...] = g
        d_ref[...], nm_ref[...], nv_ref[...] = _adamw_math(w_ref[...], g, m_ref[...], v_ref[...])

    blk = pl.BlockSpec((ADAM_ROWS, c), lambda i, chip_ref: (i, 0))
    return pl.pallas_call(
        body, name=name,
        grid_spec=pltpu.PrefetchScalarGridSpec(
            num_scalar_prefetch=1, grid=(r // ADAM_ROWS,),
            in_specs=[blk, pl.BlockSpec((None, ADAM_ROWS, c), lambda i, chip_ref: (chip_ref[0], i, 0)),
                      pl.BlockSpec((3, ADAM_ROWS, c), lambda i, chip_ref: (0, i, 0)), blk, blk],
            out_specs=[blk] * 4),
        out_shape=[_sds((r, c), F32)] * 4, compiler_params=_cparams("parallel"),
    )(chip, w, h, others, m, v)


def _adamw_chips_layers(w, hs, others, m, v, chip, *, name):
    nl, r, c = w.shape

    def body(chip_ref, w_ref, m_ref, v_ref, *rest):
        h_refs, o_refs = rest[:nl], rest[nl:2 * nl]
        g_ref, d_ref, nm_ref, nv_ref = rest[2 * nl:]
        layer = pl.program_id(0)
        g = jnp.zeros((ADAM_ROWS, c), F32)
        for k in range(nl):
            gk = h_refs[k][...].astype(F32)
            for j in range(3):
                gk = gk + o_refs[k][j].astype(F32)
            g = jnp.where(layer == k, gk, g)
        g_ref[...] = g
        d_ref[...], nm_ref[...], nv_ref[...] = _adamw_math(w_ref[...], g, m_ref[...], v_ref[...])

    def rows(k):
        return lambda l, i: jnp.where(l == k, i, 0)

    blk = pl.BlockSpec((None, ADAM_ROWS, c), lambda l, i, chip_ref: (l, i, 0))
    h_specs = [pl.BlockSpec((None, ADAM_ROWS, c), lambda l, i, chip_ref, f=rows(k): (chip_ref[0], f(l, i), 0))
               for k in range(nl)]
    o_specs = [pl.BlockSpec((3, ADAM_ROWS, c), lambda l, i, chip_ref, f=rows(k): (0, f(l, i), 0)) for k in range(nl)]
    return pl.pallas_call(
        body, name=name,
        grid_spec=pltpu.PrefetchScalarGridSpec(
            num_scalar_prefetch=1, grid=(nl, r // ADAM_ROWS),
            in_specs=[blk, blk, blk] + h_specs + o_specs, out_specs=[blk] * 4),
        out_shape=[_sds((nl, r, c), F32)] * 4, compiler_params=_cparams("arbitrary", "arbitrary"),
    )(chip, w, m, v, *hs, *others)


def _gather_begin(xs, tag, dep=None):
    return _ag_start(xs, name=f"ag_start_{tag}", dep=dep)


def _gather_end(started, after, tag):
    _, lands = _ag_wait(started, after, name=f"ag_wait_{tag}")
    return _ag_forward(lands, name=f"ag_forward_{tag}")


def _scatter_begin(gs, tag):
    core = lax.axis_index("c").astype(jnp.int32).reshape(1)
    g4s = [g.reshape((4, 2) + g.shape[1:]) for g in gs]
    got = _rs_sibling(g4s, name=f"rs_sibling_{tag}")
    hs = [_rs_pair_sum(g4, a, core, name=f"rs_pair_sum_{tag}_{i}") for i, (g4, a) in enumerate(zip(g4s, got))]
    return _rs_start(hs, name=f"rs_start_{tag}")


def _scatter_end(started, after, tag):
    return _rs_wait(started, after, name=f"rs_wait_{tag}")


MM_TM = 512


def _ffn_fwd(x_mid, g_row, wup_p, wdown, conv_w, conv_b, tag, dep=None):
    hb, _ = _rms_fwd(x_mid, g_row, want_f32=False, name=f"ffn_norm_{tag}", dep=dep)
    up = _mm_nn_pieces(hb, wup_p, tm=MM_TM, name=f"ffn_up_{tag}")
    a = _conv_gate_fwd(up, conv_w, conv_b, name=f"ffn_conv_{tag}")
    x_out = _mm_nn(a, wdown, tm=1024, tn=512, name=f"ffn_down_{tag}", res=x_mid)
    return x_out, (hb, up, a)


def _ffn_bwd(dx, x_mid, g_row, wup_p, wdown, conv_w, conv_b, saved, tag, scatter):
    hb, up, a = saved
    dxb = dx.astype(BF16)
    da = _mm_nt(dxb, wdown, tm=MM_TM, tn=1408, name=f"ffn_da_{tag}")
    dwdown = _mm_tn(a, dxb, tm=512, tn=1024, name=f"ffn_dwdown_{tag}")
    dup_v, dup_g, dconv_w, dconv_b = _conv_gate_bwd(up, da, conv_w, conv_b, name=f"ffn_dconv_{tag}")
    dup = jnp.concatenate([dup_v, dup_g], axis=1)
    dh = _mm_nt_pieces(dup, wup_p, tm=MM_TM, tn=1024, name=f"ffn_dh_{tag}")
    dwup = _mm_tn_pieces(hb, dup, pieces=N_DEV, tm=MM_TM, name=f"ffn_dwup_{tag}")
    started, token = scatter([dwup, dwdown.reshape(N_DEV, D_FF // N_DEV, D_MODEL)], f"ffn_{tag}")
    dx_mid, dg = _rms_bwd(x_mid, g_row, dh, dx, name=f"ffn_dnorm_{tag}", dep=token)
    return dx_mid, dg, dconv_w, dconv_b, started


def _pool_layer_fwd(x, g_row, w, b_row, sc_row, tag, dep=None):
    _, hf = _rms_fwd(x, g_row, want_f32=True, name=f"pool_norm_{tag}", dep=dep)
    return _pool_fwd(hf, x, w, b_row, sc_row, name=f"pool_fwd_{tag}"), (hf,)


def _pool_layer_bwd(dx_mid, x, g_row, w, b_row, sc_row, saved, tag):
    (hf,) = saved
    dh, dw, db, dsc = _pool_bwd(hf, dx_mid, w, b_row, sc_row, name=f"pool_bwd_{tag}")
    dx, dg = _rms_bwd(x, g_row, dh, dx_mid, name=f"pool_dnorm_{tag}")
    return dx, dg, dw, db, dsc


def _sb_layer_fwd(x, g_row, wqkv_p, gains, wo, tag):
    hb, _ = _rms_fwd(x, g_row, want_f32=False, name=f"sb_norm_{tag}")
    qkv = _mm_nn_pieces(hb, wqkv_p, tm=MM_TM, name=f"sb_qkv_{tag}")
    qkn = _qk_norm_fwd(qkv, gains, name=f"sb_qknorm_{tag}")
    vb = qkv[:, 2 * D_MODEL:].astype(BF16)
    o = _sb_fwd(qkn, vb, name=f"sb_att_{tag}")
    x_mid = _mm_nn(o, wo, tm=MM_TM, tn=512, name=f"sb_out_{tag}", res=x)
    return x_mid, (hb, qkv, qkn, vb, o)


def _sb_layer_bwd(dx_mid, x, g_row, wqkv_p, gains, wo, saved, tag, scatter):
    hb, qkv, qkn, vb, o = saved
    dmb = dx_mid.astype(BF16)
    do = _mm_nt(dmb, wo, tm=MM_TM, tn=512, name=f"sb_do_{tag}", out_dtype=BF16)
    dwo = _mm_tn(o, dmb, tm=512, tn=1024, name=f"sb_dwo_{tag}")
    dqn, dkn, dv = _sb_bwd(qkn, vb, do, name=f"sb_datt_{tag}")
    dq, dqg = _qk_norm_bwd(qkv, gains, dqn, which=0, name=f"sb_dqnorm_{tag}")
    dk, dkg = _qk_norm_bwd(qkv, gains, dkn, which=1, name=f"sb_dknorm_{tag}")
    dqkv = jnp.concatenate([dq, dk, dv.astype(BF16)], axis=1)
    dh = _mm_nt_pieces(dqkv, wqkv_p, tm=MM_TM, tn=1024, name=f"sb_dh_{tag}")
    dwqkv = _mm_tn_pieces(hb, dqkv, pieces=N_DEV, tm=MM_TM, name=f"sb_dwqkv_{tag}")
    started, token = scatter([dwqkv, dwo.reshape(N_DEV, D_MODEL // N_DEV, D_MODEL)], f"sb_{tag}")
    dx, dg = _rms_bwd(x, g_row, dh, dx_mid, name=f"sb_dnorm_{tag}", dep=token)
    return dx, dg, dqg, dkg, started


def _ssm_params(lam_re, lam_im, log_step, b_re, b_im):
    g, p = SSM_GROUPS, SSM_STATE
    return (lam_re.reshape(g, 1, p), lam_im.reshape(g, 1, p), log_step.reshape(g, 1, 1),
            jnp.transpose(b_re, (0, 2, 1)), jnp.transpose(b_im, (0, 2, 1)))


def _ssm_layer_fwd(x, g_row, raw, c_re, c_im, d_row, wglu_p, bglu_row, tag):
    g, p, ch = SSM_GROUPS, SSM_STATE, SSM_CH
    _, hf = _rms_fwd(x, g_row, want_f32=True, name=f"ssm_norm_{tag}")
    ar, ai, bbr, bbi = _ssm_prep_fwd(*raw, name=f"ssm_prep_{tag}")
    wb = jnp.concatenate([bbr.reshape(g * ch, p), bbi.reshape(g * ch, p)], axis=1)
    wc = jnp.concatenate([c_re.reshape(g * ch, p), -c_im.reshape(g * ch, p)], axis=1)
    a_re, a_im = ar.reshape(1, g * p), ai.reshape(1, g * p)
    ylin, yg = _ssm_core_fwd(hf, wb, wc, a_re, a_im, d_row, name=f"ssm_core_{tag}")
    x_mid, val, gate = _glu_fwd(yg, wglu_p, bglu_row, x, tm=MM_TM, name=f"ssm_glu_{tag}")
    return x_mid, (hf, wb, wc, a_re, a_im, ylin, yg, val, gate)


def _ssm_layer_bwd(dx_mid, x, g_row, raw, d_row, wglu_p, saved, tag, scatter):
    g, p, ch = SSM_GROUPS, SSM_STATE, SSM_CH
    hf, wb, wc, a_re, a_im, ylin, yg, val, gate = saved
    dgv, dbglu = _glu_bwd(dx_mid, val, gate, name=f"ssm_dglu_{tag}")
    dyg = _mm_nt_pieces(dgv, wglu_p, tm=MM_TM, tn=1024, name=f"ssm_dyg_{tag}")
    dwglu = _mm_tn_pieces(yg, dgv, pieces=N_DEV, tm=MM_TM, name=f"ssm_dwglu_{tag}")
    du, dwb, dwc, dar, dai, dd = _ssm_core_bwd(hf, ylin, dyg, wb, wc, a_re, a_im, d_row, name=f"ssm_dcore_{tag}")
    dc_re = dwc[:, :p].reshape(g, ch, p)
    dc_im = -dwc[:, p:].reshape(g, ch, p)
    dlr, dli, dls, dbtr, dbti = _ssm_prep_bwd(
        *raw, dar.reshape(g, 1, p), dai.reshape(g, 1, p), dwb[:, :p].reshape(g, ch, p), dwb[:, p:].reshape(g, ch, p),
        name=f"ssm_dprep_{tag}")
    started, token = scatter([dwglu], f"ssm_{tag}")
    dx, dg = _rms_bwd(x, g_row, du, dx_mid, name=f"ssm_dnorm_{tag}", dep=token)
    grads = dict(ssm_lam_re=dlr.reshape(1, g, p), ssm_lam_im=dli.reshape(1, g, p), ssm_log_step=dls.reshape(1, g),
                 ssm_b_re=jnp.transpose(dbtr, (0, 2, 1))[None], ssm_b_im=jnp.transpose(dbti, (0, 2, 1))[None],
                 ssm_c_re=dc_re[None], ssm_c_im=dc_im[None], ssm_d=dd, ssm_b_glu=dbglu)
    return dx, dg, grads, started


_WEIGHTS = ["norm_mix_g", "norm_ffn_g", "pool_w", "pool_b", "pool_scale", "sb_w_qkv", "sb_q_gain", "sb_k_gain", "sb_w_o",
            "ssm_lam_re", "ssm_lam_im", "ssm_log_step", "ssm_b_re", "ssm_b_im", "ssm_c_re", "ssm_c_im", "ssm_d",
            "ssm_w_glu", "ssm_b_glu", "ffn_w_up", "ffn_conv_w", "ffn_conv_b", "ffn_w_down"]
_INPUTS = ["x"] + _WEIGHTS + ["loss_target"] + ["m_" + n for n in _WEIGHTS] + ["v_" + n for n in _WEIGHTS]
_REPLICATED = ["norm_mix_g", "norm_ffn_g", "sb_q_gain", "sb_k_gain", "ssm_lam_re", "ssm_lam_im", "ssm_log_step",
               "ssm_b_re", "ssm_b_im", "ssm_c_re", "ssm_c_im", "ffn_conv_b"]
_SMALL_SHARDED = {"pool_b": 1, "pool_scale": 1, "ssm_d": 1, "ssm_b_glu": 1, "ffn_conv_w": 2}
_BIG = ["pool_w", "sb_w_qkv", "sb_w_o", "ssm_w_glu", "ffn_w_up", "ffn_w_down"]
PACK_COLS = 512


def _pack(arrays):
    flat = jnp.concatenate([a.reshape(-1).astype(F32) for a in arrays])
    rows = -(-flat.shape[0] // (PACK_COLS * PACK_ROWS)) * PACK_ROWS
    return jnp.pad(flat, (0, rows * PACK_COLS - flat.shape[0])).reshape(rows, PACK_COLS)


def _unpack(packed, shapes, lead=()):
    flat = packed.reshape(lead + (-1,))
    out, off = [], 0
    for shp in shapes:
        n = math.prod(shp)
        out.append(flat[..., off:off + n].reshape(lead + tuple(shp)))
        off += n
    return out


def _unshard(gathered, axis):
    g = jnp.moveaxis(gathered, 0, axis)
    shp = g.shape
    return g.reshape(shp[:axis] + (shp[axis] * shp[axis + 1],) + shp[axis + 2:])


def _step(p):
    s = p["x"].shape[1]
    x = p["x"].reshape(s, D_MODEL)
    me = _slot(*_place())

    small_local = [p[n] for n in _SMALL_SHARDED]
    pool_w_l = p["pool_w"].astype(BF16).reshape(-1, POOL_DIM)
    chip = (2 * lax.axis_index("x") + lax.axis_index("y")).astype(jnp.int32).reshape(1)

    def ffn_shards(i):
        return [p["ffn_w_up"][i].astype(BF16), p["ffn_w_down"][i].astype(BF16)]

    st_first, tok = _gather_begin([pool_w_l, _pack(small_local)], "first")
    st_ffn, tok_ffn = [None] * DEPTH, [None] * DEPTH
    st_ffn[0], tok = _gather_begin(ffn_shards(0), "ffn_0", dep=tok)
    st_mix, tok_ffn[0] = _gather_begin([p["sb_w_qkv"][0].astype(BF16), p["sb_w_o"][0].astype(BF16),
                                        p["ssm_w_glu"][0].astype(BF16)], "mixers", dep=tok)
    ag = _gather_end(st_first, tok_ffn[0], "first")
    n_pool = p["pool_w"].shape[0]
    pool_w = jnp.transpose(ag[0].reshape(N_DEV, n_pool, POOL_GROUPS, POOL_DIM // N_DEV, POOL_DIM), (1, 2, 0, 3, 4))
    pool_w = pool_w.reshape(n_pool, POOL_GROUPS, POOL_DIM, POOL_DIM)
    small_full = {}
    for n, g in zip(_SMALL_SHARDED, _unpack(ag[1], [a.shape for a in small_local], lead=(N_DEV,))):
        small_full[n] = _unshard(g, _SMALL_SHARDED[n])
    mix_w = {}
    wup_p, wdown = [None] * DEPTH, [None] * DEPTH

    gains = jnp.stack([p["sb_q_gain"][0], p["sb_k_gain"][0]])[:, None, :]
    ssm_raw = _ssm_params(p["ssm_lam_re"][0], p["ssm_lam_im"][0], p["ssm_log_step"][0], p["ssm_b_re"][0],
                          p["ssm_b_im"][0])

    def mixer_args(i):
        j = i // 3
        g_row = p["norm_mix_g"][i][None]
        if i % 3 == 0:
            return (g_row, pool_w[j], small_full["pool_b"][j][None], small_full["pool_scale"][j][None])
        if i % 3 == 1:
            return (g_row, mix_w["qkv"], gains, mix_w["o"])
        return (g_row, ssm_raw, p["ssm_c_re"][0], p["ssm_c_im"][0], small_full["ssm_d"], mix_w["glu"],
                small_full["ssm_b_glu"])

    def ffn_args(i):
        return (p["norm_ffn_g"][i][None], wup_p[i], wdown[i], small_full["ffn_conv_w"][i], p["ffn_conv_b"][i][None])

    xs_in, xs_mid, saved_mix, saved_ffn = [], [], [], []
    for i in range(DEPTH):
        xs_in.append(x)
        if i == 1:
            mix_w["qkv"], wo_g, mix_w["glu"] = _gather_end(st_mix, x, "mixers")
            mix_w["o"] = wo_g.reshape(D_MODEL, D_MODEL)
        fwd = (_pool_layer_fwd, _sb_layer_fwd, _ssm_layer_fwd)[i % 3]
        x, sv = fwd(x, *mixer_args(i), f"l{i}")
        saved_mix.append(sv)
        xs_mid.append(x)
        wup_p[i], wd_g = _gather_end(st_ffn[i], x, f"ffn_{i}")
        wdown[i] = wd_g.reshape(D_FF, D_MODEL)
        if i + 1 < DEPTH:
            st_ffn[i + 1], tok_ffn[i + 1] = _gather_begin(ffn_shards(i + 1), f"ffn_{i + 1}")
        x, sv = _ffn_fwd(x, *ffn_args(i), f"l{i}", dep=tok_ffn[i + 1] if i + 1 < DEPTH else None)
        saved_ffn.append(sv)
    dx, loss_part = _loss_head(x, p["loss_target"].reshape(s, D_MODEL), name="loss_head")

    grads = {}
    dg_mix, dg_ffn = [None] * DEPTH, [None] * DEPTH
    dconv_w, dconv_b = [None] * DEPTH, [None] * DEPTH
    dpool = {"w": {}, "b": {}, "scale": {}}
    out = {}

    def big_update(n, h, others, idx=None):
        w, m, v = (p[pre + n] if idx is None else p[pre + n][idx] for pre in ("", "m_", "v_"))
        cols = h.shape[-1]
        r = _adamw_chips(w.reshape(-1, cols), h, others, m.reshape(-1, cols), v.reshape(-1, cols), chip,
                         name=f"adamw_{n}" + ("" if idx is None else f"_{idx}"))
        return [a.reshape(w.shape) for a in r]

    kinds = ("grad", "delta", "new_m", "new_v")
    ffn_upd = {"ffn_w_up": [None] * DEPTH, "ffn_w_down": [None] * DEPTH}

    def finish(entry, after):
        names, idx, started, tag = entry
        hs, others = _scatter_end(started, after, tag)
        for n, h, o in zip(names, hs, others):
            if idx is None:
                for kind, a in zip(kinds, big_update(n, h, o)):
                    out[kind + "_" + n] = a
            else:
                ffn_upd[n][idx] = (h, o)

    pending = []
    for i in reversed(range(DEPTH)):
        dx, dg_ffn[i], dconv_w[i], dconv_b[i], started = _ffn_bwd(
            dx, xs_mid[i], *ffn_args(i), saved_ffn[i], f"l{i}", _scatter_begin)
        for entry in pending:
            finish(entry, dx)
        pending = [(("ffn_w_up", "ffn_w_down"), i, started, f"ffn_l{i}")]
        margs = mixer_args(i)
        if i % 3 == 0:
            j = i // 3
            dx, dg_mix[i], dpool["w"][j], dpool["b"][j], dpool["scale"][j] = _pool_layer_bwd(
                dx, xs_in[i], *margs, saved_mix[i], f"l{i}")
        elif i % 3 == 1:
            dx, dg_mix[i], dqg, dkg, started = _sb_layer_bwd(dx, xs_in[i], *margs, saved_mix[i], f"l{i}", _scatter_begin)
            grads["sb_q_gain"], grads["sb_k_gain"] = dqg, dkg
            pending.append((("sb_w_qkv", "sb_w_o"), None, started, f"sb_l{i}"))
        else:
            g_row, raw, _, _, d_row, wg, _ = margs
            dx, dg_mix[i], sg, started = _ssm_layer_bwd(dx, xs_in[i], g_row, raw, d_row, wg, saved_mix[i], f"l{i}",
                                                       _scatter_begin)
            grads.update(sg)
            pending.append((("ssm_w_glu",), None, started, f"ssm_l{i}"))
    grad_x = dx.reshape(1, s, D_MODEL)
    grads["norm_mix_g"] = jnp.concatenate(dg_mix, axis=0)
    grads["norm_ffn_g"] = jnp.concatenate(dg_ffn, axis=0)
    grads["ffn_conv_w"] = jnp.stack(dconv_w)
    grads["ffn_conv_b"] = jnp.concatenate(dconv_b, axis=0)
    grads["pool_b"] = jnp.concatenate([dpool["b"][j] for j in range(n_pool)], axis=0)
    grads["pool_scale"] = jnp.concatenate([dpool["scale"][j] for j in range(n_pool)], axis=0)
    dpw = jnp.stack([dpool["w"][j] for j in range(n_pool)])
    dpw = dpw.reshape(n_pool, POOL_GROUPS, N_DEV, POOL_DIM // N_DEV, POOL_DIM)
    started, _ = _scatter_begin([jnp.transpose(dpw, (2, 0, 1, 3, 4)).reshape(N_DEV, -1, POOL_DIM)], "pool")
    pending.append((("pool_w",), None, started, "pool"))

    small_names = _REPLICATED + list(_SMALL_SHARDED)
    full_shapes = [p[n].shape for n in _REPLICATED] + [small_full[n].shape for n in _SMALL_SHARDED]
    part = _pack([grads[n].reshape(shp) for n, shp in zip(small_names, full_shapes)] + [loss_part[0]])
    st_small, tok_small = _gather_begin([part], "small_grads")
    for entry in pending:
        finish(entry, tok_small)
    for n in ffn_upd:
        upd = _adamw_chips_layers(p[n], [ffn_upd[n][i][0] for i in range(DEPTH)], [ffn_upd[n][i][1] for i in range(DEPTH)],
                                  p["m_" + n], p["v_" + n], chip, name=f"adamw_{n}")
        for kind, a in zip(kinds, upd):
            out[kind + "_" + n] = a
    (parts,) = _gather_end(st_small, out["new_v_ffn_w_up"], "small_grads")
    summed = _unpack(_sum_parts(parts, name="sum_small_grads"), full_shapes + [(LANE,)])
    loss = summed[-1][0]
    small_g = {}
    for n, g in zip(small_names, summed[:-1]):
        if n in _SMALL_SHARDED:
            ax = _SMALL_SHARDED[n]
            g = lax.dynamic_slice_in_dim(g, me * p[n].shape[ax], p[n].shape[ax], axis=ax)
        small_g[n] = g

    local_shapes = [p[n].shape for n in small_names]
    packs = [_pack([p[pre + n] for n in small_names]) for pre in ("", "m_", "v_")]
    res = _adamw_flat(packs[0], _pack([small_g[n] for n in small_names]), packs[1], packs[2], name="adamw_small")
    for kind, packed in zip(("delta", "new_m", "new_v"), res):
        for n, a in zip(small_names, _unpack(packed, local_shapes)):
            out[kind + "_" + n] = a
    for n in small_names:
        out["grad_" + n] = small_g[n]

    return (loss, grad_x, *[out["grad_" + n] for n in _WEIGHTS], *[out["delta_" + n] for n in _WEIGHTS],
            *[out["new_m_" + n] for n in _WEIGHTS], *[out["new_v_" + n] for n in _WEIGHTS])


def kernel(x, norm_mix_g, norm_ffn_g, pool_w, pool_b, pool_scale, sb_w_qkv, sb_q_gain, sb_k_gain, sb_w_o, ssm_lam_re, ssm_lam_im, ssm_log_step, ssm_b_re, ssm_b_im, ssm_c_re, ssm_c_im, ssm_d, ssm_w_glu, ssm_b_glu, ffn_w_up, ffn_conv_w, ffn_conv_b, ffn_w_down, loss_target, m_norm_mix_g, m_norm_ffn_g, m_pool_w, m_pool_b, m_pool_scale, m_sb_w_qkv, m_sb_q_gain, m_sb_k_gain, m_sb_w_o, m_ssm_lam_re, m_ssm_lam_im, m_ssm_log_step, m_ssm_b_re, m_ssm_b_im, m_ssm_c_re, m_ssm_c_im, m_ssm_d, m_ssm_w_glu, m_ssm_b_glu, m_ffn_w_up, m_ffn_conv_w, m_ffn_conv_b, m_ffn_w_down, v_norm_mix_g, v_norm_ffn_g, v_pool_w, v_pool_b, v_pool_scale, v_sb_w_qkv, v_sb_q_gain, v_sb_k_gain, v_sb_w_o, v_ssm_lam_re, v_ssm_lam_im, v_ssm_log_step, v_ssm_b_re, v_ssm_b_im, v_ssm_c_re, v_ssm_c_im, v_ssm_d, v_ssm_w_glu, v_ssm_b_glu, v_ffn_w_up, v_ffn_conv_w, v_ffn_conv_b, v_ffn_w_down):
    args = (x, norm_mix_g, norm_ffn_g, pool_w, pool_b, pool_scale, sb_w_qkv, sb_q_gain, sb_k_gain, sb_w_o, ssm_lam_re, ssm_lam_im, ssm_log_step, ssm_b_re, ssm_b_im, ssm_c_re, ssm_c_im, ssm_d, ssm_w_glu, ssm_b_glu, ffn_w_up, ffn_conv_w, ffn_conv_b, ffn_w_down, loss_target, m_norm_mix_g, m_norm_ffn_g, m_pool_w, m_pool_b, m_pool_scale, m_sb_w_qkv, m_sb_q_gain, m_sb_k_gain, m_sb_w_o, m_ssm_lam_re, m_ssm_lam_im, m_ssm_log_step, m_ssm_b_re, m_ssm_b_im, m_ssm_c_re, m_ssm_c_im, m_ssm_d, m_ssm_w_glu, m_ssm_b_glu, m_ffn_w_up, m_ffn_conv_w, m_ffn_conv_b, m_ffn_w_down, v_norm_mix_g, v_norm_ffn_g, v_pool_w, v_pool_b, v_pool_scale, v_sb_w_qkv, v_sb_q_gain, v_sb_k_gain, v_sb_w_o, v_ssm_lam_re, v_ssm_lam_im, v_ssm_log_step, v_ssm_b_re, v_ssm_b_im, v_ssm_c_re, v_ssm_c_im, v_ssm_d, v_ssm_w_glu, v_ssm_b_glu, v_ffn_w_up, v_ffn_conv_w, v_ffn_conv_b, v_ffn_w_down)
    return _step(dict(zip(_INPUTS, args)))
```

```python
import functools
import math

import jax
import jax.numpy as jnp
from jax import lax
from jax.experimental import pallas as pl
from jax.experimental.pallas import tpu as pltpu

F32 = jnp.float32
BF16 = jnp.bfloat16

N_DEV = 8
D_MODEL = 2048
D_FF = 5632
DEPTH = 4
POOL_GROUPS = 4
POOL_DIM = 512
HEADS = 16
HEAD_DIM = 128
SSM_GROUPS = 128
SSM_CH = 16
SSM_STATE = 64
SSM_BLOCK_GROUPS = 8
SSM_BLOCK_LANES = SSM_BLOCK_GROUPS * SSM_STATE
RMS_EPS = 1e-6
ADAM_LR = 0.001
ADAM_B1 = 0.9
ADAM_B2 = 0.999
ADAM_EPS = 1e-08
ADAM_WD = 0.01
ADAM_STEP = 10

VMEM_LIMIT_BYTES = 56 * 1024 * 1024
LANE = 128
SUBLANE = 8
MESH = pl.DeviceIdType.MESH


def _cparams(*sem):
    return pltpu.CompilerParams(dimension_semantics=tuple(sem), vmem_limit_bytes=VMEM_LIMIT_BYTES)


def _sds(shape, dtype):
    return jax.ShapeDtypeStruct(tuple(shape), dtype)


def _mm(a, b, *, dims, grid, a_spec, b_spec, o_spec, out_shape, out_dtype, name, k_axis=None, acc_shape=None,
        res=None, res_spec=None, a_alt=None, b_alt=None, alt_axis=None, alt_from=None, dep=None):
    nk = grid[k_axis] if k_axis is not None else 1
    n_in = 2 + sum(e is not None for e in (res, a_alt, b_alt, dep))

    def body(*refs):
        a_ref, b_ref = refs[:2]
        rest = list(refs[2:n_in])
        r_ref = rest.pop(0) if res is not None else None
        a2_ref = rest.pop(0) if a_alt is not None else None
        b2_ref = rest.pop(0) if b_alt is not None else None
        o_ref = refs[n_in]
        scr = refs[n_in + 1:]
        av, bv = a_ref[...], b_ref[...]
        if a2_ref is not None:
            av = jnp.where(pl.program_id(alt_axis) >= alt_from, a2_ref[...], av)
        if b2_ref is not None:
            bv = jnp.where(pl.program_id(alt_axis) >= alt_from, b2_ref[...], bv)
        p = lax.dot_general(av, bv, (dims, ((), ())), preferred_element_type=F32)
        if k_axis is None:
            if r_ref is not None:
                p = p + r_ref[...]
            o_ref[...] = p.astype(o_ref.dtype)
        else:
            acc = scr[0]
            k = pl.program_id(k_axis)

            @pl.when(k == 0)
            def _():
                acc[...] = p

            @pl.when(k > 0)
            def _():
                acc[...] += p

            @pl.when(k == nk - 1)
            def _():
                r = acc[...]
                if r_ref is not None:
                    r = r + r_ref[...]
                o_ref[...] = r.astype(o_ref.dtype)

    sem = ["parallel"] * len(grid)
    if k_axis is not None:
        sem[k_axis] = "arbitrary"
    in_specs, args = [a_spec, b_spec], [a, b]
    if res is not None:
        in_specs.append(res_spec)
        args.append(res)
    for alt in (a_alt, b_alt):
        if alt is not None:
            args.append(alt[0])
            in_specs.append(alt[1])
    if dep is not None:
        args.append(dep)
        in_specs.append(pl.BlockSpec((SUBLANE, LANE), lambda *_: (0, 0)))
    scratch = [pltpu.VMEM(acc_shape, F32)] if k_axis is not None else []
    return pl.pallas_call(
        body, name=name, grid=grid, in_specs=in_specs, out_specs=o_spec, out_shape=_sds(out_shape, out_dtype),
        scratch_shapes=scratch, compiler_params=_cparams(*sem),
    )(*args)


NN = ((1,), (0,))
NT = ((1,), (1,))
TN = ((0,), (0,))


def _mm_nn_pieces(a, wp, *, tm, name, out_dtype=F32):
    s, k = a.shape
    tm = min(tm, s)
    p, _, c = wp.shape
    return _mm(a, wp, dims=NN, grid=(s // tm, p),
               a_spec=pl.BlockSpec((tm, k), lambda m, n: (m, 0)),
               b_spec=pl.BlockSpec((None, k, c), lambda m, n: (n, 0, 0)),
               o_spec=pl.BlockSpec((tm, c), lambda m, n: (m, n)),
               out_shape=(s, p * c), out_dtype=out_dtype, name=name)


def _mm_nt_pieces(a, wp, *, tm, tn, name, out_dtype=F32, dep=None):
    p, n, c = wp.shape
    halves = a if isinstance(a, tuple) else None
    a0 = halves[0] if halves else a
    s = a0.shape[0]
    tm = min(tm, s)
    h = p // 2
    alt = {}
    if halves:
        a_spec = pl.BlockSpec((tm, c), lambda m, j, k: (m, jnp.minimum(k, h - 1)))
        alt = dict(a_alt=(halves[1], pl.BlockSpec((tm, c), lambda m, j, k: (m, jnp.maximum(k - h, 0)))),
                   alt_axis=2, alt_from=h)
    else:
        a_spec = pl.BlockSpec((tm, c), lambda m, j, k: (m, k))
    return _mm(a0, wp, dims=NT, grid=(s // tm, n // tn, p), k_axis=2, acc_shape=(tm, tn), a_spec=a_spec,
               b_spec=pl.BlockSpec((None, tn, c), lambda m, j, k: (k, j, 0)),
               o_spec=pl.BlockSpec((tm, tn), lambda m, j, k: (m, j)),
               out_shape=(s, n), out_dtype=out_dtype, name=name, dep=dep, **alt)


def _mm_tn_pieces(a, g, *, pieces, tm, name, out_dtype=BF16):
    s, m = a.shape
    halves = g if isinstance(g, tuple) else None
    g0 = halves[0] if halves else g
    h = pieces // 2
    c = g0.shape[1] // (h if halves else pieces)
    alt = {}
    if halves:
        b_spec = pl.BlockSpec((s, c), lambda n, i: (0, jnp.minimum(n, h - 1)))
        alt = dict(b_alt=(halves[1], pl.BlockSpec((s, c), lambda n, i: (0, jnp.maximum(n - h, 0)))),
                   alt_axis=0, alt_from=h)
    else:
        b_spec = pl.BlockSpec((s, c), lambda n, i: (0, n))
    return _mm(a, g0, dims=TN, grid=(pieces, m // tm), a_spec=pl.BlockSpec((s, tm), lambda n, i: (0, i)),
               b_spec=b_spec, o_spec=pl.BlockSpec((None, tm, c), lambda n, i: (n, i, 0)),
               out_shape=(pieces, m, c), out_dtype=out_dtype, name=name, **alt)


def _mm_nn(a, w, *, tm, tn, name, out_dtype=F32, res=None):
    s, k = a.shape
    tm = min(tm, s)
    n = w.shape[1]
    return _mm(a, w, dims=NN, grid=(s // tm, n // tn),
               a_spec=pl.BlockSpec((tm, k), lambda m, j: (m, 0)),
               b_spec=pl.BlockSpec((k, tn), lambda m, j: (0, j)),
               o_spec=pl.BlockSpec((tm, tn), lambda m, j: (m, j)),
               res=res, res_spec=pl.BlockSpec((tm, tn), lambda m, j: (m, j)),
               out_shape=(s, n), out_dtype=out_dtype, name=name)


def _mm_nt(a, w, *, tm, tn, name, out_dtype=F32):
    s, k = a.shape
    tm = min(tm, s)
    n = w.shape[0]
    return _mm(a, w, dims=NT, grid=(s // tm, n // tn),
               a_spec=pl.BlockSpec((tm, k), lambda m, j: (m, 0)),
               b_spec=pl.BlockSpec((tn, k), lambda m, j: (j, 0)),
               o_spec=pl.BlockSpec((tm, tn), lambda m, j: (m, j)),
               out_shape=(s, n), out_dtype=out_dtype, name=name)


def _mm_tn(a, g, *, tm, tn, name, out_dtype=BF16):
    s, m = a.shape
    n = g.shape[1]
    return _mm(a, g, dims=TN, grid=(m // tm, n // tn),
               a_spec=pl.BlockSpec((s, tm), lambda i, j: (0, i)),
               b_spec=pl.BlockSpec((s, tn), lambda i, j: (0, j)),
               o_spec=pl.BlockSpec((tm, tn), lambda i, j: (i, j)),
               out_shape=(m, n), out_dtype=out_dtype, name=name)


ROW_TILE = 256


def _dep_spec():
    return pl.BlockSpec((SUBLANE, LANE), lambda i: (0, 0))


def _rms_fwd(x, g_row, *, want_f32, name, dep=None):
    s, d = x.shape
    n_in = 2 if dep is None else 3

    def body(*refs):
        x_ref, g_ref = refs[:2]
        outs = refs[n_in:]
        xv = x_ref[...]
        r = lax.rsqrt(jnp.mean(xv * xv, axis=-1, keepdims=True) + RMS_EPS)
        h = (xv * r) * g_ref[...]
        outs[0][...] = h.astype(BF16)
        if want_f32:
            outs[1][...] = h

    row = pl.BlockSpec((ROW_TILE, d), lambda i: (i, 0))
    out_shape = [_sds((s, d), BF16)] + ([_sds((s, d), F32)] if want_f32 else [])
    out = pl.pallas_call(
        body, name=name, grid=(s // ROW_TILE,),
        in_specs=[row, pl.BlockSpec((1, d), lambda i: (0, 0))] + ([] if dep is None else [_dep_spec()]),
        out_specs=[row] * len(out_shape), out_shape=out_shape, compiler_params=_cparams("parallel"),
    )(x, g_row, *(() if dep is None else (dep,)))
    return out if want_f32 else (out[0], None)


def _rms_bwd(x, g_row, dh, dres, *, name, dep=None):
    s, d = x.shape

    def body(x_ref, g_ref, dh_ref, dres_ref, *rest):
        dx_ref, dg_ref = rest[-2:]
        xv = x_ref[...]
        r = lax.rsqrt(jnp.mean(xv * xv, axis=-1, keepdims=True) + RMS_EPS)
        xn = xv * r
        dhv = dh_ref[...]
        dxn = dhv * g_ref[...]
        dx_ref[...] = dres_ref[...] + r * (dxn - xn * jnp.mean(dxn * xn, axis=-1, keepdims=True))
        part = jnp.sum(dhv * xn, axis=0, keepdims=True)

        @pl.when(pl.program_id(0) == 0)
        def _():
            dg_ref[...] = part

        @pl.when(pl.program_id(0) > 0)
        def _():
            dg_ref[...] += part

    row = pl.BlockSpec((ROW_TILE, d), lambda i: (i, 0))
    vec = pl.BlockSpec((1, d), lambda i: (0, 0))
    return pl.pallas_call(
        body, name=name, grid=(s // ROW_TILE,), in_specs=[row, vec, row, row] + ([] if dep is None else [_dep_spec()]),
        out_specs=[row, vec], out_shape=[_sds((s, d), F32), _sds((1, d), F32)], compiler_params=_cparams("arbitrary"),
    )(x, g_row, dh, dres, *(() if dep is None else (dep,)))


def _shift_down(v, k):
    row = lax.broadcasted_iota(jnp.int32, v.shape, 0)
    return jnp.where(row >= k, pltpu.roll(v, k, 0), 0.0)


def _shift_up(v, k):
    n = v.shape[0]
    row = lax.broadcasted_iota(jnp.int32, v.shape, 0)
    return jnp.where(row < n - k, pltpu.roll(v, n - k, 0), 0.0)


def _sigmoid(z):
    return 1.0 / (1.0 + jnp.exp(-z))


FF_COL_TILE = 256


def _conv3(u, w, b):
    return b + w[0:1, :] * _shift_down(u, 2) + w[1:2, :] * _shift_down(u, 1) + w[2:3, :] * u


def _conv_gate_fwd(up, conv_w, conv_b, *, name):
    s = up.shape[0]
    f = up.shape[1] // 2
    nt = f // FF_COL_TILE

    def body(uv_ref, ug_ref, wv_ref, wg_ref, bv_ref, bg_ref, a_ref):
        vc = _conv3(uv_ref[...], wv_ref[...], bv_ref[...])
        gc = _conv3(ug_ref[...], wg_ref[...], bg_ref[...])
        a_ref[...] = ((gc * _sigmoid(gc)) * vc).astype(BF16)

    def col(rows, off):
        return pl.BlockSpec((rows, FF_COL_TILE), lambda n: (0, n + off))

    return pl.pallas_call(
        body, name=name, grid=(nt,),
        in_specs=[col(s, 0), col(s, nt), col(3, 0), col(3, nt), col(1, 0), col(1, nt)],
        out_specs=col(s, 0), out_shape=_sds((s, f), BF16), compiler_params=_cparams("parallel"),
    )(up, up, conv_w, conv_w, conv_b, conv_b)


def _conv_gate_bwd(up, da, conv_w, conv_b, *, name):
    s = up.shape[0]
    f = up.shape[1] // 2
    nt = f // FF_COL_TILE

    def conv_bwd(u, w, dc):
        d0 = _shift_up(dc, 2)
        d1 = _shift_up(dc, 1)
        dup = w[0:1, :] * d0 + w[1:2, :] * d1 + w[2:3, :] * dc
        dw = jnp.concatenate([jnp.sum(u * d0, axis=0, keepdims=True), jnp.sum(u * d1, axis=0, keepdims=True),
                              jnp.sum(u * dc, axis=0, keepdims=True)], axis=0)
        return dup, dw, jnp.sum(dc, axis=0, keepdims=True)

    def body(uv_ref, ug_ref, da_ref, wv_ref, wg_ref, bv_ref, bg_ref,
             duv_ref, dug_ref, dwv_ref, dwg_ref, dbv_ref, dbg_ref):
        uv = uv_ref[...]
        ug = ug_ref[...]
        vc = _conv3(uv, wv_ref[...], bv_ref[...])
        gc = _conv3(ug, wg_ref[...], bg_ref[...])
        sg = _sigmoid(gc)
        dav = da_ref[...]
        dvc = dav * (gc * sg)
        dgc = dav * vc * (sg * (1.0 + gc * (1.0 - sg)))
        dup, dw, db = conv_bwd(uv, wv_ref[...], dvc)
        duv_ref[...] = dup.astype(BF16)
        dwv_ref[...] = dw
        dbv_ref[...] = db
        dup, dw, db = conv_bwd(ug, wg_ref[...], dgc)
        dug_ref[...] = dup.astype(BF16)
        dwg_ref[...] = dw
        dbg_ref[...] = db

    def col(rows, off):
        return pl.BlockSpec((rows, FF_COL_TILE), lambda n: (0, n + off))

    dup_v, dup_g, dw_v, dw_g, db_v, db_g = pl.pallas_call(
        body, name=name, grid=(nt,),
        in_specs=[col(s, 0), col(s, nt), col(s, 0), col(3, 0), col(3, nt), col(1, 0), col(1, nt)],
        out_specs=[col(s, 0), col(s, 0), col(3, 0), col(3, 0), col(1, 0), col(1, 0)],
        out_shape=[_sds((s, f), BF16), _sds((s, f), BF16), _sds((3, f), F32), _sds((3, f), F32),
                   _sds((1, f), F32), _sds((1, f), F32)],
        compiler_params=_cparams("parallel"),
    )(up, up, da, conv_w, conv_w, conv_b, conv_b)
    return dup_v, dup_g, jnp.concatenate([dw_v, dw_g], axis=1), jnp.concatenate([db_v, db_g], axis=1)


def _pool_counts(shape, g):
    win = jnp.left_shift(jnp.int32(2), g)
    t = lax.broadcasted_iota(jnp.int32, shape, 0)
    return win, jnp.minimum(t + 1, win).astype(F32)


def _window_sum(v, g, shift):
    for k in range(POOL_GROUPS):
        v = jnp.where(g >= k, v + shift(v, 1 << k), v)
    return v


def _pool_fwd(hf, x, w, b, scale, *, name):
    s, d = hf.shape

    def body(h_ref, x_ref, w_ref, b_ref, sc_ref, o_ref):
        g = pl.program_id(0)
        h = h_ref[...]
        _, cnt = _pool_counts(h.shape, g)
        pooled = _window_sum(h, g, _shift_down) / cnt - h
        y = jnp.dot(pooled.astype(BF16), w_ref[...], preferred_element_type=F32) + b_ref[...]
        o_ref[...] = x_ref[...] + y * sc_ref[...]

    col = pl.BlockSpec((s, POOL_DIM), lambda g: (0, g))
    vec = pl.BlockSpec((1, POOL_DIM), lambda g: (0, g))
    return pl.pallas_call(
        body, name=name, grid=(POOL_GROUPS,),
        in_specs=[col, col, pl.BlockSpec((None, POOL_DIM, POOL_DIM), lambda g: (g, 0, 0)), vec, vec],
        out_specs=col, out_shape=_sds((s, d), F32), compiler_params=_cparams("parallel"),
    )(hf, x, w, b, scale)


def _pool_bwd(hf, dm, w, b, scale, *, name):
    s, d = hf.shape

    def body(h_ref, dm_ref, w_ref, b_ref, sc_ref, dh_ref, dw_ref, db_ref, dsc_ref):
        g = pl.program_id(0)
        h = h_ref[...]
        _, cnt = _pool_counts(h.shape, g)
        pooled = (_window_sum(h, g, _shift_down) / cnt - h).astype(BF16)
        wv = w_ref[...]
        y = jnp.dot(pooled, wv, preferred_element_type=F32) + b_ref[...]
        dmv = dm_ref[...]
        dsc_ref[...] = jnp.sum(dmv * y, axis=0, keepdims=True)
        dy = dmv * sc_ref[...]
        db_ref[...] = jnp.sum(dy, axis=0, keepdims=True)
        dyb = dy.astype(BF16)
        dw_ref[...] = lax.dot_general(pooled, dyb, (TN, ((), ())), preferred_element_type=F32).astype(BF16)
        dp = lax.dot_general(dyb, wv, (NT, ((), ())), preferred_element_type=F32)
        dh_ref[...] = _window_sum(dp / cnt, g, _shift_up) - dp

    col = pl.BlockSpec((s, POOL_DIM), lambda g: (0, g))
    vec = pl.BlockSpec((1, POOL_DIM), lambda g: (0, g))
    mat = pl.BlockSpec((None, POOL_DIM, POOL_DIM), lambda g: (g, 0, 0))
    return pl.pallas_call(
        body, name=name, grid=(POOL_GROUPS,), in_specs=[col, col, mat, vec, vec], out_specs=[col, mat, vec, vec],
        out_shape=[_sds((s, d), F32), _sds((POOL_GROUPS, POOL_DIM, POOL_DIM), BF16), _sds((1, d), F32),
                   _sds((1, d), F32)],
        compiler_params=_cparams("parallel"),
    )(hf, dm, w, b, scale)


ATT_TQ = 256
ATT_TK = 256


def _qk_norm_fwd(qkv, gains, *, name):
    s = qkv.shape[0]

    def body(x_ref, g_ref, o_ref):
        xv = x_ref[...]
        r = lax.rsqrt(jnp.mean(xv * xv, axis=-1, keepdims=True) + RMS_EPS)
        o_ref[...] = ((xv * r) * g_ref[...]).astype(BF16)

    blk = pl.BlockSpec((s, HEAD_DIM), lambda hd: (0, hd))
    return pl.pallas_call(
        body, name=name, grid=(2 * HEADS,),
        in_specs=[blk, pl.BlockSpec((None, 1, HEAD_DIM), lambda hd: (hd // HEADS, 0, 0))],
        out_specs=blk, out_shape=_sds((s, 2 * HEADS * HEAD_DIM), BF16), compiler_params=_cparams("parallel"),
    )(qkv, gains)


def _qk_norm_bwd(qkv, gains, dn, *, which, name):
    s = qkv.shape[0]

    def body(x_ref, g_ref, dn_ref, dx_ref, dg_ref):
        xv = x_ref[...]
        r = lax.rsqrt(jnp.mean(xv * xv, axis=-1, keepdims=True) + RMS_EPS)
        xn = xv * r
        dnv = dn_ref[...]
        dxn = dnv * g_ref[...]
        dx_ref[...] = (r * (dxn - xn * jnp.mean(dxn * xn, axis=-1, keepdims=True))).astype(BF16)
        part = jnp.sum(dnv * xn, axis=0, keepdims=True)

        @pl.when(pl.program_id(0) == 0)
        def _():
            dg_ref[...] = part

        @pl.when(pl.program_id(0) > 0)
        def _():
            dg_ref[...] += part

    blk = pl.BlockSpec((s, HEAD_DIM), lambda hd: (0, hd))
    return pl.pallas_call(
        body, name=name, grid=(HEADS,),
        in_specs=[pl.BlockSpec((s, HEAD_DIM), lambda hd: (0, hd + which * HEADS)),
                  pl.BlockSpec((None, 1, HEAD_DIM), lambda hd: (which, 0, 0)), blk],
        out_specs=[blk, pl.BlockSpec((1, HEAD_DIM), lambda hd: (0, 0))],
        out_shape=[_sds((s, HEADS * HEAD_DIM), BF16), _sds((1, HEAD_DIM), F32)],
        compiler_params=_cparams("arbitrary"),
    )(qkv, gains, dn)


def _split_dot(v, tri):
    hi = v.astype(BF16)
    lo = (v - hi.astype(F32)).astype(BF16)
    return (jnp.dot(hi, tri, preferred_element_type=F32) + jnp.dot(lo, tri, preferred_element_type=F32))


def _causal_mask(qi, j):
    tpos = qi * ATT_TQ + lax.broadcasted_iota(jnp.int32, (ATT_TQ, ATT_TK), 0)
    spos = j * ATT_TK + lax.broadcasted_iota(jnp.int32, (ATT_TQ, ATT_TK), 1)
    return spos < tpos


def _att_tile(q, kj, qi, j):
    z = lax.dot_general(q, kj, (NT, ((), ())), preferred_element_type=F32) * (1.0 / math.sqrt(HEAD_DIM))
    mask = _causal_mask(qi, j)
    lb = jnp.minimum(z, 0.0) - jnp.log1p(jnp.exp(-jnp.abs(z)))
    l1m = jnp.where(mask, lb - z, 0.0)
    return lb, l1m, mask


def _tri(rel):
    r = lax.broadcasted_iota(jnp.int32, (ATT_TK, ATT_TK), 0)
    c = lax.broadcasted_iota(jnp.int32, (ATT_TK, ATT_TK), 1)
    return jnp.where(rel(r, c), 1.0, 0.0).astype(BF16)


def _sb_fwd(qkn, vb, *, name):
    s = vb.shape[0]

    def body(q_ref, k_ref, v_ref, o_ref):
        qi = pl.program_id(1)
        q = q_ref[...]
        after = _tri(lambda r, c: r > c)

        def step(t, carry):
            acc, run = carry
            j = qi - t
            rows = pl.ds(pl.multiple_of(j * ATT_TK, ATT_TK), ATT_TK)
            lb, l1m, mask = _att_tile(q, k_ref[rows, :], qi, j)
            remain = _split_dot(l1m, after) + run
            attn = jnp.where(mask, jnp.exp(lb + remain), 0.0)
            acc = acc + jnp.dot(attn.astype(BF16), v_ref[rows, :], preferred_element_type=F32)
            return acc, run + jnp.sum(l1m, axis=1, keepdims=True)

        acc, _ = lax.fori_loop(0, qi + 1, step, (jnp.zeros((ATT_TQ, HEAD_DIM), F32), jnp.zeros((ATT_TQ, 1), F32)))
        o_ref[...] = acc.astype(BF16)

    return pl.pallas_call(
        body, name=name, grid=(HEADS, s // ATT_TQ),
        in_specs=[pl.BlockSpec((ATT_TQ, HEAD_DIM), lambda hd, i: (i, hd)),
                  pl.BlockSpec((s, HEAD_DIM), lambda hd, i: (0, hd + HEADS)),
                  pl.BlockSpec((s, HEAD_DIM), lambda hd, i: (0, hd))],
        out_specs=pl.BlockSpec((ATT_TQ, HEAD_DIM), lambda hd, i: (i, hd)),
        out_shape=_sds((s, HEADS * HEAD_DIM), BF16), compiler_params=_cparams("parallel", "parallel"),
    )(qkn, qkn, vb)


def _sb_bwd(qkn, vb, dob, *, name):
    s = vb.shape[0]
    nkb = s // ATT_TK

    def body(q_ref, k_ref, v_ref, do_ref, dq_ref, dk_ref, dv_ref, a_buf, sig_buf):
        qi = pl.program_id(1)
        q = q_ref[...]
        do = do_ref[...]
        after = _tri(lambda r, c: r > c)
        before = _tri(lambda r, c: r < c)

        @pl.when(qi == 0)
        def _():
            dk_ref[...] = jnp.zeros_like(dk_ref)
            dv_ref[...] = jnp.zeros_like(dv_ref)

        def down(t, run):
            j = qi - t
            rows = pl.ds(pl.multiple_of(j * ATT_TK, ATT_TK), ATT_TK)
            lb, l1m, mask = _att_tile(q, k_ref[rows, :], qi, j)
            remain = _split_dot(l1m, after) + run
            a_buf[j] = jnp.where(mask, jnp.exp(lb + remain), 0.0)
            sig_buf[j] = jnp.exp(lb)
            return run + jnp.sum(l1m, axis=1, keepdims=True)

        lax.fori_loop(0, qi + 1, down, jnp.zeros((ATT_TQ, 1), F32))

        def up(j, carry):
            dq, run = carry
            rows = pl.ds(pl.multiple_of(j * ATT_TK, ATT_TK), ATT_TK)
            a = a_buf[j]
            sig = sig_buf[j]
            mask = _causal_mask(qi, j)
            da = lax.dot_general(do, v_ref[rows, :], (NT, ((), ())), preferred_element_type=F32)
            p = a * da
            c = _split_dot(p, before) + run
            dz = jnp.where(mask, p * (1.0 - sig) - c * sig, 0.0) * (1.0 / math.sqrt(HEAD_DIM))
            dzb = dz.astype(BF16)
            dq = dq + jnp.dot(dzb, k_ref[rows, :], preferred_element_type=F32)
            dk_ref[rows, :] += lax.dot_general(dzb, q, (TN, ((), ())), preferred_element_type=F32)
            dv_ref[rows, :] += lax.dot_general(a.astype(BF16), do, (TN, ((), ())), preferred_element_type=F32)
            return dq, run + jnp.sum(p, axis=1, keepdims=True)

        dq, _ = lax.fori_loop(0, qi + 1, up, (jnp.zeros((ATT_TQ, HEAD_DIM), F32), jnp.zeros((ATT_TQ, 1), F32)))
        dq_ref[...] = dq

    qblk = pl.BlockSpec((ATT_TQ, HEAD_DIM), lambda hd, i: (i, hd))
    full = pl.BlockSpec((s, HEAD_DIM), lambda hd, i: (0, hd))
    return pl.pallas_call(
        body, name=name, grid=(HEADS, s // ATT_TQ),
        in_specs=[qblk, pl.BlockSpec((s, HEAD_DIM), lambda hd, i: (0, hd + HEADS)), full, qblk],
        out_specs=[qblk, full, full],
        out_shape=[_sds((s, HEADS * HEAD_DIM), F32)] * 3,
        scratch_shapes=[pltpu.VMEM((nkb, ATT_TQ, ATT_TK), F32), pltpu.VMEM((nkb, ATT_TQ, ATT_TK), F32)],
        compiler_params=_cparams("parallel", "arbitrary"),
    )(qkn, qkn, vb, dob)


def _ssm_discretize(lam_re, lam_im, log_step, bt_re, bt_im):
    step = jnp.exp(log_step)
    mag = jnp.exp(lam_re * step)
    lb_re = mag * jnp.cos(lam_im * step)
    lb_im = mag * jnp.sin(lam_im * step)
    den = lam_re * lam_re + lam_im * lam_im
    f_re = ((lb_re - 1.0) * lam_re + lb_im * lam_im) / den
    f_im = (lb_im * lam_re - (lb_re - 1.0) * lam_im) / den
    return lb_re, lb_im, f_re * bt_re - f_im * bt_im, f_re * bt_im + f_im * bt_re


_SSM_LAM = (SSM_GROUPS, 1, SSM_STATE)
_SSM_STEP = (SSM_GROUPS, 1, 1)
_SSM_BT = (SSM_GROUPS, SSM_CH, SSM_STATE)


def _ssm_prep_fwd(lam_re, lam_im, log_step, bt_re, bt_im, *, name):
    def body(lr, li, ls, br, bi, o_ar, o_ai, o_br, o_bi):
        o_ar[...], o_ai[...], o_br[...], o_bi[...] = _ssm_discretize(lr[...], li[...], ls[...], br[...], bi[...])

    return pl.pallas_call(
        body, name=name, out_shape=[_sds(_SSM_LAM, F32), _sds(_SSM_LAM, F32), _sds(_SSM_BT, F32), _sds(_SSM_BT, F32)],
    )(lam_re, lam_im, log_step, bt_re, bt_im)


def _ssm_prep_bwd(lam_re, lam_im, log_step, bt_re, bt_im, d_ar, d_ai, d_br, d_bi, *, name):
    def body(lr, li, ls, br, bi, g_ar, g_ai, g_br, g_bi, o_lr, o_li, o_ls, o_br, o_bi):
        _, vjp = jax.vjp(_ssm_discretize, lr[...], li[...], ls[...], br[...], bi[...])
        o_lr[...], o_li[...], o_ls[...], o_br[...], o_bi[...] = vjp((g_ar[...], g_ai[...], g_br[...], g_bi[...]))

    return pl.pallas_call(
        body, name=name,
        out_shape=[_sds(_SSM_LAM, F32), _sds(_SSM_LAM, F32), _sds(_SSM_STEP, F32), _sds(_SSM_BT, F32), _sds(_SSM_BT, F32)],
    )(lam_re, lam_im, log_step, bt_re, bt_im, d_ar, d_ai, d_br, d_bi)


def _bd_masks():
    rowg = lax.broadcasted_iota(jnp.int32, (LANE, LANE), 0) // SSM_CH
    low = lax.broadcasted_iota(jnp.int32, (LANE, LANE), 1) < SSM_STATE
    return rowg, low


def _bd_expand(w):
    rowg, low = _bd_masks()
    high = jnp.logical_not(low)
    wr = pltpu.roll(w, SSM_STATE, 1)
    re = [jnp.where((rowg == 2 * k) & low, w, 0.0) + jnp.where((rowg == 2 * k + 1) & high, wr, 0.0) for k in range(4)]
    im = [jnp.where((rowg == 2 * k) & low, wr, 0.0) + jnp.where((rowg == 2 * k + 1) & high, w, 0.0) for k in range(4)]
    return jnp.concatenate(re + im, axis=1)


def _bd_extract(dbd):
    rowg, low = _bd_masks()
    high = jnp.logical_not(low)
    acc = jnp.zeros((LANE, LANE), F32)
    for k in range(4):
        c = dbd[:, LANE * k:LANE * (k + 1)]
        acc = acc + jnp.where((rowg == 2 * k) & low, c, 0.0) + jnp.where((rowg == 2 * k + 1) & low, pltpu.roll(c, SSM_STATE, 1), 0.0)
        c = dbd[:, LANE * (4 + k):LANE * (5 + k)]
        acc = acc + jnp.where((rowg == 2 * k) & high, pltpu.roll(c, SSM_STATE, 1), 0.0) + jnp.where((rowg == 2 * k + 1) & high, c, 0.0)
    return acc


def _cmul(ar, ai, br, bi):
    return ar * br - ai * bi, ar * bi + ai * br


def _scan_rows(xr, xi, ar, ai, *, reverse):
    n = xr.shape[0] // SUBLANE
    lanes = xr.shape[1]
    row = lax.broadcasted_iota(jnp.int32, (SUBLANE, lanes), 0)
    powers = [(ar, ai)]
    for _ in range(SUBLANE - 1):
        powers.append(_cmul(*powers[-1], ar, ai))
    pr = jnp.zeros((SUBLANE, lanes), F32)
    pi = jnp.zeros((SUBLANE, lanes), F32)
    for r in range(SUBLANE):
        e = (SUBLANE - 1 - r) if reverse else r
        pr = jnp.where(row == r, powers[e][0], pr)
        pi = jnp.where(row == r, powers[e][1], pi)

    def shift(v, d):
        if reverse:
            return jnp.where(row < SUBLANE - d, pltpu.roll(v, SUBLANE - d, 0), 0.0)
        return jnp.where(row >= d, pltpu.roll(v, d, 0), 0.0)

    def body(i, carry):
        cr, ci = carry
        g = (n - 1 - i) if reverse else i
        rows = pl.ds(pl.multiple_of(g * SUBLANE, SUBLANE), SUBLANE)
        br = xr[rows, :]
        bi = xi[rows, :]
        for d in (1, 2, 4):
            qr, qi = powers[d - 1]
            sr = shift(br, d)
            si = shift(bi, d)
            br, bi = br + qr * sr - qi * si, bi + qr * si + qi * sr
        br, bi = br + pr * cr - pi * ci, bi + pr * ci + pi * cr
        xr[rows, :] = br
        xi[rows, :] = bi
        edge = 0 if reverse else SUBLANE - 1
        return br[edge:edge + 1, :], bi[edge:edge + 1, :]

    zero = jnp.zeros((1, lanes), F32)
    lax.fori_loop(0, n, body, (zero, zero), unroll=2)


_GELU_C = math.sqrt(2.0 / math.pi)
_GELU_A = 0.044715


def _gelu(v):
    return 0.5 * v * (1.0 + jnp.tanh(_GELU_C * (v + _GELU_A * v * v * v)))


def _gelu_grad(v):
    t = jnp.tanh(_GELU_C * (v + _GELU_A * v * v * v))
    return 0.5 * (1.0 + t) + 0.5 * v * (1.0 - t * t) * (_GELU_C * (1.0 + 3.0 * _GELU_A * v * v))


def _ssm_states(u_b16, eb, ar, ai, xr, xi):
    nl = SSM_BLOCK_LANES
    xr[...] = jnp.dot(u_b16, eb[:, :nl], preferred_element_type=F32)
    xi[...] = jnp.dot(u_b16, eb[:, nl:], preferred_element_type=F32)
    _scan_rows(xr, xi, ar, ai, reverse=False)


def _ssm_specs(s):
    col = pl.BlockSpec((s, LANE), lambda b: (0, b))
    wsm = pl.BlockSpec((LANE, LANE), lambda b: (b, 0))
    lam = pl.BlockSpec((1, SSM_BLOCK_LANES), lambda b: (0, b))
    vec = pl.BlockSpec((1, LANE), lambda b: (0, b))
    return col, wsm, lam, vec


def _ssm_core_fwd(u, wb, wc, a_re, a_im, d_row, *, name):
    s, d = u.shape
    nl = SSM_BLOCK_LANES

    def body(u_ref, wb_ref, wc_ref, ar_ref, ai_ref, d_ref, y_ref, yg_ref, xr, xi):
        uv = u_ref[...]
        eb = _bd_expand(wb_ref[...]).astype(BF16)
        ec = _bd_expand(wc_ref[...]).astype(BF16)
        _ssm_states(uv.astype(BF16), eb, ar_ref[...], ai_ref[...], xr, xi)
        y = (lax.dot_general(xr[...].astype(BF16), ec[:, :nl], (NT, ((), ())), preferred_element_type=F32)
             + lax.dot_general(xi[...].astype(BF16), ec[:, nl:], (NT, ((), ())), preferred_element_type=F32)
             + d_ref[...] * uv)
        y_ref[...] = y
        yg_ref[...] = _gelu(y).astype(BF16)

    col, wsm, lam, vec = _ssm_specs(s)
    return pl.pallas_call(
        body, name=name, grid=(d // LANE,), in_specs=[col, wsm, wsm, lam, lam, vec], out_specs=[col, col],
        out_shape=[_sds((s, d), F32), _sds((s, d), BF16)],
        scratch_shapes=[pltpu.VMEM((s, nl), F32), pltpu.VMEM((s, nl), F32)],
        compiler_params=_cparams("parallel"),
    )(u, wb, wc, a_re, a_im, d_row)


def _ssm_core_bwd(u, ylin, dyg, wb, wc, a_re, a_im, d_row, *, name):
    s, d = u.shape
    nl = SSM_BLOCK_LANES
    n8 = s // SUBLANE

    def body(u_ref, y_ref, dyg_ref, wb_ref, wc_ref, ar_ref, ai_ref, d_ref,
             du_ref, dwb_ref, dwc_ref, dar_ref, dai_ref, dd_ref, xr, xi, gr, gi):
        uv = u_ref[...]
        ub = uv.astype(BF16)
        ar = ar_ref[...]
        ai = ai_ref[...]
        dy = dyg_ref[...] * _gelu_grad(y_ref[...])
        dd_ref[...] = jnp.sum(dy * uv, axis=0, keepdims=True)
        dyb = dy.astype(BF16)
        eb = _bd_expand(wb_ref[...]).astype(BF16)
        ec = _bd_expand(wc_ref[...]).astype(BF16)
        _ssm_states(ub, eb, ar, ai, xr, xi)
        dec = jnp.concatenate(
            [lax.dot_general(dyb, xr[...].astype(BF16), (TN, ((), ())), preferred_element_type=F32),
             lax.dot_general(dyb, xi[...].astype(BF16), (TN, ((), ())), preferred_element_type=F32)], axis=1)
        dwc_ref[...] = _bd_extract(dec)
        gr[...] = jnp.dot(dyb, ec[:, :nl], preferred_element_type=F32)
        gi[...] = jnp.dot(dyb, ec[:, nl:], preferred_element_type=F32)
        _scan_rows(gr, gi, ar, -ai, reverse=True)

        row = lax.broadcasted_iota(jnp.int32, (SUBLANE, nl), 0)

        def lam_grad(i, acc):
            acc_r, acc_i = acc
            rows = pl.ds(pl.multiple_of(i * SUBLANE, SUBLANE), SUBLANE)
            prev = pl.ds(pl.multiple_of(jnp.maximum(i - 1, 0) * SUBLANE, SUBLANE), SUBLANE)
            keep = jnp.where(i > 0, 1.0, 0.0)
            xpr = jnp.where(row == 0, pltpu.roll(xr[prev, :], 1, 0) * keep, pltpu.roll(xr[rows, :], 1, 0))
            xpi = jnp.where(row == 0, pltpu.roll(xi[prev, :], 1, 0) * keep, pltpu.roll(xi[rows, :], 1, 0))
            g_r = gr[rows, :]
            g_i = gi[rows, :]
            return acc_r + g_r * xpr + g_i * xpi, acc_i + g_i * xpr - g_r * xpi

        zero = jnp.zeros((SUBLANE, nl), F32)
        acc_r, acc_i = lax.fori_loop(0, n8, lam_grad, (zero, zero), unroll=2)
        dar_ref[...] = jnp.sum(acc_r, axis=0, keepdims=True)
        dai_ref[...] = jnp.sum(acc_i, axis=0, keepdims=True)

        grb = gr[...].astype(BF16)
        gib = gi[...].astype(BF16)
        deb = jnp.concatenate([lax.dot_general(ub, grb, (TN, ((), ())), preferred_element_type=F32),
                               lax.dot_general(ub, gib, (TN, ((), ())), preferred_element_type=F32)], axis=1)
        dwb_ref[...] = _bd_extract(deb)
        du_ref[...] = (lax.dot_general(grb, eb[:, :nl], (NT, ((), ())), preferred_element_type=F32)
                       + lax.dot_general(gib, eb[:, nl:], (NT, ((), ())), preferred_element_type=F32)
                       + d_ref[...] * dy)

    col, wsm, lam, vec = _ssm_specs(s)
    return pl.pallas_call(
        body, name=name, grid=(d // LANE,), in_specs=[col, col, col, wsm, wsm, lam, lam, vec],
        out_specs=[col, wsm, wsm, lam, lam, vec],
        out_shape=[_sds((s, d), F32), _sds((d, LANE), F32), _sds((d, LANE), F32),
                   _sds((1, SSM_GROUPS * SSM_STATE), F32), _sds((1, SSM_GROUPS * SSM_STATE), F32), _sds((1, d), F32)],
        scratch_shapes=[pltpu.VMEM((s, nl), F32)] * 4,
        compiler_params=_cparams("parallel"),
    )(u, ylin, dyg, wb, wc, a_re, a_im, d_row)


GLU_PIECE = 512


def _glu_fwd(yg, wp, b_row, x, *, tm, name):
    s, d = yg.shape
    tm = min(tm, s)
    half = N_DEV // 2

    def body(y_ref, wv_ref, wg_ref, bv_ref, bg_ref, x_ref, o_ref, val_ref, gate_ref):
        yv = y_ref[...]
        val = jnp.dot(yv, wv_ref[...], preferred_element_type=F32) + bv_ref[...]
        gate = jnp.dot(yv, wg_ref[...], preferred_element_type=F32) + bg_ref[...]
        val_ref[...] = val
        gate_ref[...] = gate
        o_ref[...] = x_ref[...] + val * _sigmoid(gate)

    blk = pl.BlockSpec((tm, GLU_PIECE), lambda m, n: (m, n))
    return pl.pallas_call(
        body, name=name, grid=(s // tm, half),
        in_specs=[pl.BlockSpec((tm, d), lambda m, n: (m, 0)),
                  pl.BlockSpec((None, d, GLU_PIECE), lambda m, n: (n, 0, 0)),
                  pl.BlockSpec((None, d, GLU_PIECE), lambda m, n: (n + half, 0, 0)),
                  pl.BlockSpec((1, GLU_PIECE), lambda m, n: (0, n)),
                  pl.BlockSpec((1, GLU_PIECE), lambda m, n: (0, n + half)), blk],
        out_specs=[blk, blk, blk], out_shape=[_sds((s, d), F32)] * 3,
        compiler_params=_cparams("parallel", "parallel"),
    )(yg, wp, wp, b_row, b_row, x)


def _glu_bwd(dout, val, gate, *, name):
    s, d = dout.shape

    def body(do_ref, val_ref, gate_ref, dgv_ref, db_ref):
        sg = _sigmoid(gate_ref[...])
        dov = do_ref[...]
        dgv = jnp.concatenate([dov * sg, dov * val_ref[...] * (sg * (1.0 - sg))], axis=1)
        dgv_ref[...] = dgv.astype(BF16)
        part = jnp.sum(dgv, axis=0, keepdims=True)

        @pl.when(pl.program_id(0) == 0)
        def _():
            db_ref[...] = part

        @pl.when(pl.program_id(0) > 0)
        def _():
            db_ref[...] += part

    row = pl.BlockSpec((ROW_TILE, d), lambda i: (i, 0))
    return pl.pallas_call(
        body, name=name, grid=(s // ROW_TILE,), in_specs=[row, row, row],
        out_specs=[pl.BlockSpec((ROW_TILE, 2 * d), lambda i: (i, 0)), pl.BlockSpec((1, 2 * d), lambda i: (0, 0))],
        out_shape=[_sds((s, 2 * d), BF16), _sds((1, 2 * d), F32)], compiler_params=_cparams("arbitrary"),
    )(dout, val, gate)


def _loss_head(y, target, *, name):
    s, d = y.shape

    def body(y_ref, t_ref, dy_ref, l_ref):
        e = y_ref[...] - t_ref[...]
        dy_ref[...] = e * (1.0 / d)
        part = jnp.zeros((SUBLANE, LANE), F32) + jnp.sum(e * e) * (0.5 / d)

        @pl.when(pl.program_id(0) == 0)
        def _():
            l_ref[...] = part

        @pl.when(pl.program_id(0) > 0)
        def _():
            l_ref[...] += part

    row = pl.BlockSpec((ROW_TILE, d), lambda i: (i, 0))
    return pl.pallas_call(
        body, name=name, grid=(s // ROW_TILE,), in_specs=[row, row],
        out_specs=[row, pl.BlockSpec((SUBLANE, LANE), lambda i: (0, 0))],
        out_shape=[_sds((s, d), F32), _sds((SUBLANE, LANE), F32)], compiler_params=_cparams("arbitrary"),
    )(y, target)


def _adamw_math(w, g, m, v):
    m = ADAM_B1 * m + (1.0 - ADAM_B1) * g
    v = ADAM_B2 * v + (1.0 - ADAM_B2) * (g * g)
    m_hat = m / (1.0 - ADAM_B1 ** ADAM_STEP)
    v_hat = v / (1.0 - ADAM_B2 ** ADAM_STEP)
    return -ADAM_LR * (m_hat / (jnp.sqrt(v_hat) + ADAM_EPS) + ADAM_WD * w), m, v


ADAM_ROWS = 64
PACK_ROWS = 64


def _sum_pieces(p_ref):
    g = p_ref[0].astype(F32)
    for k in range(1, N_DEV):
        g = g + p_ref[k].astype(F32)
    return g


def _adamw_pieces(w, pieces, m, v, *, name):
    r, c = w.shape

    def body(w_ref, p_ref, m_ref, v_ref, g_ref, d_ref, nm_ref, nv_ref):
        g = _sum_pieces(p_ref)
        g_ref[...] = g
        d_ref[...], nm_ref[...], nv_ref[...] = _adamw_math(w_ref[...], g, m_ref[...], v_ref[...])

    blk = pl.BlockSpec((ADAM_ROWS, c), lambda i: (i, 0))
    return pl.pallas_call(
        body, name=name, grid=(r // ADAM_ROWS,),
        in_specs=[blk, pl.BlockSpec((N_DEV, ADAM_ROWS, c), lambda i: (0, i, 0)), blk, blk],
        out_specs=[blk] * 4, out_shape=[_sds((r, c), F32)] * 4, compiler_params=_cparams("parallel"),
    )(w, pieces, m, v)


def _sum_parts(parts, *, name):
    _, r, c = parts.shape

    def body(p_ref, o_ref):
        o_ref[...] = _sum_pieces(p_ref)

    return pl.pallas_call(
        body, name=name, grid=(r // PACK_ROWS,),
        in_specs=[pl.BlockSpec((N_DEV, PACK_ROWS, c), lambda i: (0, i, 0))],
        out_specs=pl.BlockSpec((PACK_ROWS, c), lambda i: (i, 0)), out_shape=_sds((r, c), F32),
        compiler_params=_cparams("parallel"),
    )(parts)


def _adamw_flat(w, g, m, v, *, name):
    r, c = w.shape

    def body(w_ref, g_ref, m_ref, v_ref, d_ref, nm_ref, nv_ref):
        d_ref[...], nm_ref[...], nv_ref[...] = _adamw_math(w_ref[...], g_ref[...], m_ref[...], v_ref[...])

    blk = pl.BlockSpec((PACK_ROWS, c), lambda i: (i, 0))
    return pl.pallas_call(
        body, name=name, grid=(r // PACK_ROWS,), in_specs=[blk] * 4, out_specs=[blk] * 3,
        out_shape=[_sds((r, c), F32)] * 3, compiler_params=_cparams("parallel"),
    )(w, g, m, v)


_ANY = pl.BlockSpec(memory_space=pl.ANY)


def _place():
    return lax.axis_index("x"), lax.axis_index("y"), lax.axis_index("c")


def _slot(px, py, pc):
    return 4 * px + 2 * py + pc


def _all_gather(xs, *, name):
    n = len(xs)

    def body(*refs):
        ins, outs = refs[:n], refs[n:2 * n]
        send_sems, recv_sems, local_sems = refs[2 * n:]
        x, y, c = _place()
        me, sibling = (x, y, c), (x, y, 1 - c)
        chips = [(1 - x, y), (x, 1 - y), (1 - x, 1 - y)]

        def copy(i, k, block, to, src=None):
            rows = outs[i].at[_slot(*block)]
            return pltpu.make_async_remote_copy(
                src_ref=rows if src is None else src, dst_ref=rows, send_sem=send_sems.at[i, k],
                recv_sem=recv_sems.at[i, k], device_id=to, device_id_type=MESH)

        mine = [pltpu.make_async_copy(ins[i], outs[i].at[_slot(*me)], local_sems.at[i]) for i in range(n)]
        for cp in mine:
            cp.start()
        first = []
        for i in range(n):
            first.append(copy(i, 0, me, sibling, src=ins[i]))
            first += [copy(i, 1 + j, me, (*chip, c), src=ins[i]) for j, chip in enumerate(chips)]
        for cp in first:
            cp.start()
        passed = []
        for j, chip in enumerate(chips):
            for i in range(n):
                copy(i, 1 + j, (*chip, c), me).wait_recv()
                cp = copy(i, 4 + j, (*chip, c), sibling)
                cp.start()
                passed.append(cp)
        for i in range(n):
            copy(i, 0, sibling, me).wait_recv()
            for j, chip in enumerate(chips):
                copy(i, 4 + j, (*chip, 1 - c), me).wait_recv()
        for cp in first + passed:
            cp.wait_send()
        for cp in mine:
            cp.wait()

    return pl.pallas_call(
        body, name=name, in_specs=[_ANY] * n, out_specs=[_ANY] * n,
        out_shape=[_sds((N_DEV,) + a.shape, a.dtype) for a in xs],
        scratch_shapes=[pltpu.SemaphoreType.DMA((n, 7)), pltpu.SemaphoreType.DMA((n, 7)), pltpu.SemaphoreType.DMA((n,))],
    )(*xs)


def _exchange_pieces(gs, *, name):
    n = len(gs)
    flips = [(dx, dy, dc) for dx in (0, 1) for dy in (0, 1) for dc in (0, 1)][1:]

    def body(*refs):
        ins, outs = refs[:n], refs[n:2 * n]
        send_sems, recv_sems, local_sems = refs[2 * n:]
        x, y, c = _place()
        me = _slot(x, y, c)
        peers = [((1 - x) if dx else x, (1 - y) if dy else y, (1 - c) if dc else c) for dx, dy, dc in flips]

        def copy(i, k):
            return pltpu.make_async_remote_copy(
                src_ref=ins[i].at[_slot(*peers[k])], dst_ref=outs[i].at[me], send_sem=send_sems.at[i, k],
                recv_sem=recv_sems.at[i, k], device_id=peers[k], device_id_type=MESH)

        def landing(i, k):
            rows = outs[i].at[_slot(*peers[k])]
            return pltpu.make_async_remote_copy(
                src_ref=rows, dst_ref=rows, send_sem=send_sems.at[i, k], recv_sem=recv_sems.at[i, k],
                device_id=peers[k], device_id_type=MESH)

        mine = [pltpu.make_async_copy(ins[i].at[me], outs[i].at[me], local_sems.at[i]) for i in range(n)]
        for cp in mine:
            cp.start()
        sends = [copy(i, k) for i in range(n) for k in range(len(flips))]
        for cp in sends:
            cp.start()
        for i in range(n):
            for k in range(len(flips)):
                landing(i, k).wait_recv()
        for cp in sends:
            cp.wait_send()
        for cp in mine:
            cp.wait()

    return pl.pallas_call(
        body, name=name, in_specs=[_ANY] * n, out_specs=[_ANY] * n,
        out_shape=[_sds(a.shape, a.dtype) for a in gs],
        scratch_shapes=[pltpu.SemaphoreType.DMA((n, 7)), pltpu.SemaphoreType.DMA((n, 7)), pltpu.SemaphoreType.DMA((n,))],
    )(*gs)


_HBM = pl.BlockSpec(memory_space=pltpu.HBM)
_SEM = pl.BlockSpec(memory_space=pltpu.SEMAPHORE)
_EFFECT = pltpu.SideEffectType.DATAFLOW_SIDE_EFFECTING
_CHIP_FLIPS = ((1, 0), (0, 1), (1, 1))
_TOKEN = (SUBLANE, LANE)


def _flip(v, f):
    return (1 - v) if f else v


def _in_hbm(a):
    return pltpu.with_memory_space_constraint(a, pltpu.HBM)


def _ag_peers(x, y, c):
    return [(x, y, 1 - c)] + [(_flip(x, dx), _flip(y, dy), c) for dx, dy in _CHIP_FLIPS]


def _ag_copies(x_refs, land_refs, send_sems, recv_sems, *, landing):
    x, y, c = _place()
    peers = _ag_peers(x, y, c)
    cps = []
    for i in range(len(x_refs)):
        for k, peer in enumerate(peers):
            origin = _slot(*peer) if landing else _slot(x, y, c)
            cps.append(pltpu.make_async_remote_copy(
                src_ref=x_refs[i], dst_ref=land_refs[i].at[origin], send_sem=send_sems.at[4 * i + k],
                recv_sem=recv_sems.at[4 * i + k], device_id=peer, device_id_type=MESH))
    return cps


OWN_BLOCK_BYTES = 4 * 1024 * 1024


def _place_own(x, me, *, name):
    r, c = x.shape
    tr = r if r * c * x.dtype.itemsize <= OWN_BLOCK_BYTES else r // 4

    def body(me_ref, x_ref, o_ref):
        o_ref[...] = x_ref[...]

    return pl.pallas_call(
        body, name=name,
        grid_spec=pltpu.PrefetchScalarGridSpec(
            num_scalar_prefetch=1, grid=(r // tr,),
            in_specs=[pl.BlockSpec((tr, c), lambda i, me_ref: (i, 0))],
            out_specs=pl.BlockSpec((None, tr, c), lambda i, me_ref: (me_ref[0], i, 0))),
        out_shape=_sds((N_DEV, r, c), x.dtype), compiler_params=_cparams("parallel"),
    )(me, x)


def _ag_start(xs, *, name, dep=None):
    n = len(xs)
    me = _slot(*_place()).astype(jnp.int32).reshape(1)
    lands = [_place_own(a, me, name=f"{name}_own_{i}") for i, a in enumerate(xs)]
    n_in = 2 * n + (0 if dep is None else 1)

    def body(*refs):
        x_refs, land_refs = refs[:n], refs[n:2 * n]
        send_sems, recv_sems = refs[n_in], refs[n_in + 1]
        token = refs[-1]
        for cp in _ag_copies(x_refs, land_refs, send_sems, recv_sems, landing=False):
            cp.start()
        token[...] = jnp.zeros_like(token)

    out = pl.pallas_call(
        body, name=name,
        out_shape=(pltpu.SemaphoreType.DMA((4 * n,)), pltpu.SemaphoreType.DMA((4 * n,)),
                   *[pltpu.HBM(a.shape, a.dtype) for a in xs], *[pltpu.HBM(a.shape, a.dtype) for a in lands],
                   _sds(_TOKEN, F32)),
        in_specs=[_HBM] * (2 * n) + ([] if dep is None else [_ANY]),
        out_specs=(_SEM, _SEM, *[_HBM] * (2 * n), pl.BlockSpec(memory_space=pltpu.VMEM)),
        input_output_aliases={i: 2 + i for i in range(2 * n)},
        compiler_params=pltpu.CompilerParams(has_side_effects=_EFFECT),
    )(*[_in_hbm(a) for a in xs], *[_in_hbm(a) for a in lands], *(() if dep is None else (dep,)))
    return (out[0], out[1], list(out[2:2 + n]), list(out[2 + n:2 + 2 * n])), out[-1]


def _ag_wait(started, after, *, name):
    send_sems, recv_sems, xs, lands = started
    n = len(xs)

    def body(*refs):
        x_refs, land_refs = refs[:n], refs[n:2 * n]
        for cp in _ag_copies(x_refs, land_refs, refs[2 * n], refs[2 * n + 1], landing=True):
            cp.wait_send()
            cp.wait_recv()

    out = pl.pallas_call(
        body, name=name,
        out_shape=tuple(pltpu.HBM(a.shape, a.dtype) for a in xs + lands),
        in_specs=[_HBM] * (2 * n) + [_SEM, _SEM, _ANY], out_specs=tuple([_HBM] * (2 * n)),
        input_output_aliases={i: i for i in range(2 * n)},
        compiler_params=pltpu.CompilerParams(has_side_effects=_EFFECT),
    )(*xs, *lands, send_sems, recv_sems, after)
    return list(out[:n]), list(out[n:])


def _ag_forward(lands, *, name):
    n = len(lands)

    def body(*refs):
        land = refs[n:2 * n]
        send_sems, recv_sems = refs[2 * n:]
        x, y, c = _place()
        sibling = (x, y, 1 - c)

        def copy(i, j, core):
            dx, dy = _CHIP_FLIPS[j]
            rows = land[i].at[_slot(_flip(x, dx), _flip(y, dy), core)]
            return pltpu.make_async_remote_copy(src_ref=rows, dst_ref=rows, send_sem=send_sems.at[i, j],
                                                recv_sem=recv_sems.at[i, j], device_id=sibling, device_id_type=MESH)

        sends = [copy(i, j, c) for i in range(n) for j in range(3)]
        for cp in sends:
            cp.start()
        for i in range(n):
            for j in range(3):
                copy(i, j, 1 - c).wait_recv()
        for cp in sends:
            cp.wait_send()

    return pl.pallas_call(
        body, name=name, in_specs=[_ANY] * n, out_specs=[_ANY] * n,
        out_shape=[_sds(a.shape, a.dtype) for a in lands], input_output_aliases={i: i for i in range(n)},
        scratch_shapes=[pltpu.SemaphoreType.DMA((n, 3)), pltpu.SemaphoreType.DMA((n, 3))],
    )(*lands)


def _sib_copies(g_refs, land_refs, send_sems, recv_sems):
    x, y, c = _place()
    return [pltpu.make_async_remote_copy(
        src_ref=g_refs[i].at[:, 1 - c], dst_ref=land_refs[i], send_sem=send_sems.at[i], recv_sem=recv_sems.at[i],
        device_id=(x, y, 1 - c), device_id_type=MESH) for i in range(len(g_refs))]


def _sib_start(g4s, *, name):
    n = len(g4s)
    lands = [lax.empty((4,) + a.shape[2:], a.dtype) for a in g4s]

    def body(*refs):
        for cp in _sib_copies(refs[:n], refs[n:2 * n], refs[2 * n], refs[2 * n + 1]):
            cp.start()
        refs[-1][...] = jnp.zeros_like(refs[-1])

    out = pl.pallas_call(
        body, name=name,
        out_shape=(pltpu.SemaphoreType.DMA((n,)), pltpu.SemaphoreType.DMA((n,)),
                   *[pltpu.HBM(a.shape, a.dtype) for a in g4s], *[pltpu.HBM(a.shape, a.dtype) for a in lands],
                   _sds(_TOKEN, F32)),
        in_specs=[_HBM] * (2 * n),
        out_specs=(_SEM, _SEM, *[_HBM] * (2 * n), pl.BlockSpec(memory_space=pltpu.VMEM)),
        input_output_aliases={i: 2 + i for i in range(2 * n)},
        compiler_params=pltpu.CompilerParams(has_side_effects=_EFFECT),
    )(*[_in_hbm(a) for a in g4s], *[_in_hbm(a) for a in lands])
    return (out[0], out[1], list(out[2:2 + n]), list(out[2 + n:2 + 2 * n])), out[-1]


def _sib_wait(started, after, *, name):
    send_sems, recv_sems, g4s, lands = started
    n = len(g4s)

    def body(*refs):
        for cp in _sib_copies(refs[:n], refs[n:2 * n], refs[2 * n], refs[2 * n + 1]):
            cp.wait_send()
            cp.wait_recv()

    out = pl.pallas_call(
        body, name=name,
        out_shape=tuple(pltpu.HBM(a.shape, a.dtype) for a in g4s + lands),
        in_specs=[_HBM] * (2 * n) + [_SEM, _SEM, _ANY], out_specs=tuple([_HBM] * (2 * n)),
        input_output_aliases={i: i for i in range(2 * n)},
        compiler_params=pltpu.CompilerParams(has_side_effects=_EFFECT),
    )(*g4s, *lands, send_sems, recv_sems, after)
    return list(out[:n]), list(out[n:])


PAIR_SUM_BLOCK_BYTES = 3 * 1024 * 1024


def _rs_rows(r, c):
    tr = r
    while tr * c * 2 > PAIR_SUM_BLOCK_BYTES and tr % 2 == 0:
        tr //= 2
    return tr


def _rs_pair_sum(g4, from_sibling, core, *, name):
    _, _, r, c = g4.shape
    tr = _rs_rows(r, c)

    def body(core_ref, g_ref, a_ref, o_ref):
        o_ref[...] = (g_ref[...].astype(F32) + a_ref[...].astype(F32)).astype(BF16)

    blk = pl.BlockSpec((None, tr, c), lambda k, i, core_ref: (k, i, 0))
    return pl.pallas_call(
        body, name=name,
        grid_spec=pltpu.PrefetchScalarGridSpec(
            num_scalar_prefetch=1, grid=(4, r // tr),
            in_specs=[pl.BlockSpec((None, None, tr, c), lambda k, i, core_ref: (k, core_ref[0], i, 0)), blk],
            out_specs=blk),
        out_shape=_sds((4, r, c), BF16), compiler_params=_cparams("parallel", "parallel"),
    )(core, g4, from_sibling)


def _rs_copies(h_refs, land_refs, send_sems, recv_sems):
    x, y, c = _place()
    cps = []
    for i in range(len(h_refs)):
        for k, (dx, dy) in enumerate(_CHIP_FLIPS):
            px, py = _flip(x, dx), _flip(y, dy)
            cps.append(pltpu.make_async_remote_copy(
                src_ref=h_refs[i].at[2 * px + py], dst_ref=land_refs[i].at[k], send_sem=send_sems.at[3 * i + k],
                recv_sem=recv_sems.at[3 * i + k], device_id=(px, py, c), device_id_type=MESH))
    return cps


def _rs_start(hs, *, name):
    n = len(hs)
    lands = [lax.empty((3,) + a.shape[1:], a.dtype) for a in hs]

    def body(*refs):
        h_refs, land_refs = refs[:n], refs[n:2 * n]
        token = refs[-1]
        for cp in _rs_copies(h_refs, land_refs, refs[2 * n], refs[2 * n + 1]):
            cp.start()
        token[...] = jnp.zeros_like(token)

    out = pl.pallas_call(
        body, name=name,
        out_shape=(pltpu.SemaphoreType.DMA((3 * n,)), pltpu.SemaphoreType.DMA((3 * n,)),
                   *[pltpu.HBM(a.shape, a.dtype) for a in hs], *[pltpu.HBM(a.shape, a.dtype) for a in lands],
                   _sds(_TOKEN, F32)),
        in_specs=[_HBM] * (2 * n),
        out_specs=(_SEM, _SEM, *[_HBM] * (2 * n), pl.BlockSpec(memory_space=pltpu.VMEM)),
        input_output_aliases={i: 2 + i for i in range(2 * n)},
        compiler_params=pltpu.CompilerParams(has_side_effects=_EFFECT),
    )(*[_in_hbm(a) for a in hs], *[_in_hbm(a) for a in lands])
    return (out[0], out[1], list(out[2:2 + n]), list(out[2 + n:2 + 2 * n])), out[-1]


def _rs_wait(started, after, *, name):
    send_sems, recv_sems, hs, lands = started
    n = len(hs)

    def body(*refs):
        for cp in _rs_copies(refs[:n], refs[n:2 * n], refs[2 * n], refs[2 * n + 1]):
            cp.wait_send()
            cp.wait_recv()

    out = pl.pallas_call(
        body, name=name,
        out_shape=tuple(pltpu.HBM(a.shape, a.dtype) for a in hs + lands),
        in_specs=[_HBM] * (2 * n) + [_SEM, _SEM, _ANY], out_specs=tuple([_HBM] * (2 * n)),
        input_output_aliases={i: i for i in range(2 * n)},
        compiler_params=pltpu.CompilerParams(has_side_effects=_EFFECT),
    )(*hs, *lands, send_sems, recv_sems, after)
    return list(out[:n]), list(out[n:])


def _adamw_chips(w, h, others, m, v, chip, *, name):
    r, c = w.shape

    def body(chip_ref, w_ref, h_ref, o_ref, m_ref, v_ref, g_ref, d_ref, nm_ref, nv_ref):
        g = h_ref[...].astype(F32)
        for k in range(3):
            g = g + o_ref[k].astype(F32)
        g_ref[...] = g
        d_ref[...], nm_ref[...], nv_ref[...] = _adamw_math(w_ref[...], g, m_ref[...], v_ref[...])

    blk = pl.BlockSpec((ADAM_ROWS, c), lambda i, chip_ref: (i, 0))
    return pl.pallas_call(
        body, name=name,
        grid_spec=pltpu.PrefetchScalarGridSpec(
            num_scalar_prefetch=1, grid=(r // ADAM_ROWS,),
            in_specs=[blk, pl.BlockSpec((None, ADAM_ROWS, c), lambda i, chip_ref: (chip_ref[0], i, 0)),
                      pl.BlockSpec((3, ADAM_ROWS, c), lambda i, chip_ref: (0, i, 0)), blk, blk],
            out_specs=[blk] * 4),
        out_shape=[_sds((r, c), F32)] * 4, compiler_params=_cparams("parallel"),
    )(chip, w, h, others, m, v)


def _adamw_chips_layers(w, hs, others, m, v, chip, *, name):
    nl, r, c = w.shape

    def body(chip_ref, w_ref, m_ref, v_ref, *rest):
        h_refs, o_refs = rest[:nl], rest[nl:2 * nl]
        g_ref, d_ref, nm_ref, nv_ref = rest[2 * nl:]
        layer = pl.program_id(0)
        g = jnp.zeros((ADAM_ROWS, c), F32)
        for k in range(nl):
            gk = h_refs[k][...].astype(F32)
            for j in range(3):
                gk = gk + o_refs[k][j].astype(F32)
            g = jnp.where(layer == k, gk, g)
        g_ref[...] = g
        d_ref[...], nm_ref[...], nv_ref[...] = _adamw_math(w_ref[...], g, m_ref[...], v_ref[...])

    def rows(k):
        return lambda l, i: jnp.where(l == k, i, 0)

    blk = pl.BlockSpec((None, ADAM_ROWS, c), lambda l, i, chip_ref: (l, i, 0))
    h_specs = [pl.BlockSpec((None, ADAM_ROWS, c), lambda l, i, chip_ref, f=rows(k): (chip_ref[0], f(l, i), 0))
               for k in range(nl)]
    o_specs = [pl.BlockSpec((3, ADAM_ROWS, c), lambda l, i, chip_ref, f=rows(k): (0, f(l, i), 0)) for k in range(nl)]
    return pl.pallas_call(
        body, name=name,
        grid_spec=pltpu.PrefetchScalarGridSpec(
            num_scalar_prefetch=1, grid=(nl, r // ADAM_ROWS),
            in_specs=[blk, blk, blk] + h_specs + o_specs, out_specs=[blk] * 4),
        out_shape=[_sds((nl, r, c), F32)] * 4, compiler_params=_cparams("arbitrary", "arbitrary"),
    )(chip, w, m, v, *hs, *others)


def _gather_begin(xs, tag, dep=None):
    return _ag_start(xs, name=f"ag_start_{tag}", dep=dep)


def _gather_end(started, after, tag):
    _, lands = _ag_wait(started, after, name=f"ag_wait_{tag}")
    return _ag_forward(lands, name=f"ag_forward_{tag}")


def _scatter_pair(gs, tag):
    g4s = [g.reshape((4, 2) + g.shape[1:]) for g in gs]
    return _sib_start(g4s, name=f"rs_sib_start_{tag}")


def _scatter_chips(pair, after, tag):
    core = lax.axis_index("c").astype(jnp.int32).reshape(1)
    g4s, got = _sib_wait(pair, after, name=f"rs_sib_wait_{tag}")
    hs = [_rs_pair_sum(g4, a, core, name=f"rs_pair_sum_{tag}_{i}") for i, (g4, a) in enumerate(zip(g4s, got))]
    return _rs_start(hs, name=f"rs_start_{tag}")


def _scatter_end(started, after, tag):
    return _rs_wait(started, after, name=f"rs_wait_{tag}")


MM_TM = 512


def _ffn_fwd(x_mid, g_row, wup_p, wdown, conv_w, conv_b, tag, dep=None):
    hb, _ = _rms_fwd(x_mid, g_row, want_f32=False, name=f"ffn_norm_{tag}", dep=dep)
    up = _mm_nn_pieces(hb, wup_p, tm=MM_TM, name=f"ffn_up_{tag}")
    a = _conv_gate_fwd(up, conv_w, conv_b, name=f"ffn_conv_{tag}")
    x_out = _mm_nn(a, wdown, tm=1024, tn=512, name=f"ffn_down_{tag}", res=x_mid)
    return x_out, (hb, up, a)


def _ffn_bwd(dx, x_mid, g_row, wup_p, wdown, conv_w, conv_b, saved, tag):
    hb, up, a = saved
    dxb = dx.astype(BF16)
    da = _mm_nt(dxb, wdown, tm=MM_TM, tn=1408, name=f"ffn_da_{tag}")
    dwdown = _mm_tn(a, dxb, tm=512, tn=1024, name=f"ffn_dwdown_{tag}")
    dup_v, dup_g, dconv_w, dconv_b = _conv_gate_bwd(up, da, conv_w, conv_b, name=f"ffn_dconv_{tag}")
    dwup = _mm_tn_pieces(hb, (dup_v, dup_g), pieces=N_DEV, tm=MM_TM, name=f"ffn_dwup_{tag}")
    pair, token = _scatter_pair([dwup, dwdown.reshape(N_DEV, D_FF // N_DEV, D_MODEL)], f"ffn_{tag}")
    dh = _mm_nt_pieces((dup_v, dup_g), wup_p, tm=MM_TM, tn=1024, name=f"ffn_dh_{tag}", dep=token)
    started, token = _scatter_chips(pair, dh, f"ffn_{tag}")
    dx_mid, dg = _rms_bwd(x_mid, g_row, dh, dx, name=f"ffn_dnorm_{tag}", dep=token)
    return dx_mid, dg, dconv_w, dconv_b, started


def _pool_layer_fwd(x, g_row, w, b_row, sc_row, tag, dep=None):
    _, hf = _rms_fwd(x, g_row, want_f32=True, name=f"pool_norm_{tag}", dep=dep)
    return _pool_fwd(hf, x, w, b_row, sc_row, name=f"pool_fwd_{tag}"), (hf,)


def _pool_layer_bwd(dx_mid, x, g_row, w, b_row, sc_row, saved, tag):
    (hf,) = saved
    dh, dw, db, dsc = _pool_bwd(hf, dx_mid, w, b_row, sc_row, name=f"pool_bwd_{tag}")
    dx, dg = _rms_bwd(x, g_row, dh, dx_mid, name=f"pool_dnorm_{tag}")
    return dx, dg, dw, db, dsc


def _sb_layer_fwd(x, g_row, wqkv_p, gains, wo, tag):
    hb, _ = _rms_fwd(x, g_row, want_f32=False, name=f"sb_norm_{tag}")
    qkv = _mm_nn_pieces(hb, wqkv_p, tm=MM_TM, name=f"sb_qkv_{tag}")
    qkn = _qk_norm_fwd(qkv, gains, name=f"sb_qknorm_{tag}")
    vb = qkv[:, 2 * D_MODEL:].astype(BF16)
    o = _sb_fwd(qkn, vb, name=f"sb_att_{tag}")
    x_mid = _mm_nn(o, wo, tm=MM_TM, tn=512, name=f"sb_out_{tag}", res=x)
    return x_mid, (hb, qkv, qkn, vb, o)


def _sb_layer_bwd(dx_mid, x, g_row, wqkv_p, gains, wo, saved, tag):
    hb, qkv, qkn, vb, o = saved
    dmb = dx_mid.astype(BF16)
    do = _mm_nt(dmb, wo, tm=MM_TM, tn=512, name=f"sb_do_{tag}", out_dtype=BF16)
    dwo = _mm_tn(o, dmb, tm=512, tn=1024, name=f"sb_dwo_{tag}")
    dqn, dkn, dv = _sb_bwd(qkn, vb, do, name=f"sb_datt_{tag}")
    dq, dqg = _qk_norm_bwd(qkv, gains, dqn, which=0, name=f"sb_dqnorm_{tag}")
    dk, dkg = _qk_norm_bwd(qkv, gains, dkn, which=1, name=f"sb_dknorm_{tag}")
    dqkv = jnp.concatenate([dq, dk, dv.astype(BF16)], axis=1)
    dwqkv = _mm_tn_pieces(hb, dqkv, pieces=N_DEV, tm=MM_TM, name=f"sb_dwqkv_{tag}")
    pair, token = _scatter_pair([dwqkv, dwo.reshape(N_DEV, D_MODEL // N_DEV, D_MODEL)], f"sb_{tag}")
    dh = _mm_nt_pieces(dqkv, wqkv_p, tm=MM_TM, tn=1024, name=f"sb_dh_{tag}", dep=token)
    started, token = _scatter_chips(pair, dh, f"sb_{tag}")
    dx, dg = _rms_bwd(x, g_row, dh, dx_mid, name=f"sb_dnorm_{tag}", dep=token)
    return dx, dg, dqg, dkg, started


def _ssm_params(lam_re, lam_im, log_step, b_re, b_im):
    g, p = SSM_GROUPS, SSM_STATE
    return (lam_re.reshape(g, 1, p), lam_im.reshape(g, 1, p), log_step.reshape(g, 1, 1),
            jnp.transpose(b_re, (0, 2, 1)), jnp.transpose(b_im, (0, 2, 1)))


def _ssm_layer_fwd(x, g_row, raw, c_re, c_im, d_row, wglu_p, bglu_row, tag):
    g, p, ch = SSM_GROUPS, SSM_STATE, SSM_CH
    _, hf = _rms_fwd(x, g_row, want_f32=True, name=f"ssm_norm_{tag}")
    ar, ai, bbr, bbi = _ssm_prep_fwd(*raw, name=f"ssm_prep_{tag}")
    wb = jnp.concatenate([bbr.reshape(g * ch, p), bbi.reshape(g * ch, p)], axis=1)
    wc = jnp.concatenate([c_re.reshape(g * ch, p), -c_im.reshape(g * ch, p)], axis=1)
    a_re, a_im = ar.reshape(1, g * p), ai.reshape(1, g * p)
    ylin, yg = _ssm_core_fwd(hf, wb, wc, a_re, a_im, d_row, name=f"ssm_core_{tag}")
    x_mid, val, gate = _glu_fwd(yg, wglu_p, bglu_row, x, tm=MM_TM, name=f"ssm_glu_{tag}")
    return x_mid, (hf, wb, wc, a_re, a_im, ylin, yg, val, gate)


def _ssm_layer_bwd(dx_mid, x, g_row, raw, d_row, wglu_p, saved, tag):
    g, p, ch = SSM_GROUPS, SSM_STATE, SSM_CH
    hf, wb, wc, a_re, a_im, ylin, yg, val, gate = saved
    dgv, dbglu = _glu_bwd(dx_mid, val, gate, name=f"ssm_dglu_{tag}")
    dwglu = _mm_tn_pieces(yg, dgv, pieces=N_DEV, tm=MM_TM, name=f"ssm_dwglu_{tag}")
    pair, token = _scatter_pair([dwglu], f"ssm_{tag}")
    dyg = _mm_nt_pieces(dgv, wglu_p, tm=MM_TM, tn=1024, name=f"ssm_dyg_{tag}", dep=token)
    du, dwb, dwc, dar, dai, dd = _ssm_core_bwd(hf, ylin, dyg, wb, wc, a_re, a_im, d_row, name=f"ssm_dcore_{tag}")
    dc_re = dwc[:, :p].reshape(g, ch, p)
    dc_im = -dwc[:, p:].reshape(g, ch, p)
    dlr, dli, dls, dbtr, dbti = _ssm_prep_bwd(
        *raw, dar.reshape(g, 1, p), dai.reshape(g, 1, p), dwb[:, :p].reshape(g, ch, p), dwb[:, p:].reshape(g, ch, p),
        name=f"ssm_dprep_{tag}")
    started, token = _scatter_chips(pair, du, f"ssm_{tag}")
    dx, dg = _rms_bwd(x, g_row, du, dx_mid, name=f"ssm_dnorm_{tag}", dep=token)
    grads = dict(ssm_lam_re=dlr.reshape(1, g, p), ssm_lam_im=dli.reshape(1, g, p), ssm_log_step=dls.reshape(1, g),
                 ssm_b_re=jnp.transpose(dbtr, (0, 2, 1))[None], ssm_b_im=jnp.transpose(dbti, (0, 2, 1))[None],
                 ssm_c_re=dc_re[None], ssm_c_im=dc_im[None], ssm_d=dd, ssm_b_glu=dbglu)
    return dx, dg, grads, started


_WEIGHTS = ["norm_mix_g", "norm_ffn_g", "pool_w", "pool_b", "pool_scale", "sb_w_qkv", "sb_q_gain", "sb_k_gain", "sb_w_o",
            "ssm_lam_re", "ssm_lam_im", "ssm_log_step", "ssm_b_re", "ssm_b_im", "ssm_c_re", "ssm_c_im", "ssm_d",
            "ssm_w_glu", "ssm_b_glu", "ffn_w_up", "ffn_conv_w", "ffn_conv_b", "ffn_w_down"]
_INPUTS = ["x"] + _WEIGHTS + ["loss_target"] + ["m_" + n for n in _WEIGHTS] + ["v_" + n for n in _WEIGHTS]
_REPLICATED = ["norm_mix_g", "norm_ffn_g", "sb_q_gain", "sb_k_gain", "ssm_lam_re", "ssm_lam_im", "ssm_log_step",
               "ssm_b_re", "ssm_b_im", "ssm_c_re", "ssm_c_im", "ffn_conv_b"]
_SMALL_SHARDED = {"pool_b": 1, "pool_scale": 1, "ssm_d": 1, "ssm_b_glu": 1, "ffn_conv_w": 2}
_BIG = ["pool_w", "sb_w_qkv", "sb_w_o", "ssm_w_glu", "ffn_w_up", "ffn_w_down"]
PACK_COLS = 512


def _pack(arrays):
    flat = jnp.concatenate([a.reshape(-1).astype(F32) for a in arrays])
    rows = -(-flat.shape[0] // (PACK_COLS * PACK_ROWS)) * PACK_ROWS
    return jnp.pad(flat, (0, rows * PACK_COLS - flat.shape[0])).reshape(rows, PACK_COLS)


def _unpack(packed, shapes, lead=()):
    flat = packed.reshape(lead + (-1,))
    out, off = [], 0
    for shp in shapes:
        n = math.prod(shp)
        out.append(flat[..., off:off + n].reshape(lead + tuple(shp)))
        off += n
    return out


def _unshard(gathered, axis):
    g = jnp.moveaxis(gathered, 0, axis)
    shp = g.shape
    return g.reshape(shp[:axis] + (shp[axis] * shp[axis + 1],) + shp[axis + 2:])


def _step(p):
    s = p["x"].shape[1]
    x = p["x"].reshape(s, D_MODEL)
    me = _slot(*_place())

    small_local = [p[n] for n in _SMALL_SHARDED]
    pool_w_l = p["pool_w"].astype(BF16).reshape(-1, POOL_DIM)
    chip = (2 * lax.axis_index("x") + lax.axis_index("y")).astype(jnp.int32).reshape(1)

    def ffn_shards(i):
        return [p["ffn_w_up"][i].astype(BF16), p["ffn_w_down"][i].astype(BF16)]

    st_first, tok = _gather_begin([pool_w_l, _pack(small_local)], "first")
    st_ffn, tok_ffn = [None] * DEPTH, [None] * DEPTH
    st_ffn[0], tok = _gather_begin(ffn_shards(0), "ffn_0", dep=tok)
    st_mix, tok_ffn[0] = _gather_begin([p["sb_w_qkv"][0].astype(BF16), p["sb_w_o"][0].astype(BF16),
                                        p["ssm_w_glu"][0].astype(BF16)], "mixers", dep=tok)
    ag = _gather_end(st_first, tok_ffn[0], "first")
    n_pool = p["pool_w"].shape[0]
    pool_w = jnp.transpose(ag[0].reshape(N_DEV, n_pool, POOL_GROUPS, POOL_DIM // N_DEV, POOL_DIM), (1, 2, 0, 3, 4))
    pool_w = pool_w.reshape(n_pool, POOL_GROUPS, POOL_DIM, POOL_DIM)
    small_full = {}
    for n, g in zip(_SMALL_SHARDED, _unpack(ag[1], [a.shape for a in small_local], lead=(N_DEV,))):
        small_full[n] = _unshard(g, _SMALL_SHARDED[n])
    mix_w = {}
    wup_p, wdown = [None] * DEPTH, [None] * DEPTH

    gains = jnp.stack([p["sb_q_gain"][0], p["sb_k_gain"][0]])[:, None, :]
    ssm_raw = _ssm_params(p["ssm_lam_re"][0], p["ssm_lam_im"][0], p["ssm_log_step"][0], p["ssm_b_re"][0],
                          p["ssm_b_im"][0])

    def mixer_args(i):
        j = i // 3
        g_row = p["norm_mix_g"][i][None]
        if i % 3 == 0:
            return (g_row, pool_w[j], small_full["pool_b"][j][None], small_full["pool_scale"][j][None])
        if i % 3 == 1:
            return (g_row, mix_w["qkv"], gains, mix_w["o"])
        return (g_row, ssm_raw, p["ssm_c_re"][0], p["ssm_c_im"][0], small_full["ssm_d"], mix_w["glu"],
                small_full["ssm_b_glu"])

    def ffn_args(i):
        return (p["norm_ffn_g"][i][None], wup_p[i], wdown[i], small_full["ffn_conv_w"][i], p["ffn_conv_b"][i][None])

    xs_in, xs_mid, saved_mix, saved_ffn = [], [], [], []
    for i in range(DEPTH):
        xs_in.append(x)
        if i == 1:
            mix_w["qkv"], wo_g, mix_w["glu"] = _gather_end(st_mix, x, "mixers")
            mix_w["o"] = wo_g.reshape(D_MODEL, D_MODEL)
        fwd = (_pool_layer_fwd, _sb_layer_fwd, _ssm_layer_fwd)[i % 3]
        x, sv = fwd(x, *mixer_args(i), f"l{i}")
        saved_mix.append(sv)
        xs_mid.append(x)
        wup_p[i], wd_g = _gather_end(st_ffn[i], x, f"ffn_{i}")
        wdown[i] = wd_g.reshape(D_FF, D_MODEL)
        if i + 1 < DEPTH:
            st_ffn[i + 1], tok_ffn[i + 1] = _gather_begin(ffn_shards(i + 1), f"ffn_{i + 1}")
        x, sv = _ffn_fwd(x, *ffn_args(i), f"l{i}", dep=tok_ffn[i + 1] if i + 1 < DEPTH else None)
        saved_ffn.append(sv)
    dx, loss_part = _loss_head(x, p["loss_target"].reshape(s, D_MODEL), name="loss_head")

    grads = {}
    dg_mix, dg_ffn = [None] * DEPTH, [None] * DEPTH
    dconv_w, dconv_b = [None] * DEPTH, [None] * DEPTH
    dpool = {"w": {}, "b": {}, "scale": {}}
    out = {}

    def big_update(n, h, others, idx=None):
        w, m, v = (p[pre + n] if idx is None else p[pre + n][idx] for pre in ("", "m_", "v_"))
        cols = h.shape[-1]
        r = _adamw_chips(w.reshape(-1, cols), h, others, m.reshape(-1, cols), v.reshape(-1, cols), chip,
                         name=f"adamw_{n}" + ("" if idx is None else f"_{idx}"))
        return [a.reshape(w.shape) for a in r]

    kinds = ("grad", "delta", "new_m", "new_v")
    ffn_upd = {"ffn_w_up": [None] * DEPTH, "ffn_w_down": [None] * DEPTH}

    def finish(entry, after):
        names, idx, started, tag = entry
        hs, others = _scatter_end(started, after, tag)
        for n, h, o in zip(names, hs, others):
            if idx is None:
                for kind, a in zip(kinds, big_update(n, h, o)):
                    out[kind + "_" + n] = a
            else:
                ffn_upd[n][idx] = (h, o)

    pending = []
    for i in reversed(range(DEPTH)):
        dx, dg_ffn[i], dconv_w[i], dconv_b[i], started = _ffn_bwd(
            dx, xs_mid[i], *ffn_args(i), saved_ffn[i], f"l{i}")
        for entry in pending:
            finish(entry, dx)
        pending = [(("ffn_w_up", "ffn_w_down"), i, started, f"ffn_l{i}")]
        margs = mixer_args(i)
        if i % 3 == 0:
            j = i // 3
            dx, dg_mix[i], dpool["w"][j], dpool["b"][j], dpool["scale"][j] = _pool_layer_bwd(
                dx, xs_in[i], *margs, saved_mix[i], f"l{i}")
        elif i % 3 == 1:
            dx, dg_mix[i], dqg, dkg, started = _sb_layer_bwd(dx, xs_in[i], *margs, saved_mix[i], f"l{i}")
            grads["sb_q_gain"], grads["sb_k_gain"] = dqg, dkg
            pending.append((("sb_w_qkv", "sb_w_o"), None, started, f"sb_l{i}"))
        else:
            g_row, raw, _, _, d_row, wg, _ = margs
            dx, dg_mix[i], sg, started = _ssm_layer_bwd(dx, xs_in[i], g_row, raw, d_row, wg, saved_mix[i], f"l{i}")
            grads.update(sg)
            pending.append((("ssm_w_glu",), None, started, f"ssm_l{i}"))
    grad_x = dx.reshape(1, s, D_MODEL)
    grads["norm_mix_g"] = jnp.concatenate(dg_mix, axis=0)
    grads["norm_ffn_g"] = jnp.concatenate(dg_ffn, axis=0)
    grads["ffn_conv_w"] = jnp.stack(dconv_w)
    grads["ffn_conv_b"] = jnp.concatenate(dconv_b, axis=0)
    grads["pool_b"] = jnp.concatenate([dpool["b"][j] for j in range(n_pool)], axis=0)
    grads["pool_scale"] = jnp.concatenate([dpool["scale"][j] for j in range(n_pool)], axis=0)
    dpw = jnp.stack([dpool["w"][j] for j in range(n_pool)])
    dpw = dpw.reshape(n_pool, POOL_GROUPS, N_DEV, POOL_DIM // N_DEV, POOL_DIM)
    pair, token = _scatter_pair([jnp.transpose(dpw, (2, 0, 1, 3, 4)).reshape(N_DEV, -1, POOL_DIM)], "pool")
    started, _ = _scatter_chips(pair, token, "pool")
    pending.append((("pool_w",), None, started, "pool"))

    small_names = _REPLICATED + list(_SMALL_SHARDED)
    full_shapes = [p[n].shape for n in _REPLICATED] + [small_full[n].shape for n in _SMALL_SHARDED]
    part = _pack([grads[n].reshape(shp) for n, shp in zip(small_names, full_shapes)] + [loss_part[0]])
    st_small, tok_small = _gather_begin([part], "small_grads")
    for entry in pending:
        finish(entry, tok_small)
    for n in ffn_upd:
        upd = _adamw_chips_layers(p[n], [ffn_upd[n][i][0] for i in range(DEPTH)], [ffn_upd[n][i][1] for i in range(DEPTH)],
                                  p["m_" + n], p["v_" + n], chip, name=f"adamw_{n}")
        for kind, a in zip(kinds, upd):
            out[kind + "_" + n] = a
    (parts,) = _gather_end(st_small, out["new_v_ffn_w_up"], "small_grads")
    summed = _unpack(_sum_parts(parts, name="sum_small_grads"), full_shapes + [(LANE,)])
    loss = summed[-1][0]
    small_g = {}
    for n, g in zip(small_names, summed[:-1]):
        if n in _SMALL_SHARDED:
            ax = _SMALL_SHARDED[n]
            g = lax.dynamic_slice_in_dim(g, me * p[n].shape[ax], p[n].shape[ax], axis=ax)
        small_g[n] = g

    local_shapes = [p[n].shape for n in small_names]
    packs = [_pack([p[pre + n] for n in small_names]) for pre in ("", "m_", "v_")]
    res = _adamw_flat(packs[0], _pack([small_g[n] for n in small_names]), packs[1], packs[2], name="adamw_small")
    for kind, packed in zip(("delta", "new_m", "new_v"), res):
        for n, a in zip(small_names, _unpack(packed, local_shapes)):
            out[kind + "_" + n] = a
    for n in small_names:
        out["grad_" + n] = small_g[n]

    return (loss, grad_x, *[out["grad_" + n] for n in _WEIGHTS], *[out["delta_" + n] for n in _WEIGHTS],
            *[out["new_m_" + n] for n in _WEIGHTS], *[out["new_v_" + n] for n in _WEIGHTS])


def kernel(x, norm_mix_g, norm_ffn_g, pool_w, pool_b, pool_scale, sb_w_qkv, sb_q_gain, sb_k_gain, sb_w_o, ssm_lam_re, ssm_lam_im, ssm_log_step, ssm_b_re, ssm_b_im, ssm_c_re, ssm_c_im, ssm_d, ssm_w_glu, ssm_b_glu, ffn_w_up, ffn_conv_w, ffn_conv_b, ffn_w_down, loss_target, m_norm_mix_g, m_norm_ffn_g, m_pool_w, m_pool_b, m_pool_scale, m_sb_w_qkv, m_sb_q_gain, m_sb_k_gain, m_sb_w_o, m_ssm_lam_re, m_ssm_lam_im, m_ssm_log_step, m_ssm_b_re, m_ssm_b_im, m_ssm_c_re, m_ssm_c_im, m_ssm_d, m_ssm_w_glu, m_ssm_b_glu, m_ffn_w_up, m_ffn_conv_w, m_ffn_conv_b, m_ffn_w_down, v_norm_mix_g, v_norm_ffn_g, v_pool_w, v_pool_b, v_pool_scale, v_sb_w_qkv, v_sb_q_gain, v_sb_k_gain, v_sb_w_o, v_ssm_lam_re, v_ssm_lam_im, v_ssm_log_step, v_ssm_b_re, v_ssm_b_im, v_ssm_c_re, v_ssm_c_im, v_ssm_d, v_ssm_w_glu, v_ssm_b_glu, v_ffn_w_up, v_ffn_conv_w, v_ffn_conv_b, v_ffn_w_down):
    args = (x, norm_mix_g, norm_ffn_g, pool_w, pool_b, pool_scale, sb_w_qkv, sb_q_gain, sb_k_gain, sb_w_o, ssm_lam_re, ssm_lam_im, ssm_log_step, ssm_b_re, ssm_b_im, ssm_c_re, ssm_c_im, ssm_d, ssm_w_glu, ssm_b_glu, ffn_w_up, ffn_conv_w, ffn_conv_b, ffn_w_down, loss_target, m_norm_mix_g, m_norm_ffn_g, m_pool_w, m_pool_b, m_pool_scale, m_sb_w_qkv, m_sb_q_gain, m_sb_k_gain, m_sb_w_o, m_ssm_lam_re, m_ssm_lam_im, m_ssm_log_step, m_ssm_b_re, m_ssm_b_im, m_ssm_c_re, m_ssm_c_im, m_ssm_d, m_ssm_w_glu, m_ssm_b_glu, m_ffn_w_up, m_ffn_conv_w, m_ffn_conv_b, m_ffn_w_down, v_norm_mix_g, v_norm_ffn_g, v_pool_w, v_pool_b, v_pool_scale, v_sb_w_qkv, v_sb_q_gain, v_sb_k_gain, v_sb_w_o, v_ssm_lam_re, v_ssm_lam_im, v_ssm_log_step, v_ssm_b_re, v_ssm_b_im, v_ssm_c_re, v_ssm_c_im, v_ssm_d, v_ssm_w_glu, v_ssm_b_glu, v_ffn_w_up, v_ffn_conv_w, v_ffn_conv_b, v_ffn_w_down)
    return _step(dict(zip(_INPUTS, args)))
```

```python
import functools
import math

import jax
import jax.numpy as jnp
from jax import lax
from jax.experimental import pallas as pl
from jax.experimental.pallas import tpu as pltpu

F32 = jnp.float32
BF16 = jnp.bfloat16

N_DEV = 8
D_MODEL = 2048
D_FF = 5632
DEPTH = 4
POOL_GROUPS = 4
POOL_DIM = 512
HEADS = 16
HEAD_DIM = 128
SSM_GROUPS = 128
SSM_CH = 16
SSM_STATE = 64
SSM_BLOCK_GROUPS = 8
SSM_BLOCK_LANES = SSM_BLOCK_GROUPS * SSM_STATE
RMS_EPS = 1e-6
ADAM_LR = 0.001
ADAM_B1 = 0.9
ADAM_B2 = 0.999
ADAM_EPS = 1e-08
ADAM_WD = 0.01
ADAM_STEP = 10

VMEM_LIMIT_BYTES = 56 * 1024 * 1024
LANE = 128
SUBLANE = 8
MESH = pl.DeviceIdType.MESH


def _cparams(*sem):
    return pltpu.CompilerParams(dimension_semantics=tuple(sem), vmem_limit_bytes=VMEM_LIMIT_BYTES)


def _sds(shape, dtype):
    return jax.ShapeDtypeStruct(tuple(shape), dtype)


def _mm(a, b, *, dims, grid, a_spec, b_spec, o_spec, out_shape, out_dtype, name, k_axis=None, acc_shape=None,
        res=None, res_spec=None, a_alt=None, b_alt=None, alt_axis=None, alt_from=None, dep=None):
    nk = grid[k_axis] if k_axis is not None else 1
    n_in = 2 + sum(e is not None for e in (res, a_alt, b_alt, dep))

    def body(*refs):
        a_ref, b_ref = refs[:2]
        rest = list(refs[2:n_in])
        r_ref = rest.pop(0) if res is not None else None
        a2_ref = rest.pop(0) if a_alt is not None else None
        b2_ref = rest.pop(0) if b_alt is not None else None
        o_ref = refs[n_in]
        scr = refs[n_in + 1:]
        av, bv = a_ref[...], b_ref[...]
        if a2_ref is not None:
            av = jnp.where(pl.program_id(alt_axis) >= alt_from, a2_ref[...], av)
        if b2_ref is not None:
            bv = jnp.where(pl.program_id(alt_axis) >= alt_from, b2_ref[...], bv)
        p = lax.dot_general(av, bv, (dims, ((), ())), preferred_element_type=F32)
        if k_axis is None:
            if r_ref is not None:
                p = p + r_ref[...]
            o_ref[...] = p.astype(o_ref.dtype)
        else:
            acc = scr[0]
            k = pl.program_id(k_axis)

            @pl.when(k == 0)
            def _():
                acc[...] = p

            @pl.when(k > 0)
            def _():
                acc[...] += p

            @pl.when(k == nk - 1)
            def _():
                r = acc[...]
                if r_ref is not None:
                    r = r + r_ref[...]
                o_ref[...] = r.astype(o_ref.dtype)

    sem = ["parallel"] * len(grid)
    if k_axis is not None:
        sem[k_axis] = "arbitrary"
    in_specs, args = [a_spec, b_spec], [a, b]
    if res is not None:
        in_specs.append(res_spec)
        args.append(res)
    for alt in (a_alt, b_alt):
        if alt is not None:
            args.append(alt[0])
            in_specs.append(alt[1])
    if dep is not None:
        args.append(dep)
        in_specs.append(pl.BlockSpec((SUBLANE, LANE), lambda *_: (0, 0)))
    scratch = [pltpu.VMEM(acc_shape, F32)] if k_axis is not None else []
    return pl.pallas_call(
        body, name=name, grid=grid, in_specs=in_specs, out_specs=o_spec, out_shape=_sds(out_shape, out_dtype),
        scratch_shapes=scratch, compiler_params=_cparams(*sem),
    )(*args)


NN = ((1,), (0,))
NT = ((1,), (1,))
TN = ((0,), (0,))


def _mm_nn_pieces(a, wp, *, tm, name, out_dtype=F32):
    s, k = a.shape
    tm = min(tm, s)
    p, _, c = wp.shape
    return _mm(a, wp, dims=NN, grid=(s // tm, p),
               a_spec=pl.BlockSpec((tm, k), lambda m, n: (m, 0)),
               b_spec=pl.BlockSpec((None, k, c), lambda m, n: (n, 0, 0)),
               o_spec=pl.BlockSpec((tm, c), lambda m, n: (m, n)),
               out_shape=(s, p * c), out_dtype=out_dtype, name=name)


def _mm_nt_pieces(a, wp, *, tm, tn, name, out_dtype=F32, dep=None):
    p, n, c = wp.shape
    halves = a if isinstance(a, tuple) else None
    a0 = halves[0] if halves else a
    s = a0.shape[0]
    tm = min(tm, s)
    h = p // 2
    alt = {}
    if halves:
        a_spec = pl.BlockSpec((tm, c), lambda m, j, k: (m, jnp.minimum(k, h - 1)))
        alt = dict(a_alt=(halves[1], pl.BlockSpec((tm, c), lambda m, j, k: (m, jnp.maximum(k - h, 0)))),
                   alt_axis=2, alt_from=h)
    else:
        a_spec = pl.BlockSpec((tm, c), lambda m, j, k: (m, k))
    return _mm(a0, wp, dims=NT, grid=(s // tm, n // tn, p), k_axis=2, acc_shape=(tm, tn), a_spec=a_spec,
               b_spec=pl.BlockSpec((None, tn, c), lambda m, j, k: (k, j, 0)),
               o_spec=pl.BlockSpec((tm, tn), lambda m, j, k: (m, j)),
               out_shape=(s, n), out_dtype=out_dtype, name=name, dep=dep, **alt)


def _mm_tn_pieces(a, g, *, pieces, tm, name, out_dtype=BF16):
    s, m = a.shape
    halves = g if isinstance(g, tuple) else None
    g0 = halves[0] if halves else g
    h = pieces // 2
    c = g0.shape[1] // (h if halves else pieces)
    alt = {}
    if halves:
        b_spec = pl.BlockSpec((s, c), lambda n, i: (0, jnp.minimum(n, h - 1)))
        alt = dict(b_alt=(halves[1], pl.BlockSpec((s, c), lambda n, i: (0, jnp.maximum(n - h, 0)))),
                   alt_axis=0, alt_from=h)
    else:
        b_spec = pl.BlockSpec((s, c), lambda n, i: (0, n))
    return _mm(a, g0, dims=TN, grid=(pieces, m // tm), a_spec=pl.BlockSpec((s, tm), lambda n, i: (0, i)),
               b_spec=b_spec, o_spec=pl.BlockSpec((None, tm, c), lambda n, i: (n, i, 0)),
               out_shape=(pieces, m, c), out_dtype=out_dtype, name=name, **alt)


def _mm_nn(a, w, *, tm, tn, name, out_dtype=F32, res=None):
    s, k = a.shape
    tm = min(tm, s)
    n = w.shape[1]
    return _mm(a, w, dims=NN, grid=(s // tm, n // tn),
               a_spec=pl.BlockSpec((tm, k), lambda m, j: (m, 0)),
               b_spec=pl.BlockSpec((k, tn), lambda m, j: (0, j)),
               o_spec=pl.BlockSpec((tm, tn), lambda m, j: (m, j)),
               res=res, res_spec=pl.BlockSpec((tm, tn), lambda m, j: (m, j)),
               out_shape=(s, n), out_dtype=out_dtype, name=name)


def _mm_nt(a, w, *, tm, tn, name, out_dtype=F32):
    s, k = a.shape
    tm = min(tm, s)
    n = w.shape[0]
    return _mm(a, w, dims=NT, grid=(s // tm, n // tn),
               a_spec=pl.BlockSpec((tm, k), lambda m, j: (m, 0)),
               b_spec=pl.BlockSpec((tn, k), lambda m, j: (j, 0)),
               o_spec=pl.BlockSpec((tm, tn), lambda m, j: (m, j)),
               out_shape=(s, n), out_dtype=out_dtype, name=name)


def _mm_tn(a, g, *, tm, tn, name, out_dtype=BF16):
    s, m = a.shape
    n = g.shape[1]
    return _mm(a, g, dims=TN, grid=(m // tm, n // tn),
               a_spec=pl.BlockSpec((s, tm), lambda i, j: (0, i)),
               b_spec=pl.BlockSpec((s, tn), lambda i, j: (0, j)),
               o_spec=pl.BlockSpec((tm, tn), lambda i, j: (i, j)),
               out_shape=(m, n), out_dtype=out_dtype, name=name)


ROW_TILE = 256


def _dep_spec():
    return pl.BlockSpec((SUBLANE, LANE), lambda i: (0, 0))


def _rms_fwd(x, g_row, *, want_f32, name, dep=None):
    s, d = x.shape
    n_in = 2 if dep is None else 3

    def body(*refs):
        x_ref, g_ref = refs[:2]
        outs = refs[n_in:]
        xv = x_ref[...]
        r = lax.rsqrt(jnp.mean(xv * xv, axis=-1, keepdims=True) + RMS_EPS)
        h = (xv * r) * g_ref[...]
        outs[0][...] = h.astype(BF16)
        if want_f32:
            outs[1][...] = h

    row = pl.BlockSpec((ROW_TILE, d), lambda i: (i, 0))
    out_shape = [_sds((s, d), BF16)] + ([_sds((s, d), F32)] if want_f32 else [])
    out = pl.pallas_call(
        body, name=name, grid=(s // ROW_TILE,),
        in_specs=[row, pl.BlockSpec((1, d), lambda i: (0, 0))] + ([] if dep is None else [_dep_spec()]),
        out_specs=[row] * len(out_shape), out_shape=out_shape, compiler_params=_cparams("parallel"),
    )(x, g_row, *(() if dep is None else (dep,)))
    return out if want_f32 else (out[0], None)


def _rms_bwd(x, g_row, dh, dres, *, name, dep=None):
    s, d = x.shape

    def body(x_ref, g_ref, dh_ref, dres_ref, *rest):
        dx_ref, dg_ref = rest[-2:]
        xv = x_ref[...]
        r = lax.rsqrt(jnp.mean(xv * xv, axis=-1, keepdims=True) + RMS_EPS)
        xn = xv * r
        dhv = dh_ref[...]
        dxn = dhv * g_ref[...]
        dx_ref[...] = dres_ref[...] + r * (dxn - xn * jnp.mean(dxn * xn, axis=-1, keepdims=True))
        part = jnp.sum(dhv * xn, axis=0, keepdims=True)

        @pl.when(pl.program_id(0) == 0)
        def _():
            dg_ref[...] = part

        @pl.when(pl.program_id(0) > 0)
        def _():
            dg_ref[...] += part

    row = pl.BlockSpec((ROW_TILE, d), lambda i: (i, 0))
    vec = pl.BlockSpec((1, d), lambda i: (0, 0))
    return pl.pallas_call(
        body, name=name, grid=(s // ROW_TILE,), in_specs=[row, vec, row, row] + ([] if dep is None else [_dep_spec()]),
        out_specs=[row, vec], out_shape=[_sds((s, d), F32), _sds((1, d), F32)], compiler_params=_cparams("arbitrary"),
    )(x, g_row, dh, dres, *(() if dep is None else (dep,)))


def _shift_down(v, k):
    row = lax.broadcasted_iota(jnp.int32, v.shape, 0)
    return jnp.where(row >= k, pltpu.roll(v, k, 0), 0.0)


def _shift_up(v, k):
    n = v.shape[0]
    row = lax.broadcasted_iota(jnp.int32, v.shape, 0)
    return jnp.where(row < n - k, pltpu.roll(v, n - k, 0), 0.0)


def _sigmoid(z):
    return 1.0 / (1.0 + jnp.exp(-z))


FF_COL_TILE = 256


def _conv3(u, w, b):
    return b + w[0:1, :] * _shift_down(u, 2) + w[1:2, :] * _shift_down(u, 1) + w[2:3, :] * u


def _conv_gate_fwd(up, conv_w, conv_b, *, name):
    s = up.shape[0]
    f = up.shape[1] // 2
    nt = f // FF_COL_TILE

    def body(uv_ref, ug_ref, wv_ref, wg_ref, bv_ref, bg_ref, a_ref):
        vc = _conv3(uv_ref[...], wv_ref[...], bv_ref[...])
        gc = _conv3(ug_ref[...], wg_ref[...], bg_ref[...])
        a_ref[...] = ((gc * _sigmoid(gc)) * vc).astype(BF16)

    def col(rows, off):
        return pl.BlockSpec((rows, FF_COL_TILE), lambda n: (0, n + off))

    return pl.pallas_call(
        body, name=name, grid=(nt,),
        in_specs=[col(s, 0), col(s, nt), col(3, 0), col(3, nt), col(1, 0), col(1, nt)],
        out_specs=col(s, 0), out_shape=_sds((s, f), BF16), compiler_params=_cparams("parallel"),
    )(up, up, conv_w, conv_w, conv_b, conv_b)


def _conv_gate_bwd(up, da, conv_w, conv_b, *, name):
    s = up.shape[0]
    f = up.shape[1] // 2
    nt = f // FF_COL_TILE

    def conv_bwd(u, w, dc):
        d0 = _shift_up(dc, 2)
        d1 = _shift_up(dc, 1)
        dup = w[0:1, :] * d0 + w[1:2, :] * d1 + w[2:3, :] * dc
        dw = jnp.concatenate([jnp.sum(u * d0, axis=0, keepdims=True), jnp.sum(u * d1, axis=0, keepdims=True),
                              jnp.sum(u * dc, axis=0, keepdims=True)], axis=0)
        return dup, dw, jnp.sum(dc, axis=0, keepdims=True)

    def body(uv_ref, ug_ref, da_ref, wv_ref, wg_ref, bv_ref, bg_ref,
             duv_ref, dug_ref, dwv_ref, dwg_ref, dbv_ref, dbg_ref):
        uv = uv_ref[...]
        ug = ug_ref[...]
        vc = _conv3(uv, wv_ref[...], bv_ref[...])
        gc = _conv3(ug, wg_ref[...], bg_ref[...])
        sg = _sigmoid(gc)
        dav = da_ref[...]
        dvc = dav * (gc * sg)
        dgc = dav * vc * (sg * (1.0 + gc * (1.0 - sg)))
        dup, dw, db = conv_bwd(uv, wv_ref[...], dvc)
        duv_ref[...] = dup.astype(BF16)
        dwv_ref[...] = dw
        dbv_ref[...] = db
        dup, dw, db = conv_bwd(ug, wg_ref[...], dgc)
        dug_ref[...] = dup.astype(BF16)
        dwg_ref[...] = dw
        dbg_ref[...] = db

    def col(rows, off):
        return pl.BlockSpec((rows, FF_COL_TILE), lambda n: (0, n + off))

    dup_v, dup_g, dw_v, dw_g, db_v, db_g = pl.pallas_call(
        body, name=name, grid=(nt,),
        in_specs=[col(s, 0), col(s, nt), col(s, 0), col(3, 0), col(3, nt), col(1, 0), col(1, nt)],
        out_specs=[col(s, 0), col(s, 0), col(3, 0), col(3, 0), col(1, 0), col(1, 0)],
        out_shape=[_sds((s, f), BF16), _sds((s, f), BF16), _sds((3, f), F32), _sds((3, f), F32),
                   _sds((1, f), F32), _sds((1, f), F32)],
        compiler_params=_cparams("parallel"),
    )(up, up, da, conv_w, conv_w, conv_b, conv_b)
    return dup_v, dup_g, jnp.concatenate([dw_v, dw_g], axis=1), jnp.concatenate([db_v, db_g], axis=1)


def _pool_counts(shape, g):
    win = jnp.left_shift(jnp.int32(2), g)
    t = lax.broadcasted_iota(jnp.int32, shape, 0)
    return win, jnp.minimum(t + 1, win).astype(F32)


def _window_sum(v, g, shift):
    for k in range(POOL_GROUPS):
        v = jnp.where(g >= k, v + shift(v, 1 << k), v)
    return v


def _pool_fwd(hf, x, w, b, scale, *, name):
    s, d = hf.shape

    def body(h_ref, x_ref, w_ref, b_ref, sc_ref, o_ref):
        g = pl.program_id(0)
        h = h_ref[...]
        _, cnt = _pool_counts(h.shape, g)
        pooled = _window_sum(h, g, _shift_down) / cnt - h
        y = jnp.dot(pooled.astype(BF16), w_ref[...], preferred_element_type=F32) + b_ref[...]
        o_ref[...] = x_ref[...] + y * sc_ref[...]

    col = pl.BlockSpec((s, POOL_DIM), lambda g: (0, g))
    vec = pl.BlockSpec((1, POOL_DIM), lambda g: (0, g))
    return pl.pallas_call(
        body, name=name, grid=(POOL_GROUPS,),
        in_specs=[col, col, pl.BlockSpec((None, POOL_DIM, POOL_DIM), lambda g: (g, 0, 0)), vec, vec],
        out_specs=col, out_shape=_sds((s, d), F32), compiler_params=_cparams("parallel"),
    )(hf, x, w, b, scale)


def _pool_bwd(hf, dm, w, b, scale, *, name):
    s, d = hf.shape

    def body(h_ref, dm_ref, w_ref, b_ref, sc_ref, dh_ref, dw_ref, db_ref, dsc_ref):
        g = pl.program_id(0)
        h = h_ref[...]
        _, cnt = _pool_counts(h.shape, g)
        pooled = (_window_sum(h, g, _shift_down) / cnt - h).astype(BF16)
        wv = w_ref[...]
        y = jnp.dot(pooled, wv, preferred_element_type=F32) + b_ref[...]
        dmv = dm_ref[...]
        dsc_ref[...] = jnp.sum(dmv * y, axis=0, keepdims=True)
        dy = dmv * sc_ref[...]
        db_ref[...] = jnp.sum(dy, axis=0, keepdims=True)
        dyb = dy.astype(BF16)
        dw_ref[...] = lax.dot_general(pooled, dyb, (TN, ((), ())), preferred_element_type=F32).astype(BF16)
        dp = lax.dot_general(dyb, wv, (NT, ((), ())), preferred_element_type=F32)
        dh_ref[...] = _window_sum(dp / cnt, g, _shift_up) - dp

    col = pl.BlockSpec((s, POOL_DIM), lambda g: (0, g))
    vec = pl.BlockSpec((1, POOL_DIM), lambda g: (0, g))
    mat = pl.BlockSpec((None, POOL_DIM, POOL_DIM), lambda g: (g, 0, 0))
    return pl.pallas_call(
        body, name=name, grid=(POOL_GROUPS,), in_specs=[col, col, mat, vec, vec], out_specs=[col, mat, vec, vec],
        out_shape=[_sds((s, d), F32), _sds((POOL_GROUPS, POOL_DIM, POOL_DIM), BF16), _sds((1, d), F32),
                   _sds((1, d), F32)],
        compiler_params=_cparams("parallel"),
    )(hf, dm, w, b, scale)


ATT_TQ = 256
ATT_TK = 256


def _qk_norm_fwd(qkv, gains, *, name):
    s = qkv.shape[0]

    def body(x_ref, g_ref, o_ref):
        xv = x_ref[...]
        r = lax.rsqrt(jnp.mean(xv * xv, axis=-1, keepdims=True) + RMS_EPS)
        o_ref[...] = ((xv * r) * g_ref[...]).astype(BF16)

    blk = pl.BlockSpec((s, HEAD_DIM), lambda hd: (0, hd))
    return pl.pallas_call(
        body, name=name, grid=(2 * HEADS,),
        in_specs=[blk, pl.BlockSpec((None, 1, HEAD_DIM), lambda hd: (hd // HEADS, 0, 0))],
        out_specs=blk, out_shape=_sds((s, 2 * HEADS * HEAD_DIM), BF16), compiler_params=_cparams("parallel"),
    )(qkv, gains)


def _qk_norm_bwd(qkv, gains, dn, *, which, name):
    s = qkv.shape[0]

    def body(x_ref, g_ref, dn_ref, dx_ref, dg_ref):
        xv = x_ref[...]
        r = lax.rsqrt(jnp.mean(xv * xv, axis=-1, keepdims=True) + RMS_EPS)
        xn = xv * r
        dnv = dn_ref[...]
        dxn = dnv * g_ref[...]
        dx_ref[...] = (r * (dxn - xn * jnp.mean(dxn * xn, axis=-1, keepdims=True))).astype(BF16)
        part = jnp.sum(dnv * xn, axis=0, keepdims=True)

        @pl.when(pl.program_id(0) == 0)
        def _():
            dg_ref[...] = part

        @pl.when(pl.program_id(0) > 0)
        def _():
            dg_ref[...] += part

    blk = pl.BlockSpec((s, HEAD_DIM), lambda hd: (0, hd))
    return pl.pallas_call(
        body, name=name, grid=(HEADS,),
        in_specs=[pl.BlockSpec((s, HEAD_DIM), lambda hd: (0, hd + which * HEADS)),
                  pl.BlockSpec((None, 1, HEAD_DIM), lambda hd: (which, 0, 0)), blk],
        out_specs=[blk, pl.BlockSpec((1, HEAD_DIM), lambda hd: (0, 0))],
        out_shape=[_sds((s, HEADS * HEAD_DIM), BF16), _sds((1, HEAD_DIM), F32)],
        compiler_params=_cparams("arbitrary"),
    )(qkv, gains, dn)


def _split_dot(v, tri):
    hi = v.astype(BF16)
    lo = (v - hi.astype(F32)).astype(BF16)
    return (jnp.dot(hi, tri, preferred_element_type=F32) + jnp.dot(lo, tri, preferred_element_type=F32))


def _causal_mask(qi, j):
    tpos = qi * ATT_TQ + lax.broadcasted_iota(jnp.int32, (ATT_TQ, ATT_TK), 0)
    spos = j * ATT_TK + lax.broadcasted_iota(jnp.int32, (ATT_TQ, ATT_TK), 1)
    return spos < tpos


def _att_tile(q, kj, qi, j):
    z = lax.dot_general(q, kj, (NT, ((), ())), preferred_element_type=F32) * (1.0 / math.sqrt(HEAD_DIM))
    mask = _causal_mask(qi, j)
    lb = jnp.minimum(z, 0.0) - jnp.log1p(jnp.exp(-jnp.abs(z)))
    l1m = jnp.where(mask, lb - z, 0.0)
    return lb, l1m, mask


def _tri(rel):
    r = lax.broadcasted_iota(jnp.int32, (ATT_TK, ATT_TK), 0)
    c = lax.broadcasted_iota(jnp.int32, (ATT_TK, ATT_TK), 1)
    return jnp.where(rel(r, c), 1.0, 0.0).astype(BF16)


def _sb_fwd(qkn, vb, *, name):
    s = vb.shape[0]

    def body(q_ref, k_ref, v_ref, o_ref):
        qi = pl.program_id(1)
        q = q_ref[...]
        after = _tri(lambda r, c: r > c)

        def step(t, carry):
            acc, run = carry
            j = qi - t
            rows = pl.ds(pl.multiple_of(j * ATT_TK, ATT_TK), ATT_TK)
            lb, l1m, mask = _att_tile(q, k_ref[rows, :], qi, j)
            remain = _split_dot(l1m, after) + run
            attn = jnp.where(mask, jnp.exp(lb + remain), 0.0)
            acc = acc + jnp.dot(attn.astype(BF16), v_ref[rows, :], preferred_element_type=F32)
            return acc, run + jnp.sum(l1m, axis=1, keepdims=True)

        acc, _ = lax.fori_loop(0, qi + 1, step, (jnp.zeros((ATT_TQ, HEAD_DIM), F32), jnp.zeros((ATT_TQ, 1), F32)))
        o_ref[...] = acc.astype(BF16)

    return pl.pallas_call(
        body, name=name, grid=(HEADS, s // ATT_TQ),
        in_specs=[pl.BlockSpec((ATT_TQ, HEAD_DIM), lambda hd, i: (i, hd)),
                  pl.BlockSpec((s, HEAD_DIM), lambda hd, i: (0, hd + HEADS)),
                  pl.BlockSpec((s, HEAD_DIM), lambda hd, i: (0, hd))],
        out_specs=pl.BlockSpec((ATT_TQ, HEAD_DIM), lambda hd, i: (i, hd)),
        out_shape=_sds((s, HEADS * HEAD_DIM), BF16), compiler_params=_cparams("parallel", "parallel"),
    )(qkn, qkn, vb)


def _sb_bwd(qkn, vb, dob, *, name):
    s = vb.shape[0]
    nkb = s // ATT_TK

    def body(q_ref, k_ref, v_ref, do_ref, dq_ref, dk_ref, dv_ref, a_buf, sig_buf):
        qi = pl.program_id(1)
        q = q_ref[...]
        do = do_ref[...]
        after = _tri(lambda r, c: r > c)
        before = _tri(lambda r, c: r < c)

        @pl.when(qi == 0)
        def _():
            dk_ref[...] = jnp.zeros_like(dk_ref)
            dv_ref[...] = jnp.zeros_like(dv_ref)

        def down(t, run):
            j = qi - t
            rows = pl.ds(pl.multiple_of(j * ATT_TK, ATT_TK), ATT_TK)
            lb, l1m, mask = _att_tile(q, k_ref[rows, :], qi, j)
            remain = _split_dot(l1m, after) + run
            a_buf[j] = jnp.where(mask, jnp.exp(lb + remain), 0.0)
            sig_buf[j] = jnp.exp(lb)
            return run + jnp.sum(l1m, axis=1, keepdims=True)

        lax.fori_loop(0, qi + 1, down, jnp.zeros((ATT_TQ, 1), F32))

        def up(j, carry):
            dq, run = carry
            rows = pl.ds(pl.multiple_of(j * ATT_TK, ATT_TK), ATT_TK)
            a = a_buf[j]
            sig = sig_buf[j]
            mask = _causal_mask(qi, j)
            da = lax.dot_general(do, v_ref[rows, :], (NT, ((), ())), preferred_element_type=F32)
            p = a * da
            c = _split_dot(p, before) + run
            dz = jnp.where(mask, p * (1.0 - sig) - c * sig, 0.0) * (1.0 / math.sqrt(HEAD_DIM))
            dzb = dz.astype(BF16)
            dq = dq + jnp.dot(dzb, k_ref[rows, :], preferred_element_type=F32)
            dk_ref[rows, :] += lax.dot_general(dzb, q, (TN, ((), ())), preferred_element_type=F32)
            dv_ref[rows, :] += lax.dot_general(a.astype(BF16), do, (TN, ((), ())), preferred_element_type=F32)
            return dq, run + jnp.sum(p, axis=1, keepdims=True)

        dq, _ = lax.fori_loop(0, qi + 1, up, (jnp.zeros((ATT_TQ, HEAD_DIM), F32), jnp.zeros((ATT_TQ, 1), F32)))
        dq_ref[...] = dq

    qblk = pl.BlockSpec((ATT_TQ, HEAD_DIM), lambda hd, i: (i, hd))
    full = pl.BlockSpec((s, HEAD_DIM), lambda hd, i: (0, hd))
    return pl.pallas_call(
        body, name=name, grid=(HEADS, s // ATT_TQ),
        in_specs=[qblk, pl.BlockSpec((s, HEAD_DIM), lambda hd, i: (0, hd + HEADS)), full, qblk],
        out_specs=[qblk, full, full],
        out_shape=[_sds((s, HEADS * HEAD_DIM), F32)] * 3,
        scratch_shapes=[pltpu.VMEM((nkb, ATT_TQ, ATT_TK), F32), pltpu.VMEM((nkb, ATT_TQ, ATT_TK), F32)],
        compiler_params=_cparams("parallel", "arbitrary"),
    )(qkn, qkn, vb, dob)


def _ssm_discretize(lam_re, lam_im, log_step, bt_re, bt_im):
    step = jnp.exp(log_step)
    mag = jnp.exp(lam_re * step)
    lb_re = mag * jnp.cos(lam_im * step)
    lb_im = mag * jnp.sin(lam_im * step)
    den = lam_re * lam_re + lam_im * lam_im
    f_re = ((lb_re - 1.0) * lam_re + lb_im * lam_im) / den
    f_im = (lb_im * lam_re - (lb_re - 1.0) * lam_im) / den
    return lb_re, lb_im, f_re * bt_re - f_im * bt_im, f_re * bt_im + f_im * bt_re


_SSM_LAM = (SSM_GROUPS, 1, SSM_STATE)
_SSM_STEP = (SSM_GROUPS, 1, 1)
_SSM_BT = (SSM_GROUPS, SSM_CH, SSM_STATE)


def _ssm_prep_fwd(lam_re, lam_im, log_step, bt_re, bt_im, *, name):
    def body(lr, li, ls, br, bi, o_ar, o_ai, o_br, o_bi):
        o_ar[...], o_ai[...], o_br[...], o_bi[...] = _ssm_discretize(lr[...], li[...], ls[...], br[...], bi[...])

    return pl.pallas_call(
        body, name=name, out_shape=[_sds(_SSM_LAM, F32), _sds(_SSM_LAM, F32), _sds(_SSM_BT, F32), _sds(_SSM_BT, F32)],
    )(lam_re, lam_im, log_step, bt_re, bt_im)


def _ssm_prep_bwd(lam_re, lam_im, log_step, bt_re, bt_im, d_ar, d_ai, d_br, d_bi, *, name):
    def body(lr, li, ls, br, bi, g_ar, g_ai, g_br, g_bi, o_lr, o_li, o_ls, o_br, o_bi):
        _, vjp = jax.vjp(_ssm_discretize, lr[...], li[...], ls[...], br[...], bi[...])
        o_lr[...], o_li[...], o_ls[...], o_br[...], o_bi[...] = vjp((g_ar[...], g_ai[...], g_br[...], g_bi[...]))

    return pl.pallas_call(
        body, name=name,
        out_shape=[_sds(_SSM_LAM, F32), _sds(_SSM_LAM, F32), _sds(_SSM_STEP, F32), _sds(_SSM_BT, F32), _sds(_SSM_BT, F32)],
    )(lam_re, lam_im, log_step, bt_re, bt_im, d_ar, d_ai, d_br, d_bi)


def _bd_masks():
    rowg = lax.broadcasted_iota(jnp.int32, (LANE, LANE), 0) // SSM_CH
    low = lax.broadcasted_iota(jnp.int32, (LANE, LANE), 1) < SSM_STATE
    return rowg, low


def _bd_expand(w):
    rowg, low = _bd_masks()
    high = jnp.logical_not(low)
    wr = pltpu.roll(w, SSM_STATE, 1)
    re = [jnp.where((rowg == 2 * k) & low, w, 0.0) + jnp.where((rowg == 2 * k + 1) & high, wr, 0.0) for k in range(4)]
    im = [jnp.where((rowg == 2 * k) & low, wr, 0.0) + jnp.where((rowg == 2 * k + 1) & high, w, 0.0) for k in range(4)]
    return jnp.concatenate(re + im, axis=1)


def _bd_extract(dbd):
    rowg, low = _bd_masks()
    high = jnp.logical_not(low)
    acc = jnp.zeros((LANE, LANE), F32)
    for k in range(4):
        c = dbd[:, LANE * k:LANE * (k + 1)]
        acc = acc + jnp.where((rowg == 2 * k) & low, c, 0.0) + jnp.where((rowg == 2 * k + 1) & low, pltpu.roll(c, SSM_STATE, 1), 0.0)
        c = dbd[:, LANE * (4 + k):LANE * (5 + k)]
        acc = acc + jnp.where((rowg == 2 * k) & high, pltpu.roll(c, SSM_STATE, 1), 0.0) + jnp.where((rowg == 2 * k + 1) & high, c, 0.0)
    return acc


def _cmul(ar, ai, br, bi):
    return ar * br - ai * bi, ar * bi + ai * br


def _scan_rows(xr, xi, ar, ai, *, reverse):
    n = xr.shape[0] // SUBLANE
    lanes = xr.shape[1]
    row = lax.broadcasted_iota(jnp.int32, (SUBLANE, lanes), 0)
    powers = [(ar, ai)]
    for _ in range(SUBLANE - 1):
        powers.append(_cmul(*powers[-1], ar, ai))
    pr = jnp.zeros((SUBLANE, lanes), F32)
    pi = jnp.zeros((SUBLANE, lanes), F32)
    for r in range(SUBLANE):
        e = (SUBLANE - 1 - r) if reverse else r
        pr = jnp.where(row == r, powers[e][0], pr)
        pi = jnp.where(row == r, powers[e][1], pi)

    def shift(v, d):
        if reverse:
            return jnp.where(row < SUBLANE - d, pltpu.roll(v, SUBLANE - d, 0), 0.0)
        return jnp.where(row >= d, pltpu.roll(v, d, 0), 0.0)

    def body(i, carry):
        cr, ci = carry
        g = (n - 1 - i) if reverse else i
        rows = pl.ds(pl.multiple_of(g * SUBLANE, SUBLANE), SUBLANE)
        br = xr[rows, :]
        bi = xi[rows, :]
        for d in (1, 2, 4):
            qr, qi = powers[d - 1]
            sr = shift(br, d)
            si = shift(bi, d)
            br, bi = br + qr * sr - qi * si, bi + qr * si + qi * sr
        br, bi = br + pr * cr - pi * ci, bi + pr * ci + pi * cr
        xr[rows, :] = br
        xi[rows, :] = bi
        edge = 0 if reverse else SUBLANE - 1
        return br[edge:edge + 1, :], bi[edge:edge + 1, :]

    zero = jnp.zeros((1, lanes), F32)
    lax.fori_loop(0, n, body, (zero, zero), unroll=2)


_GELU_C = math.sqrt(2.0 / math.pi)
_GELU_A = 0.044715


def _gelu(v):
    return 0.5 * v * (1.0 + jnp.tanh(_GELU_C * (v + _GELU_A * v * v * v)))


def _gelu_grad(v):
    t = jnp.tanh(_GELU_C * (v + _GELU_A * v * v * v))
    return 0.5 * (1.0 + t) + 0.5 * v * (1.0 - t * t) * (_GELU_C * (1.0 + 3.0 * _GELU_A * v * v))


def _ssm_states(u_b16, eb, ar, ai, xr, xi):
    nl = SSM_BLOCK_LANES
    xr[...] = jnp.dot(u_b16, eb[:, :nl], preferred_element_type=F32)
    xi[...] = jnp.dot(u_b16, eb[:, nl:], preferred_element_type=F32)
    _scan_rows(xr, xi, ar, ai, reverse=False)


def _ssm_specs(s):
    col = pl.BlockSpec((s, LANE), lambda b: (0, b))
    wsm = pl.BlockSpec((LANE, LANE), lambda b: (b, 0))
    lam = pl.BlockSpec((1, SSM_BLOCK_LANES), lambda b: (0, b))
    vec = pl.BlockSpec((1, LANE), lambda b: (0, b))
    return col, wsm, lam, vec


def _ssm_core_fwd(u, wb, wc, a_re, a_im, d_row, *, name):
    s, d = u.shape
    nl = SSM_BLOCK_LANES

    def body(u_ref, wb_ref, wc_ref, ar_ref, ai_ref, d_ref, y_ref, yg_ref, xr, xi):
        uv = u_ref[...]
        eb = _bd_expand(wb_ref[...]).astype(BF16)
        ec = _bd_expand(wc_ref[...]).astype(BF16)
        _ssm_states(uv.astype(BF16), eb, ar_ref[...], ai_ref[...], xr, xi)
        y = (lax.dot_general(xr[...].astype(BF16), ec[:, :nl], (NT, ((), ())), preferred_element_type=F32)
             + lax.dot_general(xi[...].astype(BF16), ec[:, nl:], (NT, ((), ())), preferred_element_type=F32)
             + d_ref[...] * uv)
        y_ref[...] = y
        yg_ref[...] = _gelu(y).astype(BF16)

    col, wsm, lam, vec = _ssm_specs(s)
    return pl.pallas_call(
        body, name=name, grid=(d // LANE,), in_specs=[col, wsm, wsm, lam, lam, vec], out_specs=[col, col],
        out_shape=[_sds((s, d), F32), _sds((s, d), BF16)],
        scratch_shapes=[pltpu.VMEM((s, nl), F32), pltpu.VMEM((s, nl), F32)],
        compiler_params=_cparams("parallel"),
    )(u, wb, wc, a_re, a_im, d_row)


def _ssm_core_bwd(u, ylin, dyg, wb, wc, a_re, a_im, d_row, *, name):
    s, d = u.shape
    nl = SSM_BLOCK_LANES
    n8 = s // SUBLANE

    def body(u_ref, y_ref, dyg_ref, wb_ref, wc_ref, ar_ref, ai_ref, d_ref,
             du_ref, dwb_ref, dwc_ref, dar_ref, dai_ref, dd_ref, xr, xi, gr, gi):
        uv = u_ref[...]
        ub = uv.astype(BF16)
        ar = ar_ref[...]
        ai = ai_ref[...]
        dy = dyg_ref[...] * _gelu_grad(y_ref[...])
        dd_ref[...] = jnp.sum(dy * uv, axis=0, keepdims=True)
        dyb = dy.astype(BF16)
        eb = _bd_expand(wb_ref[...]).astype(BF16)
        ec = _bd_expand(wc_ref[...]).astype(BF16)
        _ssm_states(ub, eb, ar, ai, xr, xi)
        dec = jnp.concatenate(
            [lax.dot_general(dyb, xr[...].astype(BF16), (TN, ((), ())), preferred_element_type=F32),
             lax.dot_general(dyb, xi[...].astype(BF16), (TN, ((), ())), preferred_element_type=F32)], axis=1)
        dwc_ref[...] = _bd_extract(dec)
        gr[...] = jnp.dot(dyb, ec[:, :nl], preferred_element_type=F32)
        gi[...] = jnp.dot(dyb, ec[:, nl:], preferred_element_type=F32)
        _scan_rows(gr, gi, ar, -ai, reverse=True)

        row = lax.broadcasted_iota(jnp.int32, (SUBLANE, nl), 0)

        def lam_grad(i, acc):
            acc_r, acc_i = acc
            rows = pl.ds(pl.multiple_of(i * SUBLANE, SUBLANE), SUBLANE)
            prev = pl.ds(pl.multiple_of(jnp.maximum(i - 1, 0) * SUBLANE, SUBLANE), SUBLANE)
            keep = jnp.where(i > 0, 1.0, 0.0)
            xpr = jnp.where(row == 0, pltpu.roll(xr[prev, :], 1, 0) * keep, pltpu.roll(xr[rows, :], 1, 0))
            xpi = jnp.where(row == 0, pltpu.roll(xi[prev, :], 1, 0) * keep, pltpu.roll(xi[rows, :], 1, 0))
            g_r = gr[rows, :]
            g_i = gi[rows, :]
            return acc_r + g_r * xpr + g_i * xpi, acc_i + g_i * xpr - g_r * xpi

        zero = jnp.zeros((SUBLANE, nl), F32)
        acc_r, acc_i = lax.fori_loop(0, n8, lam_grad, (zero, zero), unroll=2)
        dar_ref[...] = jnp.sum(acc_r, axis=0, keepdims=True)
        dai_ref[...] = jnp.sum(acc_i, axis=0, keepdims=True)

        grb = gr[...].astype(BF16)
        gib = gi[...].astype(BF16)
        deb = jnp.concatenate([lax.dot_general(ub, grb, (TN, ((), ())), preferred_element_type=F32),
                               lax.dot_general(ub, gib, (TN, ((), ())), preferred_element_type=F32)], axis=1)
        dwb_ref[...] = _bd_extract(deb)
        du_ref[...] = (lax.dot_general(grb, eb[:, :nl], (NT, ((), ())), preferred_element_type=F32)
                       + lax.dot_general(gib, eb[:, nl:], (NT, ((), ())), preferred_element_type=F32)
                       + d_ref[...] * dy)

    col, wsm, lam, vec = _ssm_specs(s)
    return pl.pallas_call(
        body, name=name, grid=(d // LANE,), in_specs=[col, col, col, wsm, wsm, lam, lam, vec],
        out_specs=[col, wsm, wsm, lam, lam, vec],
        out_shape=[_sds((s, d), F32), _sds((d, LANE), F32), _sds((d, LANE), F32),
                   _sds((1, SSM_GROUPS * SSM_STATE), F32), _sds((1, SSM_GROUPS * SSM_STATE), F32), _sds((1, d), F32)],
        scratch_shapes=[pltpu.VMEM((s, nl), F32)] * 4,
        compiler_params=_cparams("parallel"),
    )(u, ylin, dyg, wb, wc, a_re, a_im, d_row)


GLU_PIECE = 512


def _glu_fwd(yg, wp, b_row, x, *, tm, name):
    s, d = yg.shape
    tm = min(tm, s)
    half = N_DEV // 2

    def body(y_ref, wv_ref, wg_ref, bv_ref, bg_ref, x_ref, o_ref, val_ref, gate_ref):
        yv = y_ref[...]
        val = jnp.dot(yv, wv_ref[...], preferred_element_type=F32) + bv_ref[...]
        gate = jnp.dot(yv, wg_ref[...], preferred_element_type=F32) + bg_ref[...]
        val_ref[...] = val
        gate_ref[...] = gate
        o_ref[...] = x_ref[...] + val * _sigmoid(gate)

    blk = pl.BlockSpec((tm, GLU_PIECE), lambda m, n: (m, n))
    return pl.pallas_call(
        body, name=name, grid=(s // tm, half),
        in_specs=[pl.BlockSpec((tm, d), lambda m, n: (m, 0)),
                  pl.BlockSpec((None, d, GLU_PIECE), lambda m, n: (n, 0, 0)),
                  pl.BlockSpec((None, d, GLU_PIECE), lambda m, n: (n + half, 0, 0)),
                  pl.BlockSpec((1, GLU_PIECE), lambda m, n: (0, n)),
                  pl.BlockSpec((1, GLU_PIECE), lambda m, n: (0, n + half)), blk],
        out_specs=[blk, blk, blk], out_shape=[_sds((s, d), F32)] * 3,
        compiler_params=_cparams("parallel", "parallel"),
    )(yg, wp, wp, b_row, b_row, x)


def _glu_bwd(dout, val, gate, *, name):
    s, d = dout.shape

    def body(do_ref, val_ref, gate_ref, dgv_ref, db_ref):
        sg = _sigmoid(gate_ref[...])
        dov = do_ref[...]
        dgv = jnp.concatenate([dov * sg, dov * val_ref[...] * (sg * (1.0 - sg))], axis=1)
        dgv_ref[...] = dgv.astype(BF16)
        part = jnp.sum(dgv, axis=0, keepdims=True)

        @pl.when(pl.program_id(0) == 0)
        def _():
            db_ref[...] = part

        @pl.when(pl.program_id(0) > 0)
        def _():
            db_ref[...] += part

    row = pl.BlockSpec((ROW_TILE, d), lambda i: (i, 0))
    return pl.pallas_call(
        body, name=name, grid=(s // ROW_TILE,), in_specs=[row, row, row],
        out_specs=[pl.BlockSpec((ROW_TILE, 2 * d), lambda i: (i, 0)), pl.BlockSpec((1, 2 * d), lambda i: (0, 0))],
        out_shape=[_sds((s, 2 * d), BF16), _sds((1, 2 * d), F32)], compiler_params=_cparams("arbitrary"),
    )(dout, val, gate)


def _loss_head(y, target, *, name):
    s, d = y.shape

    def body(y_ref, t_ref, dy_ref, l_ref):
        e = y_ref[...] - t_ref[...]
        dy_ref[...] = e * (1.0 / d)
        part = jnp.zeros((SUBLANE, LANE), F32) + jnp.sum(e * e) * (0.5 / d)

        @pl.when(pl.program_id(0) == 0)
        def _():
            l_ref[...] = part

        @pl.when(pl.program_id(0) > 0)
        def _():
            l_ref[...] += part

    row = pl.BlockSpec((ROW_TILE, d), lambda i: (i, 0))
    return pl.pallas_call(
        body, name=name, grid=(s // ROW_TILE,), in_specs=[row, row],
        out_specs=[row, pl.BlockSpec((SUBLANE, LANE), lambda i: (0, 0))],
        out_shape=[_sds((s, d), F32), _sds((SUBLANE, LANE), F32)], compiler_params=_cparams("arbitrary"),
    )(y, target)


def _adamw_math(w, g, m, v):
    m = ADAM_B1 * m + (1.0 - ADAM_B1) * g
    v = ADAM_B2 * v + (1.0 - ADAM_B2) * (g * g)
    m_hat = m / (1.0 - ADAM_B1 ** ADAM_STEP)
    v_hat = v / (1.0 - ADAM_B2 ** ADAM_STEP)
    return -ADAM_LR * (m_hat / (jnp.sqrt(v_hat) + ADAM_EPS) + ADAM_WD * w), m, v


ADAM_ROWS = 64
PACK_ROWS = 64


def _sum_pieces(p_ref):
    g = p_ref[0].astype(F32)
    for k in range(1, N_DEV):
        g = g + p_ref[k].astype(F32)
    return g


def _adamw_pieces(w, pieces, m, v, *, name):
    r, c = w.shape

    def body(w_ref, p_ref, m_ref, v_ref, g_ref, d_ref, nm_ref, nv_ref):
        g = _sum_pieces(p_ref)
        g_ref[...] = g
        d_ref[...], nm_ref[...], nv_ref[...] = _adamw_math(w_ref[...], g, m_ref[...], v_ref[...])

    blk = pl.BlockSpec((ADAM_ROWS, c), lambda i: (i, 0))
    return pl.pallas_call(
        body, name=name, grid=(r // ADAM_ROWS,),
        in_specs=[blk, pl.BlockSpec((N_DEV, ADAM_ROWS, c), lambda i: (0, i, 0)), blk, blk],
        out_specs=[blk] * 4, out_shape=[_sds((r, c), F32)] * 4, compiler_params=_cparams("parallel"),
    )(w, pieces, m, v)


def _sum_parts(parts, *, name):
    _, r, c = parts.shape

    def body(p_ref, o_ref):
        o_ref[...] = _sum_pieces(p_ref)

    return pl.pallas_call(
        body, name=name, grid=(r // PACK_ROWS,),
        in_specs=[pl.BlockSpec((N_DEV, PACK_ROWS, c), lambda i: (0, i, 0))],
        out_specs=pl.BlockSpec((PACK_ROWS, c), lambda i: (i, 0)), out_shape=_sds((r, c), F32),
        compiler_params=_cparams("parallel"),
    )(parts)


def _adamw_flat(w, g, m, v, *, name):
    r, c = w.shape

    def body(w_ref, g_ref, m_ref, v_ref, d_ref, nm_ref, nv_ref):
        d_ref[...], nm_ref[...], nv_ref[...] = _adamw_math(w_ref[...], g_ref[...], m_ref[...], v_ref[...])

    blk = pl.BlockSpec((PACK_ROWS, c), lambda i: (i, 0))
    return pl.pallas_call(
        body, name=name, grid=(r // PACK_ROWS,), in_specs=[blk] * 4, out_specs=[blk] * 3,
        out_shape=[_sds((r, c), F32)] * 3, compiler_params=_cparams("parallel"),
    )(w, g, m, v)


_ANY = pl.BlockSpec(memory_space=pl.ANY)


def _place():
    return lax.axis_index("x"), lax.axis_index("y"), lax.axis_index("c")


def _slot(px, py, pc):
    return 4 * px + 2 * py + pc


def _all_gather(xs, *, name):
    n = len(xs)

    def body(*refs):
        ins, outs = refs[:n], refs[n:2 * n]
        send_sems, recv_sems, local_sems = refs[2 * n:]
        x, y, c = _place()
        me, sibling = (x, y, c), (x, y, 1 - c)
        chips = [(1 - x, y), (x, 1 - y), (1 - x, 1 - y)]

        def copy(i, k, block, to, src=None):
            rows = outs[i].at[_slot(*block)]
            return pltpu.make_async_remote_copy(
                src_ref=rows if src is None else src, dst_ref=rows, send_sem=send_sems.at[i, k],
                recv_sem=recv_sems.at[i, k], device_id=to, device_id_type=MESH)

        mine = [pltpu.make_async_copy(ins[i], outs[i].at[_slot(*me)], local_sems.at[i]) for i in range(n)]
        for cp in mine:
            cp.start()
        first = []
        for i in range(n):
            first.append(copy(i, 0, me, sibling, src=ins[i]))
            first += [copy(i, 1 + j, me, (*chip, c), src=ins[i]) for j, chip in enumerate(chips)]
        for cp in first:
            cp.start()
        passed = []
        for j, chip in enumerate(chips):
            for i in range(n):
                copy(i, 1 + j, (*chip, c), me).wait_recv()
                cp = copy(i, 4 + j, (*chip, c), sibling)
                cp.start()
                passed.append(cp)
        for i in range(n):
            copy(i, 0, sibling, me).wait_recv()
            for j, chip in enumerate(chips):
                copy(i, 4 + j, (*chip, 1 - c), me).wait_recv()
        for cp in first + passed:
            cp.wait_send()
        for cp in mine:
            cp.wait()

    return pl.pallas_call(
        body, name=name, in_specs=[_ANY] * n, out_specs=[_ANY] * n,
        out_shape=[_sds((N_DEV,) + a.shape, a.dtype) for a in xs],
        scratch_shapes=[pltpu.SemaphoreType.DMA((n, 7)), pltpu.SemaphoreType.DMA((n, 7)), pltpu.SemaphoreType.DMA((n,))],
    )(*xs)


def _exchange_pieces(gs, *, name):
    n = len(gs)
    flips = [(dx, dy, dc) for dx in (0, 1) for dy in (0, 1) for dc in (0, 1)][1:]

    def body(*refs):
        ins, outs = refs[:n], refs[n:2 * n]
        send_sems, recv_sems, local_sems = refs[2 * n:]
        x, y, c = _place()
        me = _slot(x, y, c)
        peers = [((1 - x) if dx else x, (1 - y) if dy else y, (1 - c) if dc else c) for dx, dy, dc in flips]

        def copy(i, k):
            return pltpu.make_async_remote_copy(
                src_ref=ins[i].at[_slot(*peers[k])], dst_ref=outs[i].at[me], send_sem=send_sems.at[i, k],
                recv_sem=recv_sems.at[i, k], device_id=peers[k], device_id_type=MESH)

        def landing(i, k):
            rows = outs[i].at[_slot(*peers[k])]
            return pltpu.make_async_remote_copy(
                src_ref=rows, dst_ref=rows, send_sem=send_sems.at[i, k], recv_sem=recv_sems.at[i, k],
                device_id=peers[k], device_id_type=MESH)

        mine = [pltpu.make_async_copy(ins[i].at[me], outs[i].at[me], local_sems.at[i]) for i in range(n)]
        for cp in mine:
            cp.start()
        sends = [copy(i, k) for i in range(n) for k in range(len(flips))]
        for cp in sends:
            cp.start()
        for i in range(n):
            for k in range(len(flips)):
                landing(i, k).wait_recv()
        for cp in sends:
            cp.wait_send()
        for cp in mine:
            cp.wait()

    return pl.pallas_call(
        body, name=name, in_specs=[_ANY] * n, out_specs=[_ANY] * n,
        out_shape=[_sds(a.shape, a.dtype) for a in gs],
        scratch_shapes=[pltpu.SemaphoreType.DMA((n, 7)), pltpu.SemaphoreType.DMA((n, 7)), pltpu.SemaphoreType.DMA((n,))],
    )(*gs)


_HBM = pl.BlockSpec(memory_space=pltpu.HBM)
_SEM = pl.BlockSpec(memory_space=pltpu.SEMAPHORE)
_EFFECT = pltpu.SideEffectType.DATAFLOW_SIDE_EFFECTING
_CHIP_FLIPS = ((1, 0), (0, 1), (1, 1))
_TOKEN = (SUBLANE, LANE)


def _flip(v, f):
    return (1 - v) if f else v


def _in_hbm(a):
    return pltpu.with_memory_space_constraint(a, pltpu.HBM)


def _ag_peers(x, y, c):
    return [(x, y, 1 - c)] + [(_flip(x, dx), _flip(y, dy), c) for dx, dy in _CHIP_FLIPS]


def _ag_copies(x_refs, land_refs, send_sems, recv_sems, *, landing):
    x, y, c = _place()
    peers = _ag_peers(x, y, c)
    cps = []
    for i in range(len(x_refs)):
        for k, peer in enumerate(peers):
            origin = _slot(*peer) if landing else _slot(x, y, c)
            cps.append(pltpu.make_async_remote_copy(
                src_ref=x_refs[i], dst_ref=land_refs[i].at[origin], send_sem=send_sems.at[4 * i + k],
                recv_sem=recv_sems.at[4 * i + k], device_id=peer, device_id_type=MESH))
    return cps


OWN_BLOCK_BYTES = 4 * 1024 * 1024


def _place_own(x, me, *, name):
    r, c = x.shape
    tr = r if r * c * x.dtype.itemsize <= OWN_BLOCK_BYTES else r // 4

    def body(me_ref, x_ref, o_ref):
        o_ref[...] = x_ref[...]

    return pl.pallas_call(
        body, name=name,
        grid_spec=pltpu.PrefetchScalarGridSpec(
            num_scalar_prefetch=1, grid=(r // tr,),
            in_specs=[pl.BlockSpec((tr, c), lambda i, me_ref: (i, 0))],
            out_specs=pl.BlockSpec((None, tr, c), lambda i, me_ref: (me_ref[0], i, 0))),
        out_shape=_sds((N_DEV, r, c), x.dtype), compiler_params=_cparams("parallel"),
    )(me, x)


def _ag_start(xs, *, name, dep=None):
    n = len(xs)
    me = _slot(*_place()).astype(jnp.int32).reshape(1)
    lands = [_place_own(a, me, name=f"{name}_own_{i}") for i, a in enumerate(xs)]
    n_in = 2 * n + (0 if dep is None else 1)

    def body(*refs):
        x_refs, land_refs = refs[:n], refs[n:2 * n]
        send_sems, recv_sems = refs[n_in], refs[n_in + 1]
        token = refs[-1]
        for cp in _ag_copies(x_refs, land_refs, send_sems, recv_sems, landing=False):
            cp.start()
        token[...] = jnp.zeros_like(token)

    out = pl.pallas_call(
        body, name=name,
        out_shape=(pltpu.SemaphoreType.DMA((4 * n,)), pltpu.SemaphoreType.DMA((4 * n,)),
                   *[pltpu.HBM(a.shape, a.dtype) for a in xs], *[pltpu.HBM(a.shape, a.dtype) for a in lands],
                   _sds(_TOKEN, F32)),
        in_specs=[_HBM] * (2 * n) + ([] if dep is None else [_ANY]),
        out_specs=(_SEM, _SEM, *[_HBM] * (2 * n), pl.BlockSpec(memory_space=pltpu.VMEM)),
        input_output_aliases={i: 2 + i for i in range(2 * n)},
        compiler_params=pltpu.CompilerParams(has_side_effects=_EFFECT),
    )(*[_in_hbm(a) for a in xs], *[_in_hbm(a) for a in lands], *(() if dep is None else (dep,)))
    return (out[0], out[1], list(out[2:2 + n]), list(out[2 + n:2 + 2 * n])), out[-1]


def _ag_wait(started, after, *, name):
    send_sems, recv_sems, xs, lands = started
    n = len(xs)

    def body(*refs):
        x_refs, land_refs = refs[:n], refs[n:2 * n]
        for cp in _ag_copies(x_refs, land_refs, refs[2 * n], refs[2 * n + 1], landing=True):
            cp.wait_send()
            cp.wait_recv()

    out = pl.pallas_call(
        body, name=name,
        out_shape=tuple(pltpu.HBM(a.shape, a.dtype) for a in xs + lands),
        in_specs=[_HBM] * (2 * n) + [_SEM, _SEM, _ANY], out_specs=tuple([_HBM] * (2 * n)),
        input_output_aliases={i: i for i in range(2 * n)},
        compiler_params=pltpu.CompilerParams(has_side_effects=_EFFECT),
    )(*xs, *lands, send_sems, recv_sems, after)
    return list(out[:n]), list(out[n:])


def _ag_forward(lands, *, name):
    n = len(lands)

    def body(*refs):
        land = refs[n:2 * n]
        send_sems, recv_sems = refs[2 * n:]
        x, y, c = _place()
        sibling = (x, y, 1 - c)

        def copy(i, j, core):
            dx, dy = _CHIP_FLIPS[j]
            rows = land[i].at[_slot(_flip(x, dx), _flip(y, dy), core)]
            return pltpu.make_async_remote_copy(src_ref=rows, dst_ref=rows, send_sem=send_sems.at[i, j],
                                                recv_sem=recv_sems.at[i, j], device_id=sibling, device_id_type=MESH)

        sends = [copy(i, j, c) for i in range(n) for j in range(3)]
        for cp in sends:
            cp.start()
        for i in range(n):
            for j in range(3):
                copy(i, j, 1 - c).wait_recv()
        for cp in sends:
            cp.wait_send()

    return pl.pallas_call(
        body, name=name, in_specs=[_ANY] * n, out_specs=[_ANY] * n,
        out_shape=[_sds(a.shape, a.dtype) for a in lands], input_output_aliases={i: i for i in range(n)},
        scratch_shapes=[pltpu.SemaphoreType.DMA((n, 3)), pltpu.SemaphoreType.DMA((n, 3))],
    )(*lands)


def _sib_copies(g_refs, land_refs, send_sems, recv_sems):
    x, y, c = _place()
    return [pltpu.make_async_remote_copy(
        src_ref=g_refs[i].at[:, 1 - c], dst_ref=land_refs[i], send_sem=send_sems.at[i], recv_sem=recv_sems.at[i],
        device_id=(x, y, 1 - c), device_id_type=MESH) for i in range(len(g_refs))]


def _sib_start(g4s, *, name):
    n = len(g4s)
    lands = [lax.empty((4,) + a.shape[2:], a.dtype) for a in g4s]

    def body(*refs):
        for cp in _sib_copies(refs[:n], refs[n:2 * n], refs[2 * n], refs[2 * n + 1]):
            cp.start()
        refs[-1][...] = jnp.zeros_like(refs[-1])

    out = pl.pallas_call(
        body, name=name,
        out_shape=(pltpu.SemaphoreType.DMA((n,)), pltpu.SemaphoreType.DMA((n,)),
                   *[pltpu.HBM(a.shape, a.dtype) for a in g4s], *[pltpu.HBM(a.shape, a.dtype) for a in lands],
                   _sds(_TOKEN, F32)),
        in_specs=[_HBM] * (2 * n),
        out_specs=(_SEM, _SEM, *[_HBM] * (2 * n), pl.BlockSpec(memory_space=pltpu.VMEM)),
        input_output_aliases={i: 2 + i for i in range(2 * n)},
        compiler_params=pltpu.CompilerParams(has_side_effects=_EFFECT),
    )(*[_in_hbm(a) for a in g4s], *[_in_hbm(a) for a in lands])
    return (out[0], out[1], list(out[2:2 + n]), list(out[2 + n:2 + 2 * n])), out[-1]


def _sib_wait(started, after, *, name):
    send_sems, recv_sems, g4s, lands = started
    n = len(g4s)

    def body(*refs):
        for cp in _sib_copies(refs[:n], refs[n:2 * n], refs[2 * n], refs[2 * n + 1]):
            cp.wait_send()
            cp.wait_recv()

    out = pl.pallas_call(
        body, name=name,
        out_shape=tuple(pltpu.HBM(a.shape, a.dtype) for a in g4s + lands),
        in_specs=[_HBM] * (2 * n) + [_SEM, _SEM, _ANY], out_specs=tuple([_HBM] * (2 * n)),
        input_output_aliases={i: i for i in range(2 * n)},
        compiler_params=pltpu.CompilerParams(has_side_effects=_EFFECT),
    )(*g4s, *lands, send_sems, recv_sems, after)
    return list(out[:n]), list(out[n:])


PAIR_SUM_BLOCK_BYTES = 3 * 1024 * 1024


def _rs_rows(r, c):
    tr = r
    while tr * c * 2 > PAIR_SUM_BLOCK_BYTES and tr % 2 == 0:
        tr //= 2
    return tr


def _rs_pair_sum(g4, from_sibling, core, *, name):
    _, _, r, c = g4.shape
    tr = _rs_rows(r, c)

    def body(core_ref, g_ref, a_ref, o_ref):
        o_ref[...] = (g_ref[...].astype(F32) + a_ref[...].astype(F32)).astype(BF16)

    blk = pl.BlockSpec((None, tr, c), lambda k, i, core_ref: (k, i, 0))
    return pl.pallas_call(
        body, name=name,
        grid_spec=pltpu.PrefetchScalarGridSpec(
            num_scalar_prefetch=1, grid=(4, r // tr),
            in_specs=[pl.BlockSpec((None, None, tr, c), lambda k, i, core_ref: (k, core_ref[0], i, 0)), blk],
            out_specs=blk),
        out_shape=_sds((4, r, c), BF16), compiler_params=_cparams("parallel", "parallel"),
    )(core, g4, from_sibling)


def _rs_copies(h_refs, land_refs, send_sems, recv_sems):
    x, y, c = _place()
    cps = []
    for i in range(len(h_refs)):
        for k, (dx, dy) in enumerate(_CHIP_FLIPS):
            px, py = _flip(x, dx), _flip(y, dy)
            cps.append(pltpu.make_async_remote_copy(
                src_ref=h_refs[i].at[2 * px + py], dst_ref=land_refs[i].at[k], send_sem=send_sems.at[3 * i + k],
                recv_sem=recv_sems.at[3 * i + k], device_id=(px, py, c), device_id_type=MESH))
    return cps


def _rs_start(hs, *, name):
    n = len(hs)
    lands = [lax.empty((3,) + a.shape[1:], a.dtype) for a in hs]

    def body(*refs):
        h_refs, land_refs = refs[:n], refs[n:2 * n]
        token = refs[-1]
        for cp in _rs_copies(h_refs, land_refs, refs[2 * n], refs[2 * n + 1]):
            cp.start()
        token[...] = jnp.zeros_like(token)

    out = pl.pallas_call(
        body, name=name,
        out_shape=(pltpu.SemaphoreType.DMA((3 * n,)), pltpu.SemaphoreType.DMA((3 * n,)),
                   *[pltpu.HBM(a.shape, a.dtype) for a in hs], *[pltpu.HBM(a.shape, a.dtype) for a in lands],
                   _sds(_TOKEN, F32)),
        in_specs=[_HBM] * (2 * n),
        out_specs=(_SEM, _SEM, *[_HBM] * (2 * n), pl.BlockSpec(memory_space=pltpu.VMEM)),
        input_output_aliases={i: 2 + i for i in range(2 * n)},
        compiler_params=pltpu.CompilerParams(has_side_effects=_EFFECT),
    )(*[_in_hbm(a) for a in hs], *[_in_hbm(a) for a in lands])
    return (out[0], out[1], list(out[2:2 + n]), list(out[2 + n:2 + 2 * n])), out[-1]


def _rs_wait(started, after, *, name):
    send_sems, recv_sems, hs, lands = started
    n = len(hs)

    def body(*refs):
        for cp in _rs_copies(refs[:n], refs[n:2 * n], refs[2 * n], refs[2 * n + 1]):
            cp.wait_send()
            cp.wait_recv()

    out = pl.pallas_call(
        body, name=name,
        out_shape=tuple(pltpu.HBM(a.shape, a.dtype) for a in hs + lands),
        in_specs=[_HBM] * (2 * n) + [_SEM, _SEM, _ANY], out_specs=tuple([_HBM] * (2 * n)),
        input_output_aliases={i: i for i in range(2 * n)},
        compiler_params=pltpu.CompilerParams(has_side_effects=_EFFECT),
    )(*hs, *lands, send_sems, recv_sems, after)
    return list(out[:n]), list(out[n:])


def _adamw_chips(w, h, others, m, v, chip, *, name):
    r, c = w.shape

    def body(chip_ref, w_ref, h_ref, o_ref, m_ref, v_ref, g_ref, d_ref, nm_ref, nv_ref):
        g = h_ref[...].astype(F32)
        for k in range(3):
            g = g + o_ref[k].astype(F32)
        g_ref[...] = g
        d_ref[...], nm_ref[...], nv_ref[...] = _adamw_math(w_ref[...], g, m_ref[...], v_ref[...])

    blk = pl.BlockSpec((ADAM_ROWS, c), lambda i, chip_ref: (i, 0))
    return pl.pallas_call(
        body, name=name,
        grid_spec=pltpu.PrefetchScalarGridSpec(
            num_scalar_prefetch=1, grid=(r // ADAM_ROWS,),
            in_specs=[blk, pl.BlockSpec((None, ADAM_ROWS, c), lambda i, chip_ref: (chip_ref[0], i, 0)),
                      pl.BlockSpec((3, ADAM_ROWS, c), lambda i, chip_ref: (0, i, 0)), blk, blk],
            out_specs=[blk] * 4),
        out_shape=[_sds((r, c), F32)] * 4, compiler_params=_cparams("parallel"),
    )(chip, w, h, others, m, v)


def _adamw_chips_layers(w, hs, others, m, v, chip, *, first, prev=None, name):
    nl, r, c = w.shape
    n = len(hs)
    n_prev = 0 if prev is None else 4

    def body(chip_ref, w_ref, m_ref, v_ref, *rest):
        h_refs, o_refs = rest[:n], rest[n:2 * n]
        g_ref, d_ref, nm_ref, nv_ref = rest[2 * n + n_prev:]
        k_now = pl.program_id(0)
        g = jnp.zeros((ADAM_ROWS, c), F32)
        for k in range(n):
            gk = h_refs[k][...].astype(F32)
            for j in range(3):
                gk = gk + o_refs[k][j].astype(F32)
            g = jnp.where(k_now == k, gk, g)
        g_ref[...] = g
        d_ref[...], nm_ref[...], nv_ref[...] = _adamw_math(w_ref[...], g, m_ref[...], v_ref[...])

    def rows(k):
        return lambda l, i: jnp.where(l == k, i, 0)

    blk = pl.BlockSpec((None, ADAM_ROWS, c), lambda l, i, chip_ref: (l + first, i, 0))
    h_specs = [pl.BlockSpec((None, ADAM_ROWS, c), lambda l, i, chip_ref, f=rows(k): (chip_ref[0], f(l, i), 0))
               for k in range(n)]
    o_specs = [pl.BlockSpec((3, ADAM_ROWS, c), lambda l, i, chip_ref, f=rows(k): (0, f(l, i), 0)) for k in range(n)]
    return pl.pallas_call(
        body, name=name,
        grid_spec=pltpu.PrefetchScalarGridSpec(
            num_scalar_prefetch=1, grid=(n, r // ADAM_ROWS),
            in_specs=[blk, blk, blk] + h_specs + o_specs + [_ANY] * n_prev, out_specs=[blk] * 4),
        out_shape=[_sds((nl, r, c), F32)] * 4,
        input_output_aliases={4 + 2 * n + q: q for q in range(n_prev)},
        compiler_params=_cparams("arbitrary", "arbitrary"),
    )(chip, w, m, v, *hs, *others, *(() if prev is None else prev))


def _gather_begin(xs, tag, dep=None):
    return _ag_start(xs, name=f"ag_start_{tag}", dep=dep)


def _gather_end(started, after, tag):
    _, lands = _ag_wait(started, after, name=f"ag_wait_{tag}")
    return _ag_forward(lands, name=f"ag_forward_{tag}")


def _scatter_pair(gs, tag):
    g4s = [g.reshape((4, 2) + g.shape[1:]) for g in gs]
    return _sib_start(g4s, name=f"rs_sib_start_{tag}")


def _scatter_chips(pair, after, tag):
    core = lax.axis_index("c").astype(jnp.int32).reshape(1)
    g4s, got = _sib_wait(pair, after, name=f"rs_sib_wait_{tag}")
    hs = [_rs_pair_sum(g4, a, core, name=f"rs_pair_sum_{tag}_{i}") for i, (g4, a) in enumerate(zip(g4s, got))]
    return _rs_start(hs, name=f"rs_start_{tag}")


def _scatter_end(started, after, tag):
    return _rs_wait(started, after, name=f"rs_wait_{tag}")


MM_TM = 512


def _ffn_fwd(x_mid, g_row, wup_p, wdown, conv_w, conv_b, tag, dep=None):
    hb, _ = _rms_fwd(x_mid, g_row, want_f32=False, name=f"ffn_norm_{tag}", dep=dep)
    up = _mm_nn_pieces(hb, wup_p, tm=MM_TM, name=f"ffn_up_{tag}")
    a = _conv_gate_fwd(up, conv_w, conv_b, name=f"ffn_conv_{tag}")
    if callable(wdown):
        wdown = wdown(a)
    x_out = _mm_nn(a, wdown, tm=1024, tn=512, name=f"ffn_down_{tag}", res=x_mid)
    return x_out, (hb, up, a)


def _ffn_bwd(dx, x_mid, g_row, wup_p, wdown, conv_w, conv_b, saved, tag):
    hb, up, a = saved
    dxb = dx.astype(BF16)
    da = _mm_nt(dxb, wdown, tm=MM_TM, tn=1408, name=f"ffn_da_{tag}")
    dwdown = _mm_tn(a, dxb, tm=512, tn=1024, name=f"ffn_dwdown_{tag}")
    dup_v, dup_g, dconv_w, dconv_b = _conv_gate_bwd(up, da, conv_w, conv_b, name=f"ffn_dconv_{tag}")
    dwup = _mm_tn_pieces(hb, (dup_v, dup_g), pieces=N_DEV, tm=MM_TM, name=f"ffn_dwup_{tag}")
    pair, token = _scatter_pair([dwup, dwdown.reshape(N_DEV, D_FF // N_DEV, D_MODEL)], f"ffn_{tag}")
    dh = _mm_nt_pieces((dup_v, dup_g), wup_p, tm=1024, tn=1024, name=f"ffn_dh_{tag}", dep=token)
    started, token = _scatter_chips(pair, dh, f"ffn_{tag}")
    dx_mid, dg = _rms_bwd(x_mid, g_row, dh, dx, name=f"ffn_dnorm_{tag}", dep=token)
    return dx_mid, dg, dconv_w, dconv_b, started


def _pool_layer_fwd(x, g_row, w, b_row, sc_row, tag, dep=None):
    _, hf = _rms_fwd(x, g_row, want_f32=True, name=f"pool_norm_{tag}", dep=dep)
    return _pool_fwd(hf, x, w, b_row, sc_row, name=f"pool_fwd_{tag}"), (hf,)


def _pool_layer_bwd(dx_mid, x, g_row, w, b_row, sc_row, saved, tag):
    (hf,) = saved
    dh, dw, db, dsc = _pool_bwd(hf, dx_mid, w, b_row, sc_row, name=f"pool_bwd_{tag}")
    dx, dg = _rms_bwd(x, g_row, dh, dx_mid, name=f"pool_dnorm_{tag}")
    return dx, dg, dw, db, dsc


def _sb_layer_fwd(x, g_row, wqkv_p, gains, wo, tag):
    hb, _ = _rms_fwd(x, g_row, want_f32=False, name=f"sb_norm_{tag}")
    qkv = _mm_nn_pieces(hb, wqkv_p, tm=MM_TM, name=f"sb_qkv_{tag}")
    qkn = _qk_norm_fwd(qkv, gains, name=f"sb_qknorm_{tag}")
    vb = qkv[:, 2 * D_MODEL:].astype(BF16)
    o = _sb_fwd(qkn, vb, name=f"sb_att_{tag}")
    x_mid = _mm_nn(o, wo, tm=MM_TM, tn=512, name=f"sb_out_{tag}", res=x)
    return x_mid, (hb, qkv, qkn, vb, o)


def _sb_layer_bwd(dx_mid, x, g_row, wqkv_p, gains, wo, saved, tag):
    hb, qkv, qkn, vb, o = saved
    dmb = dx_mid.astype(BF16)
    do = _mm_nt(dmb, wo, tm=MM_TM, tn=512, name=f"sb_do_{tag}", out_dtype=BF16)
    dwo = _mm_tn(o, dmb, tm=512, tn=1024, name=f"sb_dwo_{tag}")
    dqn, dkn, dv = _sb_bwd(qkn, vb, do, name=f"sb_datt_{tag}")
    dq, dqg = _qk_norm_bwd(qkv, gains, dqn, which=0, name=f"sb_dqnorm_{tag}")
    dk, dkg = _qk_norm_bwd(qkv, gains, dkn, which=1, name=f"sb_dknorm_{tag}")
    dqkv = jnp.concatenate([dq, dk, dv.astype(BF16)], axis=1)
    dwqkv = _mm_tn_pieces(hb, dqkv, pieces=N_DEV, tm=MM_TM, name=f"sb_dwqkv_{tag}")
    pair, token = _scatter_pair([dwqkv, dwo.reshape(N_DEV, D_MODEL // N_DEV, D_MODEL)], f"sb_{tag}")
    dh = _mm_nt_pieces(dqkv, wqkv_p, tm=MM_TM, tn=1024, name=f"sb_dh_{tag}", dep=token)
    started, token = _scatter_chips(pair, dh, f"sb_{tag}")
    dx, dg = _rms_bwd(x, g_row, dh, dx_mid, name=f"sb_dnorm_{tag}", dep=token)
    return dx, dg, dqg, dkg, started


def _ssm_params(lam_re, lam_im, log_step, b_re, b_im):
    g, p = SSM_GROUPS, SSM_STATE
    return (lam_re.reshape(g, 1, p), lam_im.reshape(g, 1, p), log_step.reshape(g, 1, 1),
            jnp.transpose(b_re, (0, 2, 1)), jnp.transpose(b_im, (0, 2, 1)))


def _ssm_layer_fwd(x, g_row, raw, c_re, c_im, d_row, wglu_p, bglu_row, tag):
    g, p, ch = SSM_GROUPS, SSM_STATE, SSM_CH
    _, hf = _rms_fwd(x, g_row, want_f32=True, name=f"ssm_norm_{tag}")
    ar, ai, bbr, bbi = _ssm_prep_fwd(*raw, name=f"ssm_prep_{tag}")
    wb = jnp.concatenate([bbr.reshape(g * ch, p), bbi.reshape(g * ch, p)], axis=1)
    wc = jnp.concatenate([c_re.reshape(g * ch, p), -c_im.reshape(g * ch, p)], axis=1)
    a_re, a_im = ar.reshape(1, g * p), ai.reshape(1, g * p)
    ylin, yg = _ssm_core_fwd(hf, wb, wc, a_re, a_im, d_row, name=f"ssm_core_{tag}")
    x_mid, val, gate = _glu_fwd(yg, wglu_p, bglu_row, x, tm=MM_TM, name=f"ssm_glu_{tag}")
    return x_mid, (hf, wb, wc, a_re, a_im, ylin, yg, val, gate)


def _ssm_layer_bwd(dx_mid, x, g_row, raw, d_row, wglu_p, saved, tag):
    g, p, ch = SSM_GROUPS, SSM_STATE, SSM_CH
    hf, wb, wc, a_re, a_im, ylin, yg, val, gate = saved
    dgv, dbglu = _glu_bwd(dx_mid, val, gate, name=f"ssm_dglu_{tag}")
    dwglu = _mm_tn_pieces(yg, dgv, pieces=N_DEV, tm=MM_TM, name=f"ssm_dwglu_{tag}")
    pair, token = _scatter_pair([dwglu], f"ssm_{tag}")
    dyg = _mm_nt_pieces(dgv, wglu_p, tm=MM_TM, tn=1024, name=f"ssm_dyg_{tag}", dep=token)
    du, dwb, dwc, dar, dai, dd = _ssm_core_bwd(hf, ylin, dyg, wb, wc, a_re, a_im, d_row, name=f"ssm_dcore_{tag}")
    dc_re = dwc[:, :p].reshape(g, ch, p)
    dc_im = -dwc[:, p:].reshape(g, ch, p)
    dlr, dli, dls, dbtr, dbti = _ssm_prep_bwd(
        *raw, dar.reshape(g, 1, p), dai.reshape(g, 1, p), dwb[:, :p].reshape(g, ch, p), dwb[:, p:].reshape(g, ch, p),
        name=f"ssm_dprep_{tag}")
    started, token = _scatter_chips(pair, du, f"ssm_{tag}")
    dx, dg = _rms_bwd(x, g_row, du, dx_mid, name=f"ssm_dnorm_{tag}", dep=token)
    grads = dict(ssm_lam_re=dlr.reshape(1, g, p), ssm_lam_im=dli.reshape(1, g, p), ssm_log_step=dls.reshape(1, g),
                 ssm_b_re=jnp.transpose(dbtr, (0, 2, 1))[None], ssm_b_im=jnp.transpose(dbti, (0, 2, 1))[None],
                 ssm_c_re=dc_re[None], ssm_c_im=dc_im[None], ssm_d=dd, ssm_b_glu=dbglu)
    return dx, dg, grads, started


_WEIGHTS = ["norm_mix_g", "norm_ffn_g", "pool_w", "pool_b", "pool_scale", "sb_w_qkv", "sb_q_gain", "sb_k_gain", "sb_w_o",
            "ssm_lam_re", "ssm_lam_im", "ssm_log_step", "ssm_b_re", "ssm_b_im", "ssm_c_re", "ssm_c_im", "ssm_d",
            "ssm_w_glu", "ssm_b_glu", "ffn_w_up", "ffn_conv_w", "ffn_conv_b", "ffn_w_down"]
_INPUTS = ["x"] + _WEIGHTS + ["loss_target"] + ["m_" + n for n in _WEIGHTS] + ["v_" + n for n in _WEIGHTS]
_REPLICATED = ["norm_mix_g", "norm_ffn_g", "sb_q_gain", "sb_k_gain", "ssm_lam_re", "ssm_lam_im", "ssm_log_step",
               "ssm_b_re", "ssm_b_im", "ssm_c_re", "ssm_c_im", "ffn_conv_b"]
_SMALL_SHARDED = {"pool_b": 1, "pool_scale": 1, "ssm_d": 1, "ssm_b_glu": 1, "ffn_conv_w": 2}
_BIG = ["pool_w", "sb_w_qkv", "sb_w_o", "ssm_w_glu", "ffn_w_up", "ffn_w_down"]
PACK_COLS = 512


def _pack(arrays):
    flat = jnp.concatenate([a.reshape(-1).astype(F32) for a in arrays])
    rows = -(-flat.shape[0] // (PACK_COLS * PACK_ROWS)) * PACK_ROWS
    return jnp.pad(flat, (0, rows * PACK_COLS - flat.shape[0])).reshape(rows, PACK_COLS)


def _unpack(packed, shapes, lead=()):
    flat = packed.reshape(lead + (-1,))
    out, off = [], 0
    for shp in shapes:
        n = math.prod(shp)
        out.append(flat[..., off:off + n].reshape(lead + tuple(shp)))
        off += n
    return out


def _unshard(gathered, axis):
    g = jnp.moveaxis(gathered, 0, axis)
    shp = g.shape
    return g.reshape(shp[:axis] + (shp[axis] * shp[axis + 1],) + shp[axis + 2:])


def _step(p):
    s = p["x"].shape[1]
    x = p["x"].reshape(s, D_MODEL)
    me = _slot(*_place())

    small_local = [p[n] for n in _SMALL_SHARDED]
    pool_w_l = p["pool_w"].astype(BF16).reshape(-1, POOL_DIM)
    chip = (2 * lax.axis_index("x") + lax.axis_index("y")).astype(jnp.int32).reshape(1)

    def ffn_shards(i):
        return [p["ffn_w_up"][i].astype(BF16), p["ffn_w_down"][i].astype(BF16)]

    st_first, tok = _gather_begin([pool_w_l, _pack(small_local)], "first")
    st_ffn, tok_ffn = [None] * DEPTH, [None] * DEPTH
    st_ffn[0], tok = _gather_begin(ffn_shards(0)[:1], "ffn_0", dep=tok)
    st_down0, tok = _gather_begin(ffn_shards(0)[1:], "ffn_0_down", dep=tok)
    st_mix, tok_ffn[0] = _gather_begin([p["sb_w_qkv"][0].astype(BF16), p["sb_w_o"][0].astype(BF16),
                                        p["ssm_w_glu"][0].astype(BF16)], "mixers", dep=tok)
    ag = _gather_end(st_first, tok_ffn[0], "first")
    n_pool = p["pool_w"].shape[0]
    pool_w = jnp.transpose(ag[0].reshape(N_DEV, n_pool, POOL_GROUPS, POOL_DIM // N_DEV, POOL_DIM), (1, 2, 0, 3, 4))
    pool_w = pool_w.reshape(n_pool, POOL_GROUPS, POOL_DIM, POOL_DIM)
    small_full = {}
    for n, g in zip(_SMALL_SHARDED, _unpack(ag[1], [a.shape for a in small_local], lead=(N_DEV,))):
        small_full[n] = _unshard(g, _SMALL_SHARDED[n])
    mix_w = {}
    wup_p, wdown = [None] * DEPTH, [None] * DEPTH

    gains = jnp.stack([p["sb_q_gain"][0], p["sb_k_gain"][0]])[:, None, :]
    ssm_raw = _ssm_params(p["ssm_lam_re"][0], p["ssm_lam_im"][0], p["ssm_log_step"][0], p["ssm_b_re"][0],
                          p["ssm_b_im"][0])

    def mixer_args(i):
        j = i // 3
        g_row = p["norm_mix_g"][i][None]
        if i % 3 == 0:
            return (g_row, pool_w[j], small_full["pool_b"][j][None], small_full["pool_scale"][j][None])
        if i % 3 == 1:
            return (g_row, mix_w["qkv"], gains, mix_w["o"])
        return (g_row, ssm_raw, p["ssm_c_re"][0], p["ssm_c_im"][0], small_full["ssm_d"], mix_w["glu"],
                small_full["ssm_b_glu"])

    def ffn_args(i):
        return (p["norm_ffn_g"][i][None], wup_p[i], wdown[i], small_full["ffn_conv_w"][i], p["ffn_conv_b"][i][None])

    xs_in, xs_mid, saved_mix, saved_ffn = [], [], [], []
    for i in range(DEPTH):
        xs_in.append(x)
        if i == 1:
            mix_w["qkv"], wo_g, mix_w["glu"] = _gather_end(st_mix, x, "mixers")
            mix_w["o"] = wo_g.reshape(D_MODEL, D_MODEL)
        fwd = (_pool_layer_fwd, _sb_layer_fwd, _ssm_layer_fwd)[i % 3]
        x, sv = fwd(x, *mixer_args(i), f"l{i}")
        saved_mix.append(sv)
        xs_mid.append(x)
        if i == 0:
            (wup_p[i],) = _gather_end(st_ffn[i], x, f"ffn_{i}")

            def wdown_now(after):
                (wd_g,) = _gather_end(st_down0, after, "ffn_0_down")
                wdown[0] = wd_g.reshape(D_FF, D_MODEL)
                return wdown[0]
        else:
            wup_p[i], wd_g = _gather_end(st_ffn[i], x, f"ffn_{i}")
            wdown[i] = wd_g.reshape(D_FF, D_MODEL)
        if i + 1 < DEPTH:
            st_ffn[i + 1], tok_ffn[i + 1] = _gather_begin(ffn_shards(i + 1), f"ffn_{i + 1}")
        g_row, wu, wd, cw, cb = ffn_args(i)
        x, sv = _ffn_fwd(x, g_row, wu, wdown_now if i == 0 else wd, cw, cb, f"l{i}",
                         dep=tok_ffn[i + 1] if i + 1 < DEPTH else None)
        saved_ffn.append(sv)
    dx, loss_part = _loss_head(x, p["loss_target"].reshape(s, D_MODEL), name="loss_head")

    grads = {}
    dg_mix, dg_ffn = [None] * DEPTH, [None] * DEPTH
    dconv_w, dconv_b = [None] * DEPTH, [None] * DEPTH
    dpool = {"w": {}, "b": {}, "scale": {}}
    out = {}

    def big_update(n, h, others, idx=None):
        w, m, v = (p[pre + n] if idx is None else p[pre + n][idx] for pre in ("", "m_", "v_"))
        cols = h.shape[-1]
        r = _adamw_chips(w.reshape(-1, cols), h, others, m.reshape(-1, cols), v.reshape(-1, cols), chip,
                         name=f"adamw_{n}" + ("" if idx is None else f"_{idx}"))
        return [a.reshape(w.shape) for a in r]

    kinds = ("grad", "delta", "new_m", "new_v")
    ffn_upd = {"ffn_w_up": [None] * DEPTH, "ffn_w_down": [None] * DEPTH}

    def finish(entry, after):
        names, idx, started, tag = entry
        hs, others = _scatter_end(started, after, tag)
        for n, h, o in zip(names, hs, others):
            if idx is None:
                for kind, a in zip(kinds, big_update(n, h, o)):
                    out[kind + "_" + n] = a
            else:
                ffn_upd[n][idx] = (h, o)

    pending = []
    for i in reversed(range(DEPTH)):
        dx, dg_ffn[i], dconv_w[i], dconv_b[i], started = _ffn_bwd(
            dx, xs_mid[i], *ffn_args(i), saved_ffn[i], f"l{i}")
        for entry in pending:
            finish(entry, dx)
        pending = [(("ffn_w_up", "ffn_w_down"), i, started, f"ffn_l{i}")]
        margs = mixer_args(i)
        if i % 3 == 0:
            j = i // 3
            dx, dg_mix[i], dpool["w"][j], dpool["b"][j], dpool["scale"][j] = _pool_layer_bwd(
                dx, xs_in[i], *margs, saved_mix[i], f"l{i}")
        elif i % 3 == 1:
            dx, dg_mix[i], dqg, dkg, started = _sb_layer_bwd(dx, xs_in[i], *margs, saved_mix[i], f"l{i}")
            grads["sb_q_gain"], grads["sb_k_gain"] = dqg, dkg
            pending.append((("sb_w_qkv", "sb_w_o"), None, started, f"sb_l{i}"))
        else:
            g_row, raw, _, _, d_row, wg, _ = margs
            dx, dg_mix[i], sg, started = _ssm_layer_bwd(dx, xs_in[i], g_row, raw, d_row, wg, saved_mix[i], f"l{i}")
            grads.update(sg)
            pending.append((("ssm_w_glu",), None, started, f"ssm_l{i}"))
    grad_x = dx.reshape(1, s, D_MODEL)
    grads["norm_mix_g"] = jnp.concatenate(dg_mix, axis=0)
    grads["norm_ffn_g"] = jnp.concatenate(dg_ffn, axis=0)
    grads["ffn_conv_w"] = jnp.stack(dconv_w)
    grads["ffn_conv_b"] = jnp.concatenate(dconv_b, axis=0)
    grads["pool_b"] = jnp.concatenate([dpool["b"][j] for j in range(n_pool)], axis=0)
    grads["pool_scale"] = jnp.concatenate([dpool["scale"][j] for j in range(n_pool)], axis=0)
    dpw = jnp.stack([dpool["w"][j] for j in range(n_pool)])
    dpw = dpw.reshape(n_pool, POOL_GROUPS, N_DEV, POOL_DIM // N_DEV, POOL_DIM)
    pair, token = _scatter_pair([jnp.transpose(dpw, (2, 0, 1, 3, 4)).reshape(N_DEV, -1, POOL_DIM)], "pool")
    started, _ = _scatter_chips(pair, token, "pool")
    (ffn_first,) = pending

    small_names = _REPLICATED + list(_SMALL_SHARDED)
    full_shapes = [p[n].shape for n in _REPLICATED] + [small_full[n].shape for n in _SMALL_SHARDED]
    part = _pack([grads[n].reshape(shp) for n, shp in zip(small_names, full_shapes)] + [loss_part[0]])
    st_small, tok_small = _gather_begin([part], "small_grads")
    finish((("pool_w",), None, started, "pool"), tok_small)
    upper = {}
    for n in ffn_upd:
        upper[n] = _adamw_chips_layers(p[n], [ffn_upd[n][i][0] for i in range(1, DEPTH)],
                                       [ffn_upd[n][i][1] for i in range(1, DEPTH)], p["m_" + n], p["v_" + n], chip,
                                       first=1, name=f"adamw_{n}_upper")
    (parts,) = _gather_end(st_small, upper["ffn_w_up"][0], "small_grads")
    summed = _unpack(_sum_parts(parts, name="sum_small_grads"), full_shapes + [(LANE,)])
    loss = summed[-1][0]
    small_g = {}
    for n, g in zip(small_names, summed[:-1]):
        if n in _SMALL_SHARDED:
            ax = _SMALL_SHARDED[n]
            g = lax.dynamic_slice_in_dim(g, me * p[n].shape[ax], p[n].shape[ax], axis=ax)
        small_g[n] = g

    local_shapes = [p[n].shape for n in small_names]
    packs = [_pack([p[pre + n] for n in small_names]) for pre in ("", "m_", "v_")]
    res = _adamw_flat(packs[0], _pack([small_g[n] for n in small_names]), packs[1], packs[2], name="adamw_small")
    for kind, packed in zip(("delta", "new_m", "new_v"), res):
        for n, a in zip(small_names, _unpack(packed, local_shapes)):
            out[kind + "_" + n] = a
    for n in small_names:
        out["grad_" + n] = small_g[n]

    finish(ffn_first, res[0])
    for n in ffn_upd:
        h, o = ffn_upd[n][0]
        upd = _adamw_chips_layers(p[n], [h], [o], p["m_" + n], p["v_" + n], chip, first=0, prev=upper[n],
                                  name=f"adamw_{n}_first")
        for kind, a in zip(kinds, upd):
            out[kind + "_" + n] = a

    return (loss, grad_x, *[out["grad_" + n] for n in _WEIGHTS], *[out["delta_" + n] for n in _WEIGHTS],
            *[out["new_m_" + n] for n in _WEIGHTS], *[out["new_v_" + n] for n in _WEIGHTS])


def kernel(x, norm_mix_g, norm_ffn_g, pool_w, pool_b, pool_scale, sb_w_qkv, sb_q_gain, sb_k_gain, sb_w_o, ssm_lam_re, ssm_lam_im, ssm_log_step, ssm_b_re, ssm_b_im, ssm_c_re, ssm_c_im, ssm_d, ssm_w_glu, ssm_b_glu, ffn_w_up, ffn_conv_w, ffn_conv_b, ffn_w_down, loss_target, m_norm_mix_g, m_norm_ffn_g, m_pool_w, m_pool_b, m_pool_scale, m_sb_w_qkv, m_sb_q_gain, m_sb_k_gain, m_sb_w_o, m_ssm_lam_re, m_ssm_lam_im, m_ssm_log_step, m_ssm_b_re, m_ssm_b_im, m_ssm_c_re, m_ssm_c_im, m_ssm_d, m_ssm_w_glu, m_ssm_b_glu, m_ffn_w_up, m_ffn_conv_w, m_ffn_conv_b, m_ffn_w_down, v_norm_mix_g, v_norm_ffn_g, v_pool_w, v_pool_b, v_pool_scale, v_sb_w_qkv, v_sb_q_gain, v_sb_k_gain, v_sb_w_o, v_ssm_lam_re, v_ssm_lam_im, v_ssm_log_step, v_ssm_b_re, v_ssm_b_im, v_ssm_c_re, v_ssm_c_im, v_ssm_d, v_ssm_w_glu, v_ssm_b_glu, v_ffn_w_up, v_ffn_conv_w, v_ffn_conv_b, v_ffn_w_down):
    args = (x, norm_mix_g, norm_ffn_g, pool_w, pool_b, pool_scale, sb_w_qkv, sb_q_gain, sb_k_gain, sb_w_o, ssm_lam_re, ssm_lam_im, ssm_log_step, ssm_b_re, ssm_b_im, ssm_c_re, ssm_c_im, ssm_d, ssm_w_glu, ssm_b_glu, ffn_w_up, ffn_conv_w, ffn_conv_b, ffn_w_down, loss_target, m_norm_mix_g, m_norm_ffn_g, m_pool_w, m_pool_b, m_pool_scale, m_sb_w_qkv, m_sb_q_gain, m_sb_k_gain, m_sb_w_o, m_ssm_lam_re, m_ssm_lam_im, m_ssm_log_step, m_ssm_b_re, m_ssm_b_im, m_ssm_c_re, m_ssm_c_im, m_ssm_d, m_ssm_w_glu, m_ssm_b_glu, m_ffn_w_up, m_ffn_conv_w, m_ffn_conv_b, m_ffn_w_down, v_norm_mix_g, v_norm_ffn_g, v_pool_w, v_pool_b, v_pool_scale, v_sb_w_qkv, v_sb_q_gain, v_sb_k_gain, v_sb_w_o, v_ssm_lam_re, v_ssm_lam_im, v_ssm_log_step, v_ssm_b_re, v_ssm_b_im, v_ssm_c_re, v_ssm_c_im, v_ssm_d, v_ssm_w_glu, v_ssm_b_glu, v_ffn_w_up, v_ffn_conv_w, v_ffn_conv_b, v_ffn_w_down)
    return _step(dict(zip(_INPUTS, args)))
```

```python
import functools
import math

import jax
import jax.numpy as jnp
from jax import lax
from jax.experimental import pallas as pl
from jax.experimental.pallas import tpu as pltpu

F32 = jnp.float32
BF16 = jnp.bfloat16

N_DEV = 8
D_MODEL = 2048
D_FF = 5632
DEPTH = 4
POOL_GROUPS = 4
POOL_DIM = 512
HEADS = 16
HEAD_DIM = 128
SSM_GROUPS = 128
SSM_CH = 16
SSM_STATE = 64
SSM_BLOCK_GROUPS = 8
SSM_BLOCK_LANES = SSM_BLOCK_GROUPS * SSM_STATE
RMS_EPS = 1e-6
ADAM_LR = 0.001
ADAM_B1 = 0.9
ADAM_B2 = 0.999
ADAM_EPS = 1e-08
ADAM_WD = 0.01
ADAM_STEP = 10

VMEM_LIMIT_BYTES = 56 * 1024 * 1024
LANE = 128
SUBLANE = 8
MESH = pl.DeviceIdType.MESH


def _cparams(*sem):
    return pltpu.CompilerParams(dimension_semantics=tuple(sem), vmem_limit_bytes=VMEM_LIMIT_BYTES)


def _sds(shape, dtype):
    return jax.ShapeDtypeStruct(tuple(shape), dtype)


def _mm(a, b, *, dims, grid, a_spec, b_spec, o_spec, out_shape, out_dtype, name, k_axis=None, acc_shape=None,
        res=None, res_spec=None, a_alt=None, b_alt=None, alt_axis=None, alt_from=None, dep=None):
    nk = grid[k_axis] if k_axis is not None else 1
    n_in = 2 + sum(e is not None for e in (res, a_alt, b_alt, dep))

    def body(*refs):
        a_ref, b_ref = refs[:2]
        rest = list(refs[2:n_in])
        r_ref = rest.pop(0) if res is not None else None
        a2_ref = rest.pop(0) if a_alt is not None else None
        b2_ref = rest.pop(0) if b_alt is not None else None
        o_ref = refs[n_in]
        scr = refs[n_in + 1:]
        av, bv = a_ref[...], b_ref[...]
        if a2_ref is not None:
            av = jnp.where(pl.program_id(alt_axis) >= alt_from, a2_ref[...], av)
        if b2_ref is not None:
            bv = jnp.where(pl.program_id(alt_axis) >= alt_from, b2_ref[...], bv)
        p = lax.dot_general(av, bv, (dims, ((), ())), preferred_element_type=F32)
        if k_axis is None:
            if r_ref is not None:
                p = p + r_ref[...]
            o_ref[...] = p.astype(o_ref.dtype)
        else:
            acc = scr[0]
            k = pl.program_id(k_axis)

            @pl.when(k == 0)
            def _():
                acc[...] = p

            @pl.when(k > 0)
            def _():
                acc[...] += p

            @pl.when(k == nk - 1)
            def _():
                r = acc[...]
                if r_ref is not None:
                    r = r + r_ref[...]
                o_ref[...] = r.astype(o_ref.dtype)

    sem = ["parallel"] * len(grid)
    if k_axis is not None:
        sem[k_axis] = "arbitrary"
    in_specs, args = [a_spec, b_spec], [a, b]
    if res is not None:
        in_specs.append(res_spec)
        args.append(res)
    for alt in (a_alt, b_alt):
        if alt is not None:
            args.append(alt[0])
            in_specs.append(alt[1])
    if dep is not None:
        args.append(dep)
        in_specs.append(pl.BlockSpec((SUBLANE, LANE), lambda *_: (0, 0)))
    scratch = [pltpu.VMEM(acc_shape, F32)] if k_axis is not None else []
    return pl.pallas_call(
        body, name=name, grid=grid, in_specs=in_specs, out_specs=o_spec, out_shape=_sds(out_shape, out_dtype),
        scratch_shapes=scratch, compiler_params=_cparams(*sem),
    )(*args)


NN = ((1,), (0,))
NT = ((1,), (1,))
TN = ((0,), (0,))


def _mm_nn_pieces(a, wp, *, tm, name, out_dtype=F32):
    s, k = a.shape
    tm = min(tm, s)
    p, _, c = wp.shape
    return _mm(a, wp, dims=NN, grid=(s // tm, p),
               a_spec=pl.BlockSpec((tm, k), lambda m, n: (m, 0)),
               b_spec=pl.BlockSpec((None, k, c), lambda m, n: (n, 0, 0)),
               o_spec=pl.BlockSpec((tm, c), lambda m, n: (m, n)),
               out_shape=(s, p * c), out_dtype=out_dtype, name=name)


def _mm_nt_pieces(a, wp, *, tm, tn, name, out_dtype=F32, dep=None):
    p, n, c = wp.shape
    halves = a if isinstance(a, tuple) else None
    a0 = halves[0] if halves else a
    s = a0.shape[0]
    tm = min(tm, s)
    h = p // 2
    alt = {}
    if halves:
        a_spec = pl.BlockSpec((tm, c), lambda m, j, k: (m, jnp.minimum(k, h - 1)))
        alt = dict(a_alt=(halves[1], pl.BlockSpec((tm, c), lambda m, j, k: (m, jnp.maximum(k - h, 0)))),
                   alt_axis=2, alt_from=h)
    else:
        a_spec = pl.BlockSpec((tm, c), lambda m, j, k: (m, k))
    return _mm(a0, wp, dims=NT, grid=(s // tm, n // tn, p), k_axis=2, acc_shape=(tm, tn), a_spec=a_spec,
               b_spec=pl.BlockSpec((None, tn, c), lambda m, j, k: (k, j, 0)),
               o_spec=pl.BlockSpec((tm, tn), lambda m, j, k: (m, j)),
               out_shape=(s, n), out_dtype=out_dtype, name=name, dep=dep, **alt)


def _mm_tn_pieces(a, g, *, pieces, tm, name, out_dtype=BF16):
    s, m = a.shape
    halves = g if isinstance(g, tuple) else None
    g0 = halves[0] if halves else g
    h = pieces // 2
    c = g0.shape[1] // (h if halves else pieces)
    alt = {}
    if halves:
        b_spec = pl.BlockSpec((s, c), lambda n, i: (0, jnp.minimum(n, h - 1)))
        alt = dict(b_alt=(halves[1], pl.BlockSpec((s, c), lambda n, i: (0, jnp.maximum(n - h, 0)))),
                   alt_axis=0, alt_from=h)
    else:
        b_spec = pl.BlockSpec((s, c), lambda n, i: (0, n))
    return _mm(a, g0, dims=TN, grid=(pieces, m // tm), a_spec=pl.BlockSpec((s, tm), lambda n, i: (0, i)),
               b_spec=b_spec, o_spec=pl.BlockSpec((None, tm, c), lambda n, i: (n, i, 0)),
               out_shape=(pieces, m, c), out_dtype=out_dtype, name=name, **alt)


def _mm_nn(a, w, *, tm, tn, name, out_dtype=F32, res=None):
    s, k = a.shape
    tm = min(tm, s)
    n = w.shape[1]
    return _mm(a, w, dims=NN, grid=(s // tm, n // tn),
               a_spec=pl.BlockSpec((tm, k), lambda m, j: (m, 0)),
               b_spec=pl.BlockSpec((k, tn), lambda m, j: (0, j)),
               o_spec=pl.BlockSpec((tm, tn), lambda m, j: (m, j)),
               res=res, res_spec=pl.BlockSpec((tm, tn), lambda m, j: (m, j)),
               out_shape=(s, n), out_dtype=out_dtype, name=name)


def _mm_nt(a, w, *, tm, tn, name, out_dtype=F32):
    s, k = a.shape
    tm = min(tm, s)
    n = w.shape[0]
    return _mm(a, w, dims=NT, grid=(s // tm, n // tn),
               a_spec=pl.BlockSpec((tm, k), lambda m, j: (m, 0)),
               b_spec=pl.BlockSpec((tn, k), lambda m, j: (j, 0)),
               o_spec=pl.BlockSpec((tm, tn), lambda m, j: (m, j)),
               out_shape=(s, n), out_dtype=out_dtype, name=name)


def _mm_tn(a, g, *, tm, tn, name, out_dtype=BF16):
    s, m = a.shape
    n = g.shape[1]
    return _mm(a, g, dims=TN, grid=(m // tm, n // tn),
               a_spec=pl.BlockSpec((s, tm), lambda i, j: (0, i)),
               b_spec=pl.BlockSpec((s, tn), lambda i, j: (0, j)),
               o_spec=pl.BlockSpec((tm, tn), lambda i, j: (i, j)),
               out_shape=(m, n), out_dtype=out_dtype, name=name)


ROW_TILE = 256


def _dep_spec():
    return pl.BlockSpec((SUBLANE, LANE), lambda i: (0, 0))


def _rms_fwd(x, g_row, *, want_f32, name, dep=None):
    s, d = x.shape
    n_in = 2 if dep is None else 3

    def body(*refs):
        x_ref, g_ref = refs[:2]
        outs = refs[n_in:]
        xv = x_ref[...]
        r = lax.rsqrt(jnp.mean(xv * xv, axis=-1, keepdims=True) + RMS_EPS)
        h = (xv * r) * g_ref[...]
        outs[0][...] = h.astype(BF16)
        if want_f32:
            outs[1][...] = h

    row = pl.BlockSpec((ROW_TILE, d), lambda i: (i, 0))
    out_shape = [_sds((s, d), BF16)] + ([_sds((s, d), F32)] if want_f32 else [])
    out = pl.pallas_call(
        body, name=name, grid=(s // ROW_TILE,),
        in_specs=[row, pl.BlockSpec((1, d), lambda i: (0, 0))] + ([] if dep is None else [_dep_spec()]),
        out_specs=[row] * len(out_shape), out_shape=out_shape, compiler_params=_cparams("parallel"),
    )(x, g_row, *(() if dep is None else (dep,)))
    return out if want_f32 else (out[0], None)


def _rms_bwd(x, g_row, dh, dres, *, name, dep=None):
    s, d = x.shape

    def body(x_ref, g_ref, dh_ref, dres_ref, *rest):
        dx_ref, dg_ref = rest[-2:]
        xv = x_ref[...]
        r = lax.rsqrt(jnp.mean(xv * xv, axis=-1, keepdims=True) + RMS_EPS)
        xn = xv * r
        dhv = dh_ref[...]
        dxn = dhv * g_ref[...]
        dx_ref[...] = dres_ref[...] + r * (dxn - xn * jnp.mean(dxn * xn, axis=-1, keepdims=True))
        part = jnp.sum(dhv * xn, axis=0, keepdims=True)

        @pl.when(pl.program_id(0) == 0)
        def _():
            dg_ref[...] = part

        @pl.when(pl.program_id(0) > 0)
        def _():
            dg_ref[...] += part

    row = pl.BlockSpec((ROW_TILE, d), lambda i: (i, 0))
    vec = pl.BlockSpec((1, d), lambda i: (0, 0))
    return pl.pallas_call(
        body, name=name, grid=(s // ROW_TILE,), in_specs=[row, vec, row, row] + ([] if dep is None else [_dep_spec()]),
        out_specs=[row, vec], out_shape=[_sds((s, d), F32), _sds((1, d), F32)], compiler_params=_cparams("arbitrary"),
    )(x, g_row, dh, dres, *(() if dep is None else (dep,)))


def _shift_down(v, k):
    row = lax.broadcasted_iota(jnp.int32, v.shape, 0)
    return jnp.where(row >= k, pltpu.roll(v, k, 0), 0.0)


def _shift_up(v, k):
    n = v.shape[0]
    row = lax.broadcasted_iota(jnp.int32, v.shape, 0)
    return jnp.where(row < n - k, pltpu.roll(v, n - k, 0), 0.0)


def _sigmoid(z):
    return 1.0 / (1.0 + jnp.exp(-z))


FF_COL_TILE = 256


def _conv3(u, w, b):
    return b + w[0:1, :] * _shift_down(u, 2) + w[1:2, :] * _shift_down(u, 1) + w[2:3, :] * u


def _conv_gate_fwd(up, conv_w, conv_b, *, name):
    s = up.shape[0]
    f = up.shape[1] // 2
    nt = f // FF_COL_TILE

    def body(uv_ref, ug_ref, wv_ref, wg_ref, bv_ref, bg_ref, a_ref):
        vc = _conv3(uv_ref[...], wv_ref[...], bv_ref[...])
        gc = _conv3(ug_ref[...], wg_ref[...], bg_ref[...])
        a_ref[...] = ((gc * _sigmoid(gc)) * vc).astype(BF16)

    def col(rows, off):
        return pl.BlockSpec((rows, FF_COL_TILE), lambda n: (0, n + off))

    return pl.pallas_call(
        body, name=name, grid=(nt,),
        in_specs=[col(s, 0), col(s, nt), col(3, 0), col(3, nt), col(1, 0), col(1, nt)],
        out_specs=col(s, 0), out_shape=_sds((s, f), BF16), compiler_params=_cparams("parallel"),
    )(up, up, conv_w, conv_w, conv_b, conv_b)


def _conv_gate_bwd(up, da, conv_w, conv_b, *, name):
    s = up.shape[0]
    f = up.shape[1] // 2
    nt = f // FF_COL_TILE

    def conv_bwd(u, w, dc):
        d0 = _shift_up(dc, 2)
        d1 = _shift_up(dc, 1)
        dup = w[0:1, :] * d0 + w[1:2, :] * d1 + w[2:3, :] * dc
        dw = jnp.concatenate([jnp.sum(u * d0, axis=0, keepdims=True), jnp.sum(u * d1, axis=0, keepdims=True),
                              jnp.sum(u * dc, axis=0, keepdims=True)], axis=0)
        return dup, dw, jnp.sum(dc, axis=0, keepdims=True)

    def body(uv_ref, ug_ref, da_ref, wv_ref, wg_ref, bv_ref, bg_ref,
             duv_ref, dug_ref, dwv_ref, dwg_ref, dbv_ref, dbg_ref):
        uv = uv_ref[...]
        ug = ug_ref[...]
        vc = _conv3(uv, wv_ref[...], bv_ref[...])
        gc = _conv3(ug, wg_ref[...], bg_ref[...])
        sg = _sigmoid(gc)
        dav = da_ref[...]
        dvc = dav * (gc * sg)
        dgc = dav * vc * (sg * (1.0 + gc * (1.0 - sg)))
        dup, dw, db = conv_bwd(uv, wv_ref[...], dvc)
        duv_ref[...] = dup.astype(BF16)
        dwv_ref[...] = dw
        dbv_ref[...] = db
        dup, dw, db = conv_bwd(ug, wg_ref[...], dgc)
        dug_ref[...] = dup.astype(BF16)
        dwg_ref[...] = dw
        dbg_ref[...] = db

    def col(rows, off):
        return pl.BlockSpec((rows, FF_COL_TILE), lambda n: (0, n + off))

    dup_v, dup_g, dw_v, dw_g, db_v, db_g = pl.pallas_call(
        body, name=name, grid=(nt,),
        in_specs=[col(s, 0), col(s, nt), col(s, 0), col(3, 0), col(3, nt), col(1, 0), col(1, nt)],
        out_specs=[col(s, 0), col(s, 0), col(3, 0), col(3, 0), col(1, 0), col(1, 0)],
        out_shape=[_sds((s, f), BF16), _sds((s, f), BF16), _sds((3, f), F32), _sds((3, f), F32),
                   _sds((1, f), F32), _sds((1, f), F32)],
        compiler_params=_cparams("parallel"),
    )(up, up, da, conv_w, conv_w, conv_b, conv_b)
    return dup_v, dup_g, jnp.concatenate([dw_v, dw_g], axis=1), jnp.concatenate([db_v, db_g], axis=1)


def _pool_counts(shape, g):
    win = jnp.left_shift(jnp.int32(2), g)
    t = lax.broadcasted_iota(jnp.int32, shape, 0)
    return win, jnp.minimum(t + 1, win).astype(F32)


def _window_sum(v, g, shift):
    for k in range(POOL_GROUPS):
        v = jnp.where(g >= k, v + shift(v, 1 << k), v)
    return v


def _pool_fwd(hf, x, w, b, scale, *, name):
    s, d = hf.shape

    def body(h_ref, x_ref, w_ref, b_ref, sc_ref, o_ref):
        g = pl.program_id(0)
        h = h_ref[...]
        _, cnt = _pool_counts(h.shape, g)
        pooled = _window_sum(h, g, _shift_down) / cnt - h
        y = jnp.dot(pooled.astype(BF16), w_ref[...], preferred_element_type=F32) + b_ref[...]
        o_ref[...] = x_ref[...] + y * sc_ref[...]

    col = pl.BlockSpec((s, POOL_DIM), lambda g: (0, g))
    vec = pl.BlockSpec((1, POOL_DIM), lambda g: (0, g))
    return pl.pallas_call(
        body, name=name, grid=(POOL_GROUPS,),
        in_specs=[col, col, pl.BlockSpec((None, POOL_DIM, POOL_DIM), lambda g: (g, 0, 0)), vec, vec],
        out_specs=col, out_shape=_sds((s, d), F32), compiler_params=_cparams("parallel"),
    )(hf, x, w, b, scale)


def _pool_bwd(hf, dm, w, b, scale, *, name):
    s, d = hf.shape

    def body(h_ref, dm_ref, w_ref, b_ref, sc_ref, dh_ref, dw_ref, db_ref, dsc_ref):
        g = pl.program_id(0)
        h = h_ref[...]
        _, cnt = _pool_counts(h.shape, g)
        pooled = (_window_sum(h, g, _shift_down) / cnt - h).astype(BF16)
        wv = w_ref[...]
        y = jnp.dot(pooled, wv, preferred_element_type=F32) + b_ref[...]
        dmv = dm_ref[...]
        dsc_ref[...] = jnp.sum(dmv * y, axis=0, keepdims=True)
        dy = dmv * sc_ref[...]
        db_ref[...] = jnp.sum(dy, axis=0, keepdims=True)
        dyb = dy.astype(BF16)
        dw_ref[...] = lax.dot_general(pooled, dyb, (TN, ((), ())), preferred_element_type=F32).astype(BF16)
        dp = lax.dot_general(dyb, wv, (NT, ((), ())), preferred_element_type=F32)
        dh_ref[...] = _window_sum(dp / cnt, g, _shift_up) - dp

    col = pl.BlockSpec((s, POOL_DIM), lambda g: (0, g))
    vec = pl.BlockSpec((1, POOL_DIM), lambda g: (0, g))
    mat = pl.BlockSpec((None, POOL_DIM, POOL_DIM), lambda g: (g, 0, 0))
    return pl.pallas_call(
        body, name=name, grid=(POOL_GROUPS,), in_specs=[col, col, mat, vec, vec], out_specs=[col, mat, vec, vec],
        out_shape=[_sds((s, d), F32), _sds((POOL_GROUPS, POOL_DIM, POOL_DIM), BF16), _sds((1, d), F32),
                   _sds((1, d), F32)],
        compiler_params=_cparams("parallel"),
    )(hf, dm, w, b, scale)


ATT_TQ = 256
ATT_TK = 256


def _qk_norm_fwd(qkv, gains, *, name):
    s = qkv.shape[0]

    def body(x_ref, g_ref, o_ref):
        xv = x_ref[...]
        r = lax.rsqrt(jnp.mean(xv * xv, axis=-1, keepdims=True) + RMS_EPS)
        o_ref[...] = ((xv * r) * g_ref[...]).astype(BF16)

    blk = pl.BlockSpec((s, HEAD_DIM), lambda hd: (0, hd))
    return pl.pallas_call(
        body, name=name, grid=(2 * HEADS,),
        in_specs=[blk, pl.BlockSpec((None, 1, HEAD_DIM), lambda hd: (hd // HEADS, 0, 0))],
        out_specs=blk, out_shape=_sds((s, 2 * HEADS * HEAD_DIM), BF16), compiler_params=_cparams("parallel"),
    )(qkv, gains)


def _qk_norm_bwd(qkv, gains, dn, *, which, name):
    s = qkv.shape[0]

    def body(x_ref, g_ref, dn_ref, dx_ref, dg_ref):
        xv = x_ref[...]
        r = lax.rsqrt(jnp.mean(xv * xv, axis=-1, keepdims=True) + RMS_EPS)
        xn = xv * r
        dnv = dn_ref[...]
        dxn = dnv * g_ref[...]
        dx_ref[...] = (r * (dxn - xn * jnp.mean(dxn * xn, axis=-1, keepdims=True))).astype(BF16)
        part = jnp.sum(dnv * xn, axis=0, keepdims=True)

        @pl.when(pl.program_id(0) == 0)
        def _():
            dg_ref[...] = part

        @pl.when(pl.program_id(0) > 0)
        def _():
            dg_ref[...] += part

    blk = pl.BlockSpec((s, HEAD_DIM), lambda hd: (0, hd))
    return pl.pallas_call(
        body, name=name, grid=(HEADS,),
        in_specs=[pl.BlockSpec((s, HEAD_DIM), lambda hd: (0, hd + which * HEADS)),
                  pl.BlockSpec((None, 1, HEAD_DIM), lambda hd: (which, 0, 0)), blk],
        out_specs=[blk, pl.BlockSpec((1, HEAD_DIM), lambda hd: (0, 0))],
        out_shape=[_sds((s, HEADS * HEAD_DIM), BF16), _sds((1, HEAD_DIM), F32)],
        compiler_params=_cparams("arbitrary"),
    )(qkv, gains, dn)


def _split_dot(v, tri):
    hi = v.astype(BF16)
    lo = (v - hi.astype(F32)).astype(BF16)
    return (jnp.dot(hi, tri, preferred_element_type=F32) + jnp.dot(lo, tri, preferred_element_type=F32))


def _causal_mask(qi, j):
    tpos = qi * ATT_TQ + lax.broadcasted_iota(jnp.int32, (ATT_TQ, ATT_TK), 0)
    spos = j * ATT_TK + lax.broadcasted_iota(jnp.int32, (ATT_TQ, ATT_TK), 1)
    return spos < tpos


def _att_tile(q, kj, qi, j):
    z = lax.dot_general(q, kj, (NT, ((), ())), preferred_element_type=F32) * (1.0 / math.sqrt(HEAD_DIM))
    mask = _causal_mask(qi, j)
    lb = jnp.minimum(z, 0.0) - jnp.log1p(jnp.exp(-jnp.abs(z)))
    l1m = jnp.where(mask, lb - z, 0.0)
    return lb, l1m, mask


def _tri(rel):
    r = lax.broadcasted_iota(jnp.int32, (ATT_TK, ATT_TK), 0)
    c = lax.broadcasted_iota(jnp.int32, (ATT_TK, ATT_TK), 1)
    return jnp.where(rel(r, c), 1.0, 0.0).astype(BF16)


def _sb_fwd(qkn, vb, *, name):
    s = vb.shape[0]

    def body(q_ref, k_ref, v_ref, o_ref):
        qi = pl.program_id(1)
        q = q_ref[...]
        after = _tri(lambda r, c: r > c)

        def step(t, carry):
            acc, run = carry
            j = qi - t
            rows = pl.ds(pl.multiple_of(j * ATT_TK, ATT_TK), ATT_TK)
            lb, l1m, mask = _att_tile(q, k_ref[rows, :], qi, j)
            remain = _split_dot(l1m, after) + run
            attn = jnp.where(mask, jnp.exp(lb + remain), 0.0)
            acc = acc + jnp.dot(attn.astype(BF16), v_ref[rows, :], preferred_element_type=F32)
            return acc, run + jnp.sum(l1m, axis=1, keepdims=True)

        acc, _ = lax.fori_loop(0, qi + 1, step, (jnp.zeros((ATT_TQ, HEAD_DIM), F32), jnp.zeros((ATT_TQ, 1), F32)))
        o_ref[...] = acc.astype(BF16)

    return pl.pallas_call(
        body, name=name, grid=(HEADS, s // ATT_TQ),
        in_specs=[pl.BlockSpec((ATT_TQ, HEAD_DIM), lambda hd, i: (i, hd)),
                  pl.BlockSpec((s, HEAD_DIM), lambda hd, i: (0, hd + HEADS)),
                  pl.BlockSpec((s, HEAD_DIM), lambda hd, i: (0, hd))],
        out_specs=pl.BlockSpec((ATT_TQ, HEAD_DIM), lambda hd, i: (i, hd)),
        out_shape=_sds((s, HEADS * HEAD_DIM), BF16), compiler_params=_cparams("parallel", "parallel"),
    )(qkn, qkn, vb)


def _sb_bwd(qkn, vb, dob, *, name):
    s = vb.shape[0]
    nkb = s // ATT_TK

    def body(q_ref, k_ref, v_ref, do_ref, dq_ref, dk_ref, dv_ref, a_buf, sig_buf):
        qi = pl.program_id(1)
        q = q_ref[...]
        do = do_ref[...]
        after = _tri(lambda r, c: r > c)
        before = _tri(lambda r, c: r < c)

        @pl.when(qi == 0)
        def _():
            dk_ref[...] = jnp.zeros_like(dk_ref)
            dv_ref[...] = jnp.zeros_like(dv_ref)

        def down(t, run):
            j = qi - t
            rows = pl.ds(pl.multiple_of(j * ATT_TK, ATT_TK), ATT_TK)
            lb, l1m, mask = _att_tile(q, k_ref[rows, :], qi, j)
            remain = _split_dot(l1m, after) + run
            a_buf[j] = jnp.where(mask, jnp.exp(lb + remain), 0.0)
            sig_buf[j] = jnp.exp(lb)
            return run + jnp.sum(l1m, axis=1, keepdims=True)

        lax.fori_loop(0, qi + 1, down, jnp.zeros((ATT_TQ, 1), F32))

        def up(j, carry):
            dq, run = carry
            rows = pl.ds(pl.multiple_of(j * ATT_TK, ATT_TK), ATT_TK)
            a = a_buf[j]
            sig = sig_buf[j]
            mask = _causal_mask(qi, j)
            da = lax.dot_general(do, v_ref[rows, :], (NT, ((), ())), preferred_element_type=F32)
            p = a * da
            c = _split_dot(p, before) + run
            dz = jnp.where(mask, p * (1.0 - sig) - c * sig, 0.0) * (1.0 / math.sqrt(HEAD_DIM))
            dzb = dz.astype(BF16)
            dq = dq + jnp.dot(dzb, k_ref[rows, :], preferred_element_type=F32)
            dk_ref[rows, :] += lax.dot_general(dzb, q, (TN, ((), ())), preferred_element_type=F32)
            dv_ref[rows, :] += lax.dot_general(a.astype(BF16), do, (TN, ((), ())), preferred_element_type=F32)
            return dq, run + jnp.sum(p, axis=1, keepdims=True)

        dq, _ = lax.fori_loop(0, qi + 1, up, (jnp.zeros((ATT_TQ, HEAD_DIM), F32), jnp.zeros((ATT_TQ, 1), F32)))
        dq_ref[...] = dq

    qblk = pl.BlockSpec((ATT_TQ, HEAD_DIM), lambda hd, i: (i, hd))
    full = pl.BlockSpec((s, HEAD_DIM), lambda hd, i: (0, hd))
    return pl.pallas_call(
        body, name=name, grid=(HEADS, s // ATT_TQ),
        in_specs=[qblk, pl.BlockSpec((s, HEAD_DIM), lambda hd, i: (0, hd + HEADS)), full, qblk],
        out_specs=[qblk, full, full],
        out_shape=[_sds((s, HEADS * HEAD_DIM), F32)] * 3,
        scratch_shapes=[pltpu.VMEM((nkb, ATT_TQ, ATT_TK), F32), pltpu.VMEM((nkb, ATT_TQ, ATT_TK), F32)],
        compiler_params=_cparams("parallel", "arbitrary"),
    )(qkn, qkn, vb, dob)


def _ssm_discretize(lam_re, lam_im, log_step, bt_re, bt_im):
    step = jnp.exp(log_step)
    mag = jnp.exp(lam_re * step)
    lb_re = mag * jnp.cos(lam_im * step)
    lb_im = mag * jnp.sin(lam_im * step)
    den = lam_re * lam_re + lam_im * lam_im
    f_re = ((lb_re - 1.0) * lam_re + lb_im * lam_im) / den
    f_im = (lb_im * lam_re - (lb_re - 1.0) * lam_im) / den
    return lb_re, lb_im, f_re * bt_re - f_im * bt_im, f_re * bt_im + f_im * bt_re


_SSM_LAM = (SSM_GROUPS, 1, SSM_STATE)
_SSM_STEP = (SSM_GROUPS, 1, 1)
_SSM_BT = (SSM_GROUPS, SSM_CH, SSM_STATE)


def _ssm_prep_fwd(lam_re, lam_im, log_step, bt_re, bt_im, *, name):
    def body(lr, li, ls, br, bi, o_ar, o_ai, o_br, o_bi):
        o_ar[...], o_ai[...], o_br[...], o_bi[...] = _ssm_discretize(lr[...], li[...], ls[...], br[...], bi[...])

    return pl.pallas_call(
        body, name=name, out_shape=[_sds(_SSM_LAM, F32), _sds(_SSM_LAM, F32), _sds(_SSM_BT, F32), _sds(_SSM_BT, F32)],
    )(lam_re, lam_im, log_step, bt_re, bt_im)


def _ssm_prep_bwd(lam_re, lam_im, log_step, bt_re, bt_im, d_ar, d_ai, d_br, d_bi, *, name):
    def body(lr, li, ls, br, bi, g_ar, g_ai, g_br, g_bi, o_lr, o_li, o_ls, o_br, o_bi):
        _, vjp = jax.vjp(_ssm_discretize, lr[...], li[...], ls[...], br[...], bi[...])
        o_lr[...], o_li[...], o_ls[...], o_br[...], o_bi[...] = vjp((g_ar[...], g_ai[...], g_br[...], g_bi[...]))

    return pl.pallas_call(
        body, name=name,
        out_shape=[_sds(_SSM_LAM, F32), _sds(_SSM_LAM, F32), _sds(_SSM_STEP, F32), _sds(_SSM_BT, F32), _sds(_SSM_BT, F32)],
    )(lam_re, lam_im, log_step, bt_re, bt_im, d_ar, d_ai, d_br, d_bi)


def _bd_masks():
    rowg = lax.broadcasted_iota(jnp.int32, (LANE, LANE), 0) // SSM_CH
    low = lax.broadcasted_iota(jnp.int32, (LANE, LANE), 1) < SSM_STATE
    return rowg, low


def _bd_expand(w):
    rowg, low = _bd_masks()
    high = jnp.logical_not(low)
    wr = pltpu.roll(w, SSM_STATE, 1)
    re = [jnp.where((rowg == 2 * k) & low, w, 0.0) + jnp.where((rowg == 2 * k + 1) & high, wr, 0.0) for k in range(4)]
    im = [jnp.where((rowg == 2 * k) & low, wr, 0.0) + jnp.where((rowg == 2 * k + 1) & high, w, 0.0) for k in range(4)]
    return jnp.concatenate(re + im, axis=1)


def _bd_extract(dbd):
    rowg, low = _bd_masks()
    high = jnp.logical_not(low)
    acc = jnp.zeros((LANE, LANE), F32)
    for k in range(4):
        c = dbd[:, LANE * k:LANE * (k + 1)]
        acc = acc + jnp.where((rowg == 2 * k) & low, c, 0.0) + jnp.where((rowg == 2 * k + 1) & low, pltpu.roll(c, SSM_STATE, 1), 0.0)
        c = dbd[:, LANE * (4 + k):LANE * (5 + k)]
        acc = acc + jnp.where((rowg == 2 * k) & high, pltpu.roll(c, SSM_STATE, 1), 0.0) + jnp.where((rowg == 2 * k + 1) & high, c, 0.0)
    return acc


def _cmul(ar, ai, br, bi):
    return ar * br - ai * bi, ar * bi + ai * br


def _scan_rows(xr, xi, ar, ai, *, reverse):
    n = xr.shape[0] // SUBLANE
    lanes = xr.shape[1]
    row = lax.broadcasted_iota(jnp.int32, (SUBLANE, lanes), 0)
    powers = [(ar, ai)]
    for _ in range(SUBLANE - 1):
        powers.append(_cmul(*powers[-1], ar, ai))
    pr = jnp.zeros((SUBLANE, lanes), F32)
    pi = jnp.zeros((SUBLANE, lanes), F32)
    for r in range(SUBLANE):
        e = (SUBLANE - 1 - r) if reverse else r
        pr = jnp.where(row == r, powers[e][0], pr)
        pi = jnp.where(row == r, powers[e][1], pi)

    def shift(v, d):
        if reverse:
            return jnp.where(row < SUBLANE - d, pltpu.roll(v, SUBLANE - d, 0), 0.0)
        return jnp.where(row >= d, pltpu.roll(v, d, 0), 0.0)

    def body(i, carry):
        cr, ci = carry
        g = (n - 1 - i) if reverse else i
        rows = pl.ds(pl.multiple_of(g * SUBLANE, SUBLANE), SUBLANE)
        br = xr[rows, :]
        bi = xi[rows, :]
        for d in (1, 2, 4):
            qr, qi = powers[d - 1]
            sr = shift(br, d)
            si = shift(bi, d)
            br, bi = br + qr * sr - qi * si, bi + qr * si + qi * sr
        br, bi = br + pr * cr - pi * ci, bi + pr * ci + pi * cr
        xr[rows, :] = br
        xi[rows, :] = bi
        edge = 0 if reverse else SUBLANE - 1
        return br[edge:edge + 1, :], bi[edge:edge + 1, :]

    zero = jnp.zeros((1, lanes), F32)
    lax.fori_loop(0, n, body, (zero, zero), unroll=2)


_GELU_C = math.sqrt(2.0 / math.pi)
_GELU_A = 0.044715


def _gelu(v):
    return 0.5 * v * (1.0 + jnp.tanh(_GELU_C * (v + _GELU_A * v * v * v)))


def _gelu_grad(v):
    t = jnp.tanh(_GELU_C * (v + _GELU_A * v * v * v))
    return 0.5 * (1.0 + t) + 0.5 * v * (1.0 - t * t) * (_GELU_C * (1.0 + 3.0 * _GELU_A * v * v))


def _ssm_states(u_b16, eb, ar, ai, xr, xi):
    nl = SSM_BLOCK_LANES
    xr[...] = jnp.dot(u_b16, eb[:, :nl], preferred_element_type=F32)
    xi[...] = jnp.dot(u_b16, eb[:, nl:], preferred_element_type=F32)
    _scan_rows(xr, xi, ar, ai, reverse=False)


def _ssm_specs(s):
    col = pl.BlockSpec((s, LANE), lambda b: (0, b))
    wsm = pl.BlockSpec((LANE, LANE), lambda b: (b, 0))
    lam = pl.BlockSpec((1, SSM_BLOCK_LANES), lambda b: (0, b))
    vec = pl.BlockSpec((1, LANE), lambda b: (0, b))
    return col, wsm, lam, vec


def _ssm_core_fwd(u, wb, wc, a_re, a_im, d_row, *, name):
    s, d = u.shape
    nl = SSM_BLOCK_LANES

    def body(u_ref, wb_ref, wc_ref, ar_ref, ai_ref, d_ref, y_ref, yg_ref, xr, xi):
        uv = u_ref[...]
        eb = _bd_expand(wb_ref[...]).astype(BF16)
        ec = _bd_expand(wc_ref[...]).astype(BF16)
        _ssm_states(uv.astype(BF16), eb, ar_ref[...], ai_ref[...], xr, xi)
        y = (lax.dot_general(xr[...].astype(BF16), ec[:, :nl], (NT, ((), ())), preferred_element_type=F32)
             + lax.dot_general(xi[...].astype(BF16), ec[:, nl:], (NT, ((), ())), preferred_element_type=F32)
             + d_ref[...] * uv)
        y_ref[...] = y
        yg_ref[...] = _gelu(y).astype(BF16)

    col, wsm, lam, vec = _ssm_specs(s)
    return pl.pallas_call(
        body, name=name, grid=(d // LANE,), in_specs=[col, wsm, wsm, lam, lam, vec], out_specs=[col, col],
        out_shape=[_sds((s, d), F32), _sds((s, d), BF16)],
        scratch_shapes=[pltpu.VMEM((s, nl), F32), pltpu.VMEM((s, nl), F32)],
        compiler_params=_cparams("parallel"),
    )(u, wb, wc, a_re, a_im, d_row)


def _ssm_core_bwd(u, ylin, dyg, wb, wc, a_re, a_im, d_row, *, name):
    s, d = u.shape
    nl = SSM_BLOCK_LANES
    n8 = s // SUBLANE

    def body(u_ref, y_ref, dyg_ref, wb_ref, wc_ref, ar_ref, ai_ref, d_ref,
             du_ref, dwb_ref, dwc_ref, dar_ref, dai_ref, dd_ref, xr, xi, gr, gi):
        uv = u_ref[...]
        ub = uv.astype(BF16)
        ar = ar_ref[...]
        ai = ai_ref[...]
        dy = dyg_ref[...] * _gelu_grad(y_ref[...])
        dd_ref[...] = jnp.sum(dy * uv, axis=0, keepdims=True)
        dyb = dy.astype(BF16)
        eb = _bd_expand(wb_ref[...]).astype(BF16)
        ec = _bd_expand(wc_ref[...]).astype(BF16)
        _ssm_states(ub, eb, ar, ai, xr, xi)
        dec = jnp.concatenate(
            [lax.dot_general(dyb, xr[...].astype(BF16), (TN, ((), ())), preferred_element_type=F32),
             lax.dot_general(dyb, xi[...].astype(BF16), (TN, ((), ())), preferred_element_type=F32)], axis=1)
        dwc_ref[...] = _bd_extract(dec)
        gr[...] = jnp.dot(dyb, ec[:, :nl], preferred_element_type=F32)
        gi[...] = jnp.dot(dyb, ec[:, nl:], preferred_element_type=F32)
        _scan_rows(gr, gi, ar, -ai, reverse=True)

        row = lax.broadcasted_iota(jnp.int32, (SUBLANE, nl), 0)

        def lam_grad(i, acc):
            acc_r, acc_i = acc
            rows = pl.ds(pl.multiple_of(i * SUBLANE, SUBLANE), SUBLANE)
            prev = pl.ds(pl.multiple_of(jnp.maximum(i - 1, 0) * SUBLANE, SUBLANE), SUBLANE)
            keep = jnp.where(i > 0, 1.0, 0.0)
            xpr = jnp.where(row == 0, pltpu.roll(xr[prev, :], 1, 0) * keep, pltpu.roll(xr[rows, :], 1, 0))
            xpi = jnp.where(row == 0, pltpu.roll(xi[prev, :], 1, 0) * keep, pltpu.roll(xi[rows, :], 1, 0))
            g_r = gr[rows, :]
            g_i = gi[rows, :]
            return acc_r + g_r * xpr + g_i * xpi, acc_i + g_i * xpr - g_r * xpi

        zero = jnp.zeros((SUBLANE, nl), F32)
        acc_r, acc_i = lax.fori_loop(0, n8, lam_grad, (zero, zero), unroll=2)
        dar_ref[...] = jnp.sum(acc_r, axis=0, keepdims=True)
        dai_ref[...] = jnp.sum(acc_i, axis=0, keepdims=True)

        grb = gr[...].astype(BF16)
        gib = gi[...].astype(BF16)
        deb = jnp.concatenate([lax.dot_general(ub, grb, (TN, ((), ())), preferred_element_type=F32),
                               lax.dot_general(ub, gib, (TN, ((), ())), preferred_element_type=F32)], axis=1)
        dwb_ref[...] = _bd_extract(deb)
        du_ref[...] = (lax.dot_general(grb, eb[:, :nl], (NT, ((), ())), preferred_element_type=F32)
                       + lax.dot_general(gib, eb[:, nl:], (NT, ((), ())), preferred_element_type=F32)
                       + d_ref[...] * dy)

    col, wsm, lam, vec = _ssm_specs(s)
    return pl.pallas_call(
        body, name=name, grid=(d // LANE,), in_specs=[col, col, col, wsm, wsm, lam, lam, vec],
        out_specs=[col, wsm, wsm, lam, lam, vec],
        out_shape=[_sds((s, d), F32), _sds((d, LANE), F32), _sds((d, LANE), F32),
                   _sds((1, SSM_GROUPS * SSM_STATE), F32), _sds((1, SSM_GROUPS * SSM_STATE), F32), _sds((1, d), F32)],
        scratch_shapes=[pltpu.VMEM((s, nl), F32)] * 4,
        compiler_params=_cparams("parallel"),
    )(u, ylin, dyg, wb, wc, a_re, a_im, d_row)


GLU_PIECE = 512


def _glu_fwd(yg, wp, b_row, x, *, tm, name):
    s, d = yg.shape
    tm = min(tm, s)
    half = N_DEV // 2

    def body(y_ref, wv_ref, wg_ref, bv_ref, bg_ref, x_ref, o_ref, val_ref, gate_ref):
        yv = y_ref[...]
        val = jnp.dot(yv, wv_ref[...], preferred_element_type=F32) + bv_ref[...]
        gate = jnp.dot(yv, wg_ref[...], preferred_element_type=F32) + bg_ref[...]
        val_ref[...] = val
        gate_ref[...] = gate
        o_ref[...] = x_ref[...] + val * _sigmoid(gate)

    blk = pl.BlockSpec((tm, GLU_PIECE), lambda m, n: (m, n))
    return pl.pallas_call(
        body, name=name, grid=(s // tm, half),
        in_specs=[pl.BlockSpec((tm, d), lambda m, n: (m, 0)),
                  pl.BlockSpec((None, d, GLU_PIECE), lambda m, n: (n, 0, 0)),
                  pl.BlockSpec((None, d, GLU_PIECE), lambda m, n: (n + half, 0, 0)),
                  pl.BlockSpec((1, GLU_PIECE), lambda m, n: (0, n)),
                  pl.BlockSpec((1, GLU_PIECE), lambda m, n: (0, n + half)), blk],
        out_specs=[blk, blk, blk], out_shape=[_sds((s, d), F32)] * 3,
        compiler_params=_cparams("parallel", "parallel"),
    )(yg, wp, wp, b_row, b_row, x)


def _glu_bwd(dout, val, gate, *, name):
    s, d = dout.shape

    def body(do_ref, val_ref, gate_ref, dgv_ref, db_ref):
        sg = _sigmoid(gate_ref[...])
        dov = do_ref[...]
        dgv = jnp.concatenate([dov * sg, dov * val_ref[...] * (sg * (1.0 - sg))], axis=1)
        dgv_ref[...] = dgv.astype(BF16)
        part = jnp.sum(dgv, axis=0, keepdims=True)

        @pl.when(pl.program_id(0) == 0)
        def _():
            db_ref[...] = part

        @pl.when(pl.program_id(0) > 0)
        def _():
            db_ref[...] += part

    row = pl.BlockSpec((ROW_TILE, d), lambda i: (i, 0))
    return pl.pallas_call(
        body, name=name, grid=(s // ROW_TILE,), in_specs=[row, row, row],
        out_specs=[pl.BlockSpec((ROW_TILE, 2 * d), lambda i: (i, 0)), pl.BlockSpec((1, 2 * d), lambda i: (0, 0))],
        out_shape=[_sds((s, 2 * d), BF16), _sds((1, 2 * d), F32)], compiler_params=_cparams("arbitrary"),
    )(dout, val, gate)


def _loss_head(y, target, *, name):
    s, d = y.shape

    def body(y_ref, t_ref, dy_ref, l_ref):
        e = y_ref[...] - t_ref[...]
        dy_ref[...] = e * (1.0 / d)
        part = jnp.zeros((SUBLANE, LANE), F32) + jnp.sum(e * e) * (0.5 / d)

        @pl.when(pl.program_id(0) == 0)
        def _():
            l_ref[...] = part

        @pl.when(pl.program_id(0) > 0)
        def _():
            l_ref[...] += part

    row = pl.BlockSpec((ROW_TILE, d), lambda i: (i, 0))
    return pl.pallas_call(
        body, name=name, grid=(s // ROW_TILE,), in_specs=[row, row],
        out_specs=[row, pl.BlockSpec((SUBLANE, LANE), lambda i: (0, 0))],
        out_shape=[_sds((s, d), F32), _sds((SUBLANE, LANE), F32)], compiler_params=_cparams("arbitrary"),
    )(y, target)


def _adamw_math(w, g, m, v):
    m = ADAM_B1 * m + (1.0 - ADAM_B1) * g
    v = ADAM_B2 * v + (1.0 - ADAM_B2) * (g * g)
    m_hat = m / (1.0 - ADAM_B1 ** ADAM_STEP)
    v_hat = v / (1.0 - ADAM_B2 ** ADAM_STEP)
    return -ADAM_LR * (m_hat / (jnp.sqrt(v_hat) + ADAM_EPS) + ADAM_WD * w), m, v


ADAM_ROWS = 64
PACK_ROWS = 64


def _sum_pieces(p_ref):
    g = p_ref[0].astype(F32)
    for k in range(1, N_DEV):
        g = g + p_ref[k].astype(F32)
    return g


def _adamw_pieces(w, pieces, m, v, *, name):
    r, c = w.shape

    def body(w_ref, p_ref, m_ref, v_ref, g_ref, d_ref, nm_ref, nv_ref):
        g = _sum_pieces(p_ref)
        g_ref[...] = g
        d_ref[...], nm_ref[...], nv_ref[...] = _adamw_math(w_ref[...], g, m_ref[...], v_ref[...])

    blk = pl.BlockSpec((ADAM_ROWS, c), lambda i: (i, 0))
    return pl.pallas_call(
        body, name=name, grid=(r // ADAM_ROWS,),
        in_specs=[blk, pl.BlockSpec((N_DEV, ADAM_ROWS, c), lambda i: (0, i, 0)), blk, blk],
        out_specs=[blk] * 4, out_shape=[_sds((r, c), F32)] * 4, compiler_params=_cparams("parallel"),
    )(w, pieces, m, v)


def _sum_parts(parts, *, name):
    _, r, c = parts.shape

    def body(p_ref, o_ref):
        o_ref[...] = _sum_pieces(p_ref)

    return pl.pallas_call(
        body, name=name, grid=(r // PACK_ROWS,),
        in_specs=[pl.BlockSpec((N_DEV, PACK_ROWS, c), lambda i: (0, i, 0))],
        out_specs=pl.BlockSpec((PACK_ROWS, c), lambda i: (i, 0)), out_shape=_sds((r, c), F32),
        compiler_params=_cparams("parallel"),
    )(parts)


def _adamw_flat(w, g, m, v, *, name):
    r, c = w.shape

    def body(w_ref, g_ref, m_ref, v_ref, d_ref, nm_ref, nv_ref):
        d_ref[...], nm_ref[...], nv_ref[...] = _adamw_math(w_ref[...], g_ref[...], m_ref[...], v_ref[...])

    blk = pl.BlockSpec((PACK_ROWS, c), lambda i: (i, 0))
    return pl.pallas_call(
        body, name=name, grid=(r // PACK_ROWS,), in_specs=[blk] * 4, out_specs=[blk] * 3,
        out_shape=[_sds((r, c), F32)] * 3, compiler_params=_cparams("parallel"),
    )(w, g, m, v)


_ANY = pl.BlockSpec(memory_space=pl.ANY)


def _place():
    return lax.axis_index("x"), lax.axis_index("y"), lax.axis_index("c")


def _slot(px, py, pc):
    return 4 * px + 2 * py + pc


def _all_gather(xs, *, name):
    n = len(xs)

    def body(*refs):
        ins, outs = refs[:n], refs[n:2 * n]
        send_sems, recv_sems, local_sems = refs[2 * n:]
        x, y, c = _place()
        me, sibling = (x, y, c), (x, y, 1 - c)
        chips = [(1 - x, y), (x, 1 - y), (1 - x, 1 - y)]

        def copy(i, k, block, to, src=None):
            rows = outs[i].at[_slot(*block)]
            return pltpu.make_async_remote_copy(
                src_ref=rows if src is None else src, dst_ref=rows, send_sem=send_sems.at[i, k],
                recv_sem=recv_sems.at[i, k], device_id=to, device_id_type=MESH)

        mine = [pltpu.make_async_copy(ins[i], outs[i].at[_slot(*me)], local_sems.at[i]) for i in range(n)]
        for cp in mine:
            cp.start()
        first = []
        for i in range(n):
            first.append(copy(i, 0, me, sibling, src=ins[i]))
            first += [copy(i, 1 + j, me, (*chip, c), src=ins[i]) for j, chip in enumerate(chips)]
        for cp in first:
            cp.start()
        passed = []
        for j, chip in enumerate(chips):
            for i in range(n):
                copy(i, 1 + j, (*chip, c), me).wait_recv()
                cp = copy(i, 4 + j, (*chip, c), sibling)
                cp.start()
                passed.append(cp)
        for i in range(n):
            copy(i, 0, sibling, me).wait_recv()
            for j, chip in enumerate(chips):
                copy(i, 4 + j, (*chip, 1 - c), me).wait_recv()
        for cp in first + passed:
            cp.wait_send()
        for cp in mine:
            cp.wait()

    return pl.pallas_call(
        body, name=name, in_specs=[_ANY] * n, out_specs=[_ANY] * n,
        out_shape=[_sds((N_DEV,) + a.shape, a.dtype) for a in xs],
        scratch_shapes=[pltpu.SemaphoreType.DMA((n, 7)), pltpu.SemaphoreType.DMA((n, 7)), pltpu.SemaphoreType.DMA((n,))],
    )(*xs)


def _exchange_pieces(gs, *, name):
    n = len(gs)
    flips = [(dx, dy, dc) for dx in (0, 1) for dy in (0, 1) for dc in (0, 1)][1:]

    def body(*refs):
        ins, outs = refs[:n], refs[n:2 * n]
        send_sems, recv_sems, local_sems = refs[2 * n:]
        x, y, c = _place()
        me = _slot(x, y, c)
        peers = [((1 - x) if dx else x, (1 - y) if dy else y, (1 - c) if dc else c) for dx, dy, dc in flips]

        def copy(i, k):
            return pltpu.make_async_remote_copy(
                src_ref=ins[i].at[_slot(*peers[k])], dst_ref=outs[i].at[me], send_sem=send_sems.at[i, k],
                recv_sem=recv_sems.at[i, k], device_id=peers[k], device_id_type=MESH)

        def landing(i, k):
            rows = outs[i].at[_slot(*peers[k])]
            return pltpu.make_async_remote_copy(
                src_ref=rows, dst_ref=rows, send_sem=send_sems.at[i, k], recv_sem=recv_sems.at[i, k],
                device_id=peers[k], device_id_type=MESH)

        mine = [pltpu.make_async_copy(ins[i].at[me], outs[i].at[me], local_sems.at[i]) for i in range(n)]
        for cp in mine:
            cp.start()
        sends = [copy(i, k) for i in range(n) for k in range(len(flips))]
        for cp in sends:
            cp.start()
        for i in range(n):
            for k in range(len(flips)):
                landing(i, k).wait_recv()
        for cp in sends:
            cp.wait_send()
        for cp in mine:
            cp.wait()

    return pl.pallas_call(
        body, name=name, in_specs=[_ANY] * n, out_specs=[_ANY] * n,
        out_shape=[_sds(a.shape, a.dtype) for a in gs],
        scratch_shapes=[pltpu.SemaphoreType.DMA((n, 7)), pltpu.SemaphoreType.DMA((n, 7)), pltpu.SemaphoreType.DMA((n,))],
    )(*gs)


_HBM = pl.BlockSpec(memory_space=pltpu.HBM)
_SEM = pl.BlockSpec(memory_space=pltpu.SEMAPHORE)
_EFFECT = pltpu.SideEffectType.DATAFLOW_SIDE_EFFECTING
_CHIP_FLIPS = ((1, 0), (0, 1), (1, 1))
_TOKEN = (SUBLANE, LANE)


def _flip(v, f):
    return (1 - v) if f else v


def _in_hbm(a):
    return pltpu.with_memory_space_constraint(a, pltpu.HBM)


def _ag_peers(x, y, c):
    return [(x, y, 1 - c)] + [(_flip(x, dx), _flip(y, dy), c) for dx, dy in _CHIP_FLIPS]


def _ag_copies(x_refs, land_refs, send_sems, recv_sems, *, landing):
    x, y, c = _place()
    peers = _ag_peers(x, y, c)
    cps = []
    for i in range(len(x_refs)):
        for k, peer in enumerate(peers):
            origin = _slot(*peer) if landing else _slot(x, y, c)
            cps.append(pltpu.make_async_remote_copy(
                src_ref=x_refs[i], dst_ref=land_refs[i].at[origin], send_sem=send_sems.at[4 * i + k],
                recv_sem=recv_sems.at[4 * i + k], device_id=peer, device_id_type=MESH))
    return cps


OWN_BLOCK_BYTES = 4 * 1024 * 1024


def _place_own(x, me, *, name):
    r, c = x.shape
    tr = r if r * c * x.dtype.itemsize <= OWN_BLOCK_BYTES else r // 4

    def body(me_ref, x_ref, o_ref):
        o_ref[...] = x_ref[...]

    return pl.pallas_call(
        body, name=name,
        grid_spec=pltpu.PrefetchScalarGridSpec(
            num_scalar_prefetch=1, grid=(r // tr,),
            in_specs=[pl.BlockSpec((tr, c), lambda i, me_ref: (i, 0))],
            out_specs=pl.BlockSpec((None, tr, c), lambda i, me_ref: (me_ref[0], i, 0))),
        out_shape=_sds((N_DEV, r, c), x.dtype), compiler_params=_cparams("parallel"),
    )(me, x)


def _ag_start(xs, *, name, dep=None):
    n = len(xs)
    me = _slot(*_place()).astype(jnp.int32).reshape(1)
    lands = [_place_own(a, me, name=f"{name}_own_{i}") for i, a in enumerate(xs)]
    n_in = 2 * n + (0 if dep is None else 1)

    def body(*refs):
        x_refs, land_refs = refs[:n], refs[n:2 * n]
        send_sems, recv_sems = refs[n_in], refs[n_in + 1]
        token = refs[-1]
        for cp in _ag_copies(x_refs, land_refs, send_sems, recv_sems, landing=False):
            cp.start()
        token[...] = jnp.zeros_like(token)

    out = pl.pallas_call(
        body, name=name,
        out_shape=(pltpu.SemaphoreType.DMA((4 * n,)), pltpu.SemaphoreType.DMA((4 * n,)),
                   *[pltpu.HBM(a.shape, a.dtype) for a in xs], *[pltpu.HBM(a.shape, a.dtype) for a in lands],
                   _sds(_TOKEN, F32)),
        in_specs=[_HBM] * (2 * n) + ([] if dep is None else [_ANY]),
        out_specs=(_SEM, _SEM, *[_HBM] * (2 * n), pl.BlockSpec(memory_space=pltpu.VMEM)),
        input_output_aliases={i: 2 + i for i in range(2 * n)},
        compiler_params=pltpu.CompilerParams(has_side_effects=_EFFECT),
    )(*[_in_hbm(a) for a in xs], *[_in_hbm(a) for a in lands], *(() if dep is None else (dep,)))
    return (out[0], out[1], list(out[2:2 + n]), list(out[2 + n:2 + 2 * n])), out[-1]


def _ag_wait(started, after, *, name):
    send_sems, recv_sems, xs, lands = started
    n = len(xs)

    def body(*refs):
        x_refs, land_refs = refs[:n], refs[n:2 * n]
        for cp in _ag_copies(x_refs, land_refs, refs[2 * n], refs[2 * n + 1], landing=True):
            cp.wait_send()
            cp.wait_recv()

    out = pl.pallas_call(
        body, name=name,
        out_shape=tuple(pltpu.HBM(a.shape, a.dtype) for a in xs + lands),
        in_specs=[_HBM] * (2 * n) + [_SEM, _SEM, _ANY], out_specs=tuple([_HBM] * (2 * n)),
        input_output_aliases={i: i for i in range(2 * n)},
        compiler_params=pltpu.CompilerParams(has_side_effects=_EFFECT),
    )(*xs, *lands, send_sems, recv_sems, after)
    return list(out[:n]), list(out[n:])


def _ag_forward(lands, *, name):
    n = len(lands)

    def body(*refs):
        land = refs[n:2 * n]
        send_sems, recv_sems = refs[2 * n:]
        x, y, c = _place()
        sibling = (x, y, 1 - c)

        def copy(i, j, core):
            dx, dy = _CHIP_FLIPS[j]
            rows = land[i].at[_slot(_flip(x, dx), _flip(y, dy), core)]
            return pltpu.make_async_remote_copy(src_ref=rows, dst_ref=rows, send_sem=send_sems.at[i, j],
                                                recv_sem=recv_sems.at[i, j], device_id=sibling, device_id_type=MESH)

        sends = [copy(i, j, c) for i in range(n) for j in range(3)]
        for cp in sends:
            cp.start()
        for i in range(n):
            for j in range(3):
                copy(i, j, 1 - c).wait_recv()
        for cp in sends:
            cp.wait_send()

    return pl.pallas_call(
        body, name=name, in_specs=[_ANY] * n, out_specs=[_ANY] * n,
        out_shape=[_sds(a.shape, a.dtype) for a in lands], input_output_aliases={i: i for i in range(n)},
        scratch_shapes=[pltpu.SemaphoreType.DMA((n, 3)), pltpu.SemaphoreType.DMA((n, 3))],
    )(*lands)


def _sib_copies(g_refs, land_refs, send_sems, recv_sems):
    x, y, c = _place()
    return [pltpu.make_async_remote_copy(
        src_ref=g_refs[i].at[:, 1 - c], dst_ref=land_refs[i], send_sem=send_sems.at[i], recv_sem=recv_sems.at[i],
        device_id=(x, y, 1 - c), device_id_type=MESH) for i in range(len(g_refs))]


def _sib_start(g4s, *, name):
    n = len(g4s)
    lands = [lax.empty((4,) + a.shape[2:], a.dtype) for a in g4s]

    def body(*refs):
        for cp in _sib_copies(refs[:n], refs[n:2 * n], refs[2 * n], refs[2 * n + 1]):
            cp.start()
        refs[-1][...] = jnp.zeros_like(refs[-1])

    out = pl.pallas_call(
        body, name=name,
        out_shape=(pltpu.SemaphoreType.DMA((n,)), pltpu.SemaphoreType.DMA((n,)),
                   *[pltpu.HBM(a.shape, a.dtype) for a in g4s], *[pltpu.HBM(a.shape, a.dtype) for a in lands],
                   _sds(_TOKEN, F32)),
        in_specs=[_HBM] * (2 * n),
        out_specs=(_SEM, _SEM, *[_HBM] * (2 * n), pl.BlockSpec(memory_space=pltpu.VMEM)),
        input_output_aliases={i: 2 + i for i in range(2 * n)},
        compiler_params=pltpu.CompilerParams(has_side_effects=_EFFECT),
    )(*[_in_hbm(a) for a in g4s], *[_in_hbm(a) for a in lands])
    return (out[0], out[1], list(out[2:2 + n]), list(out[2 + n:2 + 2 * n])), out[-1]


def _sib_wait(started, after, *, name):
    send_sems, recv_sems, g4s, lands = started
    n = len(g4s)

    def body(*refs):
        for cp in _sib_copies(refs[:n], refs[n:2 * n], refs[2 * n], refs[2 * n + 1]):
            cp.wait_send()
            cp.wait_recv()

    out = pl.pallas_call(
        body, name=name,
        out_shape=tuple(pltpu.HBM(a.shape, a.dtype) for a in g4s + lands),
        in_specs=[_HBM] * (2 * n) + [_SEM, _SEM, _ANY], out_specs=tuple([_HBM] * (2 * n)),
        input_output_aliases={i: i for i in range(2 * n)},
        compiler_params=pltpu.CompilerParams(has_side_effects=_EFFECT),
    )(*g4s, *lands, send_sems, recv_sems, after)
    return list(out[:n]), list(out[n:])


PAIR_SUM_BLOCK_BYTES = 3 * 1024 * 1024


def _rs_rows(r, c):
    tr = r
    while tr * c * 2 > PAIR_SUM_BLOCK_BYTES and tr % 2 == 0:
        tr //= 2
    return tr


def _rs_pair_sum(g4, from_sibling, core, *, name):
    _, _, r, c = g4.shape
    tr = _rs_rows(r, c)

    def body(core_ref, g_ref, a_ref, o_ref):
        o_ref[...] = (g_ref[...].astype(F32) + a_ref[...].astype(F32)).astype(BF16)

    blk = pl.BlockSpec((None, tr, c), lambda k, i, core_ref: (k, i, 0))
    return pl.pallas_call(
        body, name=name,
        grid_spec=pltpu.PrefetchScalarGridSpec(
            num_scalar_prefetch=1, grid=(4, r // tr),
            in_specs=[pl.BlockSpec((None, None, tr, c), lambda k, i, core_ref: (k, core_ref[0], i, 0)), blk],
            out_specs=blk),
        out_shape=_sds((4, r, c), BF16), compiler_params=_cparams("parallel", "parallel"),
    )(core, g4, from_sibling)


def _rs_copies(h_refs, land_refs, send_sems, recv_sems):
    x, y, c = _place()
    cps = []
    for i in range(len(h_refs)):
        for k, (dx, dy) in enumerate(_CHIP_FLIPS):
            px, py = _flip(x, dx), _flip(y, dy)
            cps.append(pltpu.make_async_remote_copy(
                src_ref=h_refs[i].at[2 * px + py], dst_ref=land_refs[i].at[k], send_sem=send_sems.at[3 * i + k],
                recv_sem=recv_sems.at[3 * i + k], device_id=(px, py, c), device_id_type=MESH))
    return cps


def _rs_start(hs, *, name):
    n = len(hs)
    lands = [lax.empty((3,) + a.shape[1:], a.dtype) for a in hs]

    def body(*refs):
        h_refs, land_refs = refs[:n], refs[n:2 * n]
        token = refs[-1]
        for cp in _rs_copies(h_refs, land_refs, refs[2 * n], refs[2 * n + 1]):
            cp.start()
        token[...] = jnp.zeros_like(token)

    out = pl.pallas_call(
        body, name=name,
        out_shape=(pltpu.SemaphoreType.DMA((3 * n,)), pltpu.SemaphoreType.DMA((3 * n,)),
                   *[pltpu.HBM(a.shape, a.dtype) for a in hs], *[pltpu.HBM(a.shape, a.dtype) for a in lands],
                   _sds(_TOKEN, F32)),
        in_specs=[_HBM] * (2 * n),
        out_specs=(_SEM, _SEM, *[_HBM] * (2 * n), pl.BlockSpec(memory_space=pltpu.VMEM)),
        input_output_aliases={i: 2 + i for i in range(2 * n)},
        compiler_params=pltpu.CompilerParams(has_side_effects=_EFFECT),
    )(*[_in_hbm(a) for a in hs], *[_in_hbm(a) for a in lands])
    return (out[0], out[1], list(out[2:2 + n]), list(out[2 + n:2 + 2 * n])), out[-1]


def _rs_wait(started, after, *, name):
    send_sems, recv_sems, hs, lands = started
    n = len(hs)

    def body(*refs):
        for cp in _rs_copies(refs[:n], refs[n:2 * n], refs[2 * n], refs[2 * n + 1]):
            cp.wait_send()
            cp.wait_recv()

    out = pl.pallas_call(
        body, name=name,
        out_shape=tuple(pltpu.HBM(a.shape, a.dtype) for a in hs + lands),
        in_specs=[_HBM] * (2 * n) + [_SEM, _SEM, _ANY], out_specs=tuple([_HBM] * (2 * n)),
        input_output_aliases={i: i for i in range(2 * n)},
        compiler_params=pltpu.CompilerParams(has_side_effects=_EFFECT),
    )(*hs, *lands, send_sems, recv_sems, after)
    return list(out[:n]), list(out[n:])


def _adamw_chips(w, h, others, m, v, chip, *, name):
    r, c = w.shape

    def body(chip_ref, w_ref, h_ref, o_ref, m_ref, v_ref, g_ref, d_ref, nm_ref, nv_ref):
        g = h_ref[...].astype(F32)
        for k in range(3):
            g = g + o_ref[k].astype(F32)
        g_ref[...] = g
        d_ref[...], nm_ref[...], nv_ref[...] = _adamw_math(w_ref[...], g, m_ref[...], v_ref[...])

    blk = pl.BlockSpec((ADAM_ROWS, c), lambda i, chip_ref: (i, 0))
    return pl.pallas_call(
        body, name=name,
        grid_spec=pltpu.PrefetchScalarGridSpec(
            num_scalar_prefetch=1, grid=(r // ADAM_ROWS,),
            in_specs=[blk, pl.BlockSpec((None, ADAM_ROWS, c), lambda i, chip_ref: (chip_ref[0], i, 0)),
                      pl.BlockSpec((3, ADAM_ROWS, c), lambda i, chip_ref: (0, i, 0)), blk, blk],
            out_specs=[blk] * 4),
        out_shape=[_sds((r, c), F32)] * 4, compiler_params=_cparams("parallel"),
    )(chip, w, h, others, m, v)


def _adamw_chips_layers(w, hs, others, m, v, chip, *, first, prev=None, deps=(), name):
    nl, r, c = w.shape
    n = len(hs)
    n_prev = 0 if prev is None else 4

    def body(chip_ref, w_ref, m_ref, v_ref, *rest):
        h_refs, o_refs = rest[:n], rest[n:2 * n]
        g_ref, d_ref, nm_ref, nv_ref = rest[2 * n + n_prev + len(deps):]
        k_now = pl.program_id(0)
        g = jnp.zeros((ADAM_ROWS, c), F32)
        for k in range(n):
            gk = h_refs[k][...].astype(F32)
            for j in range(3):
                gk = gk + o_refs[k][j].astype(F32)
            g = jnp.where(k_now == k, gk, g)
        g_ref[...] = g
        d_ref[...], nm_ref[...], nv_ref[...] = _adamw_math(w_ref[...], g, m_ref[...], v_ref[...])

    def rows(k):
        return lambda l, i: jnp.where(l == k, i, 0)

    blk = pl.BlockSpec((None, ADAM_ROWS, c), lambda l, i, chip_ref: (l + first, i, 0))
    h_specs = [pl.BlockSpec((None, ADAM_ROWS, c), lambda l, i, chip_ref, f=rows(k): (chip_ref[0], f(l, i), 0))
               for k in range(n)]
    o_specs = [pl.BlockSpec((3, ADAM_ROWS, c), lambda l, i, chip_ref, f=rows(k): (0, f(l, i), 0)) for k in range(n)]
    return pl.pallas_call(
        body, name=name,
        grid_spec=pltpu.PrefetchScalarGridSpec(
            num_scalar_prefetch=1, grid=(n, r // ADAM_ROWS),
            in_specs=[blk, blk, blk] + h_specs + o_specs + [_ANY] * (n_prev + len(deps)), out_specs=[blk] * 4),
        out_shape=[_sds((nl, r, c), F32)] * 4,
        input_output_aliases={4 + 2 * n + q: q for q in range(n_prev)},
        compiler_params=_cparams("arbitrary", "arbitrary"),
    )(chip, w, m, v, *hs, *others, *(() if prev is None else prev), *deps)


def _gather_begin(xs, tag, dep=None):
    return _ag_start(xs, name=f"ag_start_{tag}", dep=dep)


def _gather_end(started, after, tag):
    _, lands = _ag_wait(started, after, name=f"ag_wait_{tag}")
    return _ag_forward(lands, name=f"ag_forward_{tag}")


def _scatter_pair(gs, tag):
    g4s = [g.reshape((4, 2) + g.shape[1:]) for g in gs]
    return _sib_start(g4s, name=f"rs_sib_start_{tag}")


def _scatter_chips(pair, after, tag):
    core = lax.axis_index("c").astype(jnp.int32).reshape(1)
    g4s, got = _sib_wait(pair, after, name=f"rs_sib_wait_{tag}")
    hs = [_rs_pair_sum(g4, a, core, name=f"rs_pair_sum_{tag}_{i}") for i, (g4, a) in enumerate(zip(g4s, got))]
    return _rs_start(hs, name=f"rs_start_{tag}")


def _scatter_end(started, after, tag):
    return _rs_wait(started, after, name=f"rs_wait_{tag}")


MM_TM = 512


def _ffn_fwd(x_mid, g_row, wup_p, wdown, conv_w, conv_b, tag, dep=None):
    hb, _ = _rms_fwd(x_mid, g_row, want_f32=False, name=f"ffn_norm_{tag}", dep=dep)
    up = _mm_nn_pieces(hb, wup_p, tm=MM_TM, name=f"ffn_up_{tag}")
    a = _conv_gate_fwd(up, conv_w, conv_b, name=f"ffn_conv_{tag}")
    if callable(wdown):
        wdown = wdown(a)
    x_out = _mm_nn(a, wdown, tm=1024, tn=512, name=f"ffn_down_{tag}", res=x_mid)
    return x_out, (hb, up, a)


def _ffn_bwd(dx, x_mid, g_row, wup_p, wdown, conv_w, conv_b, saved, tag):
    hb, up, a = saved
    dxb = dx.astype(BF16)
    da = _mm_nt(dxb, wdown, tm=MM_TM, tn=1408, name=f"ffn_da_{tag}")
    dwdown = _mm_tn(a, dxb, tm=512, tn=1024, name=f"ffn_dwdown_{tag}")
    dup_v, dup_g, dconv_w, dconv_b = _conv_gate_bwd(up, da, conv_w, conv_b, name=f"ffn_dconv_{tag}")
    dwup = _mm_tn_pieces(hb, (dup_v, dup_g), pieces=N_DEV, tm=MM_TM, name=f"ffn_dwup_{tag}")
    pair, token = _scatter_pair([dwup, dwdown.reshape(N_DEV, D_FF // N_DEV, D_MODEL)], f"ffn_{tag}")
    dh = _mm_nt_pieces((dup_v, dup_g), wup_p, tm=1024, tn=1024, name=f"ffn_dh_{tag}", dep=token)
    started, token = _scatter_chips(pair, dh, f"ffn_{tag}")
    dx_mid, dg = _rms_bwd(x_mid, g_row, dh, dx, name=f"ffn_dnorm_{tag}", dep=token)
    return dx_mid, dg, dconv_w, dconv_b, started


def _pool_layer_fwd(x, g_row, w, b_row, sc_row, tag, dep=None):
    _, hf = _rms_fwd(x, g_row, want_f32=True, name=f"pool_norm_{tag}", dep=dep)
    return _pool_fwd(hf, x, w, b_row, sc_row, name=f"pool_fwd_{tag}"), (hf,)


def _pool_layer_bwd(dx_mid, x, g_row, w, b_row, sc_row, saved, tag):
    (hf,) = saved
    dh, dw, db, dsc = _pool_bwd(hf, dx_mid, w, b_row, sc_row, name=f"pool_bwd_{tag}")
    dx, dg = _rms_bwd(x, g_row, dh, dx_mid, name=f"pool_dnorm_{tag}")
    return dx, dg, dw, db, dsc


def _sb_layer_fwd(x, g_row, wqkv_p, gains, wo, tag, dep=None):
    hb, _ = _rms_fwd(x, g_row, want_f32=False, name=f"sb_norm_{tag}", dep=dep)
    qkv = _mm_nn_pieces(hb, wqkv_p, tm=MM_TM, name=f"sb_qkv_{tag}")
    qkn = _qk_norm_fwd(qkv, gains, name=f"sb_qknorm_{tag}")
    vb = qkv[:, 2 * D_MODEL:].astype(BF16)
    o = _sb_fwd(qkn, vb, name=f"sb_att_{tag}")
    x_mid = _mm_nn(o, wo, tm=MM_TM, tn=512, name=f"sb_out_{tag}", res=x)
    return x_mid, (hb, qkv, qkn, vb, o)


def _sb_layer_bwd(dx_mid, x, g_row, wqkv_p, gains, wo, saved, tag):
    hb, qkv, qkn, vb, o = saved
    dmb = dx_mid.astype(BF16)
    do = _mm_nt(dmb, wo, tm=MM_TM, tn=512, name=f"sb_do_{tag}", out_dtype=BF16)
    dwo = _mm_tn(o, dmb, tm=512, tn=1024, name=f"sb_dwo_{tag}")
    dqn, dkn, dv = _sb_bwd(qkn, vb, do, name=f"sb_datt_{tag}")
    dq, dqg = _qk_norm_bwd(qkv, gains, dqn, which=0, name=f"sb_dqnorm_{tag}")
    dk, dkg = _qk_norm_bwd(qkv, gains, dkn, which=1, name=f"sb_dknorm_{tag}")
    dqkv = jnp.concatenate([dq, dk, dv.astype(BF16)], axis=1)
    dwqkv = _mm_tn_pieces(hb, dqkv, pieces=N_DEV, tm=MM_TM, name=f"sb_dwqkv_{tag}")
    pair, token = _scatter_pair([dwqkv, dwo.reshape(N_DEV, D_MODEL // N_DEV, D_MODEL)], f"sb_{tag}")
    dh = _mm_nt_pieces(dqkv, wqkv_p, tm=MM_TM, tn=1024, name=f"sb_dh_{tag}", dep=token)
    started, token = _scatter_chips(pair, dh, f"sb_{tag}")
    dx, dg = _rms_bwd(x, g_row, dh, dx_mid, name=f"sb_dnorm_{tag}", dep=token)
    return dx, dg, dqg, dkg, started


def _ssm_params(lam_re, lam_im, log_step, b_re, b_im):
    g, p = SSM_GROUPS, SSM_STATE
    return (lam_re.reshape(g, 1, p), lam_im.reshape(g, 1, p), log_step.reshape(g, 1, 1),
            jnp.transpose(b_re, (0, 2, 1)), jnp.transpose(b_im, (0, 2, 1)))


def _ssm_layer_fwd(x, g_row, raw, c_re, c_im, d_row, wglu_p, bglu_row, tag, dep=None):
    g, p, ch = SSM_GROUPS, SSM_STATE, SSM_CH
    _, hf = _rms_fwd(x, g_row, want_f32=True, name=f"ssm_norm_{tag}", dep=dep)
    ar, ai, bbr, bbi = _ssm_prep_fwd(*raw, name=f"ssm_prep_{tag}")
    wb = jnp.concatenate([bbr.reshape(g * ch, p), bbi.reshape(g * ch, p)], axis=1)
    wc = jnp.concatenate([c_re.reshape(g * ch, p), -c_im.reshape(g * ch, p)], axis=1)
    a_re, a_im = ar.reshape(1, g * p), ai.reshape(1, g * p)
    ylin, yg = _ssm_core_fwd(hf, wb, wc, a_re, a_im, d_row, name=f"ssm_core_{tag}")
    x_mid, val, gate = _glu_fwd(yg, wglu_p, bglu_row, x, tm=MM_TM, name=f"ssm_glu_{tag}")
    return x_mid, (hf, wb, wc, a_re, a_im, ylin, yg, val, gate)


def _ssm_layer_bwd(dx_mid, x, g_row, raw, d_row, wglu_p, saved, tag):
    g, p, ch = SSM_GROUPS, SSM_STATE, SSM_CH
    hf, wb, wc, a_re, a_im, ylin, yg, val, gate = saved
    dgv, dbglu = _glu_bwd(dx_mid, val, gate, name=f"ssm_dglu_{tag}")
    dwglu = _mm_tn_pieces(yg, dgv, pieces=N_DEV, tm=MM_TM, name=f"ssm_dwglu_{tag}")
    pair, token = _scatter_pair([dwglu], f"ssm_{tag}")
    dyg = _mm_nt_pieces(dgv, wglu_p, tm=MM_TM, tn=1024, name=f"ssm_dyg_{tag}", dep=token)
    du, dwb, dwc, dar, dai, dd = _ssm_core_bwd(hf, ylin, dyg, wb, wc, a_re, a_im, d_row, name=f"ssm_dcore_{tag}")
    dc_re = dwc[:, :p].reshape(g, ch, p)
    dc_im = -dwc[:, p:].reshape(g, ch, p)
    dlr, dli, dls, dbtr, dbti = _ssm_prep_bwd(
        *raw, dar.reshape(g, 1, p), dai.reshape(g, 1, p), dwb[:, :p].reshape(g, ch, p), dwb[:, p:].reshape(g, ch, p),
        name=f"ssm_dprep_{tag}")
    started, token = _scatter_chips(pair, du, f"ssm_{tag}")
    dx, dg = _rms_bwd(x, g_row, du, dx_mid, name=f"ssm_dnorm_{tag}", dep=token)
    grads = dict(ssm_lam_re=dlr.reshape(1, g, p), ssm_lam_im=dli.reshape(1, g, p), ssm_log_step=dls.reshape(1, g),
                 ssm_b_re=jnp.transpose(dbtr, (0, 2, 1))[None], ssm_b_im=jnp.transpose(dbti, (0, 2, 1))[None],
                 ssm_c_re=dc_re[None], ssm_c_im=dc_im[None], ssm_d=dd, ssm_b_glu=dbglu)
    return dx, dg, grads, started


_WEIGHTS = ["norm_mix_g", "norm_ffn_g", "pool_w", "pool_b", "pool_scale", "sb_w_qkv", "sb_q_gain", "sb_k_gain", "sb_w_o",
            "ssm_lam_re", "ssm_lam_im", "ssm_log_step", "ssm_b_re", "ssm_b_im", "ssm_c_re", "ssm_c_im", "ssm_d",
            "ssm_w_glu", "ssm_b_glu", "ffn_w_up", "ffn_conv_w", "ffn_conv_b", "ffn_w_down"]
_INPUTS = ["x"] + _WEIGHTS + ["loss_target"] + ["m_" + n for n in _WEIGHTS] + ["v_" + n for n in _WEIGHTS]
_REPLICATED = ["norm_mix_g", "norm_ffn_g", "sb_q_gain", "sb_k_gain", "ssm_lam_re", "ssm_lam_im", "ssm_log_step",
               "ssm_b_re", "ssm_b_im", "ssm_c_re", "ssm_c_im", "ffn_conv_b"]
_SMALL_SHARDED = {"pool_b": 1, "pool_scale": 1, "ssm_d": 1, "ssm_b_glu": 1, "ffn_conv_w": 2}
_BIG = ["pool_w", "sb_w_qkv", "sb_w_o", "ssm_w_glu", "ffn_w_up", "ffn_w_down"]
PACK_COLS = 512


def _pack(arrays):
    flat = jnp.concatenate([a.reshape(-1).astype(F32) for a in arrays])
    rows = -(-flat.shape[0] // (PACK_COLS * PACK_ROWS)) * PACK_ROWS
    return jnp.pad(flat, (0, rows * PACK_COLS - flat.shape[0])).reshape(rows, PACK_COLS)


def _unpack(packed, shapes, lead=()):
    flat = packed.reshape(lead + (-1,))
    out, off = [], 0
    for shp in shapes:
        n = math.prod(shp)
        out.append(flat[..., off:off + n].reshape(lead + tuple(shp)))
        off += n
    return out


def _unshard(gathered, axis):
    g = jnp.moveaxis(gathered, 0, axis)
    shp = g.shape
    return g.reshape(shp[:axis] + (shp[axis] * shp[axis + 1],) + shp[axis + 2:])


def _step(p):
    s = p["x"].shape[1]
    x = p["x"].reshape(s, D_MODEL)
    me = _slot(*_place())

    small_local = [p[n] for n in _SMALL_SHARDED]
    pool_w_l = p["pool_w"].astype(BF16).reshape(-1, POOL_DIM)
    chip = (2 * lax.axis_index("x") + lax.axis_index("y")).astype(jnp.int32).reshape(1)

    def ffn_shards(i):
        return [p["ffn_w_up"][i].astype(BF16), p["ffn_w_down"][i].astype(BF16)]

    st_first, tok = _gather_begin([pool_w_l, _pack(small_local)], "first")
    st_ffn, tok_ffn = [None] * DEPTH, [None] * DEPTH
    st_ffn[0], tok = _gather_begin(ffn_shards(0)[:1], "ffn_0", dep=tok)
    st_down0, tok = _gather_begin(ffn_shards(0)[1:], "ffn_0_down", dep=tok)
    st_sb, tok = _gather_begin([p["sb_w_qkv"][0].astype(BF16), p["sb_w_o"][0].astype(BF16)], "sb", dep=tok)
    st_ffn[1], tok = _gather_begin(ffn_shards(1), "ffn_1", dep=tok)
    st_glu, tok = _gather_begin([p["ssm_w_glu"][0].astype(BF16)], "glu", dep=tok)
    ag = _gather_end(st_first, tok, "first")
    n_pool = p["pool_w"].shape[0]
    pool_w = jnp.transpose(ag[0].reshape(N_DEV, n_pool, POOL_GROUPS, POOL_DIM // N_DEV, POOL_DIM), (1, 2, 0, 3, 4))
    pool_w = pool_w.reshape(n_pool, POOL_GROUPS, POOL_DIM, POOL_DIM)
    small_full = {}
    for n, g in zip(_SMALL_SHARDED, _unpack(ag[1], [a.shape for a in small_local], lead=(N_DEV,))):
        small_full[n] = _unshard(g, _SMALL_SHARDED[n])
    mix_w = {}
    wup_p, wdown = [None] * DEPTH, [None] * DEPTH

    gains = jnp.stack([p["sb_q_gain"][0], p["sb_k_gain"][0]])[:, None, :]
    ssm_raw = _ssm_params(p["ssm_lam_re"][0], p["ssm_lam_im"][0], p["ssm_log_step"][0], p["ssm_b_re"][0],
                          p["ssm_b_im"][0])

    def mixer_args(i):
        j = i // 3
        g_row = p["norm_mix_g"][i][None]
        if i % 3 == 0:
            return (g_row, pool_w[j], small_full["pool_b"][j][None], small_full["pool_scale"][j][None])
        if i % 3 == 1:
            return (g_row, mix_w["qkv"], gains, mix_w["o"])
        return (g_row, ssm_raw, p["ssm_c_re"][0], p["ssm_c_im"][0], small_full["ssm_d"], mix_w["glu"],
                small_full["ssm_b_glu"])

    def ffn_args(i):
        return (p["norm_ffn_g"][i][None], wup_p[i], wdown[i], small_full["ffn_conv_w"][i], p["ffn_conv_b"][i][None])

    xs_in, xs_mid, saved_mix, saved_ffn = [], [], [], []
    for i in range(DEPTH):
        xs_in.append(x)
        if i == 1:
            mix_w["qkv"], wo_g = _gather_end(st_sb, x, "sb")
            mix_w["o"] = wo_g.reshape(D_MODEL, D_MODEL)
        if i == 2:
            (mix_w["glu"],) = _gather_end(st_glu, x, "glu")
        dep = None
        if 1 <= i and i + 1 < DEPTH:
            st_ffn[i + 1], dep = _gather_begin(ffn_shards(i + 1), f"ffn_{i + 1}")
        fwd = (_pool_layer_fwd, _sb_layer_fwd, _ssm_layer_fwd)[i % 3]
        x, sv = fwd(x, *mixer_args(i), f"l{i}", dep=dep)
        saved_mix.append(sv)
        xs_mid.append(x)
        if i == 0:
            (wup_p[i],) = _gather_end(st_ffn[i], x, f"ffn_{i}")

            def wdown_now(after):
                (wd_g,) = _gather_end(st_down0, after, "ffn_0_down")
                wdown[0] = wd_g.reshape(D_FF, D_MODEL)
                return wdown[0]
        else:
            wup_p[i], wd_g = _gather_end(st_ffn[i], x, f"ffn_{i}")
            wdown[i] = wd_g.reshape(D_FF, D_MODEL)
        g_row, wu, wd, cw, cb = ffn_args(i)
        x, sv = _ffn_fwd(x, g_row, wu, wdown_now if i == 0 else wd, cw, cb, f"l{i}")
        saved_ffn.append(sv)
    dx, loss_part = _loss_head(x, p["loss_target"].reshape(s, D_MODEL), name="loss_head")

    grads = {}
    dg_mix, dg_ffn = [None] * DEPTH, [None] * DEPTH
    dconv_w, dconv_b = [None] * DEPTH, [None] * DEPTH
    dpool = {"w": {}, "b": {}, "scale": {}}
    out = {}

    def big_update(n, h, others, idx=None):
        w, m, v = (p[pre + n] if idx is None else p[pre + n][idx] for pre in ("", "m_", "v_"))
        cols = h.shape[-1]
        r = _adamw_chips(w.reshape(-1, cols), h, others, m.reshape(-1, cols), v.reshape(-1, cols), chip,
                         name=f"adamw_{n}" + ("" if idx is None else f"_{idx}"))
        return [a.reshape(w.shape) for a in r]

    kinds = ("grad", "delta", "new_m", "new_v")
    ffn_upd = {"ffn_w_up": [None] * DEPTH, "ffn_w_down": [None] * DEPTH}

    def finish(entry, after):
        names, idx, started, tag = entry
        hs, others = _scatter_end(started, after, tag)
        for n, h, o in zip(names, hs, others):
            if idx is None:
                for kind, a in zip(kinds, big_update(n, h, o)):
                    out[kind + "_" + n] = a
            else:
                ffn_upd[n][idx] = (h, o)

    pending = []
    for i in reversed(range(DEPTH)):
        dx, dg_ffn[i], dconv_w[i], dconv_b[i], started = _ffn_bwd(
            dx, xs_mid[i], *ffn_args(i), saved_ffn[i], f"l{i}")
        for entry in pending:
            finish(entry, dx)
        pending = [(("ffn_w_up", "ffn_w_down"), i, started, f"ffn_l{i}")]
        margs = mixer_args(i)
        if i % 3 == 0:
            j = i // 3
            dx, dg_mix[i], dpool["w"][j], dpool["b"][j], dpool["scale"][j] = _pool_layer_bwd(
                dx, xs_in[i], *margs, saved_mix[i], f"l{i}")
        elif i % 3 == 1:
            dx, dg_mix[i], dqg, dkg, started = _sb_layer_bwd(dx, xs_in[i], *margs, saved_mix[i], f"l{i}")
            grads["sb_q_gain"], grads["sb_k_gain"] = dqg, dkg
            pending.append((("sb_w_qkv", "sb_w_o"), None, started, f"sb_l{i}"))
        else:
            g_row, raw, _, _, d_row, wg, _ = margs
            dx, dg_mix[i], sg, started = _ssm_layer_bwd(dx, xs_in[i], g_row, raw, d_row, wg, saved_mix[i], f"l{i}")
            grads.update(sg)
            pending.append((("ssm_w_glu",), None, started, f"ssm_l{i}"))
    grad_x = dx.reshape(1, s, D_MODEL)
    grads["norm_mix_g"] = jnp.concatenate(dg_mix, axis=0)
    grads["norm_ffn_g"] = jnp.concatenate(dg_ffn, axis=0)
    grads["ffn_conv_w"] = jnp.stack(dconv_w)
    grads["ffn_conv_b"] = jnp.concatenate(dconv_b, axis=0)
    grads["pool_b"] = jnp.concatenate([dpool["b"][j] for j in range(n_pool)], axis=0)
    grads["pool_scale"] = jnp.concatenate([dpool["scale"][j] for j in range(n_pool)], axis=0)
    dpw = jnp.stack([dpool["w"][j] for j in range(n_pool)])
    dpw = dpw.reshape(n_pool, POOL_GROUPS, N_DEV, POOL_DIM // N_DEV, POOL_DIM)
    pair, token = _scatter_pair([jnp.transpose(dpw, (2, 0, 1, 3, 4)).reshape(N_DEV, -1, POOL_DIM)], "pool")
    pool_started, tok_pool = _scatter_chips(pair, token, "pool")
    (ffn_first,) = pending

    small_names = _REPLICATED + list(_SMALL_SHARDED)
    full_shapes = [p[n].shape for n in _REPLICATED] + [small_full[n].shape for n in _SMALL_SHARDED]
    part = _pack([grads[n].reshape(shp) for n, shp in zip(small_names, full_shapes)] + [loss_part[0]])
    st_small, tok_small = _gather_begin([part], "small_grads")
    upper = {}
    for n in ffn_upd:
        upper[n] = _adamw_chips_layers(p[n], [ffn_upd[n][i][0] for i in range(1, DEPTH)],
                                       [ffn_upd[n][i][1] for i in range(1, DEPTH)], p["m_" + n], p["v_" + n], chip,
                                       first=1, deps=(tok_small, tok_pool), name=f"adamw_{n}_upper")
    finish((("pool_w",), None, pool_started, "pool"), upper["ffn_w_down"][0])
    (parts,) = _gather_end(st_small, upper["ffn_w_up"][0], "small_grads")
    summed = _unpack(_sum_parts(parts, name="sum_small_grads"), full_shapes + [(LANE,)])
    loss = summed[-1][0]
    small_g = {}
    for n, g in zip(small_names, summed[:-1]):
        if n in _SMALL_SHARDED:
            ax = _SMALL_SHARDED[n]
            g = lax.dynamic_slice_in_dim(g, me * p[n].shape[ax], p[n].shape[ax], axis=ax)
        small_g[n] = g

    local_shapes = [p[n].shape for n in small_names]
    packs = [_pack([p[pre + n] for n in small_names]) for pre in ("", "m_", "v_")]
    res = _adamw_flat(packs[0], _pack([small_g[n] for n in small_names]), packs[1], packs[2], name="adamw_small")
    for kind, packed in zip(("delta", "new_m", "new_v"), res):
        for n, a in zip(small_names, _unpack(packed, local_shapes)):
            out[kind + "_" + n] = a
    for n in small_names:
        out["grad_" + n] = small_g[n]

    finish(ffn_first, res[0])
    for n in ffn_upd:
        h, o = ffn_upd[n][0]
        upd = _adamw_chips_layers(p[n], [h], [o], p["m_" + n], p["v_" + n], chip, first=0, prev=upper[n],
                                  name=f"adamw_{n}_first")
        for kind, a in zip(kinds, upd):
            out[kind + "_" + n] = a

    return (loss, grad_x, *[out["grad_" + n] for n in _WEIGHTS], *[out["delta_" + n] for n in _WEIGHTS],
            *[out["new_m_" + n] for n in _WEIGHTS], *[out["new_v_" + n] for n in _WEIGHTS])


def kernel(x, norm_mix_g, norm_ffn_g, pool_w, pool_b, pool_scale, sb_w_qkv, sb_q_gain, sb_k_gain, sb_w_o, ssm_lam_re, ssm_lam_im, ssm_log_step, ssm_b_re, ssm_b_im, ssm_c_re, ssm_c_im, ssm_d, ssm_w_glu, ssm_b_glu, ffn_w_up, ffn_conv_w, ffn_conv_b, ffn_w_down, loss_target, m_norm_mix_g, m_norm_ffn_g, m_pool_w, m_pool_b, m_pool_scale, m_sb_w_qkv, m_sb_q_gain, m_sb_k_gain, m_sb_w_o, m_ssm_lam_re, m_ssm_lam_im, m_ssm_log_step, m_ssm_b_re, m_ssm_b_im, m_ssm_c_re, m_ssm_c_im, m_ssm_d, m_ssm_w_glu, m_ssm_b_glu, m_ffn_w_up, m_ffn_conv_w, m_ffn_conv_b, m_ffn_w_down, v_norm_mix_g, v_norm_ffn_g, v_pool_w, v_pool_b, v_pool_scale, v_sb_w_qkv, v_sb_q_gain, v_sb_k_gain, v_sb_w_o, v_ssm_lam_re, v_ssm_lam_im, v_ssm_log_step, v_ssm_b_re, v_ssm_b_im, v_ssm_c_re, v_ssm_c_im, v_ssm_d, v_ssm_w_glu, v_ssm_b_glu, v_ffn_w_up, v_ffn_conv_w, v_ffn_conv_b, v_ffn_w_down):
    args = (x, norm_mix_g, norm_ffn_g, pool_w, pool_b, pool_scale, sb_w_qkv, sb_q_gain, sb_k_gain, sb_w_o, ssm_lam_re, ssm_lam_im, ssm_log_step, ssm_b_re, ssm_b_im, ssm_c_re, ssm_c_im, ssm_d, ssm_w_glu, ssm_b_glu, ffn_w_up, ffn_conv_w, ffn_conv_b, ffn_w_down, loss_target, m_norm_mix_g, m_norm_ffn_g, m_pool_w, m_pool_b, m_pool_scale, m_sb_w_qkv, m_sb_q_gain, m_sb_k_gain, m_sb_w_o, m_ssm_lam_re, m_ssm_lam_im, m_ssm_log_step, m_ssm_b_re, m_ssm_b_im, m_ssm_c_re, m_ssm_c_im, m_ssm_d, m_ssm_w_glu, m_ssm_b_glu, m_ffn_w_up, m_ffn_conv_w, m_ffn_conv_b, m_ffn_w_down, v_norm_mix_g, v_norm_ffn_g, v_pool_w, v_pool_b, v_pool_scale, v_sb_w_qkv, v_sb_q_gain, v_sb_k_gain, v_sb_w_o, v_ssm_lam_re, v_ssm_lam_im, v_ssm_log_step, v_ssm_b_re, v_ssm_b_im, v_ssm_c_re, v_ssm_c_im, v_ssm_d, v_ssm_w_glu, v_ssm_b_glu, v_ffn_w_up, v_ffn_conv_w, v_ffn_conv_b, v_ffn_w_down)
    return _step(dict(zip(_INPUTS, args)))
```

```python
import functools
import math

import jax
import jax.numpy as jnp
from jax import lax
from jax.experimental import pallas as pl
from jax.experimental.pallas import tpu as pltpu

F32 = jnp.float32
BF16 = jnp.bfloat16

N_DEV = 8
D_MODEL = 2048
D_FF = 5632
DEPTH = 4
POOL_GROUPS = 4
POOL_DIM = 512
HEADS = 16
HEAD_DIM = 128
SSM_GROUPS = 128
SSM_CH = 16
SSM_STATE = 64
SSM_BLOCK_GROUPS = 8
SSM_BLOCK_LANES = SSM_BLOCK_GROUPS * SSM_STATE
RMS_EPS = 1e-6
ADAM_LR = 0.001
ADAM_B1 = 0.9
ADAM_B2 = 0.999
ADAM_EPS = 1e-08
ADAM_WD = 0.01
ADAM_STEP = 10

VMEM_LIMIT_BYTES = 56 * 1024 * 1024
LANE = 128
SUBLANE = 8
MESH = pl.DeviceIdType.MESH


def _cparams(*sem):
    return pltpu.CompilerParams(dimension_semantics=tuple(sem), vmem_limit_bytes=VMEM_LIMIT_BYTES)


def _sds(shape, dtype):
    return jax.ShapeDtypeStruct(tuple(shape), dtype)


def _mm(a, b, *, dims, grid, a_spec, b_spec, o_spec, out_shape, out_dtype, name, k_axis=None, acc_shape=None,
        res=None, res_spec=None, a_alt=None, b_alt=None, alt_axis=None, alt_from=None, dep=None):
    nk = grid[k_axis] if k_axis is not None else 1
    n_in = 2 + sum(e is not None for e in (res, a_alt, b_alt, dep))

    def body(*refs):
        a_ref, b_ref = refs[:2]
        rest = list(refs[2:n_in])
        r_ref = rest.pop(0) if res is not None else None
        a2_ref = rest.pop(0) if a_alt is not None else None
        b2_ref = rest.pop(0) if b_alt is not None else None
        o_ref = refs[n_in]
        scr = refs[n_in + 1:]
        av, bv = a_ref[...], b_ref[...]
        if a2_ref is not None:
            av = jnp.where(pl.program_id(alt_axis) >= alt_from, a2_ref[...], av)
        if b2_ref is not None:
            bv = jnp.where(pl.program_id(alt_axis) >= alt_from, b2_ref[...], bv)
        p = lax.dot_general(av, bv, (dims, ((), ())), preferred_element_type=F32)
        if k_axis is None:
            if r_ref is not None:
                p = p + r_ref[...]
            o_ref[...] = p.astype(o_ref.dtype)
        else:
            acc = scr[0]
            k = pl.program_id(k_axis)

            @pl.when(k == 0)
            def _():
                acc[...] = p

            @pl.when(k > 0)
            def _():
                acc[...] += p

            @pl.when(k == nk - 1)
            def _():
                r = acc[...]
                if r_ref is not None:
                    r = r + r_ref[...]
                o_ref[...] = r.astype(o_ref.dtype)

    sem = ["parallel"] * len(grid)
    if k_axis is not None:
        sem[k_axis] = "arbitrary"
    in_specs, args = [a_spec, b_spec], [a, b]
    if res is not None:
        in_specs.append(res_spec)
        args.append(res)
    for alt in (a_alt, b_alt):
        if alt is not None:
            args.append(alt[0])
            in_specs.append(alt[1])
    if dep is not None:
        args.append(dep)
        in_specs.append(pl.BlockSpec((SUBLANE, LANE), lambda *_: (0, 0)))
    scratch = [pltpu.VMEM(acc_shape, F32)] if k_axis is not None else []
    return pl.pallas_call(
        body, name=name, grid=grid, in_specs=in_specs, out_specs=o_spec, out_shape=_sds(out_shape, out_dtype),
        scratch_shapes=scratch, compiler_params=_cparams(*sem),
    )(*args)


NN = ((1,), (0,))
NT = ((1,), (1,))
TN = ((0,), (0,))


def _mm_nn_pieces(a, wp, *, tm, name, out_dtype=F32):
    s, k = a.shape
    tm = min(tm, s)
    p, _, c = wp.shape
    return _mm(a, wp, dims=NN, grid=(s // tm, p),
               a_spec=pl.BlockSpec((tm, k), lambda m, n: (m, 0)),
               b_spec=pl.BlockSpec((None, k, c), lambda m, n: (n, 0, 0)),
               o_spec=pl.BlockSpec((tm, c), lambda m, n: (m, n)),
               out_shape=(s, p * c), out_dtype=out_dtype, name=name)


def _mm_nt_pieces(a, wp, *, tm, tn, name, out_dtype=F32, dep=None):
    p, n, c = wp.shape
    halves = a if isinstance(a, tuple) else None
    a0 = halves[0] if halves else a
    s = a0.shape[0]
    tm = min(tm, s)
    h = p // 2
    alt = {}
    if halves:
        a_spec = pl.BlockSpec((tm, c), lambda m, j, k: (m, jnp.minimum(k, h - 1)))
        alt = dict(a_alt=(halves[1], pl.BlockSpec((tm, c), lambda m, j, k: (m, jnp.maximum(k - h, 0)))),
                   alt_axis=2, alt_from=h)
    else:
        a_spec = pl.BlockSpec((tm, c), lambda m, j, k: (m, k))
    return _mm(a0, wp, dims=NT, grid=(s // tm, n // tn, p), k_axis=2, acc_shape=(tm, tn), a_spec=a_spec,
               b_spec=pl.BlockSpec((None, tn, c), lambda m, j, k: (k, j, 0)),
               o_spec=pl.BlockSpec((tm, tn), lambda m, j, k: (m, j)),
               out_shape=(s, n), out_dtype=out_dtype, name=name, dep=dep, **alt)


def _mm_tn_pieces(a, g, *, pieces, tm, name, out_dtype=BF16):
    s, m = a.shape
    halves = g if isinstance(g, tuple) else None
    g0 = halves[0] if halves else g
    h = pieces // 2
    c = g0.shape[1] // (h if halves else pieces)
    alt = {}
    if halves:
        b_spec = pl.BlockSpec((s, c), lambda n, i: (0, jnp.minimum(n, h - 1)))
        alt = dict(b_alt=(halves[1], pl.BlockSpec((s, c), lambda n, i: (0, jnp.maximum(n - h, 0)))),
                   alt_axis=0, alt_from=h)
    else:
        b_spec = pl.BlockSpec((s, c), lambda n, i: (0, n))
    return _mm(a, g0, dims=TN, grid=(pieces, m // tm), a_spec=pl.BlockSpec((s, tm), lambda n, i: (0, i)),
               b_spec=b_spec, o_spec=pl.BlockSpec((None, tm, c), lambda n, i: (n, i, 0)),
               out_shape=(pieces, m, c), out_dtype=out_dtype, name=name, **alt)


def _mm_nn(a, w, *, tm, tn, name, out_dtype=F32, res=None):
    s, k = a.shape
    tm = min(tm, s)
    n = w.shape[1]
    return _mm(a, w, dims=NN, grid=(s // tm, n // tn),
               a_spec=pl.BlockSpec((tm, k), lambda m, j: (m, 0)),
               b_spec=pl.BlockSpec((k, tn), lambda m, j: (0, j)),
               o_spec=pl.BlockSpec((tm, tn), lambda m, j: (m, j)),
               res=res, res_spec=pl.BlockSpec((tm, tn), lambda m, j: (m, j)),
               out_shape=(s, n), out_dtype=out_dtype, name=name)


def _mm_nt(a, w, *, tm, tn, name, out_dtype=F32):
    s, k = a.shape
    tm = min(tm, s)
    n = w.shape[0]
    return _mm(a, w, dims=NT, grid=(s // tm, n // tn),
               a_spec=pl.BlockSpec((tm, k), lambda m, j: (m, 0)),
               b_spec=pl.BlockSpec((tn, k), lambda m, j: (j, 0)),
               o_spec=pl.BlockSpec((tm, tn), lambda m, j: (m, j)),
               out_shape=(s, n), out_dtype=out_dtype, name=name)


def _mm_tn(a, g, *, tm, tn, name, out_dtype=BF16):
    s, m = a.shape
    n = g.shape[1]
    return _mm(a, g, dims=TN, grid=(m // tm, n // tn),
               a_spec=pl.BlockSpec((s, tm), lambda i, j: (0, i)),
               b_spec=pl.BlockSpec((s, tn), lambda i, j: (0, j)),
               o_spec=pl.BlockSpec((tm, tn), lambda i, j: (i, j)),
               out_shape=(m, n), out_dtype=out_dtype, name=name)


ROW_TILE = 256


def _dep_spec():
    return pl.BlockSpec((SUBLANE, LANE), lambda i: (0, 0))


def _rms_fwd(x, g_row, *, want_f32, name, dep=None):
    s, d = x.shape
    n_in = 2 if dep is None else 3

    def body(*refs):
        x_ref, g_ref = refs[:2]
        outs = refs[n_in:]
        xv = x_ref[...]
        r = lax.rsqrt(jnp.mean(xv * xv, axis=-1, keepdims=True) + RMS_EPS)
        h = (xv * r) * g_ref[...]
        outs[0][...] = h.astype(BF16)
        if want_f32:
            outs[1][...] = h

    row = pl.BlockSpec((ROW_TILE, d), lambda i: (i, 0))
    out_shape = [_sds((s, d), BF16)] + ([_sds((s, d), F32)] if want_f32 else [])
    out = pl.pallas_call(
        body, name=name, grid=(s // ROW_TILE,),
        in_specs=[row, pl.BlockSpec((1, d), lambda i: (0, 0))] + ([] if dep is None else [_dep_spec()]),
        out_specs=[row] * len(out_shape), out_shape=out_shape, compiler_params=_cparams("parallel"),
    )(x, g_row, *(() if dep is None else (dep,)))
    return out if want_f32 else (out[0], None)


def _rms_bwd(x, g_row, dh, dres, *, name, dep=None):
    s, d = x.shape

    def body(x_ref, g_ref, dh_ref, dres_ref, *rest):
        dx_ref, dg_ref = rest[-2:]
        xv = x_ref[...]
        r = lax.rsqrt(jnp.mean(xv * xv, axis=-1, keepdims=True) + RMS_EPS)
        xn = xv * r
        dhv = dh_ref[...]
        dxn = dhv * g_ref[...]
        dx_ref[...] = dres_ref[...] + r * (dxn - xn * jnp.mean(dxn * xn, axis=-1, keepdims=True))
        part = jnp.sum(dhv * xn, axis=0, keepdims=True)

        @pl.when(pl.program_id(0) == 0)
        def _():
            dg_ref[...] = part

        @pl.when(pl.program_id(0) > 0)
        def _():
            dg_ref[...] += part

    row = pl.BlockSpec((ROW_TILE, d), lambda i: (i, 0))
    vec = pl.BlockSpec((1, d), lambda i: (0, 0))
    return pl.pallas_call(
        body, name=name, grid=(s // ROW_TILE,), in_specs=[row, vec, row, row] + ([] if dep is None else [_dep_spec()]),
        out_specs=[row, vec], out_shape=[_sds((s, d), F32), _sds((1, d), F32)], compiler_params=_cparams("arbitrary"),
    )(x, g_row, dh, dres, *(() if dep is None else (dep,)))


def _shift_down(v, k):
    row = lax.broadcasted_iota(jnp.int32, v.shape, 0)
    return jnp.where(row >= k, pltpu.roll(v, k, 0), 0.0)


def _shift_up(v, k):
    n = v.shape[0]
    row = lax.broadcasted_iota(jnp.int32, v.shape, 0)
    return jnp.where(row < n - k, pltpu.roll(v, n - k, 0), 0.0)


def _sigmoid(z):
    return 1.0 / (1.0 + jnp.exp(-z))


FF_COL_TILE = 256


def _conv3(u, w, b):
    return b + w[0:1, :] * _shift_down(u, 2) + w[1:2, :] * _shift_down(u, 1) + w[2:3, :] * u


def _conv_gate_fwd(up, conv_w, conv_b, *, name):
    s = up.shape[0]
    f = up.shape[1] // 2
    nt = f // FF_COL_TILE

    def body(uv_ref, ug_ref, wv_ref, wg_ref, bv_ref, bg_ref, a_ref):
        vc = _conv3(uv_ref[...], wv_ref[...], bv_ref[...])
        gc = _conv3(ug_ref[...], wg_ref[...], bg_ref[...])
        a_ref[...] = ((gc * _sigmoid(gc)) * vc).astype(BF16)

    def col(rows, off):
        return pl.BlockSpec((rows, FF_COL_TILE), lambda n: (0, n + off))

    return pl.pallas_call(
        body, name=name, grid=(nt,),
        in_specs=[col(s, 0), col(s, nt), col(3, 0), col(3, nt), col(1, 0), col(1, nt)],
        out_specs=col(s, 0), out_shape=_sds((s, f), BF16), compiler_params=_cparams("parallel"),
    )(up, up, conv_w, conv_w, conv_b, conv_b)


def _conv_gate_bwd(up, da, conv_w, conv_b, *, name):
    s = up.shape[0]
    f = up.shape[1] // 2
    nt = f // FF_COL_TILE

    def conv_bwd(u, w, dc):
        d0 = _shift_up(dc, 2)
        d1 = _shift_up(dc, 1)
        dup = w[0:1, :] * d0 + w[1:2, :] * d1 + w[2:3, :] * dc
        dw = jnp.concatenate([jnp.sum(u * d0, axis=0, keepdims=True), jnp.sum(u * d1, axis=0, keepdims=True),
                              jnp.sum(u * dc, axis=0, keepdims=True)], axis=0)
        return dup, dw, jnp.sum(dc, axis=0, keepdims=True)

    def body(uv_ref, ug_ref, da_ref, wv_ref, wg_ref, bv_ref, bg_ref,
             duv_ref, dug_ref, dwv_ref, dwg_ref, dbv_ref, dbg_ref):
        uv = uv_ref[...]
        ug = ug_ref[...]
        vc = _conv3(uv, wv_ref[...], bv_ref[...])
        gc = _conv3(ug, wg_ref[...], bg_ref[...])
        sg = _sigmoid(gc)
        dav = da_ref[...]
        dvc = dav * (gc * sg)
        dgc = dav * vc * (sg * (1.0 + gc * (1.0 - sg)))
        dup, dw, db = conv_bwd(uv, wv_ref[...], dvc)
        duv_ref[...] = dup.astype(BF16)
        dwv_ref[...] = dw
        dbv_ref[...] = db
        dup, dw, db = conv_bwd(ug, wg_ref[...], dgc)
        dug_ref[...] = dup.astype(BF16)
        dwg_ref[...] = dw
        dbg_ref[...] = db

    def col(rows, off):
        return pl.BlockSpec((rows, FF_COL_TILE), lambda n: (0, n + off))

    dup_v, dup_g, dw_v, dw_g, db_v, db_g = pl.pallas_call(
        body, name=name, grid=(nt,),
        in_specs=[col(s, 0), col(s, nt), col(s, 0), col(3, 0), col(3, nt), col(1, 0), col(1, nt)],
        out_specs=[col(s, 0), col(s, 0), col(3, 0), col(3, 0), col(1, 0), col(1, 0)],
        out_shape=[_sds((s, f), BF16), _sds((s, f), BF16), _sds((3, f), F32), _sds((3, f), F32),
                   _sds((1, f), F32), _sds((1, f), F32)],
        compiler_params=_cparams("parallel"),
    )(up, up, da, conv_w, conv_w, conv_b, conv_b)
    return dup_v, dup_g, jnp.concatenate([dw_v, dw_g], axis=1), jnp.concatenate([db_v, db_g], axis=1)


def _pool_counts(shape, g):
    win = jnp.left_shift(jnp.int32(2), g)
    t = lax.broadcasted_iota(jnp.int32, shape, 0)
    return win, jnp.minimum(t + 1, win).astype(F32)


def _window_sum(v, g, shift):
    for k in range(POOL_GROUPS):
        v = jnp.where(g >= k, v + shift(v, 1 << k), v)
    return v


def _pool_fwd(hf, x, w, b, scale, *, name):
    s, d = hf.shape

    def body(h_ref, x_ref, w_ref, b_ref, sc_ref, o_ref):
        g = pl.program_id(0)
        h = h_ref[...]
        _, cnt = _pool_counts(h.shape, g)
        pooled = _window_sum(h, g, _shift_down) / cnt - h
        y = jnp.dot(pooled.astype(BF16), w_ref[...], preferred_element_type=F32) + b_ref[...]
        o_ref[...] = x_ref[...] + y * sc_ref[...]

    col = pl.BlockSpec((s, POOL_DIM), lambda g: (0, g))
    vec = pl.BlockSpec((1, POOL_DIM), lambda g: (0, g))
    return pl.pallas_call(
        body, name=name, grid=(POOL_GROUPS,),
        in_specs=[col, col, pl.BlockSpec((None, POOL_DIM, POOL_DIM), lambda g: (g, 0, 0)), vec, vec],
        out_specs=col, out_shape=_sds((s, d), F32), compiler_params=_cparams("parallel"),
    )(hf, x, w, b, scale)


def _pool_bwd(hf, dm, w, b, scale, *, name):
    s, d = hf.shape

    def body(h_ref, dm_ref, w_ref, b_ref, sc_ref, dh_ref, dw_ref, db_ref, dsc_ref):
        g = pl.program_id(0)
        h = h_ref[...]
        _, cnt = _pool_counts(h.shape, g)
        pooled = (_window_sum(h, g, _shift_down) / cnt - h).astype(BF16)
        wv = w_ref[...]
        y = jnp.dot(pooled, wv, preferred_element_type=F32) + b_ref[...]
        dmv = dm_ref[...]
        dsc_ref[...] = jnp.sum(dmv * y, axis=0, keepdims=True)
        dy = dmv * sc_ref[...]
        db_ref[...] = jnp.sum(dy, axis=0, keepdims=True)
        dyb = dy.astype(BF16)
        dw_ref[...] = lax.dot_general(pooled, dyb, (TN, ((), ())), preferred_element_type=F32).astype(BF16)
        dp = lax.dot_general(dyb, wv, (NT, ((), ())), preferred_element_type=F32)
        dh_ref[...] = _window_sum(dp / cnt, g, _shift_up) - dp

    col = pl.BlockSpec((s, POOL_DIM), lambda g: (0, g))
    vec = pl.BlockSpec((1, POOL_DIM), lambda g: (0, g))
    mat = pl.BlockSpec((None, POOL_DIM, POOL_DIM), lambda g: (g, 0, 0))
    return pl.pallas_call(
        body, name=name, grid=(POOL_GROUPS,), in_specs=[col, col, mat, vec, vec], out_specs=[col, mat, vec, vec],
        out_shape=[_sds((s, d), F32), _sds((POOL_GROUPS, POOL_DIM, POOL_DIM), BF16), _sds((1, d), F32),
                   _sds((1, d), F32)],
        compiler_params=_cparams("parallel"),
    )(hf, dm, w, b, scale)


ATT_TQ = 256
ATT_TK = 256


def _qk_norm_fwd(qkv, gains, *, name):
    s = qkv.shape[0]

    def body(x_ref, g_ref, o_ref):
        xv = x_ref[...]
        r = lax.rsqrt(jnp.mean(xv * xv, axis=-1, keepdims=True) + RMS_EPS)
        o_ref[...] = ((xv * r) * g_ref[...]).astype(BF16)

    blk = pl.BlockSpec((s, HEAD_DIM), lambda hd: (0, hd))
    return pl.pallas_call(
        body, name=name, grid=(2 * HEADS,),
        in_specs=[blk, pl.BlockSpec((None, 1, HEAD_DIM), lambda hd: (hd // HEADS, 0, 0))],
        out_specs=blk, out_shape=_sds((s, 2 * HEADS * HEAD_DIM), BF16), compiler_params=_cparams("parallel"),
    )(qkv, gains)


def _qk_norm_bwd(qkv, gains, dn, *, which, name):
    s = qkv.shape[0]

    def body(x_ref, g_ref, dn_ref, dx_ref, dg_ref):
        xv = x_ref[...]
        r = lax.rsqrt(jnp.mean(xv * xv, axis=-1, keepdims=True) + RMS_EPS)
        xn = xv * r
        dnv = dn_ref[...]
        dxn = dnv * g_ref[...]
        dx_ref[...] = (r * (dxn - xn * jnp.mean(dxn * xn, axis=-1, keepdims=True))).astype(BF16)
        part = jnp.sum(dnv * xn, axis=0, keepdims=True)

        @pl.when(pl.program_id(0) == 0)
        def _():
            dg_ref[...] = part

        @pl.when(pl.program_id(0) > 0)
        def _():
            dg_ref[...] += part

    blk = pl.BlockSpec((s, HEAD_DIM), lambda hd: (0, hd))
    return pl.pallas_call(
        body, name=name, grid=(HEADS,),
        in_specs=[pl.BlockSpec((s, HEAD_DIM), lambda hd: (0, hd + which * HEADS)),
                  pl.BlockSpec((None, 1, HEAD_DIM), lambda hd: (which, 0, 0)), blk],
        out_specs=[blk, pl.BlockSpec((1, HEAD_DIM), lambda hd: (0, 0))],
        out_shape=[_sds((s, HEADS * HEAD_DIM), BF16), _sds((1, HEAD_DIM), F32)],
        compiler_params=_cparams("arbitrary"),
    )(qkv, gains, dn)


def _split_dot(v, tri):
    hi = v.astype(BF16)
    lo = (v - hi.astype(F32)).astype(BF16)
    return (jnp.dot(hi, tri, preferred_element_type=F32) + jnp.dot(lo, tri, preferred_element_type=F32))


def _causal_mask(qi, j):
    tpos = qi * ATT_TQ + lax.broadcasted_iota(jnp.int32, (ATT_TQ, ATT_TK), 0)
    spos = j * ATT_TK + lax.broadcasted_iota(jnp.int32, (ATT_TQ, ATT_TK), 1)
    return spos < tpos


def _att_tile(q, kj, qi, j):
    z = lax.dot_general(q, kj, (NT, ((), ())), preferred_element_type=F32) * (1.0 / math.sqrt(HEAD_DIM))
    mask = _causal_mask(qi, j)
    lb = jnp.minimum(z, 0.0) - jnp.log1p(jnp.exp(-jnp.abs(z)))
    l1m = jnp.where(mask, lb - z, 0.0)
    return lb, l1m, mask


def _tri(rel):
    r = lax.broadcasted_iota(jnp.int32, (ATT_TK, ATT_TK), 0)
    c = lax.broadcasted_iota(jnp.int32, (ATT_TK, ATT_TK), 1)
    return jnp.where(rel(r, c), 1.0, 0.0).astype(BF16)


def _sb_fwd(qkn, vb, *, name):
    s = vb.shape[0]

    def body(q_ref, k_ref, v_ref, o_ref):
        qi = pl.program_id(1)
        q = q_ref[...]
        after = _tri(lambda r, c: r > c)

        def step(t, carry):
            acc, run = carry
            j = qi - t
            rows = pl.ds(pl.multiple_of(j * ATT_TK, ATT_TK), ATT_TK)
            lb, l1m, mask = _att_tile(q, k_ref[rows, :], qi, j)
            remain = _split_dot(l1m, after) + run
            attn = jnp.where(mask, jnp.exp(lb + remain), 0.0)
            acc = acc + jnp.dot(attn.astype(BF16), v_ref[rows, :], preferred_element_type=F32)
            return acc, run + jnp.sum(l1m, axis=1, keepdims=True)

        acc, _ = lax.fori_loop(0, qi + 1, step, (jnp.zeros((ATT_TQ, HEAD_DIM), F32), jnp.zeros((ATT_TQ, 1), F32)))
        o_ref[...] = acc.astype(BF16)

    return pl.pallas_call(
        body, name=name, grid=(HEADS, s // ATT_TQ),
        in_specs=[pl.BlockSpec((ATT_TQ, HEAD_DIM), lambda hd, i: (i, hd)),
                  pl.BlockSpec((s, HEAD_DIM), lambda hd, i: (0, hd + HEADS)),
                  pl.BlockSpec((s, HEAD_DIM), lambda hd, i: (0, hd))],
        out_specs=pl.BlockSpec((ATT_TQ, HEAD_DIM), lambda hd, i: (i, hd)),
        out_shape=_sds((s, HEADS * HEAD_DIM), BF16), compiler_params=_cparams("parallel", "parallel"),
    )(qkn, qkn, vb)


def _sb_bwd(qkn, vb, dob, *, name):
    s = vb.shape[0]
    nkb = s // ATT_TK

    def body(q_ref, k_ref, v_ref, do_ref, dq_ref, dk_ref, dv_ref, a_buf, sig_buf):
        qi = pl.program_id(1)
        q = q_ref[...]
        do = do_ref[...]
        after = _tri(lambda r, c: r > c)
        before = _tri(lambda r, c: r < c)

        @pl.when(qi == 0)
        def _():
            dk_ref[...] = jnp.zeros_like(dk_ref)
            dv_ref[...] = jnp.zeros_like(dv_ref)

        def down(t, run):
            j = qi - t
            rows = pl.ds(pl.multiple_of(j * ATT_TK, ATT_TK), ATT_TK)
            lb, l1m, mask = _att_tile(q, k_ref[rows, :], qi, j)
            remain = _split_dot(l1m, after) + run
            a_buf[j] = jnp.where(mask, jnp.exp(lb + remain), 0.0)
            sig_buf[j] = jnp.exp(lb)
            return run + jnp.sum(l1m, axis=1, keepdims=True)

        lax.fori_loop(0, qi + 1, down, jnp.zeros((ATT_TQ, 1), F32))

        def up(j, carry):
            dq, run = carry
            rows = pl.ds(pl.multiple_of(j * ATT_TK, ATT_TK), ATT_TK)
            a = a_buf[j]
            sig = sig_buf[j]
            mask = _causal_mask(qi, j)
            da = lax.dot_general(do, v_ref[rows, :], (NT, ((), ())), preferred_element_type=F32)
            p = a * da
            c = _split_dot(p, before) + run
            dz = jnp.where(mask, p * (1.0 - sig) - c * sig, 0.0) * (1.0 / math.sqrt(HEAD_DIM))
            dzb = dz.astype(BF16)
            dq = dq + jnp.dot(dzb, k_ref[rows, :], preferred_element_type=F32)
            dk_ref[rows, :] += lax.dot_general(dzb, q, (TN, ((), ())), preferred_element_type=F32)
            dv_ref[rows, :] += lax.dot_general(a.astype(BF16), do, (TN, ((), ())), preferred_element_type=F32)
            return dq, run + jnp.sum(p, axis=1, keepdims=True)

        dq, _ = lax.fori_loop(0, qi + 1, up, (jnp.zeros((ATT_TQ, HEAD_DIM), F32), jnp.zeros((ATT_TQ, 1), F32)))
        dq_ref[...] = dq

    qblk = pl.BlockSpec((ATT_TQ, HEAD_DIM), lambda hd, i: (i, hd))
    full = pl.BlockSpec((s, HEAD_DIM), lambda hd, i: (0, hd))
    return pl.pallas_call(
        body, name=name, grid=(HEADS, s // ATT_TQ),
        in_specs=[qblk, pl.BlockSpec((s, HEAD_DIM), lambda hd, i: (0, hd + HEADS)), full, qblk],
        out_specs=[qblk, full, full],
        out_shape=[_sds((s, HEADS * HEAD_DIM), F32)] * 3,
        scratch_shapes=[pltpu.VMEM((nkb, ATT_TQ, ATT_TK), F32), pltpu.VMEM((nkb, ATT_TQ, ATT_TK), F32)],
        compiler_params=_cparams("parallel", "arbitrary"),
    )(qkn, qkn, vb, dob)


def _ssm_discretize(lam_re, lam_im, log_step, bt_re, bt_im):
    step = jnp.exp(log_step)
    mag = jnp.exp(lam_re * step)
    lb_re = mag * jnp.cos(lam_im * step)
    lb_im = mag * jnp.sin(lam_im * step)
    den = lam_re * lam_re + lam_im * lam_im
    f_re = ((lb_re - 1.0) * lam_re + lb_im * lam_im) / den
    f_im = (lb_im * lam_re - (lb_re - 1.0) * lam_im) / den
    return lb_re, lb_im, f_re * bt_re - f_im * bt_im, f_re * bt_im + f_im * bt_re


_SSM_LAM = (SSM_GROUPS, 1, SSM_STATE)
_SSM_STEP = (SSM_GROUPS, 1, 1)
_SSM_BT = (SSM_GROUPS, SSM_CH, SSM_STATE)


def _ssm_prep_fwd(lam_re, lam_im, log_step, bt_re, bt_im, *, name):
    def body(lr, li, ls, br, bi, o_ar, o_ai, o_br, o_bi):
        o_ar[...], o_ai[...], o_br[...], o_bi[...] = _ssm_discretize(lr[...], li[...], ls[...], br[...], bi[...])

    return pl.pallas_call(
        body, name=name, out_shape=[_sds(_SSM_LAM, F32), _sds(_SSM_LAM, F32), _sds(_SSM_BT, F32), _sds(_SSM_BT, F32)],
    )(lam_re, lam_im, log_step, bt_re, bt_im)


def _ssm_prep_bwd(lam_re, lam_im, log_step, bt_re, bt_im, d_ar, d_ai, d_br, d_bi, *, name):
    def body(lr, li, ls, br, bi, g_ar, g_ai, g_br, g_bi, o_lr, o_li, o_ls, o_br, o_bi):
        _, vjp = jax.vjp(_ssm_discretize, lr[...], li[...], ls[...], br[...], bi[...])
        o_lr[...], o_li[...], o_ls[...], o_br[...], o_bi[...] = vjp((g_ar[...], g_ai[...], g_br[...], g_bi[...]))

    return pl.pallas_call(
        body, name=name,
        out_shape=[_sds(_SSM_LAM, F32), _sds(_SSM_LAM, F32), _sds(_SSM_STEP, F32), _sds(_SSM_BT, F32), _sds(_SSM_BT, F32)],
    )(lam_re, lam_im, log_step, bt_re, bt_im, d_ar, d_ai, d_br, d_bi)


def _bd_masks():
    rowg = lax.broadcasted_iota(jnp.int32, (LANE, LANE), 0) // SSM_CH
    low = lax.broadcasted_iota(jnp.int32, (LANE, LANE), 1) < SSM_STATE
    return rowg, low


def _bd_expand(w):
    rowg, low = _bd_masks()
    high = jnp.logical_not(low)
    wr = pltpu.roll(w, SSM_STATE, 1)
    re = [jnp.where((rowg == 2 * k) & low, w, 0.0) + jnp.where((rowg == 2 * k + 1) & high, wr, 0.0) for k in range(4)]
    im = [jnp.where((rowg == 2 * k) & low, wr, 0.0) + jnp.where((rowg == 2 * k + 1) & high, w, 0.0) for k in range(4)]
    return jnp.concatenate(re + im, axis=1)


def _bd_extract(dbd):
    rowg, low = _bd_masks()
    high = jnp.logical_not(low)
    acc = jnp.zeros((LANE, LANE), F32)
    for k in range(4):
        c = dbd[:, LANE * k:LANE * (k + 1)]
        acc = acc + jnp.where((rowg == 2 * k) & low, c, 0.0) + jnp.where((rowg == 2 * k + 1) & low, pltpu.roll(c, SSM_STATE, 1), 0.0)
        c = dbd[:, LANE * (4 + k):LANE * (5 + k)]
        acc = acc + jnp.where((rowg == 2 * k) & high, pltpu.roll(c, SSM_STATE, 1), 0.0) + jnp.where((rowg == 2 * k + 1) & high, c, 0.0)
    return acc


def _cmul(ar, ai, br, bi):
    return ar * br - ai * bi, ar * bi + ai * br


def _scan_rows(xr, xi, ar, ai, *, reverse):
    n = xr.shape[0] // SUBLANE
    lanes = xr.shape[1]
    row = lax.broadcasted_iota(jnp.int32, (SUBLANE, lanes), 0)
    powers = [(ar, ai)]
    for _ in range(SUBLANE - 1):
        powers.append(_cmul(*powers[-1], ar, ai))
    pr = jnp.zeros((SUBLANE, lanes), F32)
    pi = jnp.zeros((SUBLANE, lanes), F32)
    for r in range(SUBLANE):
        e = (SUBLANE - 1 - r) if reverse else r
        pr = jnp.where(row == r, powers[e][0], pr)
        pi = jnp.where(row == r, powers[e][1], pi)

    def shift(v, d):
        if reverse:
            return jnp.where(row < SUBLANE - d, pltpu.roll(v, SUBLANE - d, 0), 0.0)
        return jnp.where(row >= d, pltpu.roll(v, d, 0), 0.0)

    def body(i, carry):
        cr, ci = carry
        g = (n - 1 - i) if reverse else i
        rows = pl.ds(pl.multiple_of(g * SUBLANE, SUBLANE), SUBLANE)
        br = xr[rows, :]
        bi = xi[rows, :]
        for d in (1, 2, 4):
            qr, qi = powers[d - 1]
            sr = shift(br, d)
            si = shift(bi, d)
            br, bi = br + qr * sr - qi * si, bi + qr * si + qi * sr
        br, bi = br + pr * cr - pi * ci, bi + pr * ci + pi * cr
        xr[rows, :] = br
        xi[rows, :] = bi
        edge = 0 if reverse else SUBLANE - 1
        return br[edge:edge + 1, :], bi[edge:edge + 1, :]

    zero = jnp.zeros((1, lanes), F32)
    lax.fori_loop(0, n, body, (zero, zero), unroll=2)


_GELU_C = math.sqrt(2.0 / math.pi)
_GELU_A = 0.044715


def _gelu(v):
    return 0.5 * v * (1.0 + jnp.tanh(_GELU_C * (v + _GELU_A * v * v * v)))


def _gelu_grad(v):
    t = jnp.tanh(_GELU_C * (v + _GELU_A * v * v * v))
    return 0.5 * (1.0 + t) + 0.5 * v * (1.0 - t * t) * (_GELU_C * (1.0 + 3.0 * _GELU_A * v * v))


def _ssm_states(u_b16, eb, ar, ai, xr, xi):
    nl = SSM_BLOCK_LANES
    xr[...] = jnp.dot(u_b16, eb[:, :nl], preferred_element_type=F32)
    xi[...] = jnp.dot(u_b16, eb[:, nl:], preferred_element_type=F32)
    _scan_rows(xr, xi, ar, ai, reverse=False)


def _ssm_specs(s):
    col = pl.BlockSpec((s, LANE), lambda b: (0, b))
    wsm = pl.BlockSpec((LANE, LANE), lambda b: (b, 0))
    lam = pl.BlockSpec((1, SSM_BLOCK_LANES), lambda b: (0, b))
    vec = pl.BlockSpec((1, LANE), lambda b: (0, b))
    return col, wsm, lam, vec


def _ssm_core_fwd(u, wb, wc, a_re, a_im, d_row, *, name):
    s, d = u.shape
    nl = SSM_BLOCK_LANES

    def body(u_ref, wb_ref, wc_ref, ar_ref, ai_ref, d_ref, y_ref, yg_ref, xr, xi):
        uv = u_ref[...]
        eb = _bd_expand(wb_ref[...]).astype(BF16)
        ec = _bd_expand(wc_ref[...]).astype(BF16)
        _ssm_states(uv.astype(BF16), eb, ar_ref[...], ai_ref[...], xr, xi)
        y = (lax.dot_general(xr[...].astype(BF16), ec[:, :nl], (NT, ((), ())), preferred_element_type=F32)
             + lax.dot_general(xi[...].astype(BF16), ec[:, nl:], (NT, ((), ())), preferred_element_type=F32)
             + d_ref[...] * uv)
        y_ref[...] = y
        yg_ref[...] = _gelu(y).astype(BF16)

    col, wsm, lam, vec = _ssm_specs(s)
    return pl.pallas_call(
        body, name=name, grid=(d // LANE,), in_specs=[col, wsm, wsm, lam, lam, vec], out_specs=[col, col],
        out_shape=[_sds((s, d), F32), _sds((s, d), BF16)],
        scratch_shapes=[pltpu.VMEM((s, nl), F32), pltpu.VMEM((s, nl), F32)],
        compiler_params=_cparams("parallel"),
    )(u, wb, wc, a_re, a_im, d_row)


def _ssm_core_bwd(u, ylin, dyg, wb, wc, a_re, a_im, d_row, *, name):
    s, d = u.shape
    nl = SSM_BLOCK_LANES
    n8 = s // SUBLANE

    def body(u_ref, y_ref, dyg_ref, wb_ref, wc_ref, ar_ref, ai_ref, d_ref,
             du_ref, dwb_ref, dwc_ref, dar_ref, dai_ref, dd_ref, xr, xi, gr, gi):
        uv = u_ref[...]
        ub = uv.astype(BF16)
        ar = ar_ref[...]
        ai = ai_ref[...]
        dy = dyg_ref[...] * _gelu_grad(y_ref[...])
        dd_ref[...] = jnp.sum(dy * uv, axis=0, keepdims=True)
        dyb = dy.astype(BF16)
        eb = _bd_expand(wb_ref[...]).astype(BF16)
        ec = _bd_expand(wc_ref[...]).astype(BF16)
        _ssm_states(ub, eb, ar, ai, xr, xi)
        dec = jnp.concatenate(
            [lax.dot_general(dyb, xr[...].astype(BF16), (TN, ((), ())), preferred_element_type=F32),
             lax.dot_general(dyb, xi[...].astype(BF16), (TN, ((), ())), preferred_element_type=F32)], axis=1)
        dwc_ref[...] = _bd_extract(dec)
        gr[...] = jnp.dot(dyb, ec[:, :nl], preferred_element_type=F32)
        gi[...] = jnp.dot(dyb, ec[:, nl:], preferred_element_type=F32)
        _scan_rows(gr, gi, ar, -ai, reverse=True)

        row = lax.broadcasted_iota(jnp.int32, (SUBLANE, nl), 0)

        def lam_grad(i, acc):
            acc_r, acc_i = acc
            rows = pl.ds(pl.multiple_of(i * SUBLANE, SUBLANE), SUBLANE)
            prev = pl.ds(pl.multiple_of(jnp.maximum(i - 1, 0) * SUBLANE, SUBLANE), SUBLANE)
            keep = jnp.where(i > 0, 1.0, 0.0)
            xpr = jnp.where(row == 0, pltpu.roll(xr[prev, :], 1, 0) * keep, pltpu.roll(xr[rows, :], 1, 0))
            xpi = jnp.where(row == 0, pltpu.roll(xi[prev, :], 1, 0) * keep, pltpu.roll(xi[rows, :], 1, 0))
            g_r = gr[rows, :]
            g_i = gi[rows, :]
            return acc_r + g_r * xpr + g_i * xpi, acc_i + g_i * xpr - g_r * xpi

        zero = jnp.zeros((SUBLANE, nl), F32)
        acc_r, acc_i = lax.fori_loop(0, n8, lam_grad, (zero, zero), unroll=2)
        dar_ref[...] = jnp.sum(acc_r, axis=0, keepdims=True)
        dai_ref[...] = jnp.sum(acc_i, axis=0, keepdims=True)

        grb = gr[...].astype(BF16)
        gib = gi[...].astype(BF16)
        deb = jnp.concatenate([lax.dot_general(ub, grb, (TN, ((), ())), preferred_element_type=F32),
                               lax.dot_general(ub, gib, (TN, ((), ())), preferred_element_type=F32)], axis=1)
        dwb_ref[...] = _bd_extract(deb)
        du_ref[...] = (lax.dot_general(grb, eb[:, :nl], (NT, ((), ())), preferred_element_type=F32)
                       + lax.dot_general(gib, eb[:, nl:], (NT, ((), ())), preferred_element_type=F32)
                       + d_ref[...] * dy)

    col, wsm, lam, vec = _ssm_specs(s)
    return pl.pallas_call(
        body, name=name, grid=(d // LANE,), in_specs=[col, col, col, wsm, wsm, lam, lam, vec],
        out_specs=[col, wsm, wsm, lam, lam, vec],
        out_shape=[_sds((s, d), F32), _sds((d, LANE), F32), _sds((d, LANE), F32),
                   _sds((1, SSM_GROUPS * SSM_STATE), F32), _sds((1, SSM_GROUPS * SSM_STATE), F32), _sds((1, d), F32)],
        scratch_shapes=[pltpu.VMEM((s, nl), F32)] * 4,
        compiler_params=_cparams("parallel"),
    )(u, ylin, dyg, wb, wc, a_re, a_im, d_row)


GLU_PIECE = 512


def _glu_fwd(yg, wp, b_row, x, *, tm, name):
    s, d = yg.shape
    tm = min(tm, s)
    half = N_DEV // 2

    def body(y_ref, wv_ref, wg_ref, bv_ref, bg_ref, x_ref, o_ref, val_ref, gate_ref):
        yv = y_ref[...]
        val = jnp.dot(yv, wv_ref[...], preferred_element_type=F32) + bv_ref[...]
        gate = jnp.dot(yv, wg_ref[...], preferred_element_type=F32) + bg_ref[...]
        val_ref[...] = val
        gate_ref[...] = gate
        o_ref[...] = x_ref[...] + val * _sigmoid(gate)

    blk = pl.BlockSpec((tm, GLU_PIECE), lambda m, n: (m, n))
    return pl.pallas_call(
        body, name=name, grid=(s // tm, half),
        in_specs=[pl.BlockSpec((tm, d), lambda m, n: (m, 0)),
                  pl.BlockSpec((None, d, GLU_PIECE), lambda m, n: (n, 0, 0)),
                  pl.BlockSpec((None, d, GLU_PIECE), lambda m, n: (n + half, 0, 0)),
                  pl.BlockSpec((1, GLU_PIECE), lambda m, n: (0, n)),
                  pl.BlockSpec((1, GLU_PIECE), lambda m, n: (0, n + half)), blk],
        out_specs=[blk, blk, blk], out_shape=[_sds((s, d), F32)] * 3,
        compiler_params=_cparams("parallel", "parallel"),
    )(yg, wp, wp, b_row, b_row, x)


def _glu_bwd(dout, val, gate, *, name):
    s, d = dout.shape

    def body(do_ref, val_ref, gate_ref, dgv_ref, db_ref):
        sg = _sigmoid(gate_ref[...])
        dov = do_ref[...]
        dgv = jnp.concatenate([dov * sg, dov * val_ref[...] * (sg * (1.0 - sg))], axis=1)
        dgv_ref[...] = dgv.astype(BF16)
        part = jnp.sum(dgv, axis=0, keepdims=True)

        @pl.when(pl.program_id(0) == 0)
        def _():
            db_ref[...] = part

        @pl.when(pl.program_id(0) > 0)
        def _():
            db_ref[...] += part

    row = pl.BlockSpec((ROW_TILE, d), lambda i: (i, 0))
    return pl.pallas_call(
        body, name=name, grid=(s // ROW_TILE,), in_specs=[row, row, row],
        out_specs=[pl.BlockSpec((ROW_TILE, 2 * d), lambda i: (i, 0)), pl.BlockSpec((1, 2 * d), lambda i: (0, 0))],
        out_shape=[_sds((s, 2 * d), BF16), _sds((1, 2 * d), F32)], compiler_params=_cparams("arbitrary"),
    )(dout, val, gate)


def _loss_head(y, target, *, name):
    s, d = y.shape

    def body(y_ref, t_ref, dy_ref, l_ref):
        e = y_ref[...] - t_ref[...]
        dy_ref[...] = e * (1.0 / d)
        part = jnp.zeros((SUBLANE, LANE), F32) + jnp.sum(e * e) * (0.5 / d)

        @pl.when(pl.program_id(0) == 0)
        def _():
            l_ref[...] = part

        @pl.when(pl.program_id(0) > 0)
        def _():
            l_ref[...] += part

    row = pl.BlockSpec((ROW_TILE, d), lambda i: (i, 0))
    return pl.pallas_call(
        body, name=name, grid=(s // ROW_TILE,), in_specs=[row, row],
        out_specs=[row, pl.BlockSpec((SUBLANE, LANE), lambda i: (0, 0))],
        out_shape=[_sds((s, d), F32), _sds((SUBLANE, LANE), F32)], compiler_params=_cparams("arbitrary"),
    )(y, target)


def _adamw_math(w, g, m, v):
    m = ADAM_B1 * m + (1.0 - ADAM_B1) * g
    v = ADAM_B2 * v + (1.0 - ADAM_B2) * (g * g)
    m_hat = m / (1.0 - ADAM_B1 ** ADAM_STEP)
    v_hat = v / (1.0 - ADAM_B2 ** ADAM_STEP)
    return -ADAM_LR * (m_hat / (jnp.sqrt(v_hat) + ADAM_EPS) + ADAM_WD * w), m, v


ADAM_ROWS = 64
PACK_ROWS = 64


def _sum_pieces(p_ref):
    g = p_ref[0].astype(F32)
    for k in range(1, N_DEV):
        g = g + p_ref[k].astype(F32)
    return g


def _adamw_pieces(w, pieces, m, v, *, name):
    r, c = w.shape

    def body(w_ref, p_ref, m_ref, v_ref, g_ref, d_ref, nm_ref, nv_ref):
        g = _sum_pieces(p_ref)
        g_ref[...] = g
        d_ref[...], nm_ref[...], nv_ref[...] = _adamw_math(w_ref[...], g, m_ref[...], v_ref[...])

    blk = pl.BlockSpec((ADAM_ROWS, c), lambda i: (i, 0))
    return pl.pallas_call(
        body, name=name, grid=(r // ADAM_ROWS,),
        in_specs=[blk, pl.BlockSpec((N_DEV, ADAM_ROWS, c), lambda i: (0, i, 0)), blk, blk],
        out_specs=[blk] * 4, out_shape=[_sds((r, c), F32)] * 4, compiler_params=_cparams("parallel"),
    )(w, pieces, m, v)


def _sum_parts(parts, *, name):
    _, r, c = parts.shape

    def body(p_ref, o_ref):
        o_ref[...] = _sum_pieces(p_ref)

    return pl.pallas_call(
        body, name=name, grid=(r // PACK_ROWS,),
        in_specs=[pl.BlockSpec((N_DEV, PACK_ROWS, c), lambda i: (0, i, 0))],
        out_specs=pl.BlockSpec((PACK_ROWS, c), lambda i: (i, 0)), out_shape=_sds((r, c), F32),
        compiler_params=_cparams("parallel"),
    )(parts)


def _adamw_flat(w, g, m, v, *, name):
    r, c = w.shape

    def body(w_ref, g_ref, m_ref, v_ref, d_ref, nm_ref, nv_ref):
        d_ref[...], nm_ref[...], nv_ref[...] = _adamw_math(w_ref[...], g_ref[...], m_ref[...], v_ref[...])

    blk = pl.BlockSpec((PACK_ROWS, c), lambda i: (i, 0))
    return pl.pallas_call(
        body, name=name, grid=(r // PACK_ROWS,), in_specs=[blk] * 4, out_specs=[blk] * 3,
        out_shape=[_sds((r, c), F32)] * 3, compiler_params=_cparams("parallel"),
    )(w, g, m, v)


_ANY = pl.BlockSpec(memory_space=pl.ANY)


def _place():
    return lax.axis_index("x"), lax.axis_index("y"), lax.axis_index("c")


def _slot(px, py, pc):
    return 4 * px + 2 * py + pc


def _all_gather(xs, *, name):
    n = len(xs)

    def body(*refs):
        ins, outs = refs[:n], refs[n:2 * n]
        send_sems, recv_sems, local_sems = refs[2 * n:]
        x, y, c = _place()
        me, sibling = (x, y, c), (x, y, 1 - c)
        chips = [(1 - x, y), (x, 1 - y), (1 - x, 1 - y)]

        def copy(i, k, block, to, src=None):
            rows = outs[i].at[_slot(*block)]
            return pltpu.make_async_remote_copy(
                src_ref=rows if src is None else src, dst_ref=rows, send_sem=send_sems.at[i, k],
                recv_sem=recv_sems.at[i, k], device_id=to, device_id_type=MESH)

        mine = [pltpu.make_async_copy(ins[i], outs[i].at[_slot(*me)], local_sems.at[i]) for i in range(n)]
        for cp in mine:
            cp.start()
        first = []
        for i in range(n):
            first.append(copy(i, 0, me, sibling, src=ins[i]))
            first += [copy(i, 1 + j, me, (*chip, c), src=ins[i]) for j, chip in enumerate(chips)]
        for cp in first:
            cp.start()
        passed = []
        for j, chip in enumerate(chips):
            for i in range(n):
                copy(i, 1 + j, (*chip, c), me).wait_recv()
                cp = copy(i, 4 + j, (*chip, c), sibling)
                cp.start()
                passed.append(cp)
        for i in range(n):
            copy(i, 0, sibling, me).wait_recv()
            for j, chip in enumerate(chips):
                copy(i, 4 + j, (*chip, 1 - c), me).wait_recv()
        for cp in first + passed:
            cp.wait_send()
        for cp in mine:
            cp.wait()

    return pl.pallas_call(
        body, name=name, in_specs=[_ANY] * n, out_specs=[_ANY] * n,
        out_shape=[_sds((N_DEV,) + a.shape, a.dtype) for a in xs],
        scratch_shapes=[pltpu.SemaphoreType.DMA((n, 7)), pltpu.SemaphoreType.DMA((n, 7)), pltpu.SemaphoreType.DMA((n,))],
    )(*xs)


def _exchange_pieces(gs, *, name):
    n = len(gs)
    flips = [(dx, dy, dc) for dx in (0, 1) for dy in (0, 1) for dc in (0, 1)][1:]

    def body(*refs):
        ins, outs = refs[:n], refs[n:2 * n]
        send_sems, recv_sems, local_sems = refs[2 * n:]
        x, y, c = _place()
        me = _slot(x, y, c)
        peers = [((1 - x) if dx else x, (1 - y) if dy else y, (1 - c) if dc else c) for dx, dy, dc in flips]

        def copy(i, k):
            return pltpu.make_async_remote_copy(
                src_ref=ins[i].at[_slot(*peers[k])], dst_ref=outs[i].at[me], send_sem=send_sems.at[i, k],
                recv_sem=recv_sems.at[i, k], device_id=peers[k], device_id_type=MESH)

        def landing(i, k):
            rows = outs[i].at[_slot(*peers[k])]
            return pltpu.make_async_remote_copy(
                src_ref=rows, dst_ref=rows, send_sem=send_sems.at[i, k], recv_sem=recv_sems.at[i, k],
                device_id=peers[k], device_id_type=MESH)

        mine = [pltpu.make_async_copy(ins[i].at[me], outs[i].at[me], local_sems.at[i]) for i in range(n)]
        for cp in mine:
            cp.start()
        sends = [copy(i, k) for i in range(n) for k in range(len(flips))]
        for cp in sends:
            cp.start()
        for i in range(n):
            for k in range(len(flips)):
                landing(i, k).wait_recv()
        for cp in sends:
            cp.wait_send()
        for cp in mine:
            cp.wait()

    return pl.pallas_call(
        body, name=name, in_specs=[_ANY] * n, out_specs=[_ANY] * n,
        out_shape=[_sds(a.shape, a.dtype) for a in gs],
        scratch_shapes=[pltpu.SemaphoreType.DMA((n, 7)), pltpu.SemaphoreType.DMA((n, 7)), pltpu.SemaphoreType.DMA((n,))],
    )(*gs)


_HBM = pl.BlockSpec(memory_space=pltpu.HBM)
_SEM = pl.BlockSpec(memory_space=pltpu.SEMAPHORE)
_EFFECT = pltpu.SideEffectType.DATAFLOW_SIDE_EFFECTING
_CHIP_FLIPS = ((1, 0), (0, 1), (1, 1))
_TOKEN = (SUBLANE, LANE)


def _flip(v, f):
    return (1 - v) if f else v


def _in_hbm(a):
    return pltpu.with_memory_space_constraint(a, pltpu.HBM)


def _ag_peers(x, y, c):
    return [(x, y, 1 - c)] + [(_flip(x, dx), _flip(y, dy), c) for dx, dy in _CHIP_FLIPS]


def _ag_copies(x_refs, land_refs, send_sems, recv_sems, *, landing):
    x, y, c = _place()
    peers = _ag_peers(x, y, c)
    cps = []
    for i in range(len(x_refs)):
        for k, peer in enumerate(peers):
            origin = _slot(*peer) if landing else _slot(x, y, c)
            cps.append(pltpu.make_async_remote_copy(
                src_ref=x_refs[i], dst_ref=land_refs[i].at[origin], send_sem=send_sems.at[4 * i + k],
                recv_sem=recv_sems.at[4 * i + k], device_id=peer, device_id_type=MESH))
    return cps


OWN_BLOCK_BYTES = 4 * 1024 * 1024


def _place_own(x, me, *, name):
    r, c = x.shape
    tr = r if r * c * x.dtype.itemsize <= OWN_BLOCK_BYTES else r // 4

    def body(me_ref, x_ref, o_ref):
        o_ref[...] = x_ref[...]

    return pl.pallas_call(
        body, name=name,
        grid_spec=pltpu.PrefetchScalarGridSpec(
            num_scalar_prefetch=1, grid=(r // tr,),
            in_specs=[pl.BlockSpec((tr, c), lambda i, me_ref: (i, 0))],
            out_specs=pl.BlockSpec((None, tr, c), lambda i, me_ref: (me_ref[0], i, 0))),
        out_shape=_sds((N_DEV, r, c), x.dtype), compiler_params=_cparams("parallel"),
    )(me, x)


def _ag_start(xs, *, name, dep=None):
    n = len(xs)
    me = _slot(*_place()).astype(jnp.int32).reshape(1)
    lands = [_place_own(a, me, name=f"{name}_own_{i}") for i, a in enumerate(xs)]
    n_in = 2 * n + (0 if dep is None else 1)

    def body(*refs):
        x_refs, land_refs = refs[:n], refs[n:2 * n]
        send_sems, recv_sems = refs[n_in], refs[n_in + 1]
        token = refs[-1]
        for cp in _ag_copies(x_refs, land_refs, send_sems, recv_sems, landing=False):
            cp.start()
        token[...] = jnp.zeros_like(token)

    out = pl.pallas_call(
        body, name=name,
        out_shape=(pltpu.SemaphoreType.DMA((4 * n,)), pltpu.SemaphoreType.DMA((4 * n,)),
                   *[pltpu.HBM(a.shape, a.dtype) for a in xs], *[pltpu.HBM(a.shape, a.dtype) for a in lands],
                   _sds(_TOKEN, F32)),
        in_specs=[_HBM] * (2 * n) + ([] if dep is None else [_ANY]),
        out_specs=(_SEM, _SEM, *[_HBM] * (2 * n), pl.BlockSpec(memory_space=pltpu.VMEM)),
        input_output_aliases={i: 2 + i for i in range(2 * n)},
        compiler_params=pltpu.CompilerParams(has_side_effects=_EFFECT),
    )(*[_in_hbm(a) for a in xs], *[_in_hbm(a) for a in lands], *(() if dep is None else (dep,)))
    return (out[0], out[1], list(out[2:2 + n]), list(out[2 + n:2 + 2 * n])), out[-1]


def _ag_wait(started, after, *, name):
    send_sems, recv_sems, xs, lands = started
    n = len(xs)

    def body(*refs):
        x_refs, land_refs = refs[:n], refs[n:2 * n]
        for cp in _ag_copies(x_refs, land_refs, refs[2 * n], refs[2 * n + 1], landing=True):
            cp.wait_send()
            cp.wait_recv()

    out = pl.pallas_call(
        body, name=name,
        out_shape=tuple(pltpu.HBM(a.shape, a.dtype) for a in xs + lands),
        in_specs=[_HBM] * (2 * n) + [_SEM, _SEM, _ANY], out_specs=tuple([_HBM] * (2 * n)),
        input_output_aliases={i: i for i in range(2 * n)},
        compiler_params=pltpu.CompilerParams(has_side_effects=_EFFECT),
    )(*xs, *lands, send_sems, recv_sems, after)
    return list(out[:n]), list(out[n:])


def _ag_forward(lands, *, name):
    n = len(lands)

    def body(*refs):
        land = refs[n:2 * n]
        send_sems, recv_sems = refs[2 * n:]
        x, y, c = _place()
        sibling = (x, y, 1 - c)

        def copy(i, j, core):
            dx, dy = _CHIP_FLIPS[j]
            rows = land[i].at[_slot(_flip(x, dx), _flip(y, dy), core)]
            return pltpu.make_async_remote_copy(src_ref=rows, dst_ref=rows, send_sem=send_sems.at[i, j],
                                                recv_sem=recv_sems.at[i, j], device_id=sibling, device_id_type=MESH)

        sends = [copy(i, j, c) for i in range(n) for j in range(3)]
        for cp in sends:
            cp.start()
        for i in range(n):
            for j in range(3):
                copy(i, j, 1 - c).wait_recv()
        for cp in sends:
            cp.wait_send()

    return pl.pallas_call(
        body, name=name, in_specs=[_ANY] * n, out_specs=[_ANY] * n,
        out_shape=[_sds(a.shape, a.dtype) for a in lands], input_output_aliases={i: i for i in range(n)},
        scratch_shapes=[pltpu.SemaphoreType.DMA((n, 3)), pltpu.SemaphoreType.DMA((n, 3))],
    )(*lands)


def _fw_copies(land_refs, send_sems, recv_sems, *, landing):
    x, y, c = _place()
    cps = []
    for i in range(len(land_refs)):
        for j, (dx, dy) in enumerate(_CHIP_FLIPS):
            rows = land_refs[i].at[_slot(_flip(x, dx), _flip(y, dy), (1 - c) if landing else c)]
            cps.append(pltpu.make_async_remote_copy(
                src_ref=rows, dst_ref=rows, send_sem=send_sems.at[3 * i + j], recv_sem=recv_sems.at[3 * i + j],
                device_id=(x, y, 1 - c), device_id_type=MESH))
    return cps


def _fw_start(lands, *, name, dep=None):
    n = len(lands)
    n_in = n + (0 if dep is None else 1)

    def body(*refs):
        for cp in _fw_copies(refs[:n], refs[n_in], refs[n_in + 1], landing=False):
            cp.start()
        refs[-1][...] = jnp.zeros_like(refs[-1])

    out = pl.pallas_call(
        body, name=name,
        out_shape=(pltpu.SemaphoreType.DMA((3 * n,)), pltpu.SemaphoreType.DMA((3 * n,)),
                   *[pltpu.HBM(a.shape, a.dtype) for a in lands], _sds(_TOKEN, F32)),
        in_specs=[_HBM] * n + ([] if dep is None else [_ANY]),
        out_specs=(_SEM, _SEM, *[_HBM] * n, pl.BlockSpec(memory_space=pltpu.VMEM)),
        input_output_aliases={i: 2 + i for i in range(n)},
        compiler_params=pltpu.CompilerParams(has_side_effects=_EFFECT),
    )(*[_in_hbm(a) for a in lands], *(() if dep is None else (dep,)))
    return (out[0], out[1], list(out[2:2 + n])), out[-1]


def _fw_wait(started, after, *, name):
    send_sems, recv_sems, lands = started
    n = len(lands)

    def body(*refs):
        for cp in _fw_copies(refs[:n], refs[n], refs[n + 1], landing=True):
            cp.wait_send()
            cp.wait_recv()

    out = pl.pallas_call(
        body, name=name,
        out_shape=tuple(pltpu.HBM(a.shape, a.dtype) for a in lands),
        in_specs=[_HBM] * n + [_SEM, _SEM, _ANY], out_specs=tuple([_HBM] * n),
        input_output_aliases={i: i for i in range(n)},
        compiler_params=pltpu.CompilerParams(has_side_effects=_EFFECT),
    )(*lands, send_sems, recv_sems, after)
    return list(out)


def _sib_copies(g_refs, land_refs, send_sems, recv_sems):
    x, y, c = _place()
    return [pltpu.make_async_remote_copy(
        src_ref=g_refs[i].at[:, 1 - c], dst_ref=land_refs[i], send_sem=send_sems.at[i], recv_sem=recv_sems.at[i],
        device_id=(x, y, 1 - c), device_id_type=MESH) for i in range(len(g_refs))]


def _sib_start(g4s, *, name):
    n = len(g4s)
    lands = [lax.empty((4,) + a.shape[2:], a.dtype) for a in g4s]

    def body(*refs):
        for cp in _sib_copies(refs[:n], refs[n:2 * n], refs[2 * n], refs[2 * n + 1]):
            cp.start()
        refs[-1][...] = jnp.zeros_like(refs[-1])

    out = pl.pallas_call(
        body, name=name,
        out_shape=(pltpu.SemaphoreType.DMA((n,)), pltpu.SemaphoreType.DMA((n,)),
                   *[pltpu.HBM(a.shape, a.dtype) for a in g4s], *[pltpu.HBM(a.shape, a.dtype) for a in lands],
                   _sds(_TOKEN, F32)),
        in_specs=[_HBM] * (2 * n),
        out_specs=(_SEM, _SEM, *[_HBM] * (2 * n), pl.BlockSpec(memory_space=pltpu.VMEM)),
        input_output_aliases={i: 2 + i for i in range(2 * n)},
        compiler_params=pltpu.CompilerParams(has_side_effects=_EFFECT),
    )(*[_in_hbm(a) for a in g4s], *[_in_hbm(a) for a in lands])
    return (out[0], out[1], list(out[2:2 + n]), list(out[2 + n:2 + 2 * n])), out[-1]


def _sib_wait(started, after, *, name):
    send_sems, recv_sems, g4s, lands = started
    n = len(g4s)

    def body(*refs):
        for cp in _sib_copies(refs[:n], refs[n:2 * n], refs[2 * n], refs[2 * n + 1]):
            cp.wait_send()
            cp.wait_recv()

    out = pl.pallas_call(
        body, name=name,
        out_shape=tuple(pltpu.HBM(a.shape, a.dtype) for a in g4s + lands),
        in_specs=[_HBM] * (2 * n) + [_SEM, _SEM, _ANY], out_specs=tuple([_HBM] * (2 * n)),
        input_output_aliases={i: i for i in range(2 * n)},
        compiler_params=pltpu.CompilerParams(has_side_effects=_EFFECT),
    )(*g4s, *lands, send_sems, recv_sems, after)
    return list(out[:n]), list(out[n:])


PAIR_SUM_BLOCK_BYTES = 3 * 1024 * 1024


def _rs_rows(r, c):
    tr = r
    while tr * c * 2 > PAIR_SUM_BLOCK_BYTES and tr % 2 == 0:
        tr //= 2
    return tr


def _rs_pair_sum(g4, from_sibling, core, *, name):
    _, _, r, c = g4.shape
    tr = _rs_rows(r, c)

    def body(core_ref, g_ref, a_ref, o_ref):
        o_ref[...] = (g_ref[...].astype(F32) + a_ref[...].astype(F32)).astype(BF16)

    blk = pl.BlockSpec((None, tr, c), lambda k, i, core_ref: (k, i, 0))
    return pl.pallas_call(
        body, name=name,
        grid_spec=pltpu.PrefetchScalarGridSpec(
            num_scalar_prefetch=1, grid=(4, r // tr),
            in_specs=[pl.BlockSpec((None, None, tr, c), lambda k, i, core_ref: (k, core_ref[0], i, 0)), blk],
            out_specs=blk),
        out_shape=_sds((4, r, c), BF16), compiler_params=_cparams("parallel", "parallel"),
    )(core, g4, from_sibling)


def _rs_copies(h_refs, land_refs, send_sems, recv_sems):
    x, y, c = _place()
    cps = []
    for i in range(len(h_refs)):
        for k, (dx, dy) in enumerate(_CHIP_FLIPS):
            px, py = _flip(x, dx), _flip(y, dy)
            cps.append(pltpu.make_async_remote_copy(
                src_ref=h_refs[i].at[2 * px + py], dst_ref=land_refs[i].at[k], send_sem=send_sems.at[3 * i + k],
                recv_sem=recv_sems.at[3 * i + k], device_id=(px, py, c), device_id_type=MESH))
    return cps


def _rs_start(hs, *, name):
    n = len(hs)
    lands = [lax.empty((3,) + a.shape[1:], a.dtype) for a in hs]

    def body(*refs):
        h_refs, land_refs = refs[:n], refs[n:2 * n]
        token = refs[-1]
        for cp in _rs_copies(h_refs, land_refs, refs[2 * n], refs[2 * n + 1]):
            cp.start()
        token[...] = jnp.zeros_like(token)

    out = pl.pallas_call(
        body, name=name,
        out_shape=(pltpu.SemaphoreType.DMA((3 * n,)), pltpu.SemaphoreType.DMA((3 * n,)),
                   *[pltpu.HBM(a.shape, a.dtype) for a in hs], *[pltpu.HBM(a.shape, a.dtype) for a in lands],
                   _sds(_TOKEN, F32)),
        in_specs=[_HBM] * (2 * n),
        out_specs=(_SEM, _SEM, *[_HBM] * (2 * n), pl.BlockSpec(memory_space=pltpu.VMEM)),
        input_output_aliases={i: 2 + i for i in range(2 * n)},
        compiler_params=pltpu.CompilerParams(has_side_effects=_EFFECT),
    )(*[_in_hbm(a) for a in hs], *[_in_hbm(a) for a in lands])
    return (out[0], out[1], list(out[2:2 + n]), list(out[2 + n:2 + 2 * n])), out[-1]


def _rs_wait(started, after, *, name):
    send_sems, recv_sems, hs, lands = started
    n = len(hs)

    def body(*refs):
        for cp in _rs_copies(refs[:n], refs[n:2 * n], refs[2 * n], refs[2 * n + 1]):
            cp.wait_send()
            cp.wait_recv()

    out = pl.pallas_call(
        body, name=name,
        out_shape=tuple(pltpu.HBM(a.shape, a.dtype) for a in hs + lands),
        in_specs=[_HBM] * (2 * n) + [_SEM, _SEM, _ANY], out_specs=tuple([_HBM] * (2 * n)),
        input_output_aliases={i: i for i in range(2 * n)},
        compiler_params=pltpu.CompilerParams(has_side_effects=_EFFECT),
    )(*hs, *lands, send_sems, recv_sems, after)
    return list(out[:n]), list(out[n:])


def _adamw_chips(w, h, others, m, v, chip, *, name):
    r, c = w.shape

    def body(chip_ref, w_ref, h_ref, o_ref, m_ref, v_ref, g_ref, d_ref, nm_ref, nv_ref):
        g = h_ref[...].astype(F32)
        for k in range(3):
            g = g + o_ref[k].astype(F32)
        g_ref[...] = g
        d_ref[...], nm_ref[...], nv_ref[...] = _adamw_math(w_ref[...], g, m_ref[...], v_ref[...])

    blk = pl.BlockSpec((ADAM_ROWS, c), lambda i, chip_ref: (i, 0))
    return pl.pallas_call(
        body, name=name,
        grid_spec=pltpu.PrefetchScalarGridSpec(
            num_scalar_prefetch=1, grid=(r // ADAM_ROWS,),
            in_specs=[blk, pl.BlockSpec((None, ADAM_ROWS, c), lambda i, chip_ref: (chip_ref[0], i, 0)),
                      pl.BlockSpec((3, ADAM_ROWS, c), lambda i, chip_ref: (0, i, 0)), blk, blk],
            out_specs=[blk] * 4),
        out_shape=[_sds((r, c), F32)] * 4, compiler_params=_cparams("parallel"),
    )(chip, w, h, others, m, v)


def _adamw_chips_layers(w, hs, others, m, v, chip, *, first, prev=None, deps=(), name):
    nl, r, c = w.shape
    n = len(hs)
    n_prev = 0 if prev is None else 4

    def body(chip_ref, w_ref, m_ref, v_ref, *rest):
        h_refs, o_refs = rest[:n], rest[n:2 * n]
        g_ref, d_ref, nm_ref, nv_ref = rest[2 * n + n_prev + len(deps):]
        k_now = pl.program_id(0)
        g = jnp.zeros((ADAM_ROWS, c), F32)
        for k in range(n):
            gk = h_refs[k][...].astype(F32)
            for j in range(3):
                gk = gk + o_refs[k][j].astype(F32)
            g = jnp.where(k_now == k, gk, g)
        g_ref[...] = g
        d_ref[...], nm_ref[...], nv_ref[...] = _adamw_math(w_ref[...], g, m_ref[...], v_ref[...])

    def rows(k):
        return lambda l, i: jnp.where(l == k, i, 0)

    blk = pl.BlockSpec((None, ADAM_ROWS, c), lambda l, i, chip_ref: (l + first, i, 0))
    h_specs = [pl.BlockSpec((None, ADAM_ROWS, c), lambda l, i, chip_ref, f=rows(k): (chip_ref[0], f(l, i), 0))
               for k in range(n)]
    o_specs = [pl.BlockSpec((3, ADAM_ROWS, c), lambda l, i, chip_ref, f=rows(k): (0, f(l, i), 0)) for k in range(n)]
    return pl.pallas_call(
        body, name=name,
        grid_spec=pltpu.PrefetchScalarGridSpec(
            num_scalar_prefetch=1, grid=(n, r // ADAM_ROWS),
            in_specs=[blk, blk, blk] + h_specs + o_specs + [_ANY] * (n_prev + len(deps)), out_specs=[blk] * 4),
        out_shape=[_sds((nl, r, c), F32)] * 4,
        input_output_aliases={4 + 2 * n + q: q for q in range(n_prev)},
        compiler_params=_cparams("arbitrary", "arbitrary"),
    )(chip, w, m, v, *hs, *others, *(() if prev is None else prev), *deps)


def _gather_begin(xs, tag, dep=None):
    return _ag_start(xs, name=f"ag_start_{tag}", dep=dep)


def _gather_end(started, after, tag):
    _, lands = _ag_wait(started, after, name=f"ag_wait_{tag}")
    return _ag_forward(lands, name=f"ag_forward_{tag}")


def _scatter_pair(gs, tag):
    g4s = [g.reshape((4, 2) + g.shape[1:]) for g in gs]
    return _sib_start(g4s, name=f"rs_sib_start_{tag}")


def _scatter_chips(pair, after, tag):
    core = lax.axis_index("c").astype(jnp.int32).reshape(1)
    g4s, got = _sib_wait(pair, after, name=f"rs_sib_wait_{tag}")
    hs = [_rs_pair_sum(g4, a, core, name=f"rs_pair_sum_{tag}_{i}") for i, (g4, a) in enumerate(zip(g4s, got))]
    return _rs_start(hs, name=f"rs_start_{tag}")


def _scatter_end(started, after, tag):
    return _rs_wait(started, after, name=f"rs_wait_{tag}")


MM_TM = 512


def _ffn_fwd(x_mid, g_row, wup_p, wdown, conv_w, conv_b, tag, dep=None):
    hb, _ = _rms_fwd(x_mid, g_row, want_f32=False, name=f"ffn_norm_{tag}", dep=dep)
    up = _mm_nn_pieces(hb, wup_p, tm=MM_TM, name=f"ffn_up_{tag}")
    a = _conv_gate_fwd(up, conv_w, conv_b, name=f"ffn_conv_{tag}")
    if callable(wdown):
        wdown = wdown(a)
    x_out = _mm_nn(a, wdown, tm=1024, tn=512, name=f"ffn_down_{tag}", res=x_mid)
    return x_out, (hb, up, a)


def _ffn_bwd(dx, x_mid, g_row, wup_p, wdown, conv_w, conv_b, saved, tag):
    hb, up, a = saved
    dxb = dx.astype(BF16)
    da = _mm_nt(dxb, wdown, tm=MM_TM, tn=1408, name=f"ffn_da_{tag}")
    dwdown = _mm_tn(a, dxb, tm=512, tn=1024, name=f"ffn_dwdown_{tag}")
    dup_v, dup_g, dconv_w, dconv_b = _conv_gate_bwd(up, da, conv_w, conv_b, name=f"ffn_dconv_{tag}")
    dwup = _mm_tn_pieces(hb, (dup_v, dup_g), pieces=N_DEV, tm=MM_TM, name=f"ffn_dwup_{tag}")
    pair, token = _scatter_pair([dwup, dwdown.reshape(N_DEV, D_FF // N_DEV, D_MODEL)], f"ffn_{tag}")
    dh = _mm_nt_pieces((dup_v, dup_g), wup_p, tm=1024, tn=1024, name=f"ffn_dh_{tag}", dep=token)
    started, token = _scatter_chips(pair, dh, f"ffn_{tag}")
    dx_mid, dg = _rms_bwd(x_mid, g_row, dh, dx, name=f"ffn_dnorm_{tag}", dep=token)
    return dx_mid, dg, dconv_w, dconv_b, started


def _pool_layer_fwd(x, g_row, w, b_row, sc_row, tag, dep=None):
    _, hf = _rms_fwd(x, g_row, want_f32=True, name=f"pool_norm_{tag}", dep=dep)
    return _pool_fwd(hf, x, w, b_row, sc_row, name=f"pool_fwd_{tag}"), (hf,)


def _pool_layer_bwd(dx_mid, x, g_row, w, b_row, sc_row, saved, tag):
    (hf,) = saved
    dh, dw, db, dsc = _pool_bwd(hf, dx_mid, w, b_row, sc_row, name=f"pool_bwd_{tag}")
    dx, dg = _rms_bwd(x, g_row, dh, dx_mid, name=f"pool_dnorm_{tag}")
    return dx, dg, dw, db, dsc


def _sb_layer_fwd(x, g_row, wqkv_p, gains, wo, tag, dep=None):
    hb, _ = _rms_fwd(x, g_row, want_f32=False, name=f"sb_norm_{tag}", dep=dep)
    qkv = _mm_nn_pieces(hb, wqkv_p, tm=MM_TM, name=f"sb_qkv_{tag}")
    qkn = _qk_norm_fwd(qkv, gains, name=f"sb_qknorm_{tag}")
    vb = qkv[:, 2 * D_MODEL:].astype(BF16)
    o = _sb_fwd(qkn, vb, name=f"sb_att_{tag}")
    x_mid = _mm_nn(o, wo, tm=MM_TM, tn=512, name=f"sb_out_{tag}", res=x)
    return x_mid, (hb, qkv, qkn, vb, o)


def _sb_layer_bwd(dx_mid, x, g_row, wqkv_p, gains, wo, saved, tag):
    hb, qkv, qkn, vb, o = saved
    dmb = dx_mid.astype(BF16)
    do = _mm_nt(dmb, wo, tm=MM_TM, tn=512, name=f"sb_do_{tag}", out_dtype=BF16)
    dwo = _mm_tn(o, dmb, tm=512, tn=1024, name=f"sb_dwo_{tag}")
    dqn, dkn, dv = _sb_bwd(qkn, vb, do, name=f"sb_datt_{tag}")
    dq, dqg = _qk_norm_bwd(qkv, gains, dqn, which=0, name=f"sb_dqnorm_{tag}")
    dk, dkg = _qk_norm_bwd(qkv, gains, dkn, which=1, name=f"sb_dknorm_{tag}")
    dqkv = jnp.concatenate([dq, dk, dv.astype(BF16)], axis=1)
    dwqkv = _mm_tn_pieces(hb, dqkv, pieces=N_DEV, tm=MM_TM, name=f"sb_dwqkv_{tag}")
    pair, token = _scatter_pair([dwqkv, dwo.reshape(N_DEV, D_MODEL // N_DEV, D_MODEL)], f"sb_{tag}")
    dh = _mm_nt_pieces(dqkv, wqkv_p, tm=MM_TM, tn=1024, name=f"sb_dh_{tag}", dep=token)
    started, token = _scatter_chips(pair, dh, f"sb_{tag}")
    dx, dg = _rms_bwd(x, g_row, dh, dx_mid, name=f"sb_dnorm_{tag}", dep=token)
    return dx, dg, dqg, dkg, started


def _ssm_params(lam_re, lam_im, log_step, b_re, b_im):
    g, p = SSM_GROUPS, SSM_STATE
    return (lam_re.reshape(g, 1, p), lam_im.reshape(g, 1, p), log_step.reshape(g, 1, 1),
            jnp.transpose(b_re, (0, 2, 1)), jnp.transpose(b_im, (0, 2, 1)))


def _ssm_layer_fwd(x, g_row, raw, c_re, c_im, d_row, wglu_p, bglu_row, tag, dep=None):
    g, p, ch = SSM_GROUPS, SSM_STATE, SSM_CH
    _, hf = _rms_fwd(x, g_row, want_f32=True, name=f"ssm_norm_{tag}", dep=dep)
    ar, ai, bbr, bbi = _ssm_prep_fwd(*raw, name=f"ssm_prep_{tag}")
    wb = jnp.concatenate([bbr.reshape(g * ch, p), bbi.reshape(g * ch, p)], axis=1)
    wc = jnp.concatenate([c_re.reshape(g * ch, p), -c_im.reshape(g * ch, p)], axis=1)
    a_re, a_im = ar.reshape(1, g * p), ai.reshape(1, g * p)
    ylin, yg = _ssm_core_fwd(hf, wb, wc, a_re, a_im, d_row, name=f"ssm_core_{tag}")
    x_mid, val, gate = _glu_fwd(yg, wglu_p, bglu_row, x, tm=MM_TM, name=f"ssm_glu_{tag}")
    return x_mid, (hf, wb, wc, a_re, a_im, ylin, yg, val, gate)


def _ssm_layer_bwd(dx_mid, x, g_row, raw, d_row, wglu_p, saved, tag):
    g, p, ch = SSM_GROUPS, SSM_STATE, SSM_CH
    hf, wb, wc, a_re, a_im, ylin, yg, val, gate = saved
    dgv, dbglu = _glu_bwd(dx_mid, val, gate, name=f"ssm_dglu_{tag}")
    dwglu = _mm_tn_pieces(yg, dgv, pieces=N_DEV, tm=MM_TM, name=f"ssm_dwglu_{tag}")
    pair, token = _scatter_pair([dwglu], f"ssm_{tag}")
    dyg = _mm_nt_pieces(dgv, wglu_p, tm=MM_TM, tn=1024, name=f"ssm_dyg_{tag}", dep=token)
    du, dwb, dwc, dar, dai, dd = _ssm_core_bwd(hf, ylin, dyg, wb, wc, a_re, a_im, d_row, name=f"ssm_dcore_{tag}")
    dc_re = dwc[:, :p].reshape(g, ch, p)
    dc_im = -dwc[:, p:].reshape(g, ch, p)
    dlr, dli, dls, dbtr, dbti = _ssm_prep_bwd(
        *raw, dar.reshape(g, 1, p), dai.reshape(g, 1, p), dwb[:, :p].reshape(g, ch, p), dwb[:, p:].reshape(g, ch, p),
        name=f"ssm_dprep_{tag}")
    started, token = _scatter_chips(pair, du, f"ssm_{tag}")
    dx, dg = _rms_bwd(x, g_row, du, dx_mid, name=f"ssm_dnorm_{tag}", dep=token)
    grads = dict(ssm_lam_re=dlr.reshape(1, g, p), ssm_lam_im=dli.reshape(1, g, p), ssm_log_step=dls.reshape(1, g),
                 ssm_b_re=jnp.transpose(dbtr, (0, 2, 1))[None], ssm_b_im=jnp.transpose(dbti, (0, 2, 1))[None],
                 ssm_c_re=dc_re[None], ssm_c_im=dc_im[None], ssm_d=dd, ssm_b_glu=dbglu)
    return dx, dg, grads, started


_WEIGHTS = ["norm_mix_g", "norm_ffn_g", "pool_w", "pool_b", "pool_scale", "sb_w_qkv", "sb_q_gain", "sb_k_gain", "sb_w_o",
            "ssm_lam_re", "ssm_lam_im", "ssm_log_step", "ssm_b_re", "ssm_b_im", "ssm_c_re", "ssm_c_im", "ssm_d",
            "ssm_w_glu", "ssm_b_glu", "ffn_w_up", "ffn_conv_w", "ffn_conv_b", "ffn_w_down"]
_INPUTS = ["x"] + _WEIGHTS + ["loss_target"] + ["m_" + n for n in _WEIGHTS] + ["v_" + n for n in _WEIGHTS]
_REPLICATED = ["norm_mix_g", "norm_ffn_g", "sb_q_gain", "sb_k_gain", "ssm_lam_re", "ssm_lam_im", "ssm_log_step",
               "ssm_b_re", "ssm_b_im", "ssm_c_re", "ssm_c_im", "ffn_conv_b"]
_SMALL_SHARDED = {"pool_b": 1, "pool_scale": 1, "ssm_d": 1, "ssm_b_glu": 1, "ffn_conv_w": 2}
_BIG = ["pool_w", "sb_w_qkv", "sb_w_o", "ssm_w_glu", "ffn_w_up", "ffn_w_down"]
PACK_COLS = 512


def _pack(arrays):
    flat = jnp.concatenate([a.reshape(-1).astype(F32) for a in arrays])
    rows = -(-flat.shape[0] // (PACK_COLS * PACK_ROWS)) * PACK_ROWS
    return jnp.pad(flat, (0, rows * PACK_COLS - flat.shape[0])).reshape(rows, PACK_COLS)


def _unpack(packed, shapes, lead=()):
    flat = packed.reshape(lead + (-1,))
    out, off = [], 0
    for shp in shapes:
        n = math.prod(shp)
        out.append(flat[..., off:off + n].reshape(lead + tuple(shp)))
        off += n
    return out


def _unshard(gathered, axis):
    g = jnp.moveaxis(gathered, 0, axis)
    shp = g.shape
    return g.reshape(shp[:axis] + (shp[axis] * shp[axis + 1],) + shp[axis + 2:])


def _step(p):
    s = p["x"].shape[1]
    x = p["x"].reshape(s, D_MODEL)
    me = _slot(*_place())

    small_local = [p[n] for n in _SMALL_SHARDED]
    pool_w_l = p["pool_w"].astype(BF16).reshape(-1, POOL_DIM)
    chip = (2 * lax.axis_index("x") + lax.axis_index("y")).astype(jnp.int32).reshape(1)

    def ffn_shards(i):
        return [p["ffn_w_up"][i].astype(BF16), p["ffn_w_down"][i].astype(BF16)]

    st_first, tok = _gather_begin([pool_w_l, _pack(small_local)], "first")
    st_ffn, tok_ffn = [None] * DEPTH, [None] * DEPTH
    st_ffn[0], tok = _gather_begin(ffn_shards(0)[:1], "ffn_0", dep=tok)
    st_down0, tok = _gather_begin(ffn_shards(0)[1:], "ffn_0_down", dep=tok)
    st_sb, tok = _gather_begin([p["sb_w_qkv"][0].astype(BF16), p["sb_w_o"][0].astype(BF16)], "sb", dep=tok)
    st_ffn[1], tok = _gather_begin(ffn_shards(1), "ffn_1", dep=tok)
    st_glu, tok = _gather_begin([p["ssm_w_glu"][0].astype(BF16)], "glu", dep=tok)
    ag = _gather_end(st_first, tok, "first")
    n_pool = p["pool_w"].shape[0]
    pool_w = jnp.transpose(ag[0].reshape(N_DEV, n_pool, POOL_GROUPS, POOL_DIM // N_DEV, POOL_DIM), (1, 2, 0, 3, 4))
    pool_w = pool_w.reshape(n_pool, POOL_GROUPS, POOL_DIM, POOL_DIM)
    small_full = {}
    for n, g in zip(_SMALL_SHARDED, _unpack(ag[1], [a.shape for a in small_local], lead=(N_DEV,))):
        small_full[n] = _unshard(g, _SMALL_SHARDED[n])
    mix_w = {}
    wup_p, wdown = [None] * DEPTH, [None] * DEPTH

    gains = jnp.stack([p["sb_q_gain"][0], p["sb_k_gain"][0]])[:, None, :]
    ssm_raw = _ssm_params(p["ssm_lam_re"][0], p["ssm_lam_im"][0], p["ssm_log_step"][0], p["ssm_b_re"][0],
                          p["ssm_b_im"][0])

    def mixer_args(i):
        j = i // 3
        g_row = p["norm_mix_g"][i][None]
        if i % 3 == 0:
            return (g_row, pool_w[j], small_full["pool_b"][j][None], small_full["pool_scale"][j][None])
        if i % 3 == 1:
            return (g_row, mix_w["qkv"], gains, mix_w["o"])
        return (g_row, ssm_raw, p["ssm_c_re"][0], p["ssm_c_im"][0], small_full["ssm_d"], mix_w["glu"],
                small_full["ssm_b_glu"])

    def ffn_args(i):
        return (p["norm_ffn_g"][i][None], wup_p[i], wdown[i], small_full["ffn_conv_w"][i], p["ffn_conv_b"][i][None])

    xs_in, xs_mid, saved_mix, saved_ffn = [], [], [], []
    for i in range(DEPTH):
        xs_in.append(x)
        if i == 1:
            mix_w["qkv"], wo_g = _gather_end(st_sb, x, "sb")
            mix_w["o"] = wo_g.reshape(D_MODEL, D_MODEL)
        if i == 2:
            (mix_w["glu"],) = _gather_end(st_glu, x, "glu")
        dep, handing = None, None
        if i >= 2:
            _, lands = _ag_wait(st_ffn[i], x, name=f"ag_wait_ffn_{i}")
            handing, dep = _fw_start(lands, name=f"ag_fw_start_ffn_{i}")
        if 1 <= i and i + 1 < DEPTH:
            st_ffn[i + 1], dep = _gather_begin(ffn_shards(i + 1), f"ffn_{i + 1}", dep=dep)
        fwd = (_pool_layer_fwd, _sb_layer_fwd, _ssm_layer_fwd)[i % 3]
        x, sv = fwd(x, *mixer_args(i), f"l{i}", dep=dep)
        saved_mix.append(sv)
        xs_mid.append(x)
        if i == 0:
            (wup_p[i],) = _gather_end(st_ffn[i], x, f"ffn_{i}")

            def wdown_now(after):
                (wd_g,) = _gather_end(st_down0, after, "ffn_0_down")
                wdown[0] = wd_g.reshape(D_FF, D_MODEL)
                return wdown[0]
        else:
            if handing is None:
                wup_p[i], wd_g = _gather_end(st_ffn[i], x, f"ffn_{i}")
            else:
                wup_p[i], wd_g = _fw_wait(handing, x, name=f"ag_fw_wait_ffn_{i}")
            wdown[i] = wd_g.reshape(D_FF, D_MODEL)
        g_row, wu, wd, cw, cb = ffn_args(i)
        x, sv = _ffn_fwd(x, g_row, wu, wdown_now if i == 0 else wd, cw, cb, f"l{i}")
        saved_ffn.append(sv)
    dx, loss_part = _loss_head(x, p["loss_target"].reshape(s, D_MODEL), name="loss_head")

    grads = {}
    dg_mix, dg_ffn = [None] * DEPTH, [None] * DEPTH
    dconv_w, dconv_b = [None] * DEPTH, [None] * DEPTH
    dpool = {"w": {}, "b": {}, "scale": {}}
    out = {}

    def big_update(n, h, others, idx=None):
        w, m, v = (p[pre + n] if idx is None else p[pre + n][idx] for pre in ("", "m_", "v_"))
        cols = h.shape[-1]
        r = _adamw_chips(w.reshape(-1, cols), h, others, m.reshape(-1, cols), v.reshape(-1, cols), chip,
                         name=f"adamw_{n}" + ("" if idx is None else f"_{idx}"))
        return [a.reshape(w.shape) for a in r]

    kinds = ("grad", "delta", "new_m", "new_v")
    ffn_upd = {"ffn_w_up": [None] * DEPTH, "ffn_w_down": [None] * DEPTH}

    def finish(entry, after):
        names, idx, started, tag = entry
        hs, others = _scatter_end(started, after, tag)
        for n, h, o in zip(names, hs, others):
            if idx is None:
                for kind, a in zip(kinds, big_update(n, h, o)):
                    out[kind + "_" + n] = a
            else:
                ffn_upd[n][idx] = (h, o)

    pending = []
    for i in reversed(range(DEPTH)):
        dx, dg_ffn[i], dconv_w[i], dconv_b[i], started = _ffn_bwd(
            dx, xs_mid[i], *ffn_args(i), saved_ffn[i], f"l{i}")
        for entry in pending:
            finish(entry, dx)
        pending = [(("ffn_w_up", "ffn_w_down"), i, started, f"ffn_l{i}")]
        margs = mixer_args(i)
        if i % 3 == 0:
            j = i // 3
            dx, dg_mix[i], dpool["w"][j], dpool["b"][j], dpool["scale"][j] = _pool_layer_bwd(
                dx, xs_in[i], *margs, saved_mix[i], f"l{i}")
        elif i % 3 == 1:
            dx, dg_mix[i], dqg, dkg, started = _sb_layer_bwd(dx, xs_in[i], *margs, saved_mix[i], f"l{i}")
            grads["sb_q_gain"], grads["sb_k_gain"] = dqg, dkg
            pending.append((("sb_w_qkv", "sb_w_o"), None, started, f"sb_l{i}"))
        else:
            g_row, raw, _, _, d_row, wg, _ = margs
            dx, dg_mix[i], sg, started = _ssm_layer_bwd(dx, xs_in[i], g_row, raw, d_row, wg, saved_mix[i], f"l{i}")
            grads.update(sg)
            pending.append((("ssm_w_glu",), None, started, f"ssm_l{i}"))
    grad_x = dx.reshape(1, s, D_MODEL)
    grads["norm_mix_g"] = jnp.concatenate(dg_mix, axis=0)
    grads["norm_ffn_g"] = jnp.concatenate(dg_ffn, axis=0)
    grads["ffn_conv_w"] = jnp.stack(dconv_w)
    grads["ffn_conv_b"] = jnp.concatenate(dconv_b, axis=0)
    grads["pool_b"] = jnp.concatenate([dpool["b"][j] for j in range(n_pool)], axis=0)
    grads["pool_scale"] = jnp.concatenate([dpool["scale"][j] for j in range(n_pool)], axis=0)
    dpw = jnp.stack([dpool["w"][j] for j in range(n_pool)])
    dpw = dpw.reshape(n_pool, POOL_GROUPS, N_DEV, POOL_DIM // N_DEV, POOL_DIM)
    pair, token = _scatter_pair([jnp.transpose(dpw, (2, 0, 1, 3, 4)).reshape(N_DEV, -1, POOL_DIM)], "pool")
    pool_started, tok_pool = _scatter_chips(pair, token, "pool")
    (ffn_first,) = pending

    small_names = _REPLICATED + list(_SMALL_SHARDED)
    full_shapes = [p[n].shape for n in _REPLICATED] + [small_full[n].shape for n in _SMALL_SHARDED]
    part = _pack([grads[n].reshape(shp) for n, shp in zip(small_names, full_shapes)] + [loss_part[0]])
    st_small, tok_small = _gather_begin([part], "small_grads")
    upper = {}
    for n in ffn_upd:
        upper[n] = _adamw_chips_layers(p[n], [ffn_upd[n][i][0] for i in range(1, DEPTH)],
                                       [ffn_upd[n][i][1] for i in range(1, DEPTH)], p["m_" + n], p["v_" + n], chip,
                                       first=1, deps=(tok_small, tok_pool), name=f"adamw_{n}_upper")
    finish((("pool_w",), None, pool_started, "pool"), upper["ffn_w_down"][0])
    finish(ffn_first, upper["ffn_w_up"][0])
    for n in ffn_upd:
        h, o = ffn_upd[n][0]
        upd = _adamw_chips_layers(p[n], [h], [o], p["m_" + n], p["v_" + n], chip, first=0, prev=upper[n],
                                  name=f"adamw_{n}_first")
        for kind, a in zip(kinds, upd):
            out[kind + "_" + n] = a
    done = sum(out["new_v_" + n].reshape(-1)[0] for n in _BIG).reshape(1, 1)
    (parts,) = _gather_end(st_small, done, "small_grads")
    summed = _unpack(_sum_parts(parts, name="sum_small_grads"), full_shapes + [(LANE,)])
    loss = summed[-1][0]
    small_g = {}
    for n, g in zip(small_names, summed[:-1]):
        if n in _SMALL_SHARDED:
            ax = _SMALL_SHARDED[n]
            g = lax.dynamic_slice_in_dim(g, me * p[n].shape[ax], p[n].shape[ax], axis=ax)
        small_g[n] = g

    local_shapes = [p[n].shape for n in small_names]
    packs = [_pack([p[pre + n] for n in small_names]) for pre in ("", "m_", "v_")]
    res = _adamw_flat(packs[0], _pack([small_g[n] for n in small_names]), packs[1], packs[2], name="adamw_small")
    for kind, packed in zip(("delta", "new_m", "new_v"), res):
        for n, a in zip(small_names, _unpack(packed, local_shapes)):
            out[kind + "_" + n] = a
    for n in small_names:
        out["grad_" + n] = small_g[n]

    return (loss, grad_x, *[out["grad_" + n] for n in _WEIGHTS], *[out["delta_" + n] for n in _WEIGHTS],
            *[out["new_m_" + n] for n in _WEIGHTS], *[out["new_v_" + n] for n in _WEIGHTS])


def kernel(x, norm_mix_g, norm_ffn_g, pool_w, pool_b, pool_scale, sb_w_qkv, sb_q_gain, sb_k_gain, sb_w_o, ssm_lam_re, ssm_lam_im, ssm_log_step, ssm_b_re, ssm_b_im, ssm_c_re, ssm_c_im, ssm_d, ssm_w_glu, ssm_b_glu, ffn_w_up, ffn_conv_w, ffn_conv_b, ffn_w_down, loss_target, m_norm_mix_g, m_norm_ffn_g, m_pool_w, m_pool_b, m_pool_scale, m_sb_w_qkv, m_sb_q_gain, m_sb_k_gain, m_sb_w_o, m_ssm_lam_re, m_ssm_lam_im, m_ssm_log_step, m_ssm_b_re, m_ssm_b_im, m_ssm_c_re, m_ssm_c_im, m_ssm_d, m_ssm_w_glu, m_ssm_b_glu, m_ffn_w_up, m_ffn_conv_w, m_ffn_conv_b, m_ffn_w_down, v_norm_mix_g, v_norm_ffn_g, v_pool_w, v_pool_b, v_pool_scale, v_sb_w_qkv, v_sb_q_gain, v_sb_k_gain, v_sb_w_o, v_ssm_lam_re, v_ssm_lam_im, v_ssm_log_step, v_ssm_b_re, v_ssm_b_im, v_ssm_c_re, v_ssm_c_im, v_ssm_d, v_ssm_w_glu, v_ssm_b_glu, v_ffn_w_up, v_ffn_conv_w, v_ffn_conv_b, v_ffn_w_down):
    args = (x, norm_mix_g, norm_ffn_g, pool_w, pool_b, pool_scale, sb_w_qkv, sb_q_gain, sb_k_gain, sb_w_o, ssm_lam_re, ssm_lam_im, ssm_log_step, ssm_b_re, ssm_b_im, ssm_c_re, ssm_c_im, ssm_d, ssm_w_glu, ssm_b_glu, ffn_w_up, ffn_conv_w, ffn_conv_b, ffn_w_down, loss_target, m_norm_mix_g, m_norm_ffn_g, m_pool_w, m_pool_b, m_pool_scale, m_sb_w_qkv, m_sb_q_gain, m_sb_k_gain, m_sb_w_o, m_ssm_lam_re, m_ssm_lam_im, m_ssm_log_step, m_ssm_b_re, m_ssm_b_im, m_ssm_c_re, m_ssm_c_im, m_ssm_d, m_ssm_w_glu, m_ssm_b_glu, m_ffn_w_up, m_ffn_conv_w, m_ffn_conv_b, m_ffn_w_down, v_norm_mix_g, v_norm_ffn_g, v_pool_w, v_pool_b, v_pool_scale, v_sb_w_qkv, v_sb_q_gain, v_sb_k_gain, v_sb_w_o, v_ssm_lam_re, v_ssm_lam_im, v_ssm_log_step, v_ssm_b_re, v_ssm_b_im, v_ssm_c_re, v_ssm_c_im, v_ssm_d, v_ssm_w_glu, v_ssm_b_glu, v_ffn_w_up, v_ffn_conv_w, v_ffn_conv_b, v_ffn_w_down)
    return _step(dict(zip(_INPUTS, args)))
```

```python
import functools
import math

import jax
import jax.numpy as jnp
from jax import lax
from jax.experimental import pallas as pl
from jax.experimental.pallas import tpu as pltpu

F32 = jnp.float32
BF16 = jnp.bfloat16

N_DEV = 8
D_MODEL = 2048
D_FF = 5632
DEPTH = 4
POOL_GROUPS = 4
POOL_DIM = 512
HEADS = 16
HEAD_DIM = 128
SSM_GROUPS = 128
SSM_CH = 16
SSM_STATE = 64
SSM_BLOCK_GROUPS = 8
SSM_BLOCK_LANES = SSM_BLOCK_GROUPS * SSM_STATE
RMS_EPS = 1e-6
ADAM_LR = 0.001
ADAM_B1 = 0.9
ADAM_B2 = 0.999
ADAM_EPS = 1e-08
ADAM_WD = 0.01
ADAM_STEP = 10

VMEM_LIMIT_BYTES = 56 * 1024 * 1024
LANE = 128
SUBLANE = 8
MESH = pl.DeviceIdType.MESH


def _cparams(*sem):
    return pltpu.CompilerParams(dimension_semantics=tuple(sem), vmem_limit_bytes=VMEM_LIMIT_BYTES)


def _sds(shape, dtype):
    return jax.ShapeDtypeStruct(tuple(shape), dtype)


def _mm(a, b, *, dims, grid, a_spec, b_spec, o_spec, out_shape, out_dtype, name, k_axis=None, acc_shape=None,
        res=None, res_spec=None, a_alt=None, b_alt=None, alt_axis=None, alt_from=None, dep=None):
    nk = grid[k_axis] if k_axis is not None else 1
    n_in = 2 + sum(e is not None for e in (res, a_alt, b_alt, dep))

    def body(*refs):
        a_ref, b_ref = refs[:2]
        rest = list(refs[2:n_in])
        r_ref = rest.pop(0) if res is not None else None
        a2_ref = rest.pop(0) if a_alt is not None else None
        b2_ref = rest.pop(0) if b_alt is not None else None
        o_ref = refs[n_in]
        scr = refs[n_in + 1:]
        av, bv = a_ref[...], b_ref[...]
        if a2_ref is not None:
            av = jnp.where(pl.program_id(alt_axis) >= alt_from, a2_ref[...], av)
        if b2_ref is not None:
            bv = jnp.where(pl.program_id(alt_axis) >= alt_from, b2_ref[...], bv)
        p = lax.dot_general(av, bv, (dims, ((), ())), preferred_element_type=F32)
        if k_axis is None:
            if r_ref is not None:
                p = p + r_ref[...]
            o_ref[...] = p.astype(o_ref.dtype)
        else:
            acc = scr[0]
            k = pl.program_id(k_axis)

            @pl.when(k == 0)
            def _():
                acc[...] = p

            @pl.when(k > 0)
            def _():
                acc[...] += p

            @pl.when(k == nk - 1)
            def _():
                r = acc[...]
                if r_ref is not None:
                    r = r + r_ref[...]
                o_ref[...] = r.astype(o_ref.dtype)

    sem = ["parallel"] * len(grid)
    if k_axis is not None:
        sem[k_axis] = "arbitrary"
    in_specs, args = [a_spec, b_spec], [a, b]
    if res is not None:
        in_specs.append(res_spec)
        args.append(res)
    for alt in (a_alt, b_alt):
        if alt is not None:
            args.append(alt[0])
            in_specs.append(alt[1])
    if dep is not None:
        args.append(dep)
        in_specs.append(pl.BlockSpec((SUBLANE, LANE), lambda *_: (0, 0)))
    scratch = [pltpu.VMEM(acc_shape, F32)] if k_axis is not None else []
    return pl.pallas_call(
        body, name=name, grid=grid, in_specs=in_specs, out_specs=o_spec, out_shape=_sds(out_shape, out_dtype),
        scratch_shapes=scratch, compiler_params=_cparams(*sem),
    )(*args)


NN = ((1,), (0,))
NT = ((1,), (1,))
TN = ((0,), (0,))


def _mm_nn_pieces(a, wp, *, tm, name, out_dtype=F32):
    s, k = a.shape
    tm = min(tm, s)
    p, _, c = wp.shape
    return _mm(a, wp, dims=NN, grid=(s // tm, p),
               a_spec=pl.BlockSpec((tm, k), lambda m, n: (m, 0)),
               b_spec=pl.BlockSpec((None, k, c), lambda m, n: (n, 0, 0)),
               o_spec=pl.BlockSpec((tm, c), lambda m, n: (m, n)),
               out_shape=(s, p * c), out_dtype=out_dtype, name=name)


def _mm_nt_pieces(a, wp, *, tm, tn, name, out_dtype=F32, dep=None):
    p, n, c = wp.shape
    halves = a if isinstance(a, tuple) else None
    a0 = halves[0] if halves else a
    s = a0.shape[0]
    tm = min(tm, s)
    h = p // 2
    alt = {}
    if halves:
        a_spec = pl.BlockSpec((tm, c), lambda m, j, k: (m, jnp.minimum(k, h - 1)))
        alt = dict(a_alt=(halves[1], pl.BlockSpec((tm, c), lambda m, j, k: (m, jnp.maximum(k - h, 0)))),
                   alt_axis=2, alt_from=h)
    else:
        a_spec = pl.BlockSpec((tm, c), lambda m, j, k: (m, k))
    return _mm(a0, wp, dims=NT, grid=(s // tm, n // tn, p), k_axis=2, acc_shape=(tm, tn), a_spec=a_spec,
               b_spec=pl.BlockSpec((None, tn, c), lambda m, j, k: (k, j, 0)),
               o_spec=pl.BlockSpec((tm, tn), lambda m, j, k: (m, j)),
               out_shape=(s, n), out_dtype=out_dtype, name=name, dep=dep, **alt)


def _mm_tn_pieces(a, g, *, pieces, tm, name, out_dtype=BF16):
    s, m = a.shape
    halves = g if isinstance(g, tuple) else None
    g0 = halves[0] if halves else g
    h = pieces // 2
    c = g0.shape[1] // (h if halves else pieces)
    alt = {}
    if halves:
        b_spec = pl.BlockSpec((s, c), lambda n, i: (0, jnp.minimum(n, h - 1)))
        alt = dict(b_alt=(halves[1], pl.BlockSpec((s, c), lambda n, i: (0, jnp.maximum(n - h, 0)))),
                   alt_axis=0, alt_from=h)
    else:
        b_spec = pl.BlockSpec((s, c), lambda n, i: (0, n))
    return _mm(a, g0, dims=TN, grid=(pieces, m // tm), a_spec=pl.BlockSpec((s, tm), lambda n, i: (0, i)),
               b_spec=b_spec, o_spec=pl.BlockSpec((None, tm, c), lambda n, i: (n, i, 0)),
               out_shape=(pieces, m, c), out_dtype=out_dtype, name=name, **alt)


def _mm_nn(a, w, *, tm, tn, name, out_dtype=F32, res=None):
    s, k = a.shape
    tm = min(tm, s)
    n = w.shape[1]
    return _mm(a, w, dims=NN, grid=(s // tm, n // tn),
               a_spec=pl.BlockSpec((tm, k), lambda m, j: (m, 0)),
               b_spec=pl.BlockSpec((k, tn), lambda m, j: (0, j)),
               o_spec=pl.BlockSpec((tm, tn), lambda m, j: (m, j)),
               res=res, res_spec=pl.BlockSpec((tm, tn), lambda m, j: (m, j)),
               out_shape=(s, n), out_dtype=out_dtype, name=name)


def _mm_nt(a, w, *, tm, tn, name, out_dtype=F32):
    s, k = a.shape
    tm = min(tm, s)
    n = w.shape[0]
    return _mm(a, w, dims=NT, grid=(s // tm, n // tn),
               a_spec=pl.BlockSpec((tm, k), lambda m, j: (m, 0)),
               b_spec=pl.BlockSpec((tn, k), lambda m, j: (j, 0)),
               o_spec=pl.BlockSpec((tm, tn), lambda m, j: (m, j)),
               out_shape=(s, n), out_dtype=out_dtype, name=name)


def _mm_tn(a, g, *, tm, tn, name, out_dtype=BF16):
    s, m = a.shape
    n = g.shape[1]
    return _mm(a, g, dims=TN, grid=(m // tm, n // tn),
               a_spec=pl.BlockSpec((s, tm), lambda i, j: (0, i)),
               b_spec=pl.BlockSpec((s, tn), lambda i, j: (0, j)),
               o_spec=pl.BlockSpec((tm, tn), lambda i, j: (i, j)),
               out_shape=(m, n), out_dtype=out_dtype, name=name)


ROW_TILE = 256


def _dep_spec():
    return pl.BlockSpec((SUBLANE, LANE), lambda i: (0, 0))


def _rms_fwd(x, g_row, *, want_f32, name, dep=None):
    s, d = x.shape
    n_in = 2 if dep is None else 3

    def body(*refs):
        x_ref, g_ref = refs[:2]
        outs = refs[n_in:]
        xv = x_ref[...]
        r = lax.rsqrt(jnp.mean(xv * xv, axis=-1, keepdims=True) + RMS_EPS)
        h = (xv * r) * g_ref[...]
        outs[0][...] = h.astype(BF16)
        if want_f32:
            outs[1][...] = h

    row = pl.BlockSpec((ROW_TILE, d), lambda i: (i, 0))
    out_shape = [_sds((s, d), BF16)] + ([_sds((s, d), F32)] if want_f32 else [])
    out = pl.pallas_call(
        body, name=name, grid=(s // ROW_TILE,),
        in_specs=[row, pl.BlockSpec((1, d), lambda i: (0, 0))] + ([] if dep is None else [_dep_spec()]),
        out_specs=[row] * len(out_shape), out_shape=out_shape, compiler_params=_cparams("parallel"),
    )(x, g_row, *(() if dep is None else (dep,)))
    return out if want_f32 else (out[0], None)


def _rms_bwd(x, g_row, dh, dres, *, name, dep=None):
    s, d = x.shape

    def body(x_ref, g_ref, dh_ref, dres_ref, *rest):
        dx_ref, dg_ref = rest[-2:]
        xv = x_ref[...]
        r = lax.rsqrt(jnp.mean(xv * xv, axis=-1, keepdims=True) + RMS_EPS)
        xn = xv * r
        dhv = dh_ref[...]
        dxn = dhv * g_ref[...]
        dx_ref[...] = dres_ref[...] + r * (dxn - xn * jnp.mean(dxn * xn, axis=-1, keepdims=True))
        part = jnp.sum(dhv * xn, axis=0, keepdims=True)

        @pl.when(pl.program_id(0) == 0)
        def _():
            dg_ref[...] = part

        @pl.when(pl.program_id(0) > 0)
        def _():
            dg_ref[...] += part

    row = pl.BlockSpec((ROW_TILE, d), lambda i: (i, 0))
    vec = pl.BlockSpec((1, d), lambda i: (0, 0))
    return pl.pallas_call(
        body, name=name, grid=(s // ROW_TILE,), in_specs=[row, vec, row, row] + ([] if dep is None else [_dep_spec()]),
        out_specs=[row, vec], out_shape=[_sds((s, d), F32), _sds((1, d), F32)], compiler_params=_cparams("arbitrary"),
    )(x, g_row, dh, dres, *(() if dep is None else (dep,)))


def _shift_down(v, k):
    row = lax.broadcasted_iota(jnp.int32, v.shape, 0)
    return jnp.where(row >= k, pltpu.roll(v, k, 0), 0.0)


def _shift_up(v, k):
    n = v.shape[0]
    row = lax.broadcasted_iota(jnp.int32, v.shape, 0)
    return jnp.where(row < n - k, pltpu.roll(v, n - k, 0), 0.0)


def _sigmoid(z):
    return 1.0 / (1.0 + jnp.exp(-z))


FF_COL_TILE = 256


def _conv3(u, w, b):
    return b + w[0:1, :] * _shift_down(u, 2) + w[1:2, :] * _shift_down(u, 1) + w[2:3, :] * u


def _conv_gate_fwd(up, conv_w, conv_b, *, name):
    s = up.shape[0]
    f = up.shape[1] // 2
    nt = f // FF_COL_TILE

    def body(uv_ref, ug_ref, wv_ref, wg_ref, bv_ref, bg_ref, a_ref):
        vc = _conv3(uv_ref[...], wv_ref[...], bv_ref[...])
        gc = _conv3(ug_ref[...], wg_ref[...], bg_ref[...])
        a_ref[...] = ((gc * _sigmoid(gc)) * vc).astype(BF16)

    def col(rows, off):
        return pl.BlockSpec((rows, FF_COL_TILE), lambda n: (0, n + off))

    return pl.pallas_call(
        body, name=name, grid=(nt,),
        in_specs=[col(s, 0), col(s, nt), col(3, 0), col(3, nt), col(1, 0), col(1, nt)],
        out_specs=col(s, 0), out_shape=_sds((s, f), BF16), compiler_params=_cparams("parallel"),
    )(up, up, conv_w, conv_w, conv_b, conv_b)


def _conv_gate_bwd(up, da, conv_w, conv_b, *, name):
    s = up.shape[0]
    f = up.shape[1] // 2
    nt = f // FF_COL_TILE

    def conv_bwd(u, w, dc):
        d0 = _shift_up(dc, 2)
        d1 = _shift_up(dc, 1)
        dup = w[0:1, :] * d0 + w[1:2, :] * d1 + w[2:3, :] * dc
        dw = jnp.concatenate([jnp.sum(u * d0, axis=0, keepdims=True), jnp.sum(u * d1, axis=0, keepdims=True),
                              jnp.sum(u * dc, axis=0, keepdims=True)], axis=0)
        return dup, dw, jnp.sum(dc, axis=0, keepdims=True)

    def body(uv_ref, ug_ref, da_ref, wv_ref, wg_ref, bv_ref, bg_ref,
             duv_ref, dug_ref, dwv_ref, dwg_ref, dbv_ref, dbg_ref):
        uv = uv_ref[...]
        ug = ug_ref[...]
        vc = _conv3(uv, wv_ref[...], bv_ref[...])
        gc = _conv3(ug, wg_ref[...], bg_ref[...])
        sg = _sigmoid(gc)
        dav = da_ref[...]
        dvc = dav * (gc * sg)
        dgc = dav * vc * (sg * (1.0 + gc * (1.0 - sg)))
        dup, dw, db = conv_bwd(uv, wv_ref[...], dvc)
        duv_ref[...] = dup.astype(BF16)
        dwv_ref[...] = dw
        dbv_ref[...] = db
        dup, dw, db = conv_bwd(ug, wg_ref[...], dgc)
        dug_ref[...] = dup.astype(BF16)
        dwg_ref[...] = dw
        dbg_ref[...] = db

    def col(rows, off):
        return pl.BlockSpec((rows, FF_COL_TILE), lambda n: (0, n + off))

    dup_v, dup_g, dw_v, dw_g, db_v, db_g = pl.pallas_call(
        body, name=name, grid=(nt,),
        in_specs=[col(s, 0), col(s, nt), col(s, 0), col(3, 0), col(3, nt), col(1, 0), col(1, nt)],
        out_specs=[col(s, 0), col(s, 0), col(3, 0), col(3, 0), col(1, 0), col(1, 0)],
        out_shape=[_sds((s, f), BF16), _sds((s, f), BF16), _sds((3, f), F32), _sds((3, f), F32),
                   _sds((1, f), F32), _sds((1, f), F32)],
        compiler_params=_cparams("parallel"),
    )(up, up, da, conv_w, conv_w, conv_b, conv_b)
    return dup_v, dup_g, jnp.concatenate([dw_v, dw_g], axis=1), jnp.concatenate([db_v, db_g], axis=1)


def _pool_counts(shape, g):
    win = jnp.left_shift(jnp.int32(2), g)
    t = lax.broadcasted_iota(jnp.int32, shape, 0)
    return win, jnp.minimum(t + 1, win).astype(F32)


def _window_sum(v, g, shift):
    for k in range(POOL_GROUPS):
        v = jnp.where(g >= k, v + shift(v, 1 << k), v)
    return v


def _pool_fwd(hf, x, w, b, scale, *, name):
    s, d = hf.shape

    def body(h_ref, x_ref, w_ref, b_ref, sc_ref, o_ref):
        g = pl.program_id(0)
        h = h_ref[...]
        _, cnt = _pool_counts(h.shape, g)
        pooled = _window_sum(h, g, _shift_down) / cnt - h
        y = jnp.dot(pooled.astype(BF16), w_ref[...], preferred_element_type=F32) + b_ref[...]
        o_ref[...] = x_ref[...] + y * sc_ref[...]

    col = pl.BlockSpec((s, POOL_DIM), lambda g: (0, g))
    vec = pl.BlockSpec((1, POOL_DIM), lambda g: (0, g))
    return pl.pallas_call(
        body, name=name, grid=(POOL_GROUPS,),
        in_specs=[col, col, pl.BlockSpec((None, POOL_DIM, POOL_DIM), lambda g: (g, 0, 0)), vec, vec],
        out_specs=col, out_shape=_sds((s, d), F32), compiler_params=_cparams("parallel"),
    )(hf, x, w, b, scale)


def _pool_bwd(hf, dm, w, b, scale, *, name):
    s, d = hf.shape

    def body(h_ref, dm_ref, w_ref, b_ref, sc_ref, dh_ref, dw_ref, db_ref, dsc_ref):
        g = pl.program_id(0)
        h = h_ref[...]
        _, cnt = _pool_counts(h.shape, g)
        pooled = (_window_sum(h, g, _shift_down) / cnt - h).astype(BF16)
        wv = w_ref[...]
        y = jnp.dot(pooled, wv, preferred_element_type=F32) + b_ref[...]
        dmv = dm_ref[...]
        dsc_ref[...] = jnp.sum(dmv * y, axis=0, keepdims=True)
        dy = dmv * sc_ref[...]
        db_ref[...] = jnp.sum(dy, axis=0, keepdims=True)
        dyb = dy.astype(BF16)
        dw_ref[...] = lax.dot_general(pooled, dyb, (TN, ((), ())), preferred_element_type=F32).astype(BF16)
        dp = lax.dot_general(dyb, wv, (NT, ((), ())), preferred_element_type=F32)
        dh_ref[...] = _window_sum(dp / cnt, g, _shift_up) - dp

    col = pl.BlockSpec((s, POOL_DIM), lambda g: (0, g))
    vec = pl.BlockSpec((1, POOL_DIM), lambda g: (0, g))
    mat = pl.BlockSpec((None, POOL_DIM, POOL_DIM), lambda g: (g, 0, 0))
    return pl.pallas_call(
        body, name=name, grid=(POOL_GROUPS,), in_specs=[col, col, mat, vec, vec], out_specs=[col, mat, vec, vec],
        out_shape=[_sds((s, d), F32), _sds((POOL_GROUPS, POOL_DIM, POOL_DIM), BF16), _sds((1, d), F32),
                   _sds((1, d), F32)],
        compiler_params=_cparams("parallel"),
    )(hf, dm, w, b, scale)


ATT_TQ = 256
ATT_TK = 256


def _qk_norm_fwd(qkv, gains, *, name):
    s = qkv.shape[0]

    def body(x_ref, g_ref, o_ref):
        xv = x_ref[...]
        r = lax.rsqrt(jnp.mean(xv * xv, axis=-1, keepdims=True) + RMS_EPS)
        o_ref[...] = ((xv * r) * g_ref[...]).astype(BF16)

    blk = pl.BlockSpec((s, HEAD_DIM), lambda hd: (0, hd))
    return pl.pallas_call(
        body, name=name, grid=(2 * HEADS,),
        in_specs=[blk, pl.BlockSpec((None, 1, HEAD_DIM), lambda hd: (hd // HEADS, 0, 0))],
        out_specs=blk, out_shape=_sds((s, 2 * HEADS * HEAD_DIM), BF16), compiler_params=_cparams("parallel"),
    )(qkv, gains)


def _qk_norm_bwd(qkv, gains, dn, *, which, name):
    s = qkv.shape[0]

    def body(x_ref, g_ref, dn_ref, dx_ref, dg_ref):
        xv = x_ref[...]
        r = lax.rsqrt(jnp.mean(xv * xv, axis=-1, keepdims=True) + RMS_EPS)
        xn = xv * r
        dnv = dn_ref[...]
        dxn = dnv * g_ref[...]
        dx_ref[...] = (r * (dxn - xn * jnp.mean(dxn * xn, axis=-1, keepdims=True))).astype(BF16)
        part = jnp.sum(dnv * xn, axis=0, keepdims=True)

        @pl.when(pl.program_id(0) == 0)
        def _():
            dg_ref[...] = part

        @pl.when(pl.program_id(0) > 0)
        def _():
            dg_ref[...] += part

    blk = pl.BlockSpec((s, HEAD_DIM), lambda hd: (0, hd))
    return pl.pallas_call(
        body, name=name, grid=(HEADS,),
        in_specs=[pl.BlockSpec((s, HEAD_DIM), lambda hd: (0, hd + which * HEADS)),
                  pl.BlockSpec((None, 1, HEAD_DIM), lambda hd: (which, 0, 0)), blk],
        out_specs=[blk, pl.BlockSpec((1, HEAD_DIM), lambda hd: (0, 0))],
        out_shape=[_sds((s, HEADS * HEAD_DIM), BF16), _sds((1, HEAD_DIM), F32)],
        compiler_params=_cparams("arbitrary"),
    )(qkv, gains, dn)


def _split_dot(v, tri):
    hi = v.astype(BF16)
    lo = (v - hi.astype(F32)).astype(BF16)
    return (jnp.dot(hi, tri, preferred_element_type=F32) + jnp.dot(lo, tri, preferred_element_type=F32))


def _causal_mask(qi, j):
    tpos = qi * ATT_TQ + lax.broadcasted_iota(jnp.int32, (ATT_TQ, ATT_TK), 0)
    spos = j * ATT_TK + lax.broadcasted_iota(jnp.int32, (ATT_TQ, ATT_TK), 1)
    return spos < tpos


def _att_tile(q, kj, qi, j):
    z = lax.dot_general(q, kj, (NT, ((), ())), preferred_element_type=F32) * (1.0 / math.sqrt(HEAD_DIM))
    mask = _causal_mask(qi, j)
    lb = jnp.minimum(z, 0.0) - jnp.log1p(jnp.exp(-jnp.abs(z)))
    l1m = jnp.where(mask, lb - z, 0.0)
    return lb, l1m, mask


def _tri(rel):
    r = lax.broadcasted_iota(jnp.int32, (ATT_TK, ATT_TK), 0)
    c = lax.broadcasted_iota(jnp.int32, (ATT_TK, ATT_TK), 1)
    return jnp.where(rel(r, c), 1.0, 0.0).astype(BF16)


def _sb_fwd(qkn, vb, *, name):
    s = vb.shape[0]

    def body(q_ref, k_ref, v_ref, o_ref):
        qi = pl.program_id(1)
        q = q_ref[...]
        after = _tri(lambda r, c: r > c)

        def step(t, carry):
            acc, run = carry
            j = qi - t
            rows = pl.ds(pl.multiple_of(j * ATT_TK, ATT_TK), ATT_TK)
            lb, l1m, mask = _att_tile(q, k_ref[rows, :], qi, j)
            remain = _split_dot(l1m, after) + run
            attn = jnp.where(mask, jnp.exp(lb + remain), 0.0)
            acc = acc + jnp.dot(attn.astype(BF16), v_ref[rows, :], preferred_element_type=F32)
            return acc, run + jnp.sum(l1m, axis=1, keepdims=True)

        acc, _ = lax.fori_loop(0, qi + 1, step, (jnp.zeros((ATT_TQ, HEAD_DIM), F32), jnp.zeros((ATT_TQ, 1), F32)))
        o_ref[...] = acc.astype(BF16)

    return pl.pallas_call(
        body, name=name, grid=(HEADS, s // ATT_TQ),
        in_specs=[pl.BlockSpec((ATT_TQ, HEAD_DIM), lambda hd, i: (i, hd)),
                  pl.BlockSpec((s, HEAD_DIM), lambda hd, i: (0, hd + HEADS)),
                  pl.BlockSpec((s, HEAD_DIM), lambda hd, i: (0, hd))],
        out_specs=pl.BlockSpec((ATT_TQ, HEAD_DIM), lambda hd, i: (i, hd)),
        out_shape=_sds((s, HEADS * HEAD_DIM), BF16), compiler_params=_cparams("parallel", "parallel"),
    )(qkn, qkn, vb)


def _sb_bwd(qkn, vb, dob, *, name):
    s = vb.shape[0]
    nkb = s // ATT_TK

    def body(q_ref, k_ref, v_ref, do_ref, dq_ref, dk_ref, dv_ref, a_buf, sig_buf):
        qi = pl.program_id(1)
        q = q_ref[...]
        do = do_ref[...]
        after = _tri(lambda r, c: r > c)
        before = _tri(lambda r, c: r < c)

        @pl.when(qi == 0)
        def _():
            dk_ref[...] = jnp.zeros_like(dk_ref)
            dv_ref[...] = jnp.zeros_like(dv_ref)

        def down(t, run):
            j = qi - t
            rows = pl.ds(pl.multiple_of(j * ATT_TK, ATT_TK), ATT_TK)
            lb, l1m, mask = _att_tile(q, k_ref[rows, :], qi, j)
            remain = _split_dot(l1m, after) + run
            a_buf[j] = jnp.where(mask, jnp.exp(lb + remain), 0.0)
            sig_buf[j] = jnp.exp(lb)
            return run + jnp.sum(l1m, axis=1, keepdims=True)

        lax.fori_loop(0, qi + 1, down, jnp.zeros((ATT_TQ, 1), F32))

        def up(j, carry):
            dq, run = carry
            rows = pl.ds(pl.multiple_of(j * ATT_TK, ATT_TK), ATT_TK)
            a = a_buf[j]
            sig = sig_buf[j]
            mask = _causal_mask(qi, j)
            da = lax.dot_general(do, v_ref[rows, :], (NT, ((), ())), preferred_element_type=F32)
            p = a * da
            c = _split_dot(p, before) + run
            dz = jnp.where(mask, p * (1.0 - sig) - c * sig, 0.0) * (1.0 / math.sqrt(HEAD_DIM))
            dzb = dz.astype(BF16)
            dq = dq + jnp.dot(dzb, k_ref[rows, :], preferred_element_type=F32)
            dk_ref[rows, :] += lax.dot_general(dzb, q, (TN, ((), ())), preferred_element_type=F32)
            dv_ref[rows, :] += lax.dot_general(a.astype(BF16), do, (TN, ((), ())), preferred_element_type=F32)
            return dq, run + jnp.sum(p, axis=1, keepdims=True)

        dq, _ = lax.fori_loop(0, qi + 1, up, (jnp.zeros((ATT_TQ, HEAD_DIM), F32), jnp.zeros((ATT_TQ, 1), F32)))
        dq_ref[...] = dq

    qblk = pl.BlockSpec((ATT_TQ, HEAD_DIM), lambda hd, i: (i, hd))
    full = pl.BlockSpec((s, HEAD_DIM), lambda hd, i: (0, hd))
    return pl.pallas_call(
        body, name=name, grid=(HEADS, s // ATT_TQ),
        in_specs=[qblk, pl.BlockSpec((s, HEAD_DIM), lambda hd, i: (0, hd + HEADS)), full, qblk],
        out_specs=[qblk, full, full],
        out_shape=[_sds((s, HEADS * HEAD_DIM), F32)] * 3,
        scratch_shapes=[pltpu.VMEM((nkb, ATT_TQ, ATT_TK), F32), pltpu.VMEM((nkb, ATT_TQ, ATT_TK), F32)],
        compiler_params=_cparams("parallel", "arbitrary"),
    )(qkn, qkn, vb, dob)


def _ssm_discretize(lam_re, lam_im, log_step, bt_re, bt_im):
    step = jnp.exp(log_step)
    mag = jnp.exp(lam_re * step)
    lb_re = mag * jnp.cos(lam_im * step)
    lb_im = mag * jnp.sin(lam_im * step)
    den = lam_re * lam_re + lam_im * lam_im
    f_re = ((lb_re - 1.0) * lam_re + lb_im * lam_im) / den
    f_im = (lb_im * lam_re - (lb_re - 1.0) * lam_im) / den
    return lb_re, lb_im, f_re * bt_re - f_im * bt_im, f_re * bt_im + f_im * bt_re


_SSM_LAM = (SSM_GROUPS, 1, SSM_STATE)
_SSM_STEP = (SSM_GROUPS, 1, 1)
_SSM_BT = (SSM_GROUPS, SSM_CH, SSM_STATE)


def _ssm_prep_fwd(lam_re, lam_im, log_step, bt_re, bt_im, *, name):
    def body(lr, li, ls, br, bi, o_ar, o_ai, o_br, o_bi):
        o_ar[...], o_ai[...], o_br[...], o_bi[...] = _ssm_discretize(lr[...], li[...], ls[...], br[...], bi[...])

    return pl.pallas_call(
        body, name=name, out_shape=[_sds(_SSM_LAM, F32), _sds(_SSM_LAM, F32), _sds(_SSM_BT, F32), _sds(_SSM_BT, F32)],
    )(lam_re, lam_im, log_step, bt_re, bt_im)


def _ssm_prep_bwd(lam_re, lam_im, log_step, bt_re, bt_im, d_ar, d_ai, d_br, d_bi, *, name):
    def body(lr, li, ls, br, bi, g_ar, g_ai, g_br, g_bi, o_lr, o_li, o_ls, o_br, o_bi):
        _, vjp = jax.vjp(_ssm_discretize, lr[...], li[...], ls[...], br[...], bi[...])
        o_lr[...], o_li[...], o_ls[...], o_br[...], o_bi[...] = vjp((g_ar[...], g_ai[...], g_br[...], g_bi[...]))

    return pl.pallas_call(
        body, name=name,
        out_shape=[_sds(_SSM_LAM, F32), _sds(_SSM_LAM, F32), _sds(_SSM_STEP, F32), _sds(_SSM_BT, F32), _sds(_SSM_BT, F32)],
    )(lam_re, lam_im, log_step, bt_re, bt_im, d_ar, d_ai, d_br, d_bi)


def _bd_masks():
    rowg = lax.broadcasted_iota(jnp.int32, (LANE, LANE), 0) // SSM_CH
    low = lax.broadcasted_iota(jnp.int32, (LANE, LANE), 1) < SSM_STATE
    return rowg, low


def _bd_expand(w):
    rowg, low = _bd_masks()
    high = jnp.logical_not(low)
    wr = pltpu.roll(w, SSM_STATE, 1)
    re = [jnp.where((rowg == 2 * k) & low, w, 0.0) + jnp.where((rowg == 2 * k + 1) & high, wr, 0.0) for k in range(4)]
    im = [jnp.where((rowg == 2 * k) & low, wr, 0.0) + jnp.where((rowg == 2 * k + 1) & high, w, 0.0) for k in range(4)]
    return jnp.concatenate(re + im, axis=1)


def _bd_extract(dbd):
    rowg, low = _bd_masks()
    high = jnp.logical_not(low)
    acc = jnp.zeros((LANE, LANE), F32)
    for k in range(4):
        c = dbd[:, LANE * k:LANE * (k + 1)]
        acc = acc + jnp.where((rowg == 2 * k) & low, c, 0.0) + jnp.where((rowg == 2 * k + 1) & low, pltpu.roll(c, SSM_STATE, 1), 0.0)
        c = dbd[:, LANE * (4 + k):LANE * (5 + k)]
        acc = acc + jnp.where((rowg == 2 * k) & high, pltpu.roll(c, SSM_STATE, 1), 0.0) + jnp.where((rowg == 2 * k + 1) & high, c, 0.0)
    return acc


def _cmul(ar, ai, br, bi):
    return ar * br - ai * bi, ar * bi + ai * br


def _scan_rows(xr, xi, ar, ai, *, reverse):
    n = xr.shape[0] // SUBLANE
    lanes = xr.shape[1]
    row = lax.broadcasted_iota(jnp.int32, (SUBLANE, lanes), 0)
    powers = [(ar, ai)]
    for _ in range(SUBLANE - 1):
        powers.append(_cmul(*powers[-1], ar, ai))
    pr = jnp.zeros((SUBLANE, lanes), F32)
    pi = jnp.zeros((SUBLANE, lanes), F32)
    for r in range(SUBLANE):
        e = (SUBLANE - 1 - r) if reverse else r
        pr = jnp.where(row == r, powers[e][0], pr)
        pi = jnp.where(row == r, powers[e][1], pi)

    def shift(v, d):
        if reverse:
            return jnp.where(row < SUBLANE - d, pltpu.roll(v, SUBLANE - d, 0), 0.0)
        return jnp.where(row >= d, pltpu.roll(v, d, 0), 0.0)

    def body(i, carry):
        cr, ci = carry
        g = (n - 1 - i) if reverse else i
        rows = pl.ds(pl.multiple_of(g * SUBLANE, SUBLANE), SUBLANE)
        br = xr[rows, :]
        bi = xi[rows, :]
        for d in (1, 2, 4):
            qr, qi = powers[d - 1]
            sr = shift(br, d)
            si = shift(bi, d)
            br, bi = br + qr * sr - qi * si, bi + qr * si + qi * sr
        br, bi = br + pr * cr - pi * ci, bi + pr * ci + pi * cr
        xr[rows, :] = br
        xi[rows, :] = bi
        edge = 0 if reverse else SUBLANE - 1
        return br[edge:edge + 1, :], bi[edge:edge + 1, :]

    zero = jnp.zeros((1, lanes), F32)
    lax.fori_loop(0, n, body, (zero, zero), unroll=2)


_GELU_C = math.sqrt(2.0 / math.pi)
_GELU_A = 0.044715


def _gelu(v):
    return 0.5 * v * (1.0 + jnp.tanh(_GELU_C * (v + _GELU_A * v * v * v)))


def _gelu_grad(v):
    t = jnp.tanh(_GELU_C * (v + _GELU_A * v * v * v))
    return 0.5 * (1.0 + t) + 0.5 * v * (1.0 - t * t) * (_GELU_C * (1.0 + 3.0 * _GELU_A * v * v))


def _ssm_states(u_b16, eb, ar, ai, xr, xi):
    nl = SSM_BLOCK_LANES
    xr[...] = jnp.dot(u_b16, eb[:, :nl], preferred_element_type=F32)
    xi[...] = jnp.dot(u_b16, eb[:, nl:], preferred_element_type=F32)
    _scan_rows(xr, xi, ar, ai, reverse=False)


def _ssm_specs(s):
    col = pl.BlockSpec((s, LANE), lambda b: (0, b))
    wsm = pl.BlockSpec((LANE, LANE), lambda b: (b, 0))
    lam = pl.BlockSpec((1, SSM_BLOCK_LANES), lambda b: (0, b))
    vec = pl.BlockSpec((1, LANE), lambda b: (0, b))
    return col, wsm, lam, vec


def _ssm_core_fwd(u, wb, wc, a_re, a_im, d_row, *, name):
    s, d = u.shape
    nl = SSM_BLOCK_LANES

    def body(u_ref, wb_ref, wc_ref, ar_ref, ai_ref, d_ref, y_ref, yg_ref, xr, xi):
        uv = u_ref[...]
        eb = _bd_expand(wb_ref[...]).astype(BF16)
        ec = _bd_expand(wc_ref[...]).astype(BF16)
        _ssm_states(uv.astype(BF16), eb, ar_ref[...], ai_ref[...], xr, xi)
        y = (lax.dot_general(xr[...].astype(BF16), ec[:, :nl], (NT, ((), ())), preferred_element_type=F32)
             + lax.dot_general(xi[...].astype(BF16), ec[:, nl:], (NT, ((), ())), preferred_element_type=F32)
             + d_ref[...] * uv)
        y_ref[...] = y
        yg_ref[...] = _gelu(y).astype(BF16)

    col, wsm, lam, vec = _ssm_specs(s)
    return pl.pallas_call(
        body, name=name, grid=(d // LANE,), in_specs=[col, wsm, wsm, lam, lam, vec], out_specs=[col, col],
        out_shape=[_sds((s, d), F32), _sds((s, d), BF16)],
        scratch_shapes=[pltpu.VMEM((s, nl), F32), pltpu.VMEM((s, nl), F32)],
        compiler_params=_cparams("parallel"),
    )(u, wb, wc, a_re, a_im, d_row)


def _ssm_core_bwd(u, ylin, dyg, wb, wc, a_re, a_im, d_row, *, name):
    s, d = u.shape
    nl = SSM_BLOCK_LANES
    n8 = s // SUBLANE

    def body(u_ref, y_ref, dyg_ref, wb_ref, wc_ref, ar_ref, ai_ref, d_ref,
             du_ref, dwb_ref, dwc_ref, dar_ref, dai_ref, dd_ref, xr, xi, gr, gi):
        uv = u_ref[...]
        ub = uv.astype(BF16)
        ar = ar_ref[...]
        ai = ai_ref[...]
        dy = dyg_ref[...] * _gelu_grad(y_ref[...])
        dd_ref[...] = jnp.sum(dy * uv, axis=0, keepdims=True)
        dyb = dy.astype(BF16)
        eb = _bd_expand(wb_ref[...]).astype(BF16)
        ec = _bd_expand(wc_ref[...]).astype(BF16)
        _ssm_states(ub, eb, ar, ai, xr, xi)
        dec = jnp.concatenate(
            [lax.dot_general(dyb, xr[...].astype(BF16), (TN, ((), ())), preferred_element_type=F32),
             lax.dot_general(dyb, xi[...].astype(BF16), (TN, ((), ())), preferred_element_type=F32)], axis=1)
        dwc_ref[...] = _bd_extract(dec)
        gr[...] = jnp.dot(dyb, ec[:, :nl], preferred_element_type=F32)
        gi[...] = jnp.dot(dyb, ec[:, nl:], preferred_element_type=F32)
        _scan_rows(gr, gi, ar, -ai, reverse=True)

        row = lax.broadcasted_iota(jnp.int32, (SUBLANE, nl), 0)

        def lam_grad(i, acc):
            acc_r, acc_i = acc
            rows = pl.ds(pl.multiple_of(i * SUBLANE, SUBLANE), SUBLANE)
            prev = pl.ds(pl.multiple_of(jnp.maximum(i - 1, 0) * SUBLANE, SUBLANE), SUBLANE)
            keep = jnp.where(i > 0, 1.0, 0.0)
            xpr = jnp.where(row == 0, pltpu.roll(xr[prev, :], 1, 0) * keep, pltpu.roll(xr[rows, :], 1, 0))
            xpi = jnp.where(row == 0, pltpu.roll(xi[prev, :], 1, 0) * keep, pltpu.roll(xi[rows, :], 1, 0))
            g_r = gr[rows, :]
            g_i = gi[rows, :]
            return acc_r + g_r * xpr + g_i * xpi, acc_i + g_i * xpr - g_r * xpi

        zero = jnp.zeros((SUBLANE, nl), F32)
        acc_r, acc_i = lax.fori_loop(0, n8, lam_grad, (zero, zero), unroll=2)
        dar_ref[...] = jnp.sum(acc_r, axis=0, keepdims=True)
        dai_ref[...] = jnp.sum(acc_i, axis=0, keepdims=True)

        grb = gr[...].astype(BF16)
        gib = gi[...].astype(BF16)
        deb = jnp.concatenate([lax.dot_general(ub, grb, (TN, ((), ())), preferred_element_type=F32),
                               lax.dot_general(ub, gib, (TN, ((), ())), preferred_element_type=F32)], axis=1)
        dwb_ref[...] = _bd_extract(deb)
        du_ref[...] = (lax.dot_general(grb, eb[:, :nl], (NT, ((), ())), preferred_element_type=F32)
                       + lax.dot_general(gib, eb[:, nl:], (NT, ((), ())), preferred_element_type=F32)
                       + d_ref[...] * dy)

    col, wsm, lam, vec = _ssm_specs(s)
    return pl.pallas_call(
        body, name=name, grid=(d // LANE,), in_specs=[col, col, col, wsm, wsm, lam, lam, vec],
        out_specs=[col, wsm, wsm, lam, lam, vec],
        out_shape=[_sds((s, d), F32), _sds((d, LANE), F32), _sds((d, LANE), F32),
                   _sds((1, SSM_GROUPS * SSM_STATE), F32), _sds((1, SSM_GROUPS * SSM_STATE), F32), _sds((1, d), F32)],
        scratch_shapes=[pltpu.VMEM((s, nl), F32)] * 4,
        compiler_params=_cparams("parallel"),
    )(u, ylin, dyg, wb, wc, a_re, a_im, d_row)


GLU_PIECE = 512


def _glu_fwd(yg, wp, b_row, x, *, tm, name):
    s, d = yg.shape
    tm = min(tm, s)
    half = N_DEV // 2

    def body(y_ref, wv_ref, wg_ref, bv_ref, bg_ref, x_ref, o_ref, val_ref, gate_ref):
        yv = y_ref[...]
        val = jnp.dot(yv, wv_ref[...], preferred_element_type=F32) + bv_ref[...]
        gate = jnp.dot(yv, wg_ref[...], preferred_element_type=F32) + bg_ref[...]
        val_ref[...] = val
        gate_ref[...] = gate
        o_ref[...] = x_ref[...] + val * _sigmoid(gate)

    blk = pl.BlockSpec((tm, GLU_PIECE), lambda m, n: (m, n))
    return pl.pallas_call(
        body, name=name, grid=(s // tm, half),
        in_specs=[pl.BlockSpec((tm, d), lambda m, n: (m, 0)),
                  pl.BlockSpec((None, d, GLU_PIECE), lambda m, n: (n, 0, 0)),
                  pl.BlockSpec((None, d, GLU_PIECE), lambda m, n: (n + half, 0, 0)),
                  pl.BlockSpec((1, GLU_PIECE), lambda m, n: (0, n)),
                  pl.BlockSpec((1, GLU_PIECE), lambda m, n: (0, n + half)), blk],
        out_specs=[blk, blk, blk], out_shape=[_sds((s, d), F32)] * 3,
        compiler_params=_cparams("parallel", "parallel"),
    )(yg, wp, wp, b_row, b_row, x)


def _glu_bwd(dout, val, gate, *, name):
    s, d = dout.shape

    def body(do_ref, val_ref, gate_ref, dgv_ref, db_ref):
        sg = _sigmoid(gate_ref[...])
        dov = do_ref[...]
        dgv = jnp.concatenate([dov * sg, dov * val_ref[...] * (sg * (1.0 - sg))], axis=1)
        dgv_ref[...] = dgv.astype(BF16)
        part = jnp.sum(dgv, axis=0, keepdims=True)

        @pl.when(pl.program_id(0) == 0)
        def _():
            db_ref[...] = part

        @pl.when(pl.program_id(0) > 0)
        def _():
            db_ref[...] += part

    row = pl.BlockSpec((ROW_TILE, d), lambda i: (i, 0))
    return pl.pallas_call(
        body, name=name, grid=(s // ROW_TILE,), in_specs=[row, row, row],
        out_specs=[pl.BlockSpec((ROW_TILE, 2 * d), lambda i: (i, 0)), pl.BlockSpec((1, 2 * d), lambda i: (0, 0))],
        out_shape=[_sds((s, 2 * d), BF16), _sds((1, 2 * d), F32)], compiler_params=_cparams("arbitrary"),
    )(dout, val, gate)


def _loss_head(y, target, *, name):
    s, d = y.shape

    def body(y_ref, t_ref, dy_ref, l_ref):
        e = y_ref[...] - t_ref[...]
        dy_ref[...] = e * (1.0 / d)
        part = jnp.zeros((SUBLANE, LANE), F32) + jnp.sum(e * e) * (0.5 / d)

        @pl.when(pl.program_id(0) == 0)
        def _():
            l_ref[...] = part

        @pl.when(pl.program_id(0) > 0)
        def _():
            l_ref[...] += part

    row = pl.BlockSpec((ROW_TILE, d), lambda i: (i, 0))
    return pl.pallas_call(
        body, name=name, grid=(s // ROW_TILE,), in_specs=[row, row],
        out_specs=[row, pl.BlockSpec((SUBLANE, LANE), lambda i: (0, 0))],
        out_shape=[_sds((s, d), F32), _sds((SUBLANE, LANE), F32)], compiler_params=_cparams("arbitrary"),
    )(y, target)


def _adamw_math(w, g, m, v):
    m = ADAM_B1 * m + (1.0 - ADAM_B1) * g
    v = ADAM_B2 * v + (1.0 - ADAM_B2) * (g * g)
    m_hat = m / (1.0 - ADAM_B1 ** ADAM_STEP)
    v_hat = v / (1.0 - ADAM_B2 ** ADAM_STEP)
    return -ADAM_LR * (m_hat / (jnp.sqrt(v_hat) + ADAM_EPS) + ADAM_WD * w), m, v


ADAM_ROWS = 64
PACK_ROWS = 64


def _sum_pieces(p_ref):
    g = p_ref[0].astype(F32)
    for k in range(1, N_DEV):
        g = g + p_ref[k].astype(F32)
    return g


def _adamw_pieces(w, pieces, m, v, *, name):
    r, c = w.shape

    def body(w_ref, p_ref, m_ref, v_ref, g_ref, d_ref, nm_ref, nv_ref):
        g = _sum_pieces(p_ref)
        g_ref[...] = g
        d_ref[...], nm_ref[...], nv_ref[...] = _adamw_math(w_ref[...], g, m_ref[...], v_ref[...])

    blk = pl.BlockSpec((ADAM_ROWS, c), lambda i: (i, 0))
    return pl.pallas_call(
        body, name=name, grid=(r // ADAM_ROWS,),
        in_specs=[blk, pl.BlockSpec((N_DEV, ADAM_ROWS, c), lambda i: (0, i, 0)), blk, blk],
        out_specs=[blk] * 4, out_shape=[_sds((r, c), F32)] * 4, compiler_params=_cparams("parallel"),
    )(w, pieces, m, v)


def _sum_parts(parts, *, name):
    _, r, c = parts.shape

    def body(p_ref, o_ref):
        o_ref[...] = _sum_pieces(p_ref)

    return pl.pallas_call(
        body, name=name, grid=(r // PACK_ROWS,),
        in_specs=[pl.BlockSpec((N_DEV, PACK_ROWS, c), lambda i: (0, i, 0))],
        out_specs=pl.BlockSpec((PACK_ROWS, c), lambda i: (i, 0)), out_shape=_sds((r, c), F32),
        compiler_params=_cparams("parallel"),
    )(parts)


def _adamw_flat(w, g, m, v, *, name):
    r, c = w.shape

    def body(w_ref, g_ref, m_ref, v_ref, d_ref, nm_ref, nv_ref):
        d_ref[...], nm_ref[...], nv_ref[...] = _adamw_math(w_ref[...], g_ref[...], m_ref[...], v_ref[...])

    blk = pl.BlockSpec((PACK_ROWS, c), lambda i: (i, 0))
    return pl.pallas_call(
        body, name=name, grid=(r // PACK_ROWS,), in_specs=[blk] * 4, out_specs=[blk] * 3,
        out_shape=[_sds((r, c), F32)] * 3, compiler_params=_cparams("parallel"),
    )(w, g, m, v)


_ANY = pl.BlockSpec(memory_space=pl.ANY)


def _place():
    return lax.axis_index("x"), lax.axis_index("y"), lax.axis_index("c")


def _slot(px, py, pc):
    return 4 * px + 2 * py + pc


def _all_gather(xs, *, name):
    n = len(xs)

    def body(*refs):
        ins, outs = refs[:n], refs[n:2 * n]
        send_sems, recv_sems, local_sems = refs[2 * n:]
        x, y, c = _place()
        me, sibling = (x, y, c), (x, y, 1 - c)
        chips = [(1 - x, y), (x, 1 - y), (1 - x, 1 - y)]

        def copy(i, k, block, to, src=None):
            rows = outs[i].at[_slot(*block)]
            return pltpu.make_async_remote_copy(
                src_ref=rows if src is None else src, dst_ref=rows, send_sem=send_sems.at[i, k],
                recv_sem=recv_sems.at[i, k], device_id=to, device_id_type=MESH)

        mine = [pltpu.make_async_copy(ins[i], outs[i].at[_slot(*me)], local_sems.at[i]) for i in range(n)]
        for cp in mine:
            cp.start()
        first = []
        for i in range(n):
            first.append(copy(i, 0, me, sibling, src=ins[i]))
            first += [copy(i, 1 + j, me, (*chip, c), src=ins[i]) for j, chip in enumerate(chips)]
        for cp in first:
            cp.start()
        passed = []
        for j, chip in enumerate(chips):
            for i in range(n):
                copy(i, 1 + j, (*chip, c), me).wait_recv()
                cp = copy(i, 4 + j, (*chip, c), sibling)
                cp.start()
                passed.append(cp)
        for i in range(n):
            copy(i, 0, sibling, me).wait_recv()
            for j, chip in enumerate(chips):
                copy(i, 4 + j, (*chip, 1 - c), me).wait_recv()
        for cp in first + passed:
            cp.wait_send()
        for cp in mine:
            cp.wait()

    return pl.pallas_call(
        body, name=name, in_specs=[_ANY] * n, out_specs=[_ANY] * n,
        out_shape=[_sds((N_DEV,) + a.shape, a.dtype) for a in xs],
        scratch_shapes=[pltpu.SemaphoreType.DMA((n, 7)), pltpu.SemaphoreType.DMA((n, 7)), pltpu.SemaphoreType.DMA((n,))],
    )(*xs)


def _exchange_pieces(gs, *, name):
    n = len(gs)
    flips = [(dx, dy, dc) for dx in (0, 1) for dy in (0, 1) for dc in (0, 1)][1:]

    def body(*refs):
        ins, outs = refs[:n], refs[n:2 * n]
        send_sems, recv_sems, local_sems = refs[2 * n:]
        x, y, c = _place()
        me = _slot(x, y, c)
        peers = [((1 - x) if dx else x, (1 - y) if dy else y, (1 - c) if dc else c) for dx, dy, dc in flips]

        def copy(i, k):
            return pltpu.make_async_remote_copy(
                src_ref=ins[i].at[_slot(*peers[k])], dst_ref=outs[i].at[me], send_sem=send_sems.at[i, k],
                recv_sem=recv_sems.at[i, k], device_id=peers[k], device_id_type=MESH)

        def landing(i, k):
            rows = outs[i].at[_slot(*peers[k])]
            return pltpu.make_async_remote_copy(
                src_ref=rows, dst_ref=rows, send_sem=send_sems.at[i, k], recv_sem=recv_sems.at[i, k],
                device_id=peers[k], device_id_type=MESH)

        mine = [pltpu.make_async_copy(ins[i].at[me], outs[i].at[me], local_sems.at[i]) for i in range(n)]
        for cp in mine:
            cp.start()
        sends = [copy(i, k) for i in range(n) for k in range(len(flips))]
        for cp in sends:
            cp.start()
        for i in range(n):
            for k in range(len(flips)):
                landing(i, k).wait_recv()
        for cp in sends:
            cp.wait_send()
        for cp in mine:
            cp.wait()

    return pl.pallas_call(
        body, name=name, in_specs=[_ANY] * n, out_specs=[_ANY] * n,
        out_shape=[_sds(a.shape, a.dtype) for a in gs],
        scratch_shapes=[pltpu.SemaphoreType.DMA((n, 7)), pltpu.SemaphoreType.DMA((n, 7)), pltpu.SemaphoreType.DMA((n,))],
    )(*gs)


_HBM = pl.BlockSpec(memory_space=pltpu.HBM)
_SEM = pl.BlockSpec(memory_space=pltpu.SEMAPHORE)
_EFFECT = pltpu.SideEffectType.DATAFLOW_SIDE_EFFECTING
_CHIP_FLIPS = ((1, 0), (0, 1), (1, 1))
_TOKEN = (SUBLANE, LANE)


def _flip(v, f):
    return (1 - v) if f else v


def _in_hbm(a):
    return pltpu.with_memory_space_constraint(a, pltpu.HBM)


def _ag_peers(x, y, c):
    return [(x, y, 1 - c)] + [(_flip(x, dx), _flip(y, dy), c) for dx, dy in _CHIP_FLIPS]


def _ag_copies(x_refs, land_refs, send_sems, recv_sems, *, landing):
    x, y, c = _place()
    peers = _ag_peers(x, y, c)
    cps = []
    for i in range(len(x_refs)):
        for k, peer in enumerate(peers):
            origin = _slot(*peer) if landing else _slot(x, y, c)
            cps.append(pltpu.make_async_remote_copy(
                src_ref=x_refs[i], dst_ref=land_refs[i].at[origin], send_sem=send_sems.at[4 * i + k],
                recv_sem=recv_sems.at[4 * i + k], device_id=peer, device_id_type=MESH))
    return cps


OWN_BLOCK_BYTES = 4 * 1024 * 1024


def _place_own(x, me, *, name):
    r, c = x.shape
    tr = r if r * c * x.dtype.itemsize <= OWN_BLOCK_BYTES else r // 4

    def body(me_ref, x_ref, o_ref):
        o_ref[...] = x_ref[...]

    return pl.pallas_call(
        body, name=name,
        grid_spec=pltpu.PrefetchScalarGridSpec(
            num_scalar_prefetch=1, grid=(r // tr,),
            in_specs=[pl.BlockSpec((tr, c), lambda i, me_ref: (i, 0))],
            out_specs=pl.BlockSpec((None, tr, c), lambda i, me_ref: (me_ref[0], i, 0))),
        out_shape=_sds((N_DEV, r, c), x.dtype), compiler_params=_cparams("parallel"),
    )(me, x)


def _ag_start(xs, *, name, dep=None):
    n = len(xs)
    me = _slot(*_place()).astype(jnp.int32).reshape(1)
    lands = [_place_own(a, me, name=f"{name}_own_{i}") for i, a in enumerate(xs)]
    n_in = 2 * n + (0 if dep is None else 1)

    def body(*refs):
        x_refs, land_refs = refs[:n], refs[n:2 * n]
        send_sems, recv_sems = refs[n_in], refs[n_in + 1]
        token = refs[-1]
        for cp in _ag_copies(x_refs, land_refs, send_sems, recv_sems, landing=False):
            cp.start()
        token[...] = jnp.zeros_like(token)

    out = pl.pallas_call(
        body, name=name,
        out_shape=(pltpu.SemaphoreType.DMA((4 * n,)), pltpu.SemaphoreType.DMA((4 * n,)),
                   *[pltpu.HBM(a.shape, a.dtype) for a in xs], *[pltpu.HBM(a.shape, a.dtype) for a in lands],
                   _sds(_TOKEN, F32)),
        in_specs=[_HBM] * (2 * n) + ([] if dep is None else [_ANY]),
        out_specs=(_SEM, _SEM, *[_HBM] * (2 * n), pl.BlockSpec(memory_space=pltpu.VMEM)),
        input_output_aliases={i: 2 + i for i in range(2 * n)},
        compiler_params=pltpu.CompilerParams(has_side_effects=_EFFECT),
    )(*[_in_hbm(a) for a in xs], *[_in_hbm(a) for a in lands], *(() if dep is None else (dep,)))
    return (out[0], out[1], list(out[2:2 + n]), list(out[2 + n:2 + 2 * n])), out[-1]


def _ag_wait(started, after, *, name):
    send_sems, recv_sems, xs, lands = started
    n = len(xs)

    def body(*refs):
        x_refs, land_refs = refs[:n], refs[n:2 * n]
        for cp in _ag_copies(x_refs, land_refs, refs[2 * n], refs[2 * n + 1], landing=True):
            cp.wait_send()
            cp.wait_recv()

    out = pl.pallas_call(
        body, name=name,
        out_shape=tuple(pltpu.HBM(a.shape, a.dtype) for a in xs + lands),
        in_specs=[_HBM] * (2 * n) + [_SEM, _SEM, _ANY], out_specs=tuple([_HBM] * (2 * n)),
        input_output_aliases={i: i for i in range(2 * n)},
        compiler_params=pltpu.CompilerParams(has_side_effects=_EFFECT),
    )(*xs, *lands, send_sems, recv_sems, after)
    return list(out[:n]), list(out[n:])


def _ag_forward(lands, *, name):
    n = len(lands)

    def body(*refs):
        land = refs[n:2 * n]
        send_sems, recv_sems = refs[2 * n:]
        x, y, c = _place()
        sibling = (x, y, 1 - c)

        def copy(i, j, core):
            dx, dy = _CHIP_FLIPS[j]
            rows = land[i].at[_slot(_flip(x, dx), _flip(y, dy), core)]
            return pltpu.make_async_remote_copy(src_ref=rows, dst_ref=rows, send_sem=send_sems.at[i, j],
                                                recv_sem=recv_sems.at[i, j], device_id=sibling, device_id_type=MESH)

        sends = [copy(i, j, c) for i in range(n) for j in range(3)]
        for cp in sends:
            cp.start()
        for i in range(n):
            for j in range(3):
                copy(i, j, 1 - c).wait_recv()
        for cp in sends:
            cp.wait_send()

    return pl.pallas_call(
        body, name=name, in_specs=[_ANY] * n, out_specs=[_ANY] * n,
        out_shape=[_sds(a.shape, a.dtype) for a in lands], input_output_aliases={i: i for i in range(n)},
        scratch_shapes=[pltpu.SemaphoreType.DMA((n, 3)), pltpu.SemaphoreType.DMA((n, 3))],
    )(*lands)


def _fw_copies(land_refs, send_sems, recv_sems, *, landing):
    x, y, c = _place()
    cps = []
    for i in range(len(land_refs)):
        for j, (dx, dy) in enumerate(_CHIP_FLIPS):
            rows = land_refs[i].at[_slot(_flip(x, dx), _flip(y, dy), (1 - c) if landing else c)]
            cps.append(pltpu.make_async_remote_copy(
                src_ref=rows, dst_ref=rows, send_sem=send_sems.at[3 * i + j], recv_sem=recv_sems.at[3 * i + j],
                device_id=(x, y, 1 - c), device_id_type=MESH))
    return cps


def _fw_start(lands, *, name, dep=None):
    n = len(lands)
    n_in = n + (0 if dep is None else 1)

    def body(*refs):
        for cp in _fw_copies(refs[:n], refs[n_in], refs[n_in + 1], landing=False):
            cp.start()
        refs[-1][...] = jnp.zeros_like(refs[-1])

    out = pl.pallas_call(
        body, name=name,
        out_shape=(pltpu.SemaphoreType.DMA((3 * n,)), pltpu.SemaphoreType.DMA((3 * n,)),
                   *[pltpu.HBM(a.shape, a.dtype) for a in lands], _sds(_TOKEN, F32)),
        in_specs=[_HBM] * n + ([] if dep is None else [_ANY]),
        out_specs=(_SEM, _SEM, *[_HBM] * n, pl.BlockSpec(memory_space=pltpu.VMEM)),
        input_output_aliases={i: 2 + i for i in range(n)},
        compiler_params=pltpu.CompilerParams(has_side_effects=_EFFECT),
    )(*[_in_hbm(a) for a in lands], *(() if dep is None else (dep,)))
    return (out[0], out[1], list(out[2:2 + n])), out[-1]


def _fw_wait(started, after, *, name):
    send_sems, recv_sems, lands = started
    n = len(lands)

    def body(*refs):
        for cp in _fw_copies(refs[:n], refs[n], refs[n + 1], landing=True):
            cp.wait_send()
            cp.wait_recv()

    out = pl.pallas_call(
        body, name=name,
        out_shape=tuple(pltpu.HBM(a.shape, a.dtype) for a in lands),
        in_specs=[_HBM] * n + [_SEM, _SEM, _ANY], out_specs=tuple([_HBM] * n),
        input_output_aliases={i: i for i in range(n)},
        compiler_params=pltpu.CompilerParams(has_side_effects=_EFFECT),
    )(*lands, send_sems, recv_sems, after)
    return list(out)


def _sib_copies(g_refs, land_refs, send_sems, recv_sems):
    x, y, c = _place()
    return [pltpu.make_async_remote_copy(
        src_ref=g_refs[i].at[:, 1 - c], dst_ref=land_refs[i], send_sem=send_sems.at[i], recv_sem=recv_sems.at[i],
        device_id=(x, y, 1 - c), device_id_type=MESH) for i in range(len(g_refs))]


def _sib_start(g4s, *, name):
    n = len(g4s)
    lands = [lax.empty((4,) + a.shape[2:], a.dtype) for a in g4s]

    def body(*refs):
        for cp in _sib_copies(refs[:n], refs[n:2 * n], refs[2 * n], refs[2 * n + 1]):
            cp.start()
        refs[-1][...] = jnp.zeros_like(refs[-1])

    out = pl.pallas_call(
        body, name=name,
        out_shape=(pltpu.SemaphoreType.DMA((n,)), pltpu.SemaphoreType.DMA((n,)),
                   *[pltpu.HBM(a.shape, a.dtype) for a in g4s], *[pltpu.HBM(a.shape, a.dtype) for a in lands],
                   _sds(_TOKEN, F32)),
        in_specs=[_HBM] * (2 * n),
        out_specs=(_SEM, _SEM, *[_HBM] * (2 * n), pl.BlockSpec(memory_space=pltpu.VMEM)),
        input_output_aliases={i: 2 + i for i in range(2 * n)},
        compiler_params=pltpu.CompilerParams(has_side_effects=_EFFECT),
    )(*[_in_hbm(a) for a in g4s], *[_in_hbm(a) for a in lands])
    return (out[0], out[1], list(out[2:2 + n]), list(out[2 + n:2 + 2 * n])), out[-1]


def _sib_wait(started, after, *, name):
    send_sems, recv_sems, g4s, lands = started
    n = len(g4s)

    def body(*refs):
        for cp in _sib_copies(refs[:n], refs[n:2 * n], refs[2 * n], refs[2 * n + 1]):
            cp.wait_send()
            cp.wait_recv()

    out = pl.pallas_call(
        body, name=name,
        out_shape=tuple(pltpu.HBM(a.shape, a.dtype) for a in g4s + lands),
        in_specs=[_HBM] * (2 * n) + [_SEM, _SEM, _ANY], out_specs=tuple([_HBM] * (2 * n)),
        input_output_aliases={i: i for i in range(2 * n)},
        compiler_params=pltpu.CompilerParams(has_side_effects=_EFFECT),
    )(*g4s, *lands, send_sems, recv_sems, after)
    return list(out[:n]), list(out[n:])


PAIR_SUM_BLOCK_BYTES = 3 * 1024 * 1024


def _rs_rows(r, c):
    tr = r
    while tr * c * 2 > PAIR_SUM_BLOCK_BYTES and tr % 2 == 0:
        tr //= 2
    return tr


def _rs_pair_sum(g4, from_sibling, core, *, name):
    _, _, r, c = g4.shape
    tr = _rs_rows(r, c)

    def body(core_ref, g_ref, a_ref, o_ref):
        o_ref[...] = (g_ref[...].astype(F32) + a_ref[...].astype(F32)).astype(BF16)

    blk = pl.BlockSpec((None, tr, c), lambda k, i, core_ref: (k, i, 0))
    return pl.pallas_call(
        body, name=name,
        grid_spec=pltpu.PrefetchScalarGridSpec(
            num_scalar_prefetch=1, grid=(4, r // tr),
            in_specs=[pl.BlockSpec((None, None, tr, c), lambda k, i, core_ref: (k, core_ref[0], i, 0)), blk],
            out_specs=blk),
        out_shape=_sds((4, r, c), BF16), compiler_params=_cparams("parallel", "parallel"),
    )(core, g4, from_sibling)


def _rs_copies(h_refs, land_refs, send_sems, recv_sems):
    x, y, c = _place()
    cps = []
    for i in range(len(h_refs)):
        for k, (dx, dy) in enumerate(_CHIP_FLIPS):
            px, py = _flip(x, dx), _flip(y, dy)
            cps.append(pltpu.make_async_remote_copy(
                src_ref=h_refs[i].at[2 * px + py], dst_ref=land_refs[i].at[k], send_sem=send_sems.at[3 * i + k],
                recv_sem=recv_sems.at[3 * i + k], device_id=(px, py, c), device_id_type=MESH))
    return cps


def _rs_start(hs, *, name):
    n = len(hs)
    lands = [lax.empty((3,) + a.shape[1:], a.dtype) for a in hs]

    def body(*refs):
        h_refs, land_refs = refs[:n], refs[n:2 * n]
        token = refs[-1]
        for cp in _rs_copies(h_refs, land_refs, refs[2 * n], refs[2 * n + 1]):
            cp.start()
        token[...] = jnp.zeros_like(token)

    out = pl.pallas_call(
        body, name=name,
        out_shape=(pltpu.SemaphoreType.DMA((3 * n,)), pltpu.SemaphoreType.DMA((3 * n,)),
                   *[pltpu.HBM(a.shape, a.dtype) for a in hs], *[pltpu.HBM(a.shape, a.dtype) for a in lands],
                   _sds(_TOKEN, F32)),
        in_specs=[_HBM] * (2 * n),
        out_specs=(_SEM, _SEM, *[_HBM] * (2 * n), pl.BlockSpec(memory_space=pltpu.VMEM)),
        input_output_aliases={i: 2 + i for i in range(2 * n)},
        compiler_params=pltpu.CompilerParams(has_side_effects=_EFFECT),
    )(*[_in_hbm(a) for a in hs], *[_in_hbm(a) for a in lands])
    return (out[0], out[1], list(out[2:2 + n]), list(out[2 + n:2 + 2 * n])), out[-1]


def _rs_wait(started, after, *, name):
    send_sems, recv_sems, hs, lands = started
    n = len(hs)

    def body(*refs):
        for cp in _rs_copies(refs[:n], refs[n:2 * n], refs[2 * n], refs[2 * n + 1]):
            cp.wait_send()
            cp.wait_recv()

    out = pl.pallas_call(
        body, name=name,
        out_shape=tuple(pltpu.HBM(a.shape, a.dtype) for a in hs + lands),
        in_specs=[_HBM] * (2 * n) + [_SEM, _SEM, _ANY], out_specs=tuple([_HBM] * (2 * n)),
        input_output_aliases={i: i for i in range(2 * n)},
        compiler_params=pltpu.CompilerParams(has_side_effects=_EFFECT),
    )(*hs, *lands, send_sems, recv_sems, after)
    return list(out[:n]), list(out[n:])


def _adamw_chips(w, h, others, m, v, chip, *, name):
    r, c = w.shape

    def body(chip_ref, w_ref, h_ref, o_ref, m_ref, v_ref, g_ref, d_ref, nm_ref, nv_ref):
        g = h_ref[...].astype(F32)
        for k in range(3):
            g = g + o_ref[k].astype(F32)
        g_ref[...] = g
        d_ref[...], nm_ref[...], nv_ref[...] = _adamw_math(w_ref[...], g, m_ref[...], v_ref[...])

    blk = pl.BlockSpec((ADAM_ROWS, c), lambda i, chip_ref: (i, 0))
    return pl.pallas_call(
        body, name=name,
        grid_spec=pltpu.PrefetchScalarGridSpec(
            num_scalar_prefetch=1, grid=(r // ADAM_ROWS,),
            in_specs=[blk, pl.BlockSpec((None, ADAM_ROWS, c), lambda i, chip_ref: (chip_ref[0], i, 0)),
                      pl.BlockSpec((3, ADAM_ROWS, c), lambda i, chip_ref: (0, i, 0)), blk, blk],
            out_specs=[blk] * 4),
        out_shape=[_sds((r, c), F32)] * 4, compiler_params=_cparams("parallel"),
    )(chip, w, h, others, m, v)


def _adamw_chips_layers(w, hs, others, m, v, chip, *, first, prev=None, deps=(), name):
    nl, r, c = w.shape
    n = len(hs)
    n_prev = 0 if prev is None else 4

    def body(chip_ref, w_ref, m_ref, v_ref, *rest):
        h_refs, o_refs = rest[:n], rest[n:2 * n]
        g_ref, d_ref, nm_ref, nv_ref = rest[2 * n + n_prev + len(deps):]
        k_now = pl.program_id(0)
        g = jnp.zeros((ADAM_ROWS, c), F32)
        for k in range(n):
            gk = h_refs[k][...].astype(F32)
            for j in range(3):
                gk = gk + o_refs[k][j].astype(F32)
            g = jnp.where(k_now == k, gk, g)
        g_ref[...] = g
        d_ref[...], nm_ref[...], nv_ref[...] = _adamw_math(w_ref[...], g, m_ref[...], v_ref[...])

    def rows(k):
        return lambda l, i: jnp.where(l == k, i, 0)

    blk = pl.BlockSpec((None, ADAM_ROWS, c), lambda l, i, chip_ref: (l + first, i, 0))
    h_specs = [pl.BlockSpec((None, ADAM_ROWS, c), lambda l, i, chip_ref, f=rows(k): (chip_ref[0], f(l, i), 0))
               for k in range(n)]
    o_specs = [pl.BlockSpec((3, ADAM_ROWS, c), lambda l, i, chip_ref, f=rows(k): (0, f(l, i), 0)) for k in range(n)]
    return pl.pallas_call(
        body, name=name,
        grid_spec=pltpu.PrefetchScalarGridSpec(
            num_scalar_prefetch=1, grid=(n, r // ADAM_ROWS),
            in_specs=[blk, blk, blk] + h_specs + o_specs + [_ANY] * (n_prev + len(deps)), out_specs=[blk] * 4),
        out_shape=[_sds((nl, r, c), F32)] * 4,
        input_output_aliases={4 + 2 * n + q: q for q in range(n_prev)},
        compiler_params=_cparams("arbitrary", "arbitrary"),
    )(chip, w, m, v, *hs, *others, *(() if prev is None else prev), *deps)


def _gather_begin(xs, tag, dep=None):
    return _ag_start(xs, name=f"ag_start_{tag}", dep=dep)


def _gather_end(started, after, tag):
    _, lands = _ag_wait(started, after, name=f"ag_wait_{tag}")
    return _ag_forward(lands, name=f"ag_forward_{tag}")


def _scatter_pair(gs, tag):
    g4s = [g.reshape((4, 2) + g.shape[1:]) for g in gs]
    return _sib_start(g4s, name=f"rs_sib_start_{tag}")


def _scatter_chips(pair, after, tag):
    core = lax.axis_index("c").astype(jnp.int32).reshape(1)
    g4s, got = _sib_wait(pair, after, name=f"rs_sib_wait_{tag}")
    hs = [_rs_pair_sum(g4, a, core, name=f"rs_pair_sum_{tag}_{i}") for i, (g4, a) in enumerate(zip(g4s, got))]
    return _rs_start(hs, name=f"rs_start_{tag}")


def _scatter_end(started, after, tag):
    return _rs_wait(started, after, name=f"rs_wait_{tag}")


MM_TM = 512


def _ffn_fwd(x_mid, g_row, wup_p, wdown, conv_w, conv_b, tag, dep=None):
    hb, _ = _rms_fwd(x_mid, g_row, want_f32=False, name=f"ffn_norm_{tag}", dep=dep)
    up = _mm_nn_pieces(hb, wup_p, tm=MM_TM, name=f"ffn_up_{tag}")
    a = _conv_gate_fwd(up, conv_w, conv_b, name=f"ffn_conv_{tag}")
    if callable(wdown):
        wdown = wdown(a)
    x_out = _mm_nn(a, wdown, tm=1024, tn=512, name=f"ffn_down_{tag}", res=x_mid)
    return x_out, (hb, up, a)


def _ffn_bwd(dx, x_mid, g_row, wup_p, wdown, conv_w, conv_b, saved, tag):
    hb, up, a = saved
    dxb = dx.astype(BF16)
    da = _mm_nt(dxb, wdown, tm=MM_TM, tn=1408, name=f"ffn_da_{tag}")
    dwdown = _mm_tn(a, dxb, tm=512, tn=1024, name=f"ffn_dwdown_{tag}")
    dup_v, dup_g, dconv_w, dconv_b = _conv_gate_bwd(up, da, conv_w, conv_b, name=f"ffn_dconv_{tag}")
    dwup = _mm_tn_pieces(hb, (dup_v, dup_g), pieces=N_DEV, tm=MM_TM, name=f"ffn_dwup_{tag}")
    pair, token = _scatter_pair([dwup, dwdown.reshape(N_DEV, D_FF // N_DEV, D_MODEL)], f"ffn_{tag}")
    dh = _mm_nt_pieces((dup_v, dup_g), wup_p, tm=1024, tn=1024, name=f"ffn_dh_{tag}", dep=token)
    started, token = _scatter_chips(pair, dh, f"ffn_{tag}")
    dx_mid, dg = _rms_bwd(x_mid, g_row, dh, dx, name=f"ffn_dnorm_{tag}", dep=token)
    return dx_mid, dg, dconv_w, dconv_b, started


def _pool_layer_fwd(x, g_row, w, b_row, sc_row, tag, dep=None):
    _, hf = _rms_fwd(x, g_row, want_f32=True, name=f"pool_norm_{tag}", dep=dep)
    return _pool_fwd(hf, x, w, b_row, sc_row, name=f"pool_fwd_{tag}"), (hf,)


def _pool_layer_bwd(dx_mid, x, g_row, w, b_row, sc_row, saved, tag):
    (hf,) = saved
    dh, dw, db, dsc = _pool_bwd(hf, dx_mid, w, b_row, sc_row, name=f"pool_bwd_{tag}")
    dx, dg = _rms_bwd(x, g_row, dh, dx_mid, name=f"pool_dnorm_{tag}")
    return dx, dg, dw, db, dsc


def _sb_layer_fwd(x, g_row, wqkv_p, gains, wo, tag, dep=None):
    hb, _ = _rms_fwd(x, g_row, want_f32=False, name=f"sb_norm_{tag}", dep=dep)
    qkv = _mm_nn_pieces(hb, wqkv_p, tm=MM_TM, name=f"sb_qkv_{tag}")
    qkn = _qk_norm_fwd(qkv, gains, name=f"sb_qknorm_{tag}")
    vb = qkv[:, 2 * D_MODEL:].astype(BF16)
    o = _sb_fwd(qkn, vb, name=f"sb_att_{tag}")
    x_mid = _mm_nn(o, wo, tm=MM_TM, tn=512, name=f"sb_out_{tag}", res=x)
    return x_mid, (hb, qkv, qkn, vb, o)


def _sb_layer_bwd(dx_mid, x, g_row, wqkv_p, gains, wo, saved, tag):
    hb, qkv, qkn, vb, o = saved
    dmb = dx_mid.astype(BF16)
    do = _mm_nt(dmb, wo, tm=MM_TM, tn=512, name=f"sb_do_{tag}", out_dtype=BF16)
    dwo = _mm_tn(o, dmb, tm=512, tn=1024, name=f"sb_dwo_{tag}")
    dqn, dkn, dv = _sb_bwd(qkn, vb, do, name=f"sb_datt_{tag}")
    dq, dqg = _qk_norm_bwd(qkv, gains, dqn, which=0, name=f"sb_dqnorm_{tag}")
    dk, dkg = _qk_norm_bwd(qkv, gains, dkn, which=1, name=f"sb_dknorm_{tag}")
    dqkv = jnp.concatenate([dq, dk, dv.astype(BF16)], axis=1)
    dwqkv = _mm_tn_pieces(hb, dqkv, pieces=N_DEV, tm=MM_TM, name=f"sb_dwqkv_{tag}")
    pair, token = _scatter_pair([dwqkv, dwo.reshape(N_DEV, D_MODEL // N_DEV, D_MODEL)], f"sb_{tag}")
    dh = _mm_nt_pieces(dqkv, wqkv_p, tm=MM_TM, tn=1024, name=f"sb_dh_{tag}", dep=token)
    started, token = _scatter_chips(pair, dh, f"sb_{tag}")
    dx, dg = _rms_bwd(x, g_row, dh, dx_mid, name=f"sb_dnorm_{tag}", dep=token)
    return dx, dg, dqg, dkg, started


def _ssm_params(lam_re, lam_im, log_step, b_re, b_im):
    g, p = SSM_GROUPS, SSM_STATE
    return (lam_re.reshape(g, 1, p), lam_im.reshape(g, 1, p), log_step.reshape(g, 1, 1),
            jnp.transpose(b_re, (0, 2, 1)), jnp.transpose(b_im, (0, 2, 1)))


def _ssm_layer_fwd(x, g_row, raw, c_re, c_im, d_row, wglu_p, bglu_row, tag, dep=None):
    g, p, ch = SSM_GROUPS, SSM_STATE, SSM_CH
    _, hf = _rms_fwd(x, g_row, want_f32=True, name=f"ssm_norm_{tag}", dep=dep)
    ar, ai, bbr, bbi = _ssm_prep_fwd(*raw, name=f"ssm_prep_{tag}")
    wb = jnp.concatenate([bbr.reshape(g * ch, p), bbi.reshape(g * ch, p)], axis=1)
    wc = jnp.concatenate([c_re.reshape(g * ch, p), -c_im.reshape(g * ch, p)], axis=1)
    a_re, a_im = ar.reshape(1, g * p), ai.reshape(1, g * p)
    ylin, yg = _ssm_core_fwd(hf, wb, wc, a_re, a_im, d_row, name=f"ssm_core_{tag}")
    x_mid, val, gate = _glu_fwd(yg, wglu_p, bglu_row, x, tm=MM_TM, name=f"ssm_glu_{tag}")
    return x_mid, (hf, wb, wc, a_re, a_im, ylin, yg, val, gate)


def _ssm_layer_bwd(dx_mid, x, g_row, raw, d_row, wglu_p, saved, tag):
    g, p, ch = SSM_GROUPS, SSM_STATE, SSM_CH
    hf, wb, wc, a_re, a_im, ylin, yg, val, gate = saved
    dgv, dbglu = _glu_bwd(dx_mid, val, gate, name=f"ssm_dglu_{tag}")
    dwglu = _mm_tn_pieces(yg, dgv, pieces=N_DEV, tm=MM_TM, name=f"ssm_dwglu_{tag}")
    pair, token = _scatter_pair([dwglu], f"ssm_{tag}")
    dyg = _mm_nt_pieces(dgv, wglu_p, tm=MM_TM, tn=1024, name=f"ssm_dyg_{tag}", dep=token)
    du, dwb, dwc, dar, dai, dd = _ssm_core_bwd(hf, ylin, dyg, wb, wc, a_re, a_im, d_row, name=f"ssm_dcore_{tag}")
    dc_re = dwc[:, :p].reshape(g, ch, p)
    dc_im = -dwc[:, p:].reshape(g, ch, p)
    dlr, dli, dls, dbtr, dbti = _ssm_prep_bwd(
        *raw, dar.reshape(g, 1, p), dai.reshape(g, 1, p), dwb[:, :p].reshape(g, ch, p), dwb[:, p:].reshape(g, ch, p),
        name=f"ssm_dprep_{tag}")
    started, token = _scatter_chips(pair, du, f"ssm_{tag}")
    dx, dg = _rms_bwd(x, g_row, du, dx_mid, name=f"ssm_dnorm_{tag}", dep=token)
    grads = dict(ssm_lam_re=dlr.reshape(1, g, p), ssm_lam_im=dli.reshape(1, g, p), ssm_log_step=dls.reshape(1, g),
                 ssm_b_re=jnp.transpose(dbtr, (0, 2, 1))[None], ssm_b_im=jnp.transpose(dbti, (0, 2, 1))[None],
                 ssm_c_re=dc_re[None], ssm_c_im=dc_im[None], ssm_d=dd, ssm_b_glu=dbglu)
    return dx, dg, grads, started


_WEIGHTS = ["norm_mix_g", "norm_ffn_g", "pool_w", "pool_b", "pool_scale", "sb_w_qkv", "sb_q_gain", "sb_k_gain", "sb_w_o",
            "ssm_lam_re", "ssm_lam_im", "ssm_log_step", "ssm_b_re", "ssm_b_im", "ssm_c_re", "ssm_c_im", "ssm_d",
            "ssm_w_glu", "ssm_b_glu", "ffn_w_up", "ffn_conv_w", "ffn_conv_b", "ffn_w_down"]
_INPUTS = ["x"] + _WEIGHTS + ["loss_target"] + ["m_" + n for n in _WEIGHTS] + ["v_" + n for n in _WEIGHTS]
_REPLICATED = ["norm_mix_g", "norm_ffn_g", "sb_q_gain", "sb_k_gain", "ssm_lam_re", "ssm_lam_im", "ssm_log_step",
               "ssm_b_re", "ssm_b_im", "ssm_c_re", "ssm_c_im", "ffn_conv_b"]
_SMALL_SHARDED = {"pool_b": 1, "pool_scale": 1, "ssm_d": 1, "ssm_b_glu": 1, "ffn_conv_w": 2}
_BIG = ["pool_w", "sb_w_qkv", "sb_w_o", "ssm_w_glu", "ffn_w_up", "ffn_w_down"]
PACK_COLS = 512


def _pack(arrays):
    flat = jnp.concatenate([a.reshape(-1).astype(F32) for a in arrays])
    rows = -(-flat.shape[0] // (PACK_COLS * PACK_ROWS)) * PACK_ROWS
    return jnp.pad(flat, (0, rows * PACK_COLS - flat.shape[0])).reshape(rows, PACK_COLS)


def _unpack(packed, shapes, lead=()):
    flat = packed.reshape(lead + (-1,))
    out, off = [], 0
    for shp in shapes:
        n = math.prod(shp)
        out.append(flat[..., off:off + n].reshape(lead + tuple(shp)))
        off += n
    return out


def _unshard(gathered, axis):
    g = jnp.moveaxis(gathered, 0, axis)
    shp = g.shape
    return g.reshape(shp[:axis] + (shp[axis] * shp[axis + 1],) + shp[axis + 2:])


def _step(p):
    s = p["x"].shape[1]
    x = p["x"].reshape(s, D_MODEL)
    me = _slot(*_place())

    small_local = [p[n] for n in _SMALL_SHARDED]
    pool_w_l = p["pool_w"].astype(BF16).reshape(-1, POOL_DIM)
    chip = (2 * lax.axis_index("x") + lax.axis_index("y")).astype(jnp.int32).reshape(1)

    def ffn_shards(i):
        return [p["ffn_w_up"][i].astype(BF16), p["ffn_w_down"][i].astype(BF16)]

    st_first, tok = _gather_begin([pool_w_l, _pack(small_local)], "first")
    st_ffn, tok_ffn = [None] * DEPTH, [None] * DEPTH
    st_ffn[0], tok = _gather_begin(ffn_shards(0)[:1], "ffn_0", dep=tok)
    st_down0, tok = _gather_begin(ffn_shards(0)[1:], "ffn_0_down", dep=tok)
    st_sb, tok = _gather_begin([p["sb_w_qkv"][0].astype(BF16), p["sb_w_o"][0].astype(BF16)], "sb", dep=tok)
    st_ffn[1], tok = _gather_begin(ffn_shards(1), "ffn_1", dep=tok)
    st_glu, tok = _gather_begin([p["ssm_w_glu"][0].astype(BF16)], "glu", dep=tok)
    ag = _gather_end(st_first, tok, "first")
    n_pool = p["pool_w"].shape[0]
    pool_w = jnp.transpose(ag[0].reshape(N_DEV, n_pool, POOL_GROUPS, POOL_DIM // N_DEV, POOL_DIM), (1, 2, 0, 3, 4))
    pool_w = pool_w.reshape(n_pool, POOL_GROUPS, POOL_DIM, POOL_DIM)
    small_full = {}
    for n, g in zip(_SMALL_SHARDED, _unpack(ag[1], [a.shape for a in small_local], lead=(N_DEV,))):
        small_full[n] = _unshard(g, _SMALL_SHARDED[n])
    mix_w = {}
    wup_p, wdown = [None] * DEPTH, [None] * DEPTH

    gains = jnp.stack([p["sb_q_gain"][0], p["sb_k_gain"][0]])[:, None, :]
    ssm_raw = _ssm_params(p["ssm_lam_re"][0], p["ssm_lam_im"][0], p["ssm_log_step"][0], p["ssm_b_re"][0],
                          p["ssm_b_im"][0])

    def mixer_args(i):
        j = i // 3
        g_row = p["norm_mix_g"][i][None]
        if i % 3 == 0:
            return (g_row, pool_w[j], small_full["pool_b"][j][None], small_full["pool_scale"][j][None])
        if i % 3 == 1:
            return (g_row, mix_w["qkv"], gains, mix_w["o"])
        return (g_row, ssm_raw, p["ssm_c_re"][0], p["ssm_c_im"][0], small_full["ssm_d"], mix_w["glu"],
                small_full["ssm_b_glu"])

    def ffn_args(i):
        return (p["norm_ffn_g"][i][None], wup_p[i], wdown[i], small_full["ffn_conv_w"][i], p["ffn_conv_b"][i][None])

    xs_in, xs_mid, saved_mix, saved_ffn = [], [], [], []
    for i in range(DEPTH):
        xs_in.append(x)
        if i == 1:
            mix_w["qkv"], wo_g = _gather_end(st_sb, x, "sb")
            mix_w["o"] = wo_g.reshape(D_MODEL, D_MODEL)
        if i == 2:
            (mix_w["glu"],) = _gather_end(st_glu, x, "glu")
        dep, handing = None, None
        if i >= 2:
            _, lands = _ag_wait(st_ffn[i], x, name=f"ag_wait_ffn_{i}")
            handing, dep = _fw_start(lands, name=f"ag_fw_start_ffn_{i}")
        if 1 <= i and i + 1 < DEPTH:
            st_ffn[i + 1], dep = _gather_begin(ffn_shards(i + 1), f"ffn_{i + 1}", dep=dep)
        fwd = (_pool_layer_fwd, _sb_layer_fwd, _ssm_layer_fwd)[i % 3]
        x, sv = fwd(x, *mixer_args(i), f"l{i}", dep=dep)
        saved_mix.append(sv)
        xs_mid.append(x)
        if i == 0:
            (wup_p[i],) = _gather_end(st_ffn[i], x, f"ffn_{i}")

            def wdown_now(after):
                (wd_g,) = _gather_end(st_down0, after, "ffn_0_down")
                wdown[0] = wd_g.reshape(D_FF, D_MODEL)
                return wdown[0]
        else:
            if handing is None:
                wup_p[i], wd_g = _gather_end(st_ffn[i], x, f"ffn_{i}")
            else:
                wup_p[i], wd_g = _fw_wait(handing, x, name=f"ag_fw_wait_ffn_{i}")
            wdown[i] = wd_g.reshape(D_FF, D_MODEL)
        g_row, wu, wd, cw, cb = ffn_args(i)
        x, sv = _ffn_fwd(x, g_row, wu, wdown_now if i == 0 else wd, cw, cb, f"l{i}")
        saved_ffn.append(sv)
    dx, loss_part = _loss_head(x, p["loss_target"].reshape(s, D_MODEL), name="loss_head")

    grads = {}
    dg_mix, dg_ffn = [None] * DEPTH, [None] * DEPTH
    dconv_w, dconv_b = [None] * DEPTH, [None] * DEPTH
    dpool = {"w": {}, "b": {}, "scale": {}}
    out = {}

    def big_update(n, h, others, idx=None):
        w, m, v = (p[pre + n] if idx is None else p[pre + n][idx] for pre in ("", "m_", "v_"))
        cols = h.shape[-1]
        r = _adamw_chips(w.reshape(-1, cols), h, others, m.reshape(-1, cols), v.reshape(-1, cols), chip,
                         name=f"adamw_{n}" + ("" if idx is None else f"_{idx}"))
        return [a.reshape(w.shape) for a in r]

    kinds = ("grad", "delta", "new_m", "new_v")
    ffn_upd = {"ffn_w_up": [None] * DEPTH, "ffn_w_down": [None] * DEPTH}

    def finish(entry, after):
        names, idx, started, tag = entry
        hs, others = _scatter_end(started, after, tag)
        for n, h, o in zip(names, hs, others):
            if idx is None:
                for kind, a in zip(kinds, big_update(n, h, o)):
                    out[kind + "_" + n] = a
            else:
                ffn_upd[n][idx] = (h, o)

    pending = []
    for i in reversed(range(DEPTH)):
        dx, dg_ffn[i], dconv_w[i], dconv_b[i], started = _ffn_bwd(
            dx, xs_mid[i], *ffn_args(i), saved_ffn[i], f"l{i}")
        for entry in pending:
            finish(entry, dx)
        pending = [(("ffn_w_up", "ffn_w_down"), i, started, f"ffn_l{i}")]
        margs = mixer_args(i)
        if i % 3 == 0:
            j = i // 3
            dx, dg_mix[i], dpool["w"][j], dpool["b"][j], dpool["scale"][j] = _pool_layer_bwd(
                dx, xs_in[i], *margs, saved_mix[i], f"l{i}")
        elif i % 3 == 1:
            dx, dg_mix[i], dqg, dkg, started = _sb_layer_bwd(dx, xs_in[i], *margs, saved_mix[i], f"l{i}")
            grads["sb_q_gain"], grads["sb_k_gain"] = dqg, dkg
            pending.append((("sb_w_qkv", "sb_w_o"), None, started, f"sb_l{i}"))
        else:
            g_row, raw, _, _, d_row, wg, _ = margs
            dx, dg_mix[i], sg, started = _ssm_layer_bwd(dx, xs_in[i], g_row, raw, d_row, wg, saved_mix[i], f"l{i}")
            grads.update(sg)
            pending.append((("ssm_w_glu",), None, started, f"ssm_l{i}"))
    grad_x = dx.reshape(1, s, D_MODEL)
    grads["norm_mix_g"] = jnp.concatenate(dg_mix, axis=0)
    grads["norm_ffn_g"] = jnp.concatenate(dg_ffn, axis=0)
    grads["ffn_conv_w"] = jnp.stack(dconv_w)
    grads["ffn_conv_b"] = jnp.concatenate(dconv_b, axis=0)
    grads["pool_b"] = jnp.concatenate([dpool["b"][j] for j in range(n_pool)], axis=0)
    grads["pool_scale"] = jnp.concatenate([dpool["scale"][j] for j in range(n_pool)], axis=0)
    dpw = jnp.stack([dpool["w"][j] for j in range(n_pool)])
    dpw = dpw.reshape(n_pool, POOL_GROUPS, N_DEV, POOL_DIM // N_DEV, POOL_DIM)
    pair, token = _scatter_pair([jnp.transpose(dpw, (2, 0, 1, 3, 4)).reshape(N_DEV, -1, POOL_DIM)], "pool")
    pool_started, tok_pool = _scatter_chips(pair, token, "pool")
    (ffn_first,) = pending

    small_names = _REPLICATED + list(_SMALL_SHARDED)
    full_shapes = [p[n].shape for n in _REPLICATED] + [small_full[n].shape for n in _SMALL_SHARDED]
    part = _pack([grads[n].reshape(shp) for n, shp in zip(small_names, full_shapes)] + [loss_part[0]])
    st_small, tok_small = _gather_begin([part], "small_grads")
    upper = {}
    for n in ffn_upd:
        upper[n] = _adamw_chips_layers(p[n], [ffn_upd[n][i][0] for i in range(1, DEPTH)],
                                       [ffn_upd[n][i][1] for i in range(1, DEPTH)], p["m_" + n], p["v_" + n], chip,
                                       first=1, deps=(tok_small, tok_pool), name=f"adamw_{n}_upper")
    finish(ffn_first, upper["ffn_w_up"][0])
    for n in ffn_upd:
        h, o = ffn_upd[n][0]
        upd = _adamw_chips_layers(p[n], [h], [o], p["m_" + n], p["v_" + n], chip, first=0, prev=upper[n],
                                  name=f"adamw_{n}_first")
        for kind, a in zip(kinds, upd):
            out[kind + "_" + n] = a
    done = sum(out["new_v_" + n][(0,) * out["new_v_" + n].ndim] for n in _BIG if n != "pool_w").reshape(1, 1)
    finish((("pool_w",), None, pool_started, "pool"), done)
    (parts,) = _gather_end(st_small, done, "small_grads")
    summed = _unpack(_sum_parts(parts, name="sum_small_grads"), full_shapes + [(LANE,)])
    loss = summed[-1][0]
    small_g = {}
    for n, g in zip(small_names, summed[:-1]):
        if n in _SMALL_SHARDED:
            ax = _SMALL_SHARDED[n]
            g = lax.dynamic_slice_in_dim(g, me * p[n].shape[ax], p[n].shape[ax], axis=ax)
        small_g[n] = g

    local_shapes = [p[n].shape for n in small_names]
    packs = [_pack([p[pre + n] for n in small_names]) for pre in ("", "m_", "v_")]
    res = _adamw_flat(packs[0], _pack([small_g[n] for n in small_names]), packs[1], packs[2], name="adamw_small")
    for kind, packed in zip(("delta", "new_m", "new_v"), res):
        for n, a in zip(small_names, _unpack(packed, local_shapes)):
            out[kind + "_" + n] = a
    for n in small_names:
        out["grad_" + n] = small_g[n]

    return (loss, grad_x, *[out["grad_" + n] for n in _WEIGHTS], *[out["delta_" + n] for n in _WEIGHTS],
            *[out["new_m_" + n] for n in _WEIGHTS], *[out["new_v_" + n] for n in _WEIGHTS])


def kernel(x, norm_mix_g, norm_ffn_g, pool_w, pool_b, pool_scale, sb_w_qkv, sb_q_gain, sb_k_gain, sb_w_o, ssm_lam_re, ssm_lam_im, ssm_log_step, ssm_b_re, ssm_b_im, ssm_c_re, ssm_c_im, ssm_d, ssm_w_glu, ssm_b_glu, ffn_w_up, ffn_conv_w, ffn_conv_b, ffn_w_down, loss_target, m_norm_mix_g, m_norm_ffn_g, m_pool_w, m_pool_b, m_pool_scale, m_sb_w_qkv, m_sb_q_gain, m_sb_k_gain, m_sb_w_o, m_ssm_lam_re, m_ssm_lam_im, m_ssm_log_step, m_ssm_b_re, m_ssm_b_im, m_ssm_c_re, m_ssm_c_im, m_ssm_d, m_ssm_w_glu, m_ssm_b_glu, m_ffn_w_up, m_ffn_conv_w, m_ffn_conv_b, m_ffn_w_down, v_norm_mix_g, v_norm_ffn_g, v_pool_w, v_pool_b, v_pool_scale, v_sb_w_qkv, v_sb_q_gain, v_sb_k_gain, v_sb_w_o, v_ssm_lam_re, v_ssm_lam_im, v_ssm_log_step, v_ssm_b_re, v_ssm_b_im, v_ssm_c_re, v_ssm_c_im, v_ssm_d, v_ssm_w_glu, v_ssm_b_glu, v_ffn_w_up, v_ffn_conv_w, v_ffn_conv_b, v_ffn_w_down):
    args = (x, norm_mix_g, norm_ffn_g, pool_w, pool_b, pool_scale, sb_w_qkv, sb_q_gain, sb_k_gain, sb_w_o, ssm_lam_re, ssm_lam_im, ssm_log_step, ssm_b_re, ssm_b_im, ssm_c_re, ssm_c_im, ssm_d, ssm_w_glu, ssm_b_glu, ffn_w_up, ffn_conv_w, ffn_conv_b, ffn_w_down, loss_target, m_norm_mix_g, m_norm_ffn_g, m_pool_w, m_pool_b, m_pool_scale, m_sb_w_qkv, m_sb_q_gain, m_sb_k_gain, m_sb_w_o, m_ssm_lam_re, m_ssm_lam_im, m_ssm_log_step, m_ssm_b_re, m_ssm_b_im, m_ssm_c_re, m_ssm_c_im, m_ssm_d, m_ssm_w_glu, m_ssm_b_glu, m_ffn_w_up, m_ffn_conv_w, m_ffn_conv_b, m_ffn_w_down, v_norm_mix_g, v_norm_ffn_g, v_pool_w, v_pool_b, v_pool_scale, v_sb_w_qkv, v_sb_q_gain, v_sb_k_gain, v_sb_w_o, v_ssm_lam_re, v_ssm_lam_im, v_ssm_log_step, v_ssm_b_re, v_ssm_b_im, v_ssm_c_re, v_ssm_c_im, v_ssm_d, v_ssm_w_glu, v_ssm_b_glu, v_ffn_w_up, v_ffn_conv_w, v_ffn_conv_b, v_ffn_w_down)
    return _step(dict(zip(_INPUTS, args)))
```

```python
import math

import jax
import jax.numpy as jnp
from jax import lax
from jax.experimental import pallas as pl
from jax.experimental.pallas import tpu as pltpu

F32 = jnp.float32
BF16 = jnp.bfloat16

N_DEV = 8
D_MODEL = 2048
D_FF = 5632
DEPTH = 4
POOL_GROUPS = 4
POOL_DIM = 512
HEADS = 16
HEAD_DIM = 128
SSM_GROUPS = 128
SSM_CH = 16
SSM_STATE = 64
SSM_BLOCK_GROUPS = 8
SSM_BLOCK_LANES = SSM_BLOCK_GROUPS * SSM_STATE
RMS_EPS = 1e-6
ADAM_LR = 0.001
ADAM_B1 = 0.9
ADAM_B2 = 0.999
ADAM_EPS = 1e-08
ADAM_WD = 0.01
ADAM_STEP = 10

VMEM_LIMIT_BYTES = 56 * 1024 * 1024
LANE = 128
SUBLANE = 8
MESH = pl.DeviceIdType.MESH


def _cparams(*sem):
    return pltpu.CompilerParams(dimension_semantics=tuple(sem), vmem_limit_bytes=VMEM_LIMIT_BYTES)


def _sds(shape, dtype):
    return jax.ShapeDtypeStruct(tuple(shape), dtype)


def _mm(a, b, *, dims, grid, a_spec, b_spec, o_spec, out_shape, out_dtype, name, k_axis=None, acc_shape=None,
        res=None, res_spec=None, a_alt=None, b_alt=None, alt_axis=None, alt_from=None, dep=None):
    nk = grid[k_axis] if k_axis is not None else 1
    n_in = 2 + sum(e is not None for e in (res, a_alt, b_alt, dep))

    def body(*refs):
        a_ref, b_ref = refs[:2]
        rest = list(refs[2:n_in])
        r_ref = rest.pop(0) if res is not None else None
        a2_ref = rest.pop(0) if a_alt is not None else None
        b2_ref = rest.pop(0) if b_alt is not None else None
        o_ref = refs[n_in]
        scr = refs[n_in + 1:]
        av, bv = a_ref[...], b_ref[...]
        if a2_ref is not None:
            av = jnp.where(pl.program_id(alt_axis) >= alt_from, a2_ref[...], av)
        if b2_ref is not None:
            bv = jnp.where(pl.program_id(alt_axis) >= alt_from, b2_ref[...], bv)
        p = lax.dot_general(av, bv, (dims, ((), ())), preferred_element_type=F32)
        if k_axis is None:
            if r_ref is not None:
                p = p + r_ref[...]
            o_ref[...] = p.astype(o_ref.dtype)
        else:
            acc = scr[0]
            k = pl.program_id(k_axis)

            @pl.when(k == 0)
            def _():
                acc[...] = p

            @pl.when(k > 0)
            def _():
                acc[...] += p

            @pl.when(k == nk - 1)
            def _():
                r = acc[...]
                if r_ref is not None:
                    r = r + r_ref[...]
                o_ref[...] = r.astype(o_ref.dtype)

    sem = ["parallel"] * len(grid)
    if k_axis is not None:
        sem[k_axis] = "arbitrary"
    in_specs, args = [a_spec, b_spec], [a, b]
    if res is not None:
        in_specs.append(res_spec)
        args.append(res)
    for alt in (a_alt, b_alt):
        if alt is not None:
            args.append(alt[0])
            in_specs.append(alt[1])
    if dep is not None:
        args.append(dep)
        in_specs.append(pl.BlockSpec((SUBLANE, LANE), lambda *_: (0, 0)))
    scratch = [pltpu.VMEM(acc_shape, F32)] if k_axis is not None else []
    return pl.pallas_call(
        body, name=name, grid=grid, in_specs=in_specs, out_specs=o_spec, out_shape=_sds(out_shape, out_dtype),
        scratch_shapes=scratch, compiler_params=_cparams(*sem),
    )(*args)


NN = ((1,), (0,))
NT = ((1,), (1,))
TN = ((0,), (0,))


def _mm_nn_pieces(a, wp, *, tm, name, out_dtype=F32):
    s, k = a.shape
    tm = min(tm, s)
    p, _, c = wp.shape
    return _mm(a, wp, dims=NN, grid=(s // tm, p),
               a_spec=pl.BlockSpec((tm, k), lambda m, n: (m, 0)),
               b_spec=pl.BlockSpec((None, k, c), lambda m, n: (n, 0, 0)),
               o_spec=pl.BlockSpec((tm, c), lambda m, n: (m, n)),
               out_shape=(s, p * c), out_dtype=out_dtype, name=name)


def _mm_nt_pieces(a, wp, *, tm, tn, name, out_dtype=F32, dep=None):
    p, n, c = wp.shape
    halves = a if isinstance(a, tuple) else None
    a0 = halves[0] if halves else a
    s = a0.shape[0]
    tm = min(tm, s)
    h = p // 2
    alt = {}
    if halves:
        a_spec = pl.BlockSpec((tm, c), lambda m, j, k: (m, jnp.minimum(k, h - 1)))
        alt = dict(a_alt=(halves[1], pl.BlockSpec((tm, c), lambda m, j, k: (m, jnp.maximum(k - h, 0)))),
                   alt_axis=2, alt_from=h)
    else:
        a_spec = pl.BlockSpec((tm, c), lambda m, j, k: (m, k))
    return _mm(a0, wp, dims=NT, grid=(s // tm, n // tn, p), k_axis=2, acc_shape=(tm, tn), a_spec=a_spec,
               b_spec=pl.BlockSpec((None, tn, c), lambda m, j, k: (k, j, 0)),
               o_spec=pl.BlockSpec((tm, tn), lambda m, j, k: (m, j)),
               out_shape=(s, n), out_dtype=out_dtype, name=name, dep=dep, **alt)


def _mm_tn_pieces(a, g, *, pieces, tm, name, out_dtype=BF16):
    s, m = a.shape
    halves = g if isinstance(g, tuple) else None
    g0 = halves[0] if halves else g
    h = pieces // 2
    c = g0.shape[1] // (h if halves else pieces)
    alt = {}
    if halves:
        b_spec = pl.BlockSpec((s, c), lambda n, i: (0, jnp.minimum(n, h - 1)))
        alt = dict(b_alt=(halves[1], pl.BlockSpec((s, c), lambda n, i: (0, jnp.maximum(n - h, 0)))),
                   alt_axis=0, alt_from=h)
    else:
        b_spec = pl.BlockSpec((s, c), lambda n, i: (0, n))
    return _mm(a, g0, dims=TN, grid=(pieces, m // tm), a_spec=pl.BlockSpec((s, tm), lambda n, i: (0, i)),
               b_spec=b_spec, o_spec=pl.BlockSpec((None, tm, c), lambda n, i: (n, i, 0)),
               out_shape=(pieces, m, c), out_dtype=out_dtype, name=name, **alt)


def _mm_nn(a, w, *, tm, tn, name, out_dtype=F32, res=None):
    s, k = a.shape
    tm = min(tm, s)
    n = w.shape[1]
    return _mm(a, w, dims=NN, grid=(s // tm, n // tn),
               a_spec=pl.BlockSpec((tm, k), lambda m, j: (m, 0)),
               b_spec=pl.BlockSpec((k, tn), lambda m, j: (0, j)),
               o_spec=pl.BlockSpec((tm, tn), lambda m, j: (m, j)),
               res=res, res_spec=pl.BlockSpec((tm, tn), lambda m, j: (m, j)),
               out_shape=(s, n), out_dtype=out_dtype, name=name)


def _mm_nt(a, w, *, tm, tn, name, out_dtype=F32):
    s, k = a.shape
    tm = min(tm, s)
    n = w.shape[0]
    return _mm(a, w, dims=NT, grid=(s // tm, n // tn),
               a_spec=pl.BlockSpec((tm, k), lambda m, j: (m, 0)),
               b_spec=pl.BlockSpec((tn, k), lambda m, j: (j, 0)),
               o_spec=pl.BlockSpec((tm, tn), lambda m, j: (m, j)),
               out_shape=(s, n), out_dtype=out_dtype, name=name)


def _mm_tn(a, g, *, tm, tn, name, out_dtype=BF16):
    s, m = a.shape
    n = g.shape[1]
    return _mm(a, g, dims=TN, grid=(m // tm, n // tn),
               a_spec=pl.BlockSpec((s, tm), lambda i, j: (0, i)),
               b_spec=pl.BlockSpec((s, tn), lambda i, j: (0, j)),
               o_spec=pl.BlockSpec((tm, tn), lambda i, j: (i, j)),
               out_shape=(m, n), out_dtype=out_dtype, name=name)


ROW_TILE = 256


def _dep_spec():
    return pl.BlockSpec((SUBLANE, LANE), lambda i: (0, 0))


def _rms_fwd(x, g_row, *, want_f32, name, dep=None):
    s, d = x.shape
    n_in = 2 if dep is None else 3

    def body(*refs):
        x_ref, g_ref = refs[:2]
        outs = refs[n_in:]
        xv = x_ref[...]
        r = lax.rsqrt(jnp.mean(xv * xv, axis=-1, keepdims=True) + RMS_EPS)
        h = (xv * r) * g_ref[...]
        outs[0][...] = h.astype(BF16)
        if want_f32:
            outs[1][...] = h

    row = pl.BlockSpec((ROW_TILE, d), lambda i: (i, 0))
    out_shape = [_sds((s, d), BF16)] + ([_sds((s, d), F32)] if want_f32 else [])
    out = pl.pallas_call(
        body, name=name, grid=(s // ROW_TILE,),
        in_specs=[row, pl.BlockSpec((1, d), lambda i: (0, 0))] + ([] if dep is None else [_dep_spec()]),
        out_specs=[row] * len(out_shape), out_shape=out_shape, compiler_params=_cparams("parallel"),
    )(x, g_row, *(() if dep is None else (dep,)))
    return out if want_f32 else (out[0], None)


def _rms_bwd(x, g_row, dh, dres, *, name, dep=None):
    s, d = x.shape

    def body(x_ref, g_ref, dh_ref, dres_ref, *rest):
        dx_ref, dg_ref = rest[-2:]
        xv = x_ref[...]
        r = lax.rsqrt(jnp.mean(xv * xv, axis=-1, keepdims=True) + RMS_EPS)
        xn = xv * r
        dhv = dh_ref[...]
        dxn = dhv * g_ref[...]
        dx_ref[...] = dres_ref[...] + r * (dxn - xn * jnp.mean(dxn * xn, axis=-1, keepdims=True))
        part = jnp.sum(dhv * xn, axis=0, keepdims=True)

        @pl.when(pl.program_id(0) == 0)
        def _():
            dg_ref[...] = part

        @pl.when(pl.program_id(0) > 0)
        def _():
            dg_ref[...] += part

    row = pl.BlockSpec((ROW_TILE, d), lambda i: (i, 0))
    vec = pl.BlockSpec((1, d), lambda i: (0, 0))
    return pl.pallas_call(
        body, name=name, grid=(s // ROW_TILE,), in_specs=[row, vec, row, row] + ([] if dep is None else [_dep_spec()]),
        out_specs=[row, vec], out_shape=[_sds((s, d), F32), _sds((1, d), F32)], compiler_params=_cparams("arbitrary"),
    )(x, g_row, dh, dres, *(() if dep is None else (dep,)))


def _shift_down(v, k):
    row = lax.broadcasted_iota(jnp.int32, v.shape, 0)
    return jnp.where(row >= k, pltpu.roll(v, k, 0), 0.0)


def _shift_up(v, k):
    n = v.shape[0]
    row = lax.broadcasted_iota(jnp.int32, v.shape, 0)
    return jnp.where(row < n - k, pltpu.roll(v, n - k, 0), 0.0)


def _sigmoid(z):
    return 1.0 / (1.0 + jnp.exp(-z))


FF_COL_TILE = 256


def _conv3(u, w, b):
    return b + w[0:1, :] * _shift_down(u, 2) + w[1:2, :] * _shift_down(u, 1) + w[2:3, :] * u


def _conv_gate_fwd(up, conv_w, conv_b, *, name):
    s = up.shape[0]
    f = up.shape[1] // 2
    nt = f // FF_COL_TILE

    def body(uv_ref, ug_ref, wv_ref, wg_ref, bv_ref, bg_ref, a_ref):
        vc = _conv3(uv_ref[...], wv_ref[...], bv_ref[...])
        gc = _conv3(ug_ref[...], wg_ref[...], bg_ref[...])
        a_ref[...] = ((gc * _sigmoid(gc)) * vc).astype(BF16)

    def col(rows, off):
        return pl.BlockSpec((rows, FF_COL_TILE), lambda n: (0, n + off))

    return pl.pallas_call(
        body, name=name, grid=(nt,),
        in_specs=[col(s, 0), col(s, nt), col(3, 0), col(3, nt), col(1, 0), col(1, nt)],
        out_specs=col(s, 0), out_shape=_sds((s, f), BF16), compiler_params=_cparams("parallel"),
    )(up, up, conv_w, conv_w, conv_b, conv_b)


def _conv_gate_bwd(up, da, conv_w, conv_b, *, name):
    s = up.shape[0]
    f = up.shape[1] // 2
    nt = f // FF_COL_TILE

    def conv_bwd(u, w, dc):
        d0 = _shift_up(dc, 2)
        d1 = _shift_up(dc, 1)
        dup = w[0:1, :] * d0 + w[1:2, :] * d1 + w[2:3, :] * dc
        dw = jnp.concatenate([jnp.sum(u * d0, axis=0, keepdims=True), jnp.sum(u * d1, axis=0, keepdims=True),
                              jnp.sum(u * dc, axis=0, keepdims=True)], axis=0)
        return dup, dw, jnp.sum(dc, axis=0, keepdims=True)

    def body(uv_ref, ug_ref, da_ref, wv_ref, wg_ref, bv_ref, bg_ref,
             duv_ref, dug_ref, dwv_ref, dwg_ref, dbv_ref, dbg_ref):
        uv = uv_ref[...]
        ug = ug_ref[...]
        vc = _conv3(uv, wv_ref[...], bv_ref[...])
        gc = _conv3(ug, wg_ref[...], bg_ref[...])
        sg = _sigmoid(gc)
        dav = da_ref[...]
        dvc = dav * (gc * sg)
        dgc = dav * vc * (sg * (1.0 + gc * (1.0 - sg)))
        dup, dw, db = conv_bwd(uv, wv_ref[...], dvc)
        duv_ref[...] = dup.astype(BF16)
        dwv_ref[...] = dw
        dbv_ref[...] = db
        dup, dw, db = conv_bwd(ug, wg_ref[...], dgc)
        dug_ref[...] = dup.astype(BF16)
        dwg_ref[...] = dw
        dbg_ref[...] = db

    def col(rows, off):
        return pl.BlockSpec((rows, FF_COL_TILE), lambda n: (0, n + off))

    dup_v, dup_g, dw_v, dw_g, db_v, db_g = pl.pallas_call(
        body, name=name, grid=(nt,),
        in_specs=[col(s, 0), col(s, nt), col(s, 0), col(3, 0), col(3, nt), col(1, 0), col(1, nt)],
        out_specs=[col(s, 0), col(s, 0), col(3, 0), col(3, 0), col(1, 0), col(1, 0)],
        out_shape=[_sds((s, f), BF16), _sds((s, f), BF16), _sds((3, f), F32), _sds((3, f), F32),
                   _sds((1, f), F32), _sds((1, f), F32)],
        compiler_params=_cparams("parallel"),
    )(up, up, da, conv_w, conv_w, conv_b, conv_b)
    return dup_v, dup_g, jnp.concatenate([dw_v, dw_g], axis=1), jnp.concatenate([db_v, db_g], axis=1)


def _pool_counts(shape, g):
    win = jnp.left_shift(jnp.int32(2), g)
    t = lax.broadcasted_iota(jnp.int32, shape, 0)
    return win, jnp.minimum(t + 1, win).astype(F32)


def _window_sum(v, g, shift):
    for k in range(POOL_GROUPS):
        v = jnp.where(g >= k, v + shift(v, 1 << k), v)
    return v


def _pool_fwd(hf, x, w, b, scale, *, name):
    s, d = hf.shape

    def body(h_ref, x_ref, w_ref, b_ref, sc_ref, o_ref):
        g = pl.program_id(0)
        h = h_ref[...]
        _, cnt = _pool_counts(h.shape, g)
        pooled = _window_sum(h, g, _shift_down) / cnt - h
        y = jnp.dot(pooled.astype(BF16), w_ref[...], preferred_element_type=F32) + b_ref[...]
        o_ref[...] = x_ref[...] + y * sc_ref[...]

    col = pl.BlockSpec((s, POOL_DIM), lambda g: (0, g))
    vec = pl.BlockSpec((1, POOL_DIM), lambda g: (0, g))
    return pl.pallas_call(
        body, name=name, grid=(POOL_GROUPS,),
        in_specs=[col, col, pl.BlockSpec((None, POOL_DIM, POOL_DIM), lambda g: (g, 0, 0)), vec, vec],
        out_specs=col, out_shape=_sds((s, d), F32), compiler_params=_cparams("parallel"),
    )(hf, x, w, b, scale)


def _pool_bwd(hf, dm, w, b, scale, *, name):
    s, d = hf.shape

    def body(h_ref, dm_ref, w_ref, b_ref, sc_ref, dh_ref, dw_ref, db_ref, dsc_ref):
        g = pl.program_id(0)
        h = h_ref[...]
        _, cnt = _pool_counts(h.shape, g)
        pooled = (_window_sum(h, g, _shift_down) / cnt - h).astype(BF16)
        wv = w_ref[...]
        y = jnp.dot(pooled, wv, preferred_element_type=F32) + b_ref[...]
        dmv = dm_ref[...]
        dsc_ref[...] = jnp.sum(dmv * y, axis=0, keepdims=True)
        dy = dmv * sc_ref[...]
        db_ref[...] = jnp.sum(dy, axis=0, keepdims=True)
        dyb = dy.astype(BF16)
        dw_ref[...] = lax.dot_general(pooled, dyb, (TN, ((), ())), preferred_element_type=F32).astype(BF16)
        dp = lax.dot_general(dyb, wv, (NT, ((), ())), preferred_element_type=F32)
        dh_ref[...] = _window_sum(dp / cnt, g, _shift_up) - dp

    col = pl.BlockSpec((s, POOL_DIM), lambda g: (0, g))
    vec = pl.BlockSpec((1, POOL_DIM), lambda g: (0, g))
    mat = pl.BlockSpec((None, POOL_DIM, POOL_DIM), lambda g: (g, 0, 0))
    return pl.pallas_call(
        body, name=name, grid=(POOL_GROUPS,), in_specs=[col, col, mat, vec, vec], out_specs=[col, mat, vec, vec],
        out_shape=[_sds((s, d), F32), _sds((POOL_GROUPS, POOL_DIM, POOL_DIM), BF16), _sds((1, d), F32),
                   _sds((1, d), F32)],
        compiler_params=_cparams("parallel"),
    )(hf, dm, w, b, scale)


ATT_TQ = 256
ATT_TK = 256
ATT_HEADS_PER_STEP = 2


def _qk_norm_fwd(qkv, gains, *, name):
    s = qkv.shape[0]

    def body(x_ref, g_ref, o_ref):
        xv = x_ref[...]
        r = lax.rsqrt(jnp.mean(xv * xv, axis=-1, keepdims=True) + RMS_EPS)
        o_ref[...] = ((xv * r) * g_ref[...]).astype(BF16)

    blk = pl.BlockSpec((s, HEAD_DIM), lambda hd: (0, hd))
    return pl.pallas_call(
        body, name=name, grid=(2 * HEADS,),
        in_specs=[blk, pl.BlockSpec((None, 1, HEAD_DIM), lambda hd: (hd // HEADS, 0, 0))],
        out_specs=blk, out_shape=_sds((s, 2 * HEADS * HEAD_DIM), BF16), compiler_params=_cparams("parallel"),
    )(qkv, gains)


def _qk_norm_bwd(qkv, gains, dn, *, which, name):
    s = qkv.shape[0]

    def body(x_ref, g_ref, dn_ref, dx_ref, dg_ref):
        xv = x_ref[...]
        r = lax.rsqrt(jnp.mean(xv * xv, axis=-1, keepdims=True) + RMS_EPS)
        xn = xv * r
        dnv = dn_ref[...]
        dxn = dnv * g_ref[...]
        dx_ref[...] = (r * (dxn - xn * jnp.mean(dxn * xn, axis=-1, keepdims=True))).astype(BF16)
        part = jnp.sum(dnv * xn, axis=0, keepdims=True)

        @pl.when(pl.program_id(0) == 0)
        def _():
            dg_ref[...] = part

        @pl.when(pl.program_id(0) > 0)
        def _():
            dg_ref[...] += part

    blk = pl.BlockSpec((s, HEAD_DIM), lambda hd: (0, hd))
    return pl.pallas_call(
        body, name=name, grid=(HEADS,),
        in_specs=[pl.BlockSpec((s, HEAD_DIM), lambda hd: (0, hd + which * HEADS)),
                  pl.BlockSpec((None, 1, HEAD_DIM), lambda hd: (which, 0, 0)), blk],
        out_specs=[blk, pl.BlockSpec((1, HEAD_DIM), lambda hd: (0, 0))],
        out_shape=[_sds((s, HEADS * HEAD_DIM), BF16), _sds((1, HEAD_DIM), F32)],
        compiler_params=_cparams("arbitrary"),
    )(qkv, gains, dn)


def _split_dot(v, tri):
    hi = v.astype(BF16)
    lo = (v - hi.astype(F32)).astype(BF16)
    return (jnp.dot(hi, tri, preferred_element_type=F32) + jnp.dot(lo, tri, preferred_element_type=F32))


def _causal_mask(qi, j):
    tpos = qi * ATT_TQ + lax.broadcasted_iota(jnp.int32, (ATT_TQ, ATT_TK), 0)
    spos = j * ATT_TK + lax.broadcasted_iota(jnp.int32, (ATT_TQ, ATT_TK), 1)
    return spos < tpos


def _att_tile(q, kj, qi, j):
    z = lax.dot_general(q, kj, (NT, ((), ())), preferred_element_type=F32) * (1.0 / math.sqrt(HEAD_DIM))
    mask = _causal_mask(qi, j)
    lb = jnp.minimum(z, 0.0) - jnp.log1p(jnp.exp(-jnp.abs(z)))
    l1m = jnp.where(mask, lb - z, 0.0)
    return lb, l1m, mask


def _tri(rel):
    r = lax.broadcasted_iota(jnp.int32, (ATT_TK, ATT_TK), 0)
    c = lax.broadcasted_iota(jnp.int32, (ATT_TK, ATT_TK), 1)
    return jnp.where(rel(r, c), 1.0, 0.0).astype(BF16)


def _sb_fwd(qkn, vb, *, name):
    s = vb.shape[0]
    nh = ATT_HEADS_PER_STEP
    width = nh * HEAD_DIM

    def body(q_ref, k_ref, v_ref, o_ref):
        qi = pl.program_id(1)
        after = _tri(lambda r, c: r > c)

        def step(t, carry):
            j = qi - t
            rows = pl.ds(pl.multiple_of(j * ATT_TK, ATT_TK), ATT_TK)
            out = []
            for h in range(nh):
                acc, run = carry[h]
                lanes = pl.ds(h * HEAD_DIM, HEAD_DIM)
                lb, l1m, mask = _att_tile(q_ref[:, lanes], k_ref[rows, lanes], qi, j)
                remain = _split_dot(l1m, after) + run
                attn = jnp.where(mask, jnp.exp(lb + remain), 0.0)
                acc = acc + jnp.dot(attn.astype(BF16), v_ref[rows, lanes], preferred_element_type=F32)
                out.append((acc, run + jnp.sum(l1m, axis=1, keepdims=True)))
            return tuple(out)

        init = tuple((jnp.zeros((ATT_TQ, HEAD_DIM), F32), jnp.zeros((ATT_TQ, 1), F32)) for _ in range(nh))
        final = lax.fori_loop(0, qi + 1, step, init)
        for h in range(nh):
            o_ref[:, pl.ds(h * HEAD_DIM, HEAD_DIM)] = final[h][0].astype(BF16)

    return pl.pallas_call(
        body, name=name, grid=(HEADS // nh, s // ATT_TQ),
        in_specs=[pl.BlockSpec((ATT_TQ, width), lambda hd, i: (i, hd)),
                  pl.BlockSpec((s, width), lambda hd, i: (0, hd + HEADS // nh)),
                  pl.BlockSpec((s, width), lambda hd, i: (0, hd))],
        out_specs=pl.BlockSpec((ATT_TQ, width), lambda hd, i: (i, hd)),
        out_shape=_sds((s, HEADS * HEAD_DIM), BF16), compiler_params=_cparams("parallel", "parallel"),
    )(qkn, qkn, vb)


def _sb_bwd(qkn, vb, dob, *, name):
    s = vb.shape[0]
    nkb = s // ATT_TK

    def body(q_ref, k_ref, v_ref, do_ref, dq_ref, dk_ref, dv_ref, a_buf, sig_buf):
        qi = pl.program_id(1)
        q = q_ref[...]
        do = do_ref[...]
        after = _tri(lambda r, c: r > c)
        before = _tri(lambda r, c: r < c)

        @pl.when(qi == 0)
        def _():
            dk_ref[...] = jnp.zeros_like(dk_ref)
            dv_ref[...] = jnp.zeros_like(dv_ref)

        def down(t, run):
            j = qi - t
            rows = pl.ds(pl.multiple_of(j * ATT_TK, ATT_TK), ATT_TK)
            lb, l1m, mask = _att_tile(q, k_ref[rows, :], qi, j)
            remain = _split_dot(l1m, after) + run
            a_buf[j] = jnp.where(mask, jnp.exp(lb + remain), 0.0)
            sig_buf[j] = jnp.exp(lb)
            return run + jnp.sum(l1m, axis=1, keepdims=True)

        lax.fori_loop(0, qi + 1, down, jnp.zeros((ATT_TQ, 1), F32))

        def up(j, carry):
            dq, run = carry
            rows = pl.ds(pl.multiple_of(j * ATT_TK, ATT_TK), ATT_TK)
            a = a_buf[j]
            sig = sig_buf[j]
            mask = _causal_mask(qi, j)
            da = lax.dot_general(do, v_ref[rows, :], (NT, ((), ())), preferred_element_type=F32)
            p = a * da
            c = _split_dot(p, before) + run
            dz = jnp.where(mask, p * (1.0 - sig) - c * sig, 0.0) * (1.0 / math.sqrt(HEAD_DIM))
            dzb = dz.astype(BF16)
            dq = dq + jnp.dot(dzb, k_ref[rows, :], preferred_element_type=F32)
            dk_ref[rows, :] += lax.dot_general(dzb, q, (TN, ((), ())), preferred_element_type=F32)
            dv_ref[rows, :] += lax.dot_general(a.astype(BF16), do, (TN, ((), ())), preferred_element_type=F32)
            return dq, run + jnp.sum(p, axis=1, keepdims=True)

        dq, _ = lax.fori_loop(0, qi + 1, up, (jnp.zeros((ATT_TQ, HEAD_DIM), F32), jnp.zeros((ATT_TQ, 1), F32)))
        dq_ref[...] = dq

    qblk = pl.BlockSpec((ATT_TQ, HEAD_DIM), lambda hd, i: (i, hd))
    full = pl.BlockSpec((s, HEAD_DIM), lambda hd, i: (0, hd))
    return pl.pallas_call(
        body, name=name, grid=(HEADS, s // ATT_TQ),
        in_specs=[qblk, pl.BlockSpec((s, HEAD_DIM), lambda hd, i: (0, hd + HEADS)), full, qblk],
        out_specs=[qblk, full, full],
        out_shape=[_sds((s, HEADS * HEAD_DIM), F32)] * 3,
        scratch_shapes=[pltpu.VMEM((nkb, ATT_TQ, ATT_TK), F32), pltpu.VMEM((nkb, ATT_TQ, ATT_TK), F32)],
        compiler_params=_cparams("parallel", "arbitrary"),
    )(qkn, qkn, vb, dob)


def _ssm_discretize(lam_re, lam_im, log_step, bt_re, bt_im):
    step = jnp.exp(log_step)
    mag = jnp.exp(lam_re * step)
    lb_re = mag * jnp.cos(lam_im * step)
    lb_im = mag * jnp.sin(lam_im * step)
    den = lam_re * lam_re + lam_im * lam_im
    f_re = ((lb_re - 1.0) * lam_re + lb_im * lam_im) / den
    f_im = (lb_im * lam_re - (lb_re - 1.0) * lam_im) / den
    return lb_re, lb_im, f_re * bt_re - f_im * bt_im, f_re * bt_im + f_im * bt_re


_SSM_LAM = (SSM_GROUPS, 1, SSM_STATE)
_SSM_STEP = (SSM_GROUPS, 1, 1)
_SSM_BT = (SSM_GROUPS, SSM_CH, SSM_STATE)


def _ssm_prep_fwd(lam_re, lam_im, log_step, bt_re, bt_im, *, name):
    def body(lr, li, ls, br, bi, o_ar, o_ai, o_br, o_bi):
        o_ar[...], o_ai[...], o_br[...], o_bi[...] = _ssm_discretize(lr[...], li[...], ls[...], br[...], bi[...])

    return pl.pallas_call(
        body, name=name, out_shape=[_sds(_SSM_LAM, F32), _sds(_SSM_LAM, F32), _sds(_SSM_BT, F32), _sds(_SSM_BT, F32)],
    )(lam_re, lam_im, log_step, bt_re, bt_im)


def _ssm_prep_bwd(lam_re, lam_im, log_step, bt_re, bt_im, d_ar, d_ai, d_br, d_bi, *, name):
    def body(lr, li, ls, br, bi, g_ar, g_ai, g_br, g_bi, o_lr, o_li, o_ls, o_br, o_bi):
        _, vjp = jax.vjp(_ssm_discretize, lr[...], li[...], ls[...], br[...], bi[...])
        o_lr[...], o_li[...], o_ls[...], o_br[...], o_bi[...] = vjp((g_ar[...], g_ai[...], g_br[...], g_bi[...]))

    return pl.pallas_call(
        body, name=name,
        out_shape=[_sds(_SSM_LAM, F32), _sds(_SSM_LAM, F32), _sds(_SSM_STEP, F32), _sds(_SSM_BT, F32), _sds(_SSM_BT, F32)],
    )(lam_re, lam_im, log_step, bt_re, bt_im, d_ar, d_ai, d_br, d_bi)


def _bd_masks():
    rowg = lax.broadcasted_iota(jnp.int32, (LANE, LANE), 0) // SSM_CH
    low = lax.broadcasted_iota(jnp.int32, (LANE, LANE), 1) < SSM_STATE
    return rowg, low


def _bd_expand(w):
    rowg, low = _bd_masks()
    high = jnp.logical_not(low)
    wr = pltpu.roll(w, SSM_STATE, 1)
    re = [jnp.where((rowg == 2 * k) & low, w, 0.0) + jnp.where((rowg == 2 * k + 1) & high, wr, 0.0) for k in range(4)]
    im = [jnp.where((rowg == 2 * k) & low, wr, 0.0) + jnp.where((rowg == 2 * k + 1) & high, w, 0.0) for k in range(4)]
    return jnp.concatenate(re + im, axis=1)


def _bd_extract(dbd):
    rowg, low = _bd_masks()
    high = jnp.logical_not(low)
    acc = jnp.zeros((LANE, LANE), F32)
    for k in range(4):
        c = dbd[:, LANE * k:LANE * (k + 1)]
        acc = acc + jnp.where((rowg == 2 * k) & low, c, 0.0) + jnp.where((rowg == 2 * k + 1) & low, pltpu.roll(c, SSM_STATE, 1), 0.0)
        c = dbd[:, LANE * (4 + k):LANE * (5 + k)]
        acc = acc + jnp.where((rowg == 2 * k) & high, pltpu.roll(c, SSM_STATE, 1), 0.0) + jnp.where((rowg == 2 * k + 1) & high, c, 0.0)
    return acc


def _cmul(ar, ai, br, bi):
    return ar * br - ai * bi, ar * bi + ai * br


def _scan_rows(xr, xi, ar, ai, *, reverse):
    n = xr.shape[0] // SUBLANE
    lanes = xr.shape[1]
    row = lax.broadcasted_iota(jnp.int32, (SUBLANE, lanes), 0)
    powers = [(ar, ai)]
    for _ in range(SUBLANE - 1):
        powers.append(_cmul(*powers[-1], ar, ai))
    pr = jnp.zeros((SUBLANE, lanes), F32)
    pi = jnp.zeros((SUBLANE, lanes), F32)
    for r in range(SUBLANE):
        e = (SUBLANE - 1 - r) if reverse else r
        pr = jnp.where(row == r, powers[e][0], pr)
        pi = jnp.where(row == r, powers[e][1], pi)

    def shift(v, d):
        if reverse:
            return jnp.where(row < SUBLANE - d, pltpu.roll(v, SUBLANE - d, 0), 0.0)
        return jnp.where(row >= d, pltpu.roll(v, d, 0), 0.0)

    def body(i, carry):
        cr, ci = carry
        g = (n - 1 - i) if reverse else i
        rows = pl.ds(pl.multiple_of(g * SUBLANE, SUBLANE), SUBLANE)
        br = xr[rows, :]
        bi = xi[rows, :]
        for d in (1, 2, 4):
            qr, qi = powers[d - 1]
            sr = shift(br, d)
            si = shift(bi, d)
            br, bi = br + qr * sr - qi * si, bi + qr * si + qi * sr
        br, bi = br + pr * cr - pi * ci, bi + pr * ci + pi * cr
        xr[rows, :] = br
        xi[rows, :] = bi
        edge = 0 if reverse else SUBLANE - 1
        return br[edge:edge + 1, :], bi[edge:edge + 1, :]

    zero = jnp.zeros((1, lanes), F32)
    lax.fori_loop(0, n, body, (zero, zero), unroll=2)


_GELU_C = math.sqrt(2.0 / math.pi)
_GELU_A = 0.044715


def _gelu(v):
    return 0.5 * v * (1.0 + jnp.tanh(_GELU_C * (v + _GELU_A * v * v * v)))


def _gelu_grad(v):
    t = jnp.tanh(_GELU_C * (v + _GELU_A * v * v * v))
    return 0.5 * (1.0 + t) + 0.5 * v * (1.0 - t * t) * (_GELU_C * (1.0 + 3.0 * _GELU_A * v * v))


def _ssm_states(u_b16, eb, ar, ai, xr, xi):
    nl = SSM_BLOCK_LANES
    xr[...] = jnp.dot(u_b16, eb[:, :nl], preferred_element_type=F32)
    xi[...] = jnp.dot(u_b16, eb[:, nl:], preferred_element_type=F32)
    _scan_rows(xr, xi, ar, ai, reverse=False)


def _ssm_specs(s):
    col = pl.BlockSpec((s, LANE), lambda b: (0, b))
    wsm = pl.BlockSpec((LANE, LANE), lambda b: (b, 0))
    lam = pl.BlockSpec((1, SSM_BLOCK_LANES), lambda b: (0, b))
    vec = pl.BlockSpec((1, LANE), lambda b: (0, b))
    return col, wsm, lam, vec


def _ssm_core_fwd(u, wb, wc, a_re, a_im, d_row, *, name):
    s, d = u.shape
    nl = SSM_BLOCK_LANES

    def body(u_ref, wb_ref, wc_ref, ar_ref, ai_ref, d_ref, y_ref, yg_ref, xr, xi):
        uv = u_ref[...]
        eb = _bd_expand(wb_ref[...]).astype(BF16)
        ec = _bd_expand(wc_ref[...]).astype(BF16)
        _ssm_states(uv.astype(BF16), eb, ar_ref[...], ai_ref[...], xr, xi)
        y = (lax.dot_general(xr[...].astype(BF16), ec[:, :nl], (NT, ((), ())), preferred_element_type=F32)
             + lax.dot_general(xi[...].astype(BF16), ec[:, nl:], (NT, ((), ())), preferred_element_type=F32)
             + d_ref[...] * uv)
        y_ref[...] = y
        yg_ref[...] = _gelu(y).astype(BF16)

    col, wsm, lam, vec = _ssm_specs(s)
    return pl.pallas_call(
        body, name=name, grid=(d // LANE,), in_specs=[col, wsm, wsm, lam, lam, vec], out_specs=[col, col],
        out_shape=[_sds((s, d), F32), _sds((s, d), BF16)],
        scratch_shapes=[pltpu.VMEM((s, nl), F32), pltpu.VMEM((s, nl), F32)],
        compiler_params=_cparams("parallel"),
    )(u, wb, wc, a_re, a_im, d_row)


def _ssm_core_bwd(u, ylin, dyg, wb, wc, a_re, a_im, d_row, *, name):
    s, d = u.shape
    nl = SSM_BLOCK_LANES
    n8 = s // SUBLANE

    def body(u_ref, y_ref, dyg_ref, wb_ref, wc_ref, ar_ref, ai_ref, d_ref,
             du_ref, dwb_ref, dwc_ref, dar_ref, dai_ref, dd_ref, xr, xi, gr, gi):
        uv = u_ref[...]
        ub = uv.astype(BF16)
        ar = ar_ref[...]
        ai = ai_ref[...]
        dy = dyg_ref[...] * _gelu_grad(y_ref[...])
        dd_ref[...] = jnp.sum(dy * uv, axis=0, keepdims=True)
        dyb = dy.astype(BF16)
        eb = _bd_expand(wb_ref[...]).astype(BF16)
        ec = _bd_expand(wc_ref[...]).astype(BF16)
        _ssm_states(ub, eb, ar, ai, xr, xi)
        dec = jnp.concatenate(
            [lax.dot_general(dyb, xr[...].astype(BF16), (TN, ((), ())), preferred_element_type=F32),
             lax.dot_general(dyb, xi[...].astype(BF16), (TN, ((), ())), preferred_element_type=F32)], axis=1)
        dwc_ref[...] = _bd_extract(dec)
        gr[...] = jnp.dot(dyb, ec[:, :nl], preferred_element_type=F32)
        gi[...] = jnp.dot(dyb, ec[:, nl:], preferred_element_type=F32)
        _scan_rows(gr, gi, ar, -ai, reverse=True)

        row = lax.broadcasted_iota(jnp.int32, (SUBLANE, nl), 0)

        def lam_grad(i, acc):
            acc_r, acc_i = acc
            rows = pl.ds(pl.multiple_of(i * SUBLANE, SUBLANE), SUBLANE)
            prev = pl.ds(pl.multiple_of(jnp.maximum(i - 1, 0) * SUBLANE, SUBLANE), SUBLANE)
            keep = jnp.where(i > 0, 1.0, 0.0)
            xpr = jnp.where(row == 0, pltpu.roll(xr[prev, :], 1, 0) * keep, pltpu.roll(xr[rows, :], 1, 0))
            xpi = jnp.where(row == 0, pltpu.roll(xi[prev, :], 1, 0) * keep, pltpu.roll(xi[rows, :], 1, 0))
            g_r = gr[rows, :]
            g_i = gi[rows, :]
            return acc_r + g_r * xpr + g_i * xpi, acc_i + g_i * xpr - g_r * xpi

        zero = jnp.zeros((SUBLANE, nl), F32)
        acc_r, acc_i = lax.fori_loop(0, n8, lam_grad, (zero, zero), unroll=2)
        dar_ref[...] = jnp.sum(acc_r, axis=0, keepdims=True)
        dai_ref[...] = jnp.sum(acc_i, axis=0, keepdims=True)

        grb = gr[...].astype(BF16)
        gib = gi[...].astype(BF16)
        deb = jnp.concatenate([lax.dot_general(ub, grb, (TN, ((), ())), preferred_element_type=F32),
                               lax.dot_general(ub, gib, (TN, ((), ())), preferred_element_type=F32)], axis=1)
        dwb_ref[...] = _bd_extract(deb)
        du_ref[...] = (lax.dot_general(grb, eb[:, :nl], (NT, ((), ())), preferred_element_type=F32)
                       + lax.dot_general(gib, eb[:, nl:], (NT, ((), ())), preferred_element_type=F32)
                       + d_ref[...] * dy)

    col, wsm, lam, vec = _ssm_specs(s)
    return pl.pallas_call(
        body, name=name, grid=(d // LANE,), in_specs=[col, col, col, wsm, wsm, lam, lam, vec],
        out_specs=[col, wsm, wsm, lam, lam, vec],
        out_shape=[_sds((s, d), F32), _sds((d, LANE), F32), _sds((d, LANE), F32),
                   _sds((1, SSM_GROUPS * SSM_STATE), F32), _sds((1, SSM_GROUPS * SSM_STATE), F32), _sds((1, d), F32)],
        scratch_shapes=[pltpu.VMEM((s, nl), F32)] * 4,
        compiler_params=_cparams("parallel"),
    )(u, ylin, dyg, wb, wc, a_re, a_im, d_row)


GLU_PIECE = 512


def _glu_fwd(yg, wp, b_row, x, *, tm, name):
    s, d = yg.shape
    tm = min(tm, s)
    half = N_DEV // 2

    def body(y_ref, wv_ref, wg_ref, bv_ref, bg_ref, x_ref, o_ref, val_ref, gate_ref):
        yv = y_ref[...]
        val = jnp.dot(yv, wv_ref[...], preferred_element_type=F32) + bv_ref[...]
        gate = jnp.dot(yv, wg_ref[...], preferred_element_type=F32) + bg_ref[...]
        val_ref[...] = val
        gate_ref[...] = gate
        o_ref[...] = x_ref[...] + val * _sigmoid(gate)

    blk = pl.BlockSpec((tm, GLU_PIECE), lambda m, n: (m, n))
    return pl.pallas_call(
        body, name=name, grid=(s // tm, half),
        in_specs=[pl.BlockSpec((tm, d), lambda m, n: (m, 0)),
                  pl.BlockSpec((None, d, GLU_PIECE), lambda m, n: (n, 0, 0)),
                  pl.BlockSpec((None, d, GLU_PIECE), lambda m, n: (n + half, 0, 0)),
                  pl.BlockSpec((1, GLU_PIECE), lambda m, n: (0, n)),
                  pl.BlockSpec((1, GLU_PIECE), lambda m, n: (0, n + half)), blk],
        out_specs=[blk, blk, blk], out_shape=[_sds((s, d), F32)] * 3,
        compiler_params=_cparams("parallel", "parallel"),
    )(yg, wp, wp, b_row, b_row, x)


def _glu_bwd(dout, val, gate, *, name):
    s, d = dout.shape

    def body(do_ref, val_ref, gate_ref, dgv_ref, db_ref):
        sg = _sigmoid(gate_ref[...])
        dov = do_ref[...]
        dgv = jnp.concatenate([dov * sg, dov * val_ref[...] * (sg * (1.0 - sg))], axis=1)
        dgv_ref[...] = dgv.astype(BF16)
        part = jnp.sum(dgv, axis=0, keepdims=True)

        @pl.when(pl.program_id(0) == 0)
        def _():
            db_ref[...] = part

        @pl.when(pl.program_id(0) > 0)
        def _():
            db_ref[...] += part

    row = pl.BlockSpec((ROW_TILE, d), lambda i: (i, 0))
    return pl.pallas_call(
        body, name=name, grid=(s // ROW_TILE,), in_specs=[row, row, row],
        out_specs=[pl.BlockSpec((ROW_TILE, 2 * d), lambda i: (i, 0)), pl.BlockSpec((1, 2 * d), lambda i: (0, 0))],
        out_shape=[_sds((s, 2 * d), BF16), _sds((1, 2 * d), F32)], compiler_params=_cparams("arbitrary"),
    )(dout, val, gate)


def _loss_head(y, target, *, name):
    s, d = y.shape

    def body(y_ref, t_ref, dy_ref, l_ref):
        e = y_ref[...] - t_ref[...]
        dy_ref[...] = e * (1.0 / d)
        part = jnp.zeros((SUBLANE, LANE), F32) + jnp.sum(e * e) * (0.5 / d)

        @pl.when(pl.program_id(0) == 0)
        def _():
            l_ref[...] = part

        @pl.when(pl.program_id(0) > 0)
        def _():
            l_ref[...] += part

    row = pl.BlockSpec((ROW_TILE, d), lambda i: (i, 0))
    return pl.pallas_call(
        body, name=name, grid=(s // ROW_TILE,), in_specs=[row, row],
        out_specs=[row, pl.BlockSpec((SUBLANE, LANE), lambda i: (0, 0))],
        out_shape=[_sds((s, d), F32), _sds((SUBLANE, LANE), F32)], compiler_params=_cparams("arbitrary"),
    )(y, target)


def _adamw_math(w, g, m, v):
    m = ADAM_B1 * m + (1.0 - ADAM_B1) * g
    v = ADAM_B2 * v + (1.0 - ADAM_B2) * (g * g)
    m_hat = m / (1.0 - ADAM_B1 ** ADAM_STEP)
    v_hat = v / (1.0 - ADAM_B2 ** ADAM_STEP)
    return -ADAM_LR * (m_hat / (jnp.sqrt(v_hat) + ADAM_EPS) + ADAM_WD * w), m, v


ADAM_ROWS = 64
PACK_ROWS = 64


def _sum_pieces(p_ref):
    g = p_ref[0].astype(F32)
    for k in range(1, N_DEV):
        g = g + p_ref[k].astype(F32)
    return g


def _sum_parts(parts, *, name):
    _, r, c = parts.shape

    def body(p_ref, o_ref):
        o_ref[...] = _sum_pieces(p_ref)

    return pl.pallas_call(
        body, name=name, grid=(r // PACK_ROWS,),
        in_specs=[pl.BlockSpec((N_DEV, PACK_ROWS, c), lambda i: (0, i, 0))],
        out_specs=pl.BlockSpec((PACK_ROWS, c), lambda i: (i, 0)), out_shape=_sds((r, c), F32),
        compiler_params=_cparams("parallel"),
    )(parts)


def _adamw_flat(w, g, m, v, *, name):
    r, c = w.shape

    def body(w_ref, g_ref, m_ref, v_ref, d_ref, nm_ref, nv_ref):
        d_ref[...], nm_ref[...], nv_ref[...] = _adamw_math(w_ref[...], g_ref[...], m_ref[...], v_ref[...])

    blk = pl.BlockSpec((PACK_ROWS, c), lambda i: (i, 0))
    return pl.pallas_call(
        body, name=name, grid=(r // PACK_ROWS,), in_specs=[blk] * 4, out_specs=[blk] * 3,
        out_shape=[_sds((r, c), F32)] * 3, compiler_params=_cparams("parallel"),
    )(w, g, m, v)


_ANY = pl.BlockSpec(memory_space=pl.ANY)


def _place():
    return lax.axis_index("x"), lax.axis_index("y"), lax.axis_index("c")


def _slot(px, py, pc):
    return 4 * px + 2 * py + pc


_HBM = pl.BlockSpec(memory_space=pltpu.HBM)
_SEM = pl.BlockSpec(memory_space=pltpu.SEMAPHORE)
_EFFECT = pltpu.SideEffectType.DATAFLOW_SIDE_EFFECTING
_CHIP_FLIPS = ((1, 0), (0, 1), (1, 1))
_TOKEN = (SUBLANE, LANE)


def _flip(v, f):
    return (1 - v) if f else v


def _in_hbm(a):
    return pltpu.with_memory_space_constraint(a, pltpu.HBM)


def _ag_peers(x, y, c):
    return [(x, y, 1 - c)] + [(_flip(x, dx), _flip(y, dy), c) for dx, dy in _CHIP_FLIPS]


def _ag_copies(x_refs, land_refs, send_sems, recv_sems, *, landing):
    x, y, c = _place()
    peers = _ag_peers(x, y, c)
    cps = []
    for i in range(len(x_refs)):
        for k, peer in enumerate(peers):
            origin = _slot(*peer) if landing else _slot(x, y, c)
            cps.append(pltpu.make_async_remote_copy(
                src_ref=x_refs[i], dst_ref=land_refs[i].at[origin], send_sem=send_sems.at[4 * i + k],
                recv_sem=recv_sems.at[4 * i + k], device_id=peer, device_id_type=MESH))
    return cps


OWN_BLOCK_BYTES = 4 * 1024 * 1024


def _place_own(x, me, *, name):
    r, c = x.shape
    tr = r if r * c * x.dtype.itemsize <= OWN_BLOCK_BYTES else r // 4

    def body(me_ref, x_ref, o_ref):
        o_ref[...] = x_ref[...]

    return pl.pallas_call(
        body, name=name,
        grid_spec=pltpu.PrefetchScalarGridSpec(
            num_scalar_prefetch=1, grid=(r // tr,),
            in_specs=[pl.BlockSpec((tr, c), lambda i, me_ref: (i, 0))],
            out_specs=pl.BlockSpec((None, tr, c), lambda i, me_ref: (me_ref[0], i, 0))),
        out_shape=_sds((N_DEV, r, c), x.dtype), compiler_params=_cparams("parallel"),
    )(me, x)


def _ag_start(xs, *, name, dep=None):
    n = len(xs)
    me = _slot(*_place()).astype(jnp.int32).reshape(1)
    lands = [_place_own(a, me, name=f"{name}_own_{i}") for i, a in enumerate(xs)]
    n_in = 2 * n + (0 if dep is None else 1)

    def body(*refs):
        x_refs, land_refs = refs[:n], refs[n:2 * n]
        send_sems, recv_sems = refs[n_in], refs[n_in + 1]
        token = refs[-1]
        for cp in _ag_copies(x_refs, land_refs, send_sems, recv_sems, landing=False):
            cp.start()
        token[...] = jnp.zeros_like(token)

    out = pl.pallas_call(
        body, name=name,
        out_shape=(pltpu.SemaphoreType.DMA((4 * n,)), pltpu.SemaphoreType.DMA((4 * n,)),
                   *[pltpu.HBM(a.shape, a.dtype) for a in xs], *[pltpu.HBM(a.shape, a.dtype) for a in lands],
                   _sds(_TOKEN, F32)),
        in_specs=[_HBM] * (2 * n) + ([] if dep is None else [_ANY]),
        out_specs=(_SEM, _SEM, *[_HBM] * (2 * n), pl.BlockSpec(memory_space=pltpu.VMEM)),
        input_output_aliases={i: 2 + i for i in range(2 * n)},
        compiler_params=pltpu.CompilerParams(has_side_effects=_EFFECT),
    )(*[_in_hbm(a) for a in xs], *[_in_hbm(a) for a in lands], *(() if dep is None else (dep,)))
    return (out[0], out[1], list(out[2:2 + n]), list(out[2 + n:2 + 2 * n])), out[-1]


def _ag_wait(started, after, *, name):
    send_sems, recv_sems, xs, lands = started
    n = len(xs)

    def body(*refs):
        x_refs, land_refs = refs[:n], refs[n:2 * n]
        for cp in _ag_copies(x_refs, land_refs, refs[2 * n], refs[2 * n + 1], landing=True):
            cp.wait_send()
            cp.wait_recv()

    out = pl.pallas_call(
        body, name=name,
        out_shape=tuple(pltpu.HBM(a.shape, a.dtype) for a in xs + lands),
        in_specs=[_HBM] * (2 * n) + [_SEM, _SEM, _ANY], out_specs=tuple([_HBM] * (2 * n)),
        input_output_aliases={i: i for i in range(2 * n)},
        compiler_params=pltpu.CompilerParams(has_side_effects=_EFFECT),
    )(*xs, *lands, send_sems, recv_sems, after)
    return list(out[:n]), list(out[n:])


def _ag_forward(lands, *, name):
    n = len(lands)

    def body(*refs):
        land = refs[n:2 * n]
        send_sems, recv_sems = refs[2 * n:]
        x, y, c = _place()
        sibling = (x, y, 1 - c)

        def copy(i, j, core):
            dx, dy = _CHIP_FLIPS[j]
            rows = land[i].at[_slot(_flip(x, dx), _flip(y, dy), core)]
            return pltpu.make_async_remote_copy(src_ref=rows, dst_ref=rows, send_sem=send_sems.at[i, j],
                                                recv_sem=recv_sems.at[i, j], device_id=sibling, device_id_type=MESH)

        sends = [copy(i, j, c) for i in range(n) for j in range(3)]
        for cp in sends:
            cp.start()
        for i in range(n):
            for j in range(3):
                copy(i, j, 1 - c).wait_recv()
        for cp in sends:
            cp.wait_send()

    return pl.pallas_call(
        body, name=name, in_specs=[_ANY] * n, out_specs=[_ANY] * n,
        out_shape=[_sds(a.shape, a.dtype) for a in lands], input_output_aliases={i: i for i in range(n)},
        scratch_shapes=[pltpu.SemaphoreType.DMA((n, 3)), pltpu.SemaphoreType.DMA((n, 3))],
    )(*lands)


def _fw_copies(land_refs, send_sems, recv_sems, *, landing):
    x, y, c = _place()
    cps = []
    for i in range(len(land_refs)):
        for j, (dx, dy) in enumerate(_CHIP_FLIPS):
            rows = land_refs[i].at[_slot(_flip(x, dx), _flip(y, dy), (1 - c) if landing else c)]
            cps.append(pltpu.make_async_remote_copy(
                src_ref=rows, dst_ref=rows, send_sem=send_sems.at[3 * i + j], recv_sem=recv_sems.at[3 * i + j],
                device_id=(x, y, 1 - c), device_id_type=MESH))
    return cps


def _fw_start(lands, *, name, dep=None):
    n = len(lands)
    n_in = n + (0 if dep is None else 1)

    def body(*refs):
        for cp in _fw_copies(refs[:n], refs[n_in], refs[n_in + 1], landing=False):
            cp.start()
        refs[-1][...] = jnp.zeros_like(refs[-1])

    out = pl.pallas_call(
        body, name=name,
        out_shape=(pltpu.SemaphoreType.DMA((3 * n,)), pltpu.SemaphoreType.DMA((3 * n,)),
                   *[pltpu.HBM(a.shape, a.dtype) for a in lands], _sds(_TOKEN, F32)),
        in_specs=[_HBM] * n + ([] if dep is None else [_ANY]),
        out_specs=(_SEM, _SEM, *[_HBM] * n, pl.BlockSpec(memory_space=pltpu.VMEM)),
        input_output_aliases={i: 2 + i for i in range(n)},
        compiler_params=pltpu.CompilerParams(has_side_effects=_EFFECT),
    )(*[_in_hbm(a) for a in lands], *(() if dep is None else (dep,)))
    return (out[0], out[1], list(out[2:2 + n])), out[-1]


def _fw_wait(started, after, *, name):
    send_sems, recv_sems, lands = started
    n = len(lands)

    def body(*refs):
        for cp in _fw_copies(refs[:n], refs[n], refs[n + 1], landing=True):
            cp.wait_send()
            cp.wait_recv()

    out = pl.pallas_call(
        body, name=name,
        out_shape=tuple(pltpu.HBM(a.shape, a.dtype) for a in lands),
        in_specs=[_HBM] * n + [_SEM, _SEM, _ANY], out_specs=tuple([_HBM] * n),
        input_output_aliases={i: i for i in range(n)},
        compiler_params=pltpu.CompilerParams(has_side_effects=_EFFECT),
    )(*lands, send_sems, recv_sems, after)
    return list(out)


def _sib_copies(g_refs, land_refs, send_sems, recv_sems):
    x, y, c = _place()
    return [pltpu.make_async_remote_copy(
        src_ref=g_refs[i].at[:, 1 - c], dst_ref=land_refs[i], send_sem=send_sems.at[i], recv_sem=recv_sems.at[i],
        device_id=(x, y, 1 - c), device_id_type=MESH) for i in range(len(g_refs))]


def _sib_start(g4s, *, name):
    n = len(g4s)
    lands = [lax.empty((4,) + a.shape[2:], a.dtype) for a in g4s]

    def body(*refs):
        for cp in _sib_copies(refs[:n], refs[n:2 * n], refs[2 * n], refs[2 * n + 1]):
            cp.start()
        refs[-1][...] = jnp.zeros_like(refs[-1])

    out = pl.pallas_call(
        body, name=name,
        out_shape=(pltpu.SemaphoreType.DMA((n,)), pltpu.SemaphoreType.DMA((n,)),
                   *[pltpu.HBM(a.shape, a.dtype) for a in g4s], *[pltpu.HBM(a.shape, a.dtype) for a in lands],
                   _sds(_TOKEN, F32)),
        in_specs=[_HBM] * (2 * n),
        out_specs=(_SEM, _SEM, *[_HBM] * (2 * n), pl.BlockSpec(memory_space=pltpu.VMEM)),
        input_output_aliases={i: 2 + i for i in range(2 * n)},
        compiler_params=pltpu.CompilerParams(has_side_effects=_EFFECT),
    )(*[_in_hbm(a) for a in g4s], *[_in_hbm(a) for a in lands])
    return (out[0], out[1], list(out[2:2 + n]), list(out[2 + n:2 + 2 * n])), out[-1]


def _sib_wait(started, after, *, name):
    send_sems, recv_sems, g4s, lands = started
    n = len(g4s)

    def body(*refs):
        for cp in _sib_copies(refs[:n], refs[n:2 * n], refs[2 * n], refs[2 * n + 1]):
            cp.wait_send()
            cp.wait_recv()

    out = pl.pallas_call(
        body, name=name,
        out_shape=tuple(pltpu.HBM(a.shape, a.dtype) for a in g4s + lands),
        in_specs=[_HBM] * (2 * n) + [_SEM, _SEM, _ANY], out_specs=tuple([_HBM] * (2 * n)),
        input_output_aliases={i: i for i in range(2 * n)},
        compiler_params=pltpu.CompilerParams(has_side_effects=_EFFECT),
    )(*g4s, *lands, send_sems, recv_sems, after)
    return list(out[:n]), list(out[n:])


PAIR_SUM_BLOCK_BYTES = 3 * 1024 * 1024


def _rs_rows(r, c):
    tr = r
    while tr * c * 2 > PAIR_SUM_BLOCK_BYTES and tr % 2 == 0:
        tr //= 2
    return tr


def _rs_pair_sum(g4, from_sibling, core, *, name):
    _, _, r, c = g4.shape
    tr = _rs_rows(r, c)

    def body(core_ref, g_ref, a_ref, o_ref):
        o_ref[...] = (g_ref[...].astype(F32) + a_ref[...].astype(F32)).astype(BF16)

    blk = pl.BlockSpec((None, tr, c), lambda k, i, core_ref: (k, i, 0))
    return pl.pallas_call(
        body, name=name,
        grid_spec=pltpu.PrefetchScalarGridSpec(
            num_scalar_prefetch=1, grid=(4, r // tr),
            in_specs=[pl.BlockSpec((None, None, tr, c), lambda k, i, core_ref: (k, core_ref[0], i, 0)), blk],
            out_specs=blk),
        out_shape=_sds((4, r, c), BF16), compiler_params=_cparams("parallel", "parallel"),
    )(core, g4, from_sibling)


def _rs_copies(h_refs, land_refs, send_sems, recv_sems):
    x, y, c = _place()
    cps = []
    for i in range(len(h_refs)):
        for k, (dx, dy) in enumerate(_CHIP_FLIPS):
            px, py = _flip(x, dx), _flip(y, dy)
            cps.append(pltpu.make_async_remote_copy(
                src_ref=h_refs[i].at[2 * px + py], dst_ref=land_refs[i].at[k], send_sem=send_sems.at[3 * i + k],
                recv_sem=recv_sems.at[3 * i + k], device_id=(px, py, c), device_id_type=MESH))
    return cps


def _rs_start(hs, *, name):
    n = len(hs)
    lands = [lax.empty((3,) + a.shape[1:], a.dtype) for a in hs]

    def body(*refs):
        h_refs, land_refs = refs[:n], refs[n:2 * n]
        token = refs[-1]
        for cp in _rs_copies(h_refs, land_refs, refs[2 * n], refs[2 * n + 1]):
            cp.start()
        token[...] = jnp.zeros_like(token)

    out = pl.pallas_call(
        body, name=name,
        out_shape=(pltpu.SemaphoreType.DMA((3 * n,)), pltpu.SemaphoreType.DMA((3 * n,)),
                   *[pltpu.HBM(a.shape, a.dtype) for a in hs], *[pltpu.HBM(a.shape, a.dtype) for a in lands],
                   _sds(_TOKEN, F32)),
        in_specs=[_HBM] * (2 * n),
        out_specs=(_SEM, _SEM, *[_HBM] * (2 * n), pl.BlockSpec(memory_space=pltpu.VMEM)),
        input_output_aliases={i: 2 + i for i in range(2 * n)},
        compiler_params=pltpu.CompilerParams(has_side_effects=_EFFECT),
    )(*[_in_hbm(a) for a in hs], *[_in_hbm(a) for a in lands])
    return (out[0], out[1], list(out[2:2 + n]), list(out[2 + n:2 + 2 * n])), out[-1]


def _rs_wait(started, after, *, name):
    send_sems, recv_sems, hs, lands = started
    n = len(hs)

    def body(*refs):
        for cp in _rs_copies(refs[:n], refs[n:2 * n], refs[2 * n], refs[2 * n + 1]):
            cp.wait_send()
            cp.wait_recv()

    out = pl.pallas_call(
        body, name=name,
        out_shape=tuple(pltpu.HBM(a.shape, a.dtype) for a in hs + lands),
        in_specs=[_HBM] * (2 * n) + [_SEM, _SEM, _ANY], out_specs=tuple([_HBM] * (2 * n)),
        input_output_aliases={i: i for i in range(2 * n)},
        compiler_params=pltpu.CompilerParams(has_side_effects=_EFFECT),
    )(*hs, *lands, send_sems, recv_sems, after)
    return list(out[:n]), list(out[n:])


def _adamw_chips(w, h, others, m, v, chip, *, name):
    r, c = w.shape

    def body(chip_ref, w_ref, h_ref, o_ref, m_ref, v_ref, g_ref, d_ref, nm_ref, nv_ref):
        g = h_ref[...].astype(F32)
        for k in range(3):
            g = g + o_ref[k].astype(F32)
        g_ref[...] = g
        d_ref[...], nm_ref[...], nv_ref[...] = _adamw_math(w_ref[...], g, m_ref[...], v_ref[...])

    blk = pl.BlockSpec((ADAM_ROWS, c), lambda i, chip_ref: (i, 0))
    return pl.pallas_call(
        body, name=name,
        grid_spec=pltpu.PrefetchScalarGridSpec(
            num_scalar_prefetch=1, grid=(r // ADAM_ROWS,),
            in_specs=[blk, pl.BlockSpec((None, ADAM_ROWS, c), lambda i, chip_ref: (chip_ref[0], i, 0)),
                      pl.BlockSpec((3, ADAM_ROWS, c), lambda i, chip_ref: (0, i, 0)), blk, blk],
            out_specs=[blk] * 4),
        out_shape=[_sds((r, c), F32)] * 4, compiler_params=_cparams("parallel"),
    )(chip, w, h, others, m, v)


def _adamw_chips_layers(w, hs, others, m, v, chip, *, first, prev=None, deps=(), name):
    nl, r, c = w.shape
    n = len(hs)
    n_prev = 0 if prev is None else 4

    def body(chip_ref, w_ref, m_ref, v_ref, *rest):
        h_refs, o_refs = rest[:n], rest[n:2 * n]
        g_ref, d_ref, nm_ref, nv_ref = rest[2 * n + n_prev + len(deps):]
        k_now = pl.program_id(0)
        g = jnp.zeros((ADAM_ROWS, c), F32)
        for k in range(n):
            gk = h_refs[k][...].astype(F32)
            for j in range(3):
                gk = gk + o_refs[k][j].astype(F32)
            g = jnp.where(k_now == k, gk, g)
        g_ref[...] = g
        d_ref[...], nm_ref[...], nv_ref[...] = _adamw_math(w_ref[...], g, m_ref[...], v_ref[...])

    def rows(k):
        return lambda l, i: jnp.where(l == k, i, 0)

    blk = pl.BlockSpec((None, ADAM_ROWS, c), lambda l, i, chip_ref: (l + first, i, 0))
    h_specs = [pl.BlockSpec((None, ADAM_ROWS, c), lambda l, i, chip_ref, f=rows(k): (chip_ref[0], f(l, i), 0))
               for k in range(n)]
    o_specs = [pl.BlockSpec((3, ADAM_ROWS, c), lambda l, i, chip_ref, f=rows(k): (0, f(l, i), 0)) for k in range(n)]
    return pl.pallas_call(
        body, name=name,
        grid_spec=pltpu.PrefetchScalarGridSpec(
            num_scalar_prefetch=1, grid=(n, r // ADAM_ROWS),
            in_specs=[blk, blk, blk] + h_specs + o_specs + [_ANY] * (n_prev + len(deps)), out_specs=[blk] * 4),
        out_shape=[_sds((nl, r, c), F32)] * 4,
        input_output_aliases={4 + 2 * n + q: q for q in range(n_prev)},
        compiler_params=_cparams("arbitrary", "arbitrary"),
    )(chip, w, m, v, *hs, *others, *(() if prev is None else prev), *deps)


def _gather_begin(xs, tag, dep=None):
    return _ag_start(xs, name=f"ag_start_{tag}", dep=dep)


def _gather_end(started, after, tag):
    _, lands = _ag_wait(started, after, name=f"ag_wait_{tag}")
    return _ag_forward(lands, name=f"ag_forward_{tag}")


def _scatter_pair(gs, tag):
    g4s = [g.reshape((4, 2) + g.shape[1:]) for g in gs]
    return _sib_start(g4s, name=f"rs_sib_start_{tag}")


def _scatter_chips(pair, after, tag):
    core = lax.axis_index("c").astype(jnp.int32).reshape(1)
    g4s, got = _sib_wait(pair, after, name=f"rs_sib_wait_{tag}")
    hs = [_rs_pair_sum(g4, a, core, name=f"rs_pair_sum_{tag}_{i}") for i, (g4, a) in enumerate(zip(g4s, got))]
    return _rs_start(hs, name=f"rs_start_{tag}")


def _scatter_end(started, after, tag):
    return _rs_wait(started, after, name=f"rs_wait_{tag}")


MM_TM = 512


def _ffn_fwd(x_mid, g_row, wup_p, wdown, conv_w, conv_b, tag, dep=None):
    hb, _ = _rms_fwd(x_mid, g_row, want_f32=False, name=f"ffn_norm_{tag}", dep=dep)
    up = _mm_nn_pieces(hb, wup_p, tm=MM_TM, name=f"ffn_up_{tag}")
    a = _conv_gate_fwd(up, conv_w, conv_b, name=f"ffn_conv_{tag}")
    if callable(wdown):
        wdown = wdown(a)
    x_out = _mm_nn(a, wdown, tm=1024, tn=512, name=f"ffn_down_{tag}", res=x_mid)
    return x_out, (hb, up, a)


def _ffn_bwd(dx, x_mid, g_row, wup_p, wdown, conv_w, conv_b, saved, tag):
    hb, up, a = saved
    dxb = dx.astype(BF16)
    da = _mm_nt(dxb, wdown, tm=MM_TM, tn=1408, name=f"ffn_da_{tag}")
    dwdown = _mm_tn(a, dxb, tm=512, tn=1024, name=f"ffn_dwdown_{tag}")
    dup_v, dup_g, dconv_w, dconv_b = _conv_gate_bwd(up, da, conv_w, conv_b, name=f"ffn_dconv_{tag}")
    dwup = _mm_tn_pieces(hb, (dup_v, dup_g), pieces=N_DEV, tm=MM_TM, name=f"ffn_dwup_{tag}")
    pair, token = _scatter_pair([dwup, dwdown.reshape(N_DEV, D_FF // N_DEV, D_MODEL)], f"ffn_{tag}")
    dh = _mm_nt_pieces((dup_v, dup_g), wup_p, tm=1024, tn=1024, name=f"ffn_dh_{tag}", dep=token)
    started, token = _scatter_chips(pair, dh, f"ffn_{tag}")
    dx_mid, dg = _rms_bwd(x_mid, g_row, dh, dx, name=f"ffn_dnorm_{tag}", dep=token)
    return dx_mid, dg, dconv_w, dconv_b, started


def _pool_layer_fwd(x, g_row, w, b_row, sc_row, tag, dep=None):
    _, hf = _rms_fwd(x, g_row, want_f32=True, name=f"pool_norm_{tag}", dep=dep)
    return _pool_fwd(hf, x, w, b_row, sc_row, name=f"pool_fwd_{tag}"), (hf,)


def _pool_layer_bwd(dx_mid, x, g_row, w, b_row, sc_row, saved, tag):
    (hf,) = saved
    dh, dw, db, dsc = _pool_bwd(hf, dx_mid, w, b_row, sc_row, name=f"pool_bwd_{tag}")
    dx, dg = _rms_bwd(x, g_row, dh, dx_mid, name=f"pool_dnorm_{tag}")
    return dx, dg, dw, db, dsc


def _sb_layer_fwd(x, g_row, wqkv_p, gains, wo, tag, dep=None):
    hb, _ = _rms_fwd(x, g_row, want_f32=False, name=f"sb_norm_{tag}", dep=dep)
    qkv = _mm_nn_pieces(hb, wqkv_p, tm=MM_TM, name=f"sb_qkv_{tag}")
    qkn = _qk_norm_fwd(qkv, gains, name=f"sb_qknorm_{tag}")
    vb = qkv[:, 2 * D_MODEL:].astype(BF16)
    o = _sb_fwd(qkn, vb, name=f"sb_att_{tag}")
    x_mid = _mm_nn(o, wo, tm=MM_TM, tn=512, name=f"sb_out_{tag}", res=x)
    return x_mid, (hb, qkv, qkn, vb, o)


def _sb_layer_bwd(dx_mid, x, g_row, wqkv_p, gains, wo, saved, tag):
    hb, qkv, qkn, vb, o = saved
    dmb = dx_mid.astype(BF16)
    do = _mm_nt(dmb, wo, tm=MM_TM, tn=512, name=f"sb_do_{tag}", out_dtype=BF16)
    dwo = _mm_tn(o, dmb, tm=512, tn=1024, name=f"sb_dwo_{tag}")
    dqn, dkn, dv = _sb_bwd(qkn, vb, do, name=f"sb_datt_{tag}")
    dq, dqg = _qk_norm_bwd(qkv, gains, dqn, which=0, name=f"sb_dqnorm_{tag}")
    dk, dkg = _qk_norm_bwd(qkv, gains, dkn, which=1, name=f"sb_dknorm_{tag}")
    dqkv = jnp.concatenate([dq, dk, dv.astype(BF16)], axis=1)
    dwqkv = _mm_tn_pieces(hb, dqkv, pieces=N_DEV, tm=MM_TM, name=f"sb_dwqkv_{tag}")
    pair, token = _scatter_pair([dwqkv, dwo.reshape(N_DEV, D_MODEL // N_DEV, D_MODEL)], f"sb_{tag}")
    dh = _mm_nt_pieces(dqkv, wqkv_p, tm=MM_TM, tn=1024, name=f"sb_dh_{tag}", dep=token)
    started, token = _scatter_chips(pair, dh, f"sb_{tag}")
    dx, dg = _rms_bwd(x, g_row, dh, dx_mid, name=f"sb_dnorm_{tag}", dep=token)
    return dx, dg, dqg, dkg, started


def _ssm_params(lam_re, lam_im, log_step, b_re, b_im):
    g, p = SSM_GROUPS, SSM_STATE
    return (lam_re.reshape(g, 1, p), lam_im.reshape(g, 1, p), log_step.reshape(g, 1, 1),
            jnp.transpose(b_re, (0, 2, 1)), jnp.transpose(b_im, (0, 2, 1)))


def _ssm_layer_fwd(x, g_row, raw, c_re, c_im, d_row, wglu_p, bglu_row, tag, dep=None):
    g, p, ch = SSM_GROUPS, SSM_STATE, SSM_CH
    _, hf = _rms_fwd(x, g_row, want_f32=True, name=f"ssm_norm_{tag}", dep=dep)
    ar, ai, bbr, bbi = _ssm_prep_fwd(*raw, name=f"ssm_prep_{tag}")
    wb = jnp.concatenate([bbr.reshape(g * ch, p), bbi.reshape(g * ch, p)], axis=1)
    wc = jnp.concatenate([c_re.reshape(g * ch, p), -c_im.reshape(g * ch, p)], axis=1)
    a_re, a_im = ar.reshape(1, g * p), ai.reshape(1, g * p)
    ylin, yg = _ssm_core_fwd(hf, wb, wc, a_re, a_im, d_row, name=f"ssm_core_{tag}")
    x_mid, val, gate = _glu_fwd(yg, wglu_p, bglu_row, x, tm=MM_TM, name=f"ssm_glu_{tag}")
    return x_mid, (hf, wb, wc, a_re, a_im, ylin, yg, val, gate)


def _ssm_layer_bwd(dx_mid, x, g_row, raw, d_row, wglu_p, saved, tag):
    g, p, ch = SSM_GROUPS, SSM_STATE, SSM_CH
    hf, wb, wc, a_re, a_im, ylin, yg, val, gate = saved
    dgv, dbglu = _glu_bwd(dx_mid, val, gate, name=f"ssm_dglu_{tag}")
    dwglu = _mm_tn_pieces(yg, dgv, pieces=N_DEV, tm=MM_TM, name=f"ssm_dwglu_{tag}")
    pair, token = _scatter_pair([dwglu], f"ssm_{tag}")
    dyg = _mm_nt_pieces(dgv, wglu_p, tm=MM_TM, tn=1024, name=f"ssm_dyg_{tag}", dep=token)
    du, dwb, dwc, dar, dai, dd = _ssm_core_bwd(hf, ylin, dyg, wb, wc, a_re, a_im, d_row, name=f"ssm_dcore_{tag}")
    dc_re = dwc[:, :p].reshape(g, ch, p)
    dc_im = -dwc[:, p:].reshape(g, ch, p)
    dlr, dli, dls, dbtr, dbti = _ssm_prep_bwd(
        *raw, dar.reshape(g, 1, p), dai.reshape(g, 1, p), dwb[:, :p].reshape(g, ch, p), dwb[:, p:].reshape(g, ch, p),
        name=f"ssm_dprep_{tag}")
    started, token = _scatter_chips(pair, du, f"ssm_{tag}")
    dx, dg = _rms_bwd(x, g_row, du, dx_mid, name=f"ssm_dnorm_{tag}", dep=token)
    grads = dict(ssm_lam_re=dlr.reshape(1, g, p), ssm_lam_im=dli.reshape(1, g, p), ssm_log_step=dls.reshape(1, g),
                 ssm_b_re=jnp.transpose(dbtr, (0, 2, 1))[None], ssm_b_im=jnp.transpose(dbti, (0, 2, 1))[None],
                 ssm_c_re=dc_re[None], ssm_c_im=dc_im[None], ssm_d=dd, ssm_b_glu=dbglu)
    return dx, dg, grads, started


_WEIGHTS = ["norm_mix_g", "norm_ffn_g", "pool_w", "pool_b", "pool_scale", "sb_w_qkv", "sb_q_gain", "sb_k_gain", "sb_w_o",
            "ssm_lam_re", "ssm_lam_im", "ssm_log_step", "ssm_b_re", "ssm_b_im", "ssm_c_re", "ssm_c_im", "ssm_d",
            "ssm_w_glu", "ssm_b_glu", "ffn_w_up", "ffn_conv_w", "ffn_conv_b", "ffn_w_down"]
_INPUTS = ["x"] + _WEIGHTS + ["loss_target"] + ["m_" + n for n in _WEIGHTS] + ["v_" + n for n in _WEIGHTS]
_REPLICATED = ["norm_mix_g", "norm_ffn_g", "sb_q_gain", "sb_k_gain", "ssm_lam_re", "ssm_lam_im", "ssm_log_step",
               "ssm_b_re", "ssm_b_im", "ssm_c_re", "ssm_c_im", "ffn_conv_b"]
_SMALL_SHARDED = {"pool_b": 1, "pool_scale": 1, "ssm_d": 1, "ssm_b_glu": 1, "ffn_conv_w": 2}
_BIG = ["pool_w", "sb_w_qkv", "sb_w_o", "ssm_w_glu", "ffn_w_up", "ffn_w_down"]
PACK_COLS = 512


def _pack(arrays):
    flat = jnp.concatenate([a.reshape(-1).astype(F32) for a in arrays])
    rows = -(-flat.shape[0] // (PACK_COLS * PACK_ROWS)) * PACK_ROWS
    return jnp.pad(flat, (0, rows * PACK_COLS - flat.shape[0])).reshape(rows, PACK_COLS)


def _unpack(packed, shapes, lead=()):
    flat = packed.reshape(lead + (-1,))
    out, off = [], 0
    for shp in shapes:
        n = math.prod(shp)
        out.append(flat[..., off:off + n].reshape(lead + tuple(shp)))
        off += n
    return out


def _unshard(gathered, axis):
    g = jnp.moveaxis(gathered, 0, axis)
    shp = g.shape
    return g.reshape(shp[:axis] + (shp[axis] * shp[axis + 1],) + shp[axis + 2:])


def _step(p):
    s = p["x"].shape[1]
    x = p["x"].reshape(s, D_MODEL)
    me = _slot(*_place())

    small_local = [p[n] for n in _SMALL_SHARDED]
    pool_w_l = p["pool_w"].astype(BF16).reshape(-1, POOL_DIM)
    chip = (2 * lax.axis_index("x") + lax.axis_index("y")).astype(jnp.int32).reshape(1)

    def ffn_shards(i):
        return [p["ffn_w_up"][i].astype(BF16), p["ffn_w_down"][i].astype(BF16)]

    st_first, tok = _gather_begin([pool_w_l, _pack(small_local)], "first")
    st_ffn = [None] * DEPTH
    st_ffn[0], tok = _gather_begin(ffn_shards(0)[:1], "ffn_0", dep=tok)
    st_down0, tok = _gather_begin(ffn_shards(0)[1:], "ffn_0_down", dep=tok)
    st_sb, tok = _gather_begin([p["sb_w_qkv"][0].astype(BF16), p["sb_w_o"][0].astype(BF16)], "sb", dep=tok)
    st_ffn[1], tok = _gather_begin(ffn_shards(1), "ffn_1", dep=tok)
    st_glu, tok = _gather_begin([p["ssm_w_glu"][0].astype(BF16)], "glu", dep=tok)
    ag = _gather_end(st_first, tok, "first")
    n_pool = p["pool_w"].shape[0]
    pool_w = jnp.transpose(ag[0].reshape(N_DEV, n_pool, POOL_GROUPS, POOL_DIM // N_DEV, POOL_DIM), (1, 2, 0, 3, 4))
    pool_w = pool_w.reshape(n_pool, POOL_GROUPS, POOL_DIM, POOL_DIM)
    small_full = {}
    for n, g in zip(_SMALL_SHARDED, _unpack(ag[1], [a.shape for a in small_local], lead=(N_DEV,))):
        small_full[n] = _unshard(g, _SMALL_SHARDED[n])
    mix_w = {}
    wup_p, wdown = [None] * DEPTH, [None] * DEPTH

    gains = jnp.stack([p["sb_q_gain"][0], p["sb_k_gain"][0]])[:, None, :]
    ssm_raw = _ssm_params(p["ssm_lam_re"][0], p["ssm_lam_im"][0], p["ssm_log_step"][0], p["ssm_b_re"][0],
                          p["ssm_b_im"][0])

    def mixer_args(i):
        j = i // 3
        g_row = p["norm_mix_g"][i][None]
        if i % 3 == 0:
            return (g_row, pool_w[j], small_full["pool_b"][j][None], small_full["pool_scale"][j][None])
        if i % 3 == 1:
            return (g_row, mix_w["qkv"], gains, mix_w["o"])
        return (g_row, ssm_raw, p["ssm_c_re"][0], p["ssm_c_im"][0], small_full["ssm_d"], mix_w["glu"],
                small_full["ssm_b_glu"])

    def ffn_args(i):
        return (p["norm_ffn_g"][i][None], wup_p[i], wdown[i], small_full["ffn_conv_w"][i], p["ffn_conv_b"][i][None])

    xs_in, xs_mid, saved_mix, saved_ffn = [], [], [], []
    for i in range(DEPTH):
        xs_in.append(x)
        if i == 1:
            mix_w["qkv"], wo_g = _gather_end(st_sb, x, "sb")
            mix_w["o"] = wo_g.reshape(D_MODEL, D_MODEL)
        if i == 2:
            (mix_w["glu"],) = _gather_end(st_glu, x, "glu")
        dep, handing = None, None
        if i >= 2:
            _, lands = _ag_wait(st_ffn[i], x, name=f"ag_wait_ffn_{i}")
            handing, dep = _fw_start(lands, name=f"ag_fw_start_ffn_{i}")
        if 1 <= i and i + 1 < DEPTH:
            st_ffn[i + 1], dep = _gather_begin(ffn_shards(i + 1), f"ffn_{i + 1}", dep=dep)
        fwd = (_pool_layer_fwd, _sb_layer_fwd, _ssm_layer_fwd)[i % 3]
        x, sv = fwd(x, *mixer_args(i), f"l{i}", dep=dep)
        saved_mix.append(sv)
        xs_mid.append(x)
        if i == 0:
            (wup_p[i],) = _gather_end(st_ffn[i], x, f"ffn_{i}")

            def wdown_now(after):
                (wd_g,) = _gather_end(st_down0, after, "ffn_0_down")
                wdown[0] = wd_g.reshape(D_FF, D_MODEL)
                return wdown[0]
        else:
            if handing is None:
                wup_p[i], wd_g = _gather_end(st_ffn[i], x, f"ffn_{i}")
            else:
                wup_p[i], wd_g = _fw_wait(handing, x, name=f"ag_fw_wait_ffn_{i}")
            wdown[i] = wd_g.reshape(D_FF, D_MODEL)
        g_row, wu, wd, cw, cb = ffn_args(i)
        x, sv = _ffn_fwd(x, g_row, wu, wdown_now if i == 0 else wd, cw, cb, f"l{i}")
        saved_ffn.append(sv)
    dx, loss_part = _loss_head(x, p["loss_target"].reshape(s, D_MODEL), name="loss_head")

    grads = {}
    dg_mix, dg_ffn = [None] * DEPTH, [None] * DEPTH
    dconv_w, dconv_b = [None] * DEPTH, [None] * DEPTH
    dpool = {"w": {}, "b": {}, "scale": {}}
    out = {}

    def big_update(n, h, others, idx=None):
        w, m, v = (p[pre + n] if idx is None else p[pre + n][idx] for pre in ("", "m_", "v_"))
        cols = h.shape[-1]
        r = _adamw_chips(w.reshape(-1, cols), h, others, m.reshape(-1, cols), v.reshape(-1, cols), chip,
                         name=f"adamw_{n}" + ("" if idx is None else f"_{idx}"))
        return [a.reshape(w.shape) for a in r]

    kinds = ("grad", "delta", "new_m", "new_v")
    ffn_upd = {"ffn_w_up": [None] * DEPTH, "ffn_w_down": [None] * DEPTH}

    def finish(entry, after):
        names, idx, started, tag = entry
        hs, others = _scatter_end(started, after, tag)
        for n, h, o in zip(names, hs, others):
            if idx is None:
                for kind, a in zip(kinds, big_update(n, h, o)):
                    out[kind + "_" + n] = a
            else:
                ffn_upd[n][idx] = (h, o)

    pending = []
    for i in reversed(range(DEPTH)):
        dx, dg_ffn[i], dconv_w[i], dconv_b[i], started = _ffn_bwd(
            dx, xs_mid[i], *ffn_args(i), saved_ffn[i], f"l{i}")
        for entry in pending:
            finish(entry, dx)
        pending = [(("ffn_w_up", "ffn_w_down"), i, started, f"ffn_l{i}")]
        margs = mixer_args(i)
        if i % 3 == 0:
            j = i // 3
            dx, dg_mix[i], dpool["w"][j], dpool["b"][j], dpool["scale"][j] = _pool_layer_bwd(
                dx, xs_in[i], *margs, saved_mix[i], f"l{i}")
        elif i % 3 == 1:
            dx, dg_mix[i], dqg, dkg, started = _sb_layer_bwd(dx, xs_in[i], *margs, saved_mix[i], f"l{i}")
            grads["sb_q_gain"], grads["sb_k_gain"] = dqg, dkg
            pending.append((("sb_w_qkv", "sb_w_o"), None, started, f"sb_l{i}"))
        else:
            g_row, raw, _, _, d_row, wg, _ = margs
            dx, dg_mix[i], sg, started = _ssm_layer_bwd(dx, xs_in[i], g_row, raw, d_row, wg, saved_mix[i], f"l{i}")
            grads.update(sg)
            pending.append((("ssm_w_glu",), None, started, f"ssm_l{i}"))
    grad_x = dx.reshape(1, s, D_MODEL)
    grads["norm_mix_g"] = jnp.concatenate(dg_mix, axis=0)
    grads["norm_ffn_g"] = jnp.concatenate(dg_ffn, axis=0)
    grads["ffn_conv_w"] = jnp.stack(dconv_w)
    grads["ffn_conv_b"] = jnp.concatenate(dconv_b, axis=0)
    grads["pool_b"] = jnp.concatenate([dpool["b"][j] for j in range(n_pool)], axis=0)
    grads["pool_scale"] = jnp.concatenate([dpool["scale"][j] for j in range(n_pool)], axis=0)
    dpw = jnp.stack([dpool["w"][j] for j in range(n_pool)])
    dpw = dpw.reshape(n_pool, POOL_GROUPS, N_DEV, POOL_DIM // N_DEV, POOL_DIM)
    pair, token = _scatter_pair([jnp.transpose(dpw, (2, 0, 1, 3, 4)).reshape(N_DEV, -1, POOL_DIM)], "pool")
    pool_started, tok_pool = _scatter_chips(pair, token, "pool")
    (ffn_first,) = pending

    small_names = _REPLICATED + list(_SMALL_SHARDED)
    full_shapes = [p[n].shape for n in _REPLICATED] + [small_full[n].shape for n in _SMALL_SHARDED]
    part = _pack([grads[n].reshape(shp) for n, shp in zip(small_names, full_shapes)] + [loss_part[0]])
    st_small, tok_small = _gather_begin([part], "small_grads")
    upper = {}
    for n in ffn_upd:
        upper[n] = _adamw_chips_layers(p[n], [ffn_upd[n][i][0] for i in range(1, DEPTH)],
                                       [ffn_upd[n][i][1] for i in range(1, DEPTH)], p["m_" + n], p["v_" + n], chip,
                                       first=1, deps=(tok_small, tok_pool), name=f"adamw_{n}_upper")
    finish(ffn_first, upper["ffn_w_up"][0])
    for n in ffn_upd:
        h, o = ffn_upd[n][0]
        upd = _adamw_chips_layers(p[n], [h], [o], p["m_" + n], p["v_" + n], chip, first=0, prev=upper[n],
                                  name=f"adamw_{n}_first")
        for kind, a in zip(kinds, upd):
            out[kind + "_" + n] = a
    done = sum(out["new_v_" + n][(0,) * out["new_v_" + n].ndim] for n in _BIG if n != "pool_w").reshape(1, 1)
    finish((("pool_w",), None, pool_started, "pool"), done)
    (parts,) = _gather_end(st_small, done, "small_grads")
    summed = _unpack(_sum_parts(parts, name="sum_small_grads"), full_shapes + [(LANE,)])
    loss = summed[-1][0]
    small_g = {}
    for n, g in zip(small_names, summed[:-1]):
        if n in _SMALL_SHARDED:
            ax = _SMALL_SHARDED[n]
            g = lax.dynamic_slice_in_dim(g, me * p[n].shape[ax], p[n].shape[ax], axis=ax)
        small_g[n] = g

    local_shapes = [p[n].shape for n in small_names]
    packs = [_pack([p[pre + n] for n in small_names]) for pre in ("", "m_", "v_")]
    res = _adamw_flat(packs[0], _pack([small_g[n] for n in small_names]), packs[1], packs[2], name="adamw_small")
    for kind, packed in zip(("delta", "new_m", "new_v"), res):
        for n, a in zip(small_names, _unpack(packed, local_shapes)):
            out[kind + "_" + n] = a
    for n in small_names:
        out["grad_" + n] = small_g[n]

    return (loss, grad_x, *[out["grad_" + n] for n in _WEIGHTS], *[out["delta_" + n] for n in _WEIGHTS],
            *[out["new_m_" + n] for n in _WEIGHTS], *[out["new_v_" + n] for n in _WEIGHTS])


def kernel(x, norm_mix_g, norm_ffn_g, pool_w, pool_b, pool_scale, sb_w_qkv, sb_q_gain, sb_k_gain, sb_w_o, ssm_lam_re, ssm_lam_im, ssm_log_step, ssm_b_re, ssm_b_im, ssm_c_re, ssm_c_im, ssm_d, ssm_w_glu, ssm_b_glu, ffn_w_up, ffn_conv_w, ffn_conv_b, ffn_w_down, loss_target, m_norm_mix_g, m_norm_ffn_g, m_pool_w, m_pool_b, m_pool_scale, m_sb_w_qkv, m_sb_q_gain, m_sb_k_gain, m_sb_w_o, m_ssm_lam_re, m_ssm_lam_im, m_ssm_log_step, m_ssm_b_re, m_ssm_b_im, m_ssm_c_re, m_ssm_c_im, m_ssm_d, m_ssm_w_glu, m_ssm_b_glu, m_ffn_w_up, m_ffn_conv_w, m_ffn_conv_b, m_ffn_w_down, v_norm_mix_g, v_norm_ffn_g, v_pool_w, v_pool_b, v_pool_scale, v_sb_w_qkv, v_sb_q_gain, v_sb_k_gain, v_sb_w_o, v_ssm_lam_re, v_ssm_lam_im, v_ssm_log_step, v_ssm_b_re, v_ssm_b_im, v_ssm_c_re, v_ssm_c_im, v_ssm_d, v_ssm_w_glu, v_ssm_b_glu, v_ffn_w_up, v_ffn_conv_w, v_ffn_conv_b, v_ffn_w_down):
    args = (x, norm_mix_g, norm_ffn_g, pool_w, pool_b, pool_scale, sb_w_qkv, sb_q_gain, sb_k_gain, sb_w_o, ssm_lam_re, ssm_lam_im, ssm_log_step, ssm_b_re, ssm_b_im, ssm_c_re, ssm_c_im, ssm_d, ssm_w_glu, ssm_b_glu, ffn_w_up, ffn_conv_w, ffn_conv_b, ffn_w_down, loss_target, m_norm_mix_g, m_norm_ffn_g, m_pool_w, m_pool_b, m_pool_scale, m_sb_w_qkv, m_sb_q_gain, m_sb_k_gain, m_sb_w_o, m_ssm_lam_re, m_ssm_lam_im, m_ssm_log_step, m_ssm_b_re, m_ssm_b_im, m_ssm_c_re, m_ssm_c_im, m_ssm_d, m_ssm_w_glu, m_ssm_b_glu, m_ffn_w_up, m_ffn_conv_w, m_ffn_conv_b, m_ffn_w_down, v_norm_mix_g, v_norm_ffn_g, v_pool_w, v_pool_b, v_pool_scale, v_sb_w_qkv, v_sb_q_gain, v_sb_k_gain, v_sb_w_o, v_ssm_lam_re, v_ssm_lam_im, v_ssm_log_step, v_ssm_b_re, v_ssm_b_im, v_ssm_c_re, v_ssm_c_im, v_ssm_d, v_ssm_w_glu, v_ssm_b_glu, v_ffn_w_up, v_ffn_conv_w, v_ffn_conv_b, v_ffn_w_down)
    return _step(dict(zip(_INPUTS, args)))
```
